```python
import jax, jax.numpy as jnp
from jax import lax
import numpy as np

D_MODEL = 1024
BATCH = 8
SEQ = 2048
DEPTH = 1
DEC_BATCH = 128
DEC_SEQ = 1
PAST_LEN = 16384
PAGE_SIZE = 128

D_MLSTM = D_MODEL // 2
N_HEADS = 4
DK = D_MLSTM // N_HEADS
DV = DK
CHUNK = 128
D_CONV = D_MODEL - D_MLSTM
CONV_W = 31
D_IN = 4 * D_MLSTM + 2 * N_HEADS + 2 * D_CONV
N_GROUPS = 4
EXP_PER_GROUP = 8
N_EXPERTS = N_GROUPS * EXP_PER_GROUP
TOP_K = 2
D_EXPERT = 256
EPS = 1e-6

kernel_name = "hymba_mlstm_conformer_hiermoe_step"


def rms_norm(x, g):
    xf = x.astype(jnp.float32)
    y = xf * lax.rsqrt(jnp.mean(xf * xf, -1, keepdims=True) + EPS)
    return (y * g.astype(jnp.float32)).astype(x.dtype)


def layer_norm_f32(x, g, b=None):
    xf = x.astype(jnp.float32)
    mu = jnp.mean(xf, -1, keepdims=True)
    var = jnp.mean(jnp.square(xf - mu), -1, keepdims=True)
    y = (xf - mu) * lax.rsqrt(var + EPS) * g.astype(jnp.float32)
    if b is not None:
        y = y + b.astype(jnp.float32)
    return y


def mlstm_chunkwise(q, k, v, ig, fg, C0, n0, m0):
    B, H, T, _ = q.shape
    L = CHUNK if T % CHUNK == 0 else T
    nc = T // L
    logf = jax.nn.log_sigmoid(fg)

    def split(a):
        return jnp.moveaxis(a.reshape(B, H, nc, L, *a.shape[3:]), 2, 0)

    causal = jnp.tril(jnp.ones((L, L), dtype=bool))

    def step(carry, xs):
        C, n, m = carry
        qc, kc, vc, ic, lfc = xs
        b = jnp.cumsum(lfc, -1)
        dmat = b[..., :, None] - b[..., None, :] + ic[..., None, :]
        dmat = jnp.where(causal, dmat, -jnp.inf)
        inter = b + m[..., None]
        m_t = jnp.maximum(inter, jnp.max(dmat, -1))
        w = jnp.exp(dmat - m_t[..., None])
        a_inter = jnp.exp(inter - m_t)
        s = jnp.einsum('bhtd,bhsd->bhts', qc, kc) * w
        num = jnp.einsum('bhts,bhse->bhte', s, vc) + a_inter[..., None] * jnp.einsum('bhtd,bhde->bhte', qc, C)
        den = jnp.sum(s, -1) + a_inter * jnp.einsum('bhtd,bhd->bht', qc, n)
        h = num / jnp.maximum(jnp.abs(den), jnp.exp(-m_t))[..., None]
        m_new = m_t[..., -1]
        wl = jnp.exp(dmat[..., -1, :] - m_new[..., None])
        decay = jnp.exp(b[..., -1] + m - m_new)
        C_new = decay[..., None, None] * C + jnp.einsum('bhs,bhsd,bhse->bhde', wl, kc, vc)
        n_new = decay[..., None] * n + jnp.einsum('bhs,bhsd->bhd', wl, kc)
        return (C_new, n_new, m_new), h

    (C, n, m), hs = lax.scan(step, (C0, n0, m0), (split(q), split(k), split(v), split(ig), split(logf)))
    h = jnp.moveaxis(hs, 0, 2).reshape(B, H, T, DV)
    return h, C, n, m


def conformer_conv(u, buf, w_dw, b_dw, ln_g, ln_b):
    full = jnp.concatenate([buf.astype(u.dtype), u], axis=1)
    y = lax.conv_general_dilated(full, w_dw.astype(u.dtype), window_strides=(1,), padding='VALID',
                                 dimension_numbers=('NWC', 'WIO', 'NWC'),
                                 feature_group_count=D_CONV) + b_dw
    new_buf = full[:, -(CONV_W - 1):]
    y = jax.nn.silu(layer_norm_f32(y, ln_g, ln_b)).astype(u.dtype)
    return y, new_buf


def hier_moe(x, w_grp, b_grp, w_er, b_er, w_gate, w_up, w_down):
    N = x.shape[0]
    xf = x.astype(jnp.float32)
    g_logits = xf @ w_grp.astype(jnp.float32) + b_grp
    g_prob = jax.nn.softmax(g_logits, -1)
    g_sel = jnp.argmax(g_logits, -1)
    p_g = jnp.take_along_axis(g_prob, g_sel[:, None], -1)
    e_logits = (xf @ w_er.astype(jnp.float32) + b_er).reshape(N, N_GROUPS, EXP_PER_GROUP)
    e_sel = jnp.take_along_axis(e_logits, g_sel[:, None, None], 1)[:, 0]
    top_v, top_i = lax.top_k(e_sel, TOP_K)
    top_w = jax.nn.softmax(top_v, -1) * p_g
    e_idx = g_sel[:, None] * EXP_PER_GROUP + top_i
    combine = jnp.sum(jax.nn.one_hot(e_idx, N_EXPERTS, dtype=jnp.float32) * top_w[..., None], 1).astype(x.dtype)
    out = jnp.zeros_like(x)
    for grp in range(N_GROUPS):
        sl = slice(grp * EXP_PER_GROUP, (grp + 1) * EXP_PER_GROUP)
        hid = jax.nn.silu(jnp.einsum('nd,edf->nef', x, w_gate[sl])) * jnp.einsum('nd,edf->nef', x, w_up[sl])
        out = out + jnp.einsum('nef,efd->nd', hid * combine[:, sl, None], w_down[sl])
    return out


def hybrid_layer(x, c, C0, n0, m0, buf, w_ada, b_ada, norm1_g, w_in, b_igate, b_fgate, mh_norm_g,
                 w_dw, b_dw, conv_ln_g, conv_ln_b, w_out, norm2_g, w_grp_router, b_grp_router,
                 w_exp_router, b_exp_router, w_gate, w_up, w_down):
    B, T, _ = x.shape
    mod = jax.nn.silu(c) @ w_ada + b_ada
    sh1, sc1, g1, sh2, sc2, g2 = jnp.split(mod, 6, axis=-1)
    h = rms_norm(x, norm1_g) * (1 + sc1[:, None]) + sh1[:, None]
    z = h @ w_in
    cuts = [D_MLSTM, 2 * D_MLSTM, 3 * D_MLSTM, 4 * D_MLSTM, 4 * D_MLSTM + N_HEADS,
            4 * D_MLSTM + 2 * N_HEADS, 4 * D_MLSTM + 2 * N_HEADS + D_CONV]
    q, k, v, o, ig, fg, ga, gb = jnp.split(z, cuts, axis=-1)

    def heads(a):
        return a.reshape(B, T, N_HEADS, -1).transpose(0, 2, 1, 3).astype(jnp.float32)

    igh = (ig.astype(jnp.float32) + b_igate).transpose(0, 2, 1)
    fgh = (fg.astype(jnp.float32) + b_fgate).transpose(0, 2, 1)
    hm, C1, n1, m1 = mlstm_chunkwise(heads(q), heads(k) * DK ** -0.5, heads(v), igh, fgh,
                                     C0.astype(jnp.float32), n0.astype(jnp.float32), m0.astype(jnp.float32))
    hm = layer_norm_f32(hm, mh_norm_g[:, None, :])
    hm = hm.transpose(0, 2, 1, 3).reshape(B, T, D_MLSTM).astype(x.dtype) * jax.nn.sigmoid(o)

    u = ga * jax.nn.sigmoid(gb)
    hc, new_buf = conformer_conv(u, buf, w_dw, b_dw, conv_ln_g, conv_ln_b)

    x = x + g1[:, None] * (jnp.concatenate([hm, hc], axis=-1) @ w_out)
    h2 = rms_norm(x, norm2_g) * (1 + sc2[:, None]) + sh2[:, None]
    moe = hier_moe(h2.reshape(B * T, D_MODEL), w_grp_router, b_grp_router, w_exp_router, b_exp_router,
                   w_gate, w_up, w_down).reshape(B, T, D_MODEL)
    x = x + g2[:, None] * moe
    return x, C1, n1, m1, new_buf


def setup_inputs(seed: int = 0) -> dict:
    key = jax.random.key(seed)
    ks = jax.random.split(key, 32)
    f32 = jnp.float32

    def nrm(k, shape, scale):
        return jax.random.normal(k, shape, f32) * scale

    return {
        "x_prompt": nrm(ks[0], (BATCH, SEQ, D_MODEL), 1.0),
        "x_sample": nrm(ks[1], (DEC_BATCH, DEC_SEQ, D_MODEL), 1.0),
        "c_prompt": nrm(ks[2], (BATCH, D_MODEL), 1.0),
        "c_sample": nrm(ks[3], (DEC_BATCH, D_MODEL), 1.0),
        "state_mlstm_C": nrm(ks[4], (DEPTH, DEC_BATCH, N_HEADS, DK, DV), 0.1),
        "state_mlstm_n": nrm(ks[5], (DEPTH, DEC_BATCH, N_HEADS, DK), 0.1),
        "state_mlstm_m": nrm(ks[6], (DEPTH, DEC_BATCH, N_HEADS), 1.0),
        "cache_conv": nrm(ks[7], (DEPTH, DEC_BATCH, CONV_W - 1, D_CONV), 0.5),
        "w_ada": nrm(ks[8], (DEPTH, D_MODEL, 6 * D_MODEL), 0.5 * D_MODEL ** -0.5),
        "b_ada": nrm(ks[9], (DEPTH, 6 * D_MODEL), 0.02),
        "norm1_g": 1.0 + nrm(ks[10], (DEPTH, D_MODEL), 0.05),
        "w_in": nrm(ks[11], (DEPTH, D_MODEL, D_IN), D_MODEL ** -0.5),
        "b_igate": nrm(ks[12], (DEPTH, N_HEADS), 0.1),
        "b_fgate": 3.0 + nrm(ks[13], (DEPTH, N_HEADS), 0.5),
        "mh_norm_g": 1.0 + nrm(ks[14], (DEPTH, N_HEADS, DV), 0.05),
        "w_dw": nrm(ks[15], (DEPTH, CONV_W, 1, D_CONV), CONV_W ** -0.5),
        "b_dw": nrm(ks[16], (DEPTH, D_CONV), 0.02),
        "conv_ln_g": 1.0 + nrm(ks[17], (DEPTH, D_CONV), 0.05),
        "conv_ln_b": nrm(ks[18], (DEPTH, D_CONV), 0.02),
        "w_out": nrm(ks[19], (DEPTH, D_MODEL, D_MODEL), D_MODEL ** -0.5),
        "norm2_g": 1.0 + nrm(ks[20], (DEPTH, D_MODEL), 0.05),
        "w_grp_router": nrm(ks[21], (DEPTH, D_MODEL, N_GROUPS), D_MODEL ** -0.5),
        "b_grp_router": nrm(ks[22], (DEPTH, N_GROUPS), 0.01),
        "w_exp_router": nrm(ks[23], (DEPTH, D_MODEL, N_EXPERTS), D_MODEL ** -0.5),
        "b_exp_router": nrm(ks[24], (DEPTH, N_EXPERTS), 0.01),
        "w_gate": nrm(ks[25], (DEPTH, N_EXPERTS, D_MODEL, D_EXPERT), D_MODEL ** -0.5),
        "w_up": nrm(ks[26], (DEPTH, N_EXPERTS, D_MODEL, D_EXPERT), D_MODEL ** -0.5),
        "w_down": nrm(ks[27], (DEPTH, N_EXPERTS, D_EXPERT, D_MODEL), D_EXPERT ** -0.5),
        "final_g": 1.0 + nrm(ks[28], (D_MODEL,), 0.05),
    }


def reference(x_prompt, x_sample, c_prompt, c_sample, state_mlstm_C, state_mlstm_n, state_mlstm_m,
              cache_conv, w_ada, b_ada, norm1_g, w_in, b_igate, b_fgate, mh_norm_g, w_dw, b_dw,
              conv_ln_g, conv_ln_b, w_out, norm2_g, w_grp_router, b_grp_router, w_exp_router,
              b_exp_router, w_gate, w_up, w_down, final_g):
    f32 = jnp.float32
    xp, xs = x_prompt, x_sample
    Cp_l, np_l, mp_l, bp_l = [], [], [], []
    Cs_l, ns_l, ms_l, bs_l = [], [], [], []
    for l in range(DEPTH):
        w = (w_ada[l], b_ada[l], norm1_g[l], w_in[l], b_igate[l], b_fgate[l], mh_norm_g[l],
             w_dw[l], b_dw[l], conv_ln_g[l], conv_ln_b[l], w_out[l], norm2_g[l], w_grp_router[l],
             b_grp_router[l], w_exp_router[l], b_exp_router[l], w_gate[l], w_up[l], w_down[l])
        zC = jnp.zeros((BATCH, N_HEADS, DK, DV), f32)
        zn = jnp.zeros((BATCH, N_HEADS, DK), f32)
        zm = jnp.zeros((BATCH, N_HEADS), f32)
        zb = jnp.zeros((BATCH, CONV_W - 1, D_CONV), xp.dtype)
        xp, Cp, n_p, mp, bp = hybrid_layer(xp, c_prompt, zC, zn, zm, zb, *w)
        xs, Cs, n_s, ms, bs = hybrid_layer(xs, c_sample, state_mlstm_C[l], state_mlstm_n[l],
                                           state_mlstm_m[l], cache_conv[l], *w)
        Cp_l.append(Cp); np_l.append(n_p); mp_l.append(mp); bp_l.append(bp)
        Cs_l.append(Cs); ns_l.append(n_s); ms_l.append(ms); bs_l.append(bs)
    y_prompt = rms_norm(xp, final_g)
    y_sample = rms_norm(xs, final_g)
    return (y_prompt, y_sample,
            jnp.stack(Cp_l), jnp.stack(np_l), jnp.stack(mp_l), jnp.stack(bp_l),
            jnp.stack(Cs_l), jnp.stack(ns_l), jnp.stack(ms_l), jnp.stack(bs_l))
```

```python
import functools

import jax
import jax.numpy as jnp
from jax import lax
from jax.experimental import pallas as pl
from jax.experimental.pallas import tpu as pltpu

F32 = jnp.float32
BF16 = jnp.bfloat16
I32 = jnp.int32

EPS = 1e-6
LANES = 128
CHUNK = 128
N_HEADS = 4
N_GROUPS = 4
EXP_PER_GROUP = 8
N_EXPERTS = N_GROUPS * EXP_PER_GROUP
CONV_W = 31
CONV_PAD = 32
CONV_OFF = CONV_PAD - (CONV_W - 1)
MIX_TILE = 256
MOE_TILE = 256
PERM_TILE = 128
VMEM_LIMIT = 56 * 1024 * 1024


def _sigmoid(x):
    return 1.0 / (1.0 + jnp.exp(-x))


def _silu(x):
    return x * _sigmoid(x)


def _log_sigmoid(x):
    return jnp.minimum(x, 0.0) - jnp.log(1.0 + jnp.exp(-jnp.abs(x)))


def _bdot(a, b):
    return jnp.dot(a.astype(BF16), b.astype(BF16), preferred_element_type=F32)


def _bdot_nt(a, b):
    return lax.dot_general(a.astype(BF16), b.astype(BF16), (((1,), (1,)), ((), ())),
                           preferred_element_type=F32)


def _split2(x):
    hi = x.astype(BF16)
    lo = (x - hi.astype(F32)).astype(BF16)
    return hi, lo


def _dot3(a, w_hi, w_lo):
    a_hi, a_lo = _split2(a)
    return (jnp.dot(a_hi, w_hi, preferred_element_type=F32)
            + jnp.dot(a_lo, w_hi, preferred_element_type=F32)
            + jnp.dot(a_hi, w_lo, preferred_element_type=F32))


def _cumsum_rows(tril_bf16, x):
    hi = x.astype(BF16)
    r1 = x - hi.astype(F32)
    mid = r1.astype(BF16)
    lo = (r1 - mid.astype(F32)).astype(BF16)
    return (jnp.dot(tril_bf16, hi, preferred_element_type=F32)
            + jnp.dot(tril_bf16, mid, preferred_element_type=F32)
            + jnp.dot(tril_bf16, lo, preferred_element_type=F32))


def _rms(x, g):
    return x * lax.rsqrt(jnp.mean(x * x, axis=-1, keepdims=True) + EPS) * g


def _layer_norm(x, g, b=None):
    mu = jnp.mean(x, axis=-1, keepdims=True)
    xc = x - mu
    var = jnp.mean(xc * xc, axis=-1, keepdims=True)
    y = xc * lax.rsqrt(var + EPS) * g
    return y if b is None else y + b


def _route(logits):
    lane = lax.broadcasted_iota(I32, logits.shape, 1).astype(F32)
    neg = jnp.float32(-jnp.inf)
    big = jnp.float32(1e9)
    is_g = lane < N_GROUPS
    gl = jnp.where(is_g, logits, neg)
    gmax = jnp.max(gl, axis=1, keepdims=True)
    gsel = jnp.min(jnp.where(gl == gmax, lane, big), axis=1, keepdims=True)
    pg = 1.0 / jnp.sum(jnp.where(is_g, jnp.exp(gl - gmax), 0.0), axis=1, keepdims=True)
    lo = N_GROUPS + EXP_PER_GROUP * gsel
    emask = (lane >= lo) & (lane < lo + EXP_PER_GROUP)
    el = jnp.where(emask, logits, neg)
    v1 = jnp.max(el, axis=1, keepdims=True)
    i1 = jnp.min(jnp.where(el == v1, lane, big), axis=1, keepdims=True)
    el2 = jnp.where(lane == i1, neg, el)
    v2 = jnp.max(el2, axis=1, keepdims=True)
    i2 = jnp.min(jnp.where(el2 == v2, lane, big), axis=1, keepdims=True)
    d = jnp.exp(v2 - v1)
    w1 = pg / (1.0 + d)
    w2 = pg * d / (1.0 + d)
    return jnp.where(lane == 0, i1 - N_GROUPS,
                     jnp.where(lane == 1, i2 - N_GROUPS,
                               jnp.where(lane == 2, w1, jnp.where(lane == 3, w2, 0.0))))


def _ada_kernel(c_ref, w_ref, b_ref, o_ref):
    w_hi, w_lo = _split2(w_ref[...])
    o_ref[...] = _dot3(_silu(c_ref[...]), w_hi, w_lo) + b_ref[...]


def _ada(c_all, w_ada, b_ada):
    rows, d = c_all.shape
    n_out = w_ada.shape[1]
    blk = 1024
    return pl.pallas_call(
        _ada_kernel,
        grid=(n_out // blk,),
        in_specs=[pl.BlockSpec((rows, d), lambda j: (0, 0)),
                  pl.BlockSpec((d, blk), lambda j: (0, j)),
                  pl.BlockSpec((1, blk), lambda j: (0, j))],
        out_specs=pl.BlockSpec((rows, blk), lambda j: (0, j)),
        out_shape=jax.ShapeDtypeStruct((rows, n_out), F32),
        compiler_params=pltpu.CompilerParams(dimension_semantics=("arbitrary",),
                                             vmem_limit_bytes=VMEM_LIMIT),
        name="ada",
    )(c_all, w_ada, b_ada.reshape(1, n_out))


def _in_proj(x, mod, g1, wmain, wg_hi, wg_lo, gbias, d):
    sh1 = mod[:, 0:d]
    sc1 = mod[:, d:2 * d]
    h = _rms(x, g1) * (1.0 + sc1) + sh1
    z = jnp.dot(h.astype(BF16), wmain, preferred_element_type=F32)
    gates = _dot3(h, wg_hi, wg_lo) + gbias
    return z, gates


def _post(x, attn_cat, mod, g2, wout, wr_hi, wr_lo, rbias, d, wout_lo=None):
    gate1 = mod[:, 2 * d:3 * d]
    sh2 = mod[:, 3 * d:4 * d]
    sc2 = mod[:, 4 * d:5 * d]
    if wout_lo is None:
        proj = jnp.dot(attn_cat.astype(BF16), wout, preferred_element_type=F32)
    else:
        proj = _dot3(attn_cat, wout, wout_lo)
    x1 = x + gate1 * proj
    h2 = _rms(x1, g2) * (1.0 + sc2) + sh2
    logits = _dot3(h2, wr_hi, wr_lo) + rbias
    return x1, h2, _route(logits)


def _mix_prompt_kernel(x_ref, mod_ref, g1_ref, g2_ref, wmain_ref, wgh_ref, wgl_ref, gbias_ref,
                       mhg_ref, wdw_ref, bdw_ref, clg_ref, clb_ref, wout_ref, wrh_ref, wrl_ref,
                       rbias_ref,
                       x1_ref, h2_ref, route_ref, c_ref, n_ref, m_ref, cv_ref,
                       ubuf, q_s, hm_s, m_s, *, tt, d, dm, dk):
    t = pl.program_id(1)
    dc = d - dm

    @pl.when(t == 0)
    def _():
        c_ref[...] = jnp.zeros_like(c_ref)
        n_ref[...] = jnp.zeros_like(n_ref)
        m_s[...] = jnp.zeros_like(m_s)
        ubuf[0:CONV_PAD, :] = jnp.zeros((CONV_PAD, dc), F32)
        ubuf[tt + CONV_PAD:tt + CONV_PAD + 8, :] = jnp.zeros((8, dc), F32)

    x = x_ref[0]
    mod = mod_ref[0]
    z, gates = _in_proj(x, mod, g1_ref[...], wmain_ref[...], wgh_ref[...], wgl_ref[...],
                        gbias_ref[...], d)
    q_all = z[:, 0:dm].astype(BF16)
    k_all = z[:, dm:2 * dm] * (dk ** -0.5)
    v_all = z[:, 2 * dm:3 * dm].astype(BF16)
    o_all = z[:, 3 * dm:4 * dm]
    ga = z[:, 4 * dm:4 * dm + dc]
    gb = z[:, 4 * dm + dc:4 * dm + 2 * dc]

    ubuf[CONV_PAD:CONV_PAD + tt, :] = ga * _sigmoid(gb)

    def conv_rows(i, carry):
        r0 = pl.multiple_of(i * 8, 8)
        blocks = [ubuf[pl.ds(pl.multiple_of(r0 + 8 * a, 8), 8), :] for a in range(CONV_PAD // 8 + 1)]
        for s in range(8):
            acc = None
            for a in range(CONV_PAD // 8 + 1):
                j = 8 * a + s - CONV_OFF
                if 0 <= j < CONV_W:
                    term = blocks[a] * wdw_ref[j:j + 1, :]
                    acc = term if acc is None else acc + term
            q_s[s, pl.ds(r0, 8), :] = acc
        return carry

    lax.fori_loop(0, (tt + 8) // 8, conv_rows, 0)
    yc = jnp.broadcast_to(bdw_ref[...], (tt, dc))
    for s in range(8):
        yc = yc + q_s[s, s:s + tt, :]
    hc = _silu(_layer_norm(yc, clg_ref[...], clb_ref[...]))
    cv_ref[0, 0] = ubuf[tt + CONV_PAD - (CONV_W - 1):tt + CONV_PAD, :]
    ubuf[0:CONV_PAD, :] = ubuf[tt:tt + CONV_PAD, :]

    row = lax.broadcasted_iota(I32, (CHUNK, CHUNK), 0)
    col = lax.broadcasted_iota(I32, (CHUNK, CHUNK), 1)
    causal = col <= row
    tril = causal.astype(BF16)
    lane = lax.broadcasted_iota(I32, (CHUNK, LANES), 1)
    neg = jnp.float32(-jnp.inf)
    for c in range(tt // CHUNK):
        r0 = c * CHUNK
        gt = gates[r0:r0 + CHUNK, :]
        bsum = _cumsum_rows(tril, _log_sigmoid(gt))
        pk = jnp.where(lane < N_HEADS, gt - pltpu.roll(bsum, LANES - N_HEADS, 1), bsum)
        pk_t = pk.T
        for hd in range(N_HEADS):
            cs = slice(hd * dk, (hd + 1) * dk)
            qh = q_all[r0:r0 + CHUNK, cs]
            kf = k_all[r0:r0 + CHUNK, cs]
            vh = v_all[r0:r0 + CHUNK, cs]
            g_row = pk_t[hd:hd + 1, :]
            g_col = pk[:, hd:hd + 1]
            b_col = pk[:, N_HEADS + hd:N_HEADS + hd + 1]
            m_prev = m_s[hd:hd + 1, 0:1]
            c_prev = c_ref[0, 0, hd]
            n_prev = n_ref[0, 0, hd:hd + 1, :]
            gm = jnp.where(causal, g_row, neg)
            mt = jnp.maximum(m_prev, jnp.max(gm, axis=1, keepdims=True))
            w = jnp.exp(gm - mt)
            a_int = jnp.exp(m_prev - mt)
            s = _bdot_nt(qh, kf) * w
            num = _bdot(s, vh) + a_int * _bdot(qh, c_prev)
            den = (jnp.sum(s, axis=1, keepdims=True)
                   + a_int * jnp.sum(qh.astype(F32) * n_prev, axis=1, keepdims=True))
            hh = num / jnp.maximum(jnp.abs(den), jnp.exp(-(b_col + mt)))
            hn = _layer_norm(hh, mhg_ref[hd:hd + 1, :])
            hm_s[r0:r0 + CHUNK, cs] = hn * _sigmoid(o_all[r0:r0 + CHUNK, cs])
            mt_l = mt[CHUNK - 1:CHUNK, :]
            a_l = a_int[CHUNK - 1:CHUNK, :]
            kw = kf * jnp.exp(g_col - mt_l)
            c_ref[0, 0, hd] = a_l * c_prev + _bdot(kw.T, vh)
            n_ref[0, 0, hd:hd + 1, :] = a_l * n_prev + jnp.sum(kw, axis=0, keepdims=True)
            m_s[hd:hd + 1, :] = jnp.broadcast_to(b_col[CHUNK - 1:CHUNK, :] + mt_l, (1, LANES))

    lane1 = lax.broadcasted_iota(I32, (1, LANES), 1)
    m_row = jnp.zeros((1, LANES), F32)
    for hd in range(N_HEADS):
        m_row = jnp.where(lane1 == hd, m_s[hd:hd + 1, :], m_row)
    m_ref[0] = m_row

    cat = jnp.concatenate([hm_s[...], hc], axis=1)
    x1, h2, route = _post(x, cat, mod, g2_ref[...], wout_ref[...], wrh_ref[...], wrl_ref[...],
                          rbias_ref[...], d)
    x1_ref[0] = x1
    h2_ref[...] = h2
    route_ref[...] = route


def _const_spec(shape):
    nd = len(shape)
    return pl.BlockSpec(shape, lambda *_: (0,) * nd)


def _mix_prompt(x, mod, wts):
    b, t, d = x.shape
    dm = wts["dm"]
    dk = dm // N_HEADS
    dc = d - dm
    tt = min(MIX_TILE, t)
    assert t % tt == 0 and tt % CHUNK == 0 and tt >= CONV_PAD
    nt = t // tt
    kern = functools.partial(_mix_prompt_kernel, tt=tt, d=d, dm=dm, dk=dk)
    const_names = ["g1", "g2", "wmain", "wg_hi", "wg_lo", "gbias", "mhg", "wdw", "bdw", "clg",
                   "clb", "wout", "wr_hi", "wr_lo", "rbias"]
    consts = [wts[k] for k in const_names]
    in_specs = ([pl.BlockSpec((1, tt, d), lambda i, j: (i, j, 0)),
                 pl.BlockSpec((1, 1, mod.shape[-1]), lambda i, j: (i, 0, 0))]
                + [_const_spec(c.shape) for c in consts])
    out_shape = [
        jax.ShapeDtypeStruct((b, t, d), F32),
        jax.ShapeDtypeStruct((b * t, d), F32),
        jax.ShapeDtypeStruct((b * t, LANES), F32),
        jax.ShapeDtypeStruct((1, b, N_HEADS, dk, dk), F32),
        jax.ShapeDtypeStruct((1, b, N_HEADS, dk), F32),
        jax.ShapeDtypeStruct((b, 1, LANES), F32),
        jax.ShapeDtypeStruct((1, b, CONV_W - 1, dc), F32),
    ]
    out_specs = [
        pl.BlockSpec((1, tt, d), lambda i, j: (i, j, 0)),
        pl.BlockSpec((tt, d), lambda i, j: (i * nt + j, 0)),
        pl.BlockSpec((tt, LANES), lambda i, j: (i * nt + j, 0)),
        pl.BlockSpec((1, 1, N_HEADS, dk, dk), lambda i, j: (0, i, 0, 0, 0)),
        pl.BlockSpec((1, 1, N_HEADS, dk), lambda i, j: (0, i, 0, 0)),
        pl.BlockSpec((1, 1, LANES), lambda i, j: (i, 0, 0)),
        pl.BlockSpec((1, 1, CONV_W - 1, dc), lambda i, j: (0, i, 0, 0)),
    ]
    scratch = [pltpu.VMEM((tt + CONV_PAD + 8, dc), F32),
               pltpu.VMEM((8, tt + 8, dc), F32),
               pltpu.VMEM((tt, dm), F32),
               pltpu.VMEM((8, LANES), F32)]
    return pl.pallas_call(
        kern, grid=(b, nt), in_specs=in_specs, out_specs=out_specs, out_shape=out_shape,
        scratch_shapes=scratch,
        compiler_params=pltpu.CompilerParams(dimension_semantics=("arbitrary", "arbitrary"),
                                             vmem_limit_bytes=VMEM_LIMIT),
        name="mix_p",
    )(x, mod.reshape(b, 1, -1), *consts)


def _s_pre_kernel(x_ref, mod_ref, g1_ref, wmain_ref, wmainlo_ref, wgh_ref, wgl_ref, gbias_ref,
                  wkt_ref, wktlo_ref, wdw_ref, bdw_ref, clg_ref, clb_ref, cache_ref, n0_ref, m0_ref,
                  q_ref, kt_ref, vs_ref, ab_ref, sv_ref, den_ref, eb_ref, o_ref, hc_ref, u_ref,
                  n_ref, m_ref, *, d, dm, dk):
    dc = d - dm
    x = x_ref[...]
    mod = mod_ref[...]
    sh1 = mod[:, 0:d]
    sc1 = mod[:, d:2 * d]
    h = _rms(x, g1_ref[...]) * (1.0 + sc1) + sh1
    z = _dot3(h, wmain_ref[...], wmainlo_ref[...])
    gates = _dot3(h, wgh_ref[...], wgl_ref[...]) + gbias_ref[...]
    scale = dk ** -0.5
    h_hi, h_lo = _split2(h)
    kt = _bdot_nt(wkt_ref[...], h_hi) + _bdot_nt(wktlo_ref[...], h_hi) + _bdot_nt(wkt_ref[...], h_lo)
    kt_ref[...] = (kt * scale).astype(BF16)
    k_all = z[:, dm:2 * dm] * scale
    ga = z[:, 4 * dm:4 * dm + dc]
    gb = z[:, 4 * dm + dc:4 * dm + 2 * dc]
    u = ga * _sigmoid(gb)
    u_ref[...] = u
    acc = jnp.broadcast_to(bdw_ref[...], u.shape) + u * wdw_ref[CONV_W - 1:CONV_W, :]
    for j in range(CONV_W - 1):
        acc = acc + cache_ref[j] * wdw_ref[j:j + 1, :]
    hc_ref[...] = _silu(_layer_norm(acc, clg_ref[...], clb_ref[...]))
    o_ref[...] = z[:, 3 * dm:4 * dm]
    q_ref[...] = z[:, 0:dm]
    m0 = m0_ref[...]
    n0 = n0_ref[...]
    lane1 = lax.broadcasted_iota(I32, (1, LANES), 1)
    m_new = jnp.zeros(m0.shape, F32)
    for hd in range(N_HEADS):
        cs = slice(hd * dk, (hd + 1) * dk)
        ig = gates[:, hd:hd + 1]
        lf = _log_sigmoid(gates[:, N_HEADS + hd:N_HEADS + hd + 1])
        mp = m0[:, hd:hd + 1]
        inter = lf + mp
        mt = jnp.maximum(inter, ig)
        w = jnp.exp(ig - mt)
        a_int = jnp.exp(inter - mt)
        qf = z[:, cs]
        kf = k_all[:, cs]
        vf = z[:, 2 * dm + hd * dk:2 * dm + (hd + 1) * dk]
        s = jnp.sum(qf * kf, axis=1, keepdims=True) * w
        sv_ref[:, cs] = s * vf
        den_ref[:, cs] = jnp.broadcast_to(
            s + a_int * jnp.sum(qf * n0[:, cs], axis=1, keepdims=True), (x.shape[0], dk))
        eb_ref[:, cs] = jnp.broadcast_to(jnp.exp(-mt), (x.shape[0], dk))
        ab_ref[:, cs] = jnp.broadcast_to(a_int, (x.shape[0], dk))
        vs_ref[:, cs] = (vf * w).astype(BF16)
        n_ref[:, cs] = a_int * n0[:, cs] + w * kf
        m_new = jnp.where(lane1 == hd, mt, m_new)
    m_ref[...] = m_new


def _s_state_kernel(q_ref, kt_ref, vs_ref, ab_ref, c0_ref, c_ref, r_ref, *, bb, dk):
    i = pl.program_id(0)
    nb = q_ref.shape[0]
    rows = lax.broadcasted_iota(I32, (nb, dk), 0)

    @pl.when(i == 0)
    def _():
        r_ref[...] = jnp.zeros_like(r_ref)

    a_blk = ab_ref[pl.ds(pl.multiple_of(i * bb, bb), bb), :]
    for j in range(bb):
        sel = rows == i * bb + j
        for hd in range(N_HEADS):
            cs = slice(hd * dk, (hd + 1) * dk)
            c0 = c0_ref[j, hd]
            vmask = jnp.where(sel, vs_ref[:, cs], jnp.zeros((), BF16))
            c_ref[j, hd] = (a_blk[j:j + 1, cs] * c0
                            + jnp.dot(kt_ref[cs, :], vmask, preferred_element_type=F32))
            c_hi, c_lo = _split2(c0)
            q_hi, q_lo = _split2(q_ref[:, cs])
            r = (jnp.dot(q_hi, c_hi, preferred_element_type=F32)
                 + jnp.dot(q_lo, c_hi, preferred_element_type=F32)
                 + jnp.dot(q_hi, c_lo, preferred_element_type=F32))
            r_ref[:, cs] = r_ref[:, cs] + jnp.where(sel, r, 0.0)


def _s_post_kernel(x_ref, mod_ref, g2_ref, mhg_ref, r_ref, ab_ref, sv_ref, den_ref, eb_ref, o_ref,
                   hc_ref, wout_ref, woutlo_ref, wrh_ref, wrl_ref, rbias_ref,
                   x1_ref, h2_ref, route_ref, *, d, dm, dk):
    hm = []
    for hd in range(N_HEADS):
        cs = slice(hd * dk, (hd + 1) * dk)
        num = sv_ref[:, cs] + ab_ref[:, cs] * r_ref[:, cs]
        hh = num / jnp.maximum(jnp.abs(den_ref[:, cs]), eb_ref[:, cs])
        hm.append(_layer_norm(hh, mhg_ref[hd:hd + 1, :]) * _sigmoid(o_ref[:, cs]))
    cat = jnp.concatenate(hm + [hc_ref[...]], axis=1)
    x1, h2, route = _post(x_ref[...], cat, mod_ref[...], g2_ref[...], wout_ref[...], wrh_ref[...],
                          wrl_ref[...], rbias_ref[...], d, wout_lo=woutlo_ref[...])
    x1_ref[...] = x1
    h2_ref[...] = h2
    route_ref[...] = route


def _mix_sample(x, mod, c0, n0, m0, cache, wts):
    nb, d = x.shape
    dm = wts["dm"]
    dk = dm // N_HEADS
    dc = d - dm
    cp = pltpu.CompilerParams(dimension_semantics=("arbitrary",), vmem_limit_bytes=VMEM_LIMIT)
    cache_t = jnp.transpose(cache, (1, 0, 2))
    m0p = jnp.pad(m0, ((0, 0), (0, LANES - N_HEADS)))
    pre_in = [x, mod, wts["g1"], wts["wmain"], wts["wmain_lo"], wts["wg_hi"], wts["wg_lo"],
              wts["gbias"], wts["wk_t"], wts["wk_t_lo"], wts["wdw"], wts["bdw"], wts["clg"],
              wts["clb"], cache_t, n0.reshape(nb, dm), m0p]
    pre_out = [jax.ShapeDtypeStruct((nb, dm), F32),
               jax.ShapeDtypeStruct((dm, nb), BF16),
               jax.ShapeDtypeStruct((nb, dm), BF16),
               jax.ShapeDtypeStruct((nb, dm), F32),
               jax.ShapeDtypeStruct((nb, dm), F32),
               jax.ShapeDtypeStruct((nb, dm), F32),
               jax.ShapeDtypeStruct((nb, dm), F32),
               jax.ShapeDtypeStruct((nb, dm), F32),
               jax.ShapeDtypeStruct((nb, dc), F32),
               jax.ShapeDtypeStruct((nb, dc), F32),
               jax.ShapeDtypeStruct((nb, dm), F32),
               jax.ShapeDtypeStruct((nb, LANES), F32)]
    (q, kt, vs, ab, sv, den, eb, o, hc, u, n1, m1) = pl.pallas_call(
        functools.partial(_s_pre_kernel, d=d, dm=dm, dk=dk),
        grid=(1,),
        in_specs=[_const_spec(a.shape) for a in pre_in],
        out_specs=[_const_spec(s.shape) for s in pre_out],
        out_shape=pre_out, compiler_params=cp, name="s_pre")(*pre_in)

    bb = 8
    assert nb % bb == 0
    c1, r = pl.pallas_call(
        functools.partial(_s_state_kernel, bb=bb, dk=dk),
        grid=(nb // bb,),
        in_specs=[_const_spec(q.shape), _const_spec(kt.shape), _const_spec(vs.shape),
                  _const_spec(ab.shape),
                  pl.BlockSpec((bb, N_HEADS, dk, dk), lambda i: (i, 0, 0, 0))],
        out_specs=[pl.BlockSpec((bb, N_HEADS, dk, dk), lambda i: (i, 0, 0, 0)),
                   _const_spec((nb, dm))],
        out_shape=[jax.ShapeDtypeStruct((nb, N_HEADS, dk, dk), F32),
                   jax.ShapeDtypeStruct((nb, dm), F32)],
        compiler_params=cp, name="s_state")(q, kt, vs, ab, c0)

    post_in = [x, mod, wts["g2"], wts["mhg"], r, ab, sv, den, eb, o, hc, wts["wout"],
               wts["wout_lo"], wts["wr_hi"], wts["wr_lo"], wts["rbias"]]
    post_out = [jax.ShapeDtypeStruct((nb, d), F32),
                jax.ShapeDtypeStruct((nb, d), F32),
                jax.ShapeDtypeStruct((nb, LANES), F32)]
    x1, h2, route = pl.pallas_call(
        functools.partial(_s_post_kernel, d=d, dm=dm, dk=dk),
        grid=(1,),
        in_specs=[_const_spec(a.shape) for a in post_in],
        out_specs=[_const_spec(s.shape) for s in post_out],
        out_shape=post_out, compiler_params=cp, name="s_post")(*post_in)
    return x1, h2, route, c1, n1, m1, u


def _scatter_kernel(pos_ref, zoff_ref, h2p_ref, h2s_ref, xs_ref, zbuf, sem, zsem, *, tp, tm, np_tiles):
    i = pl.program_id(0)

    @pl.when(i == 0)
    def _():
        zbuf[...] = jnp.zeros_like(zbuf)

        def zero_copy(e):
            return pltpu.make_async_copy(
                zbuf, xs_ref.at[pl.ds(pl.multiple_of(zoff_ref[e], tm), tm)], zsem)

        for e in range(zoff_ref.shape[0]):
            @pl.when(zoff_ref[e] >= 0)
            def _():
                zero_copy(e).start()
        for e in range(zoff_ref.shape[0]):
            @pl.when(zoff_ref[e] >= 0)
            def _():
                zero_copy(e).wait()

    def scatter_tile(src_ref):
        def row_copy(r, slot):
            return pltpu.make_async_copy(src_ref.at[pl.ds(r, 1)],
                                         xs_ref.at[pl.ds(pos_ref[0, slot, r], 1)], sem)

        def start(r, carry):
            row_copy(r, 0).start()
            row_copy(r, 1).start()
            return carry

        def wait(r, carry):
            row_copy(r, 0).wait()
            row_copy(r, 1).wait()
            return carry

        lax.fori_loop(0, tp, start, 0)
        lax.fori_loop(0, tp, wait, 0)

    @pl.when(i < np_tiles)
    def _():
        scatter_tile(h2p_ref)

    @pl.when(i >= np_tiles)
    def _():
        scatter_tile(h2s_ref)


def _scatter_rows(h2_p, h2_s, pos, zoff, n_sorted, tm):
    (n_p, c), n_s = h2_p.shape, h2_s.shape[0]
    tp = PERM_TILE
    assert n_p % tp == 0 and n_s % tp == 0
    np_tiles, ns_tiles = n_p // tp, n_s // tp
    pos3 = pos.reshape(np_tiles + ns_tiles, tp, 2).transpose(0, 2, 1)
    return pl.pallas_call(
        functools.partial(_scatter_kernel, tp=tp, tm=tm, np_tiles=np_tiles),
        grid_spec=pltpu.PrefetchScalarGridSpec(
            num_scalar_prefetch=0,
            grid=(np_tiles + ns_tiles,),
            in_specs=[pl.BlockSpec((1, 2, tp), lambda i: (i, 0, 0), memory_space=pltpu.SMEM),
                      pl.BlockSpec(memory_space=pltpu.SMEM),
                      pl.BlockSpec((tp, c), lambda i: (jnp.minimum(i, np_tiles - 1), 0)),
                      pl.BlockSpec((tp, c), lambda i: (jnp.maximum(i - np_tiles, 0), 0))],
            out_specs=pl.BlockSpec(memory_space=pl.ANY),
            scratch_shapes=[pltpu.VMEM((tm, c), F32), pltpu.SemaphoreType.DMA(()),
                            pltpu.SemaphoreType.DMA(())]),
        out_shape=jax.ShapeDtypeStruct((n_sorted, c), F32),
        compiler_params=pltpu.CompilerParams(dimension_semantics=("arbitrary",),
                                             vmem_limit_bytes=VMEM_LIMIT),
        name="scatter",
    )(pos3, zoff, h2_p, h2_s)


def _moe_kernel(te_ref, nu_ref, xs_ref, wg_ref, wu_ref, wd_ref, ys_ref, wg_b, wu_b, wd_b):
    i = pl.program_id(0)
    used = i < nu_ref[0]

    @pl.when(used & ((i == 0) | (te_ref[i] != te_ref[jnp.maximum(i - 1, 0)])))
    def _():
        wg_b[...] = wg_ref[0].astype(BF16)
        wu_b[...] = wu_ref[0].astype(BF16)
        wd_b[...] = wd_ref[0].astype(BF16)

    @pl.when(used)
    def _():
        xb = xs_ref[...].astype(BF16)
        g = jnp.dot(xb, wg_b[...], preferred_element_type=F32)
        u = jnp.dot(xb, wu_b[...], preferred_element_type=F32)
        hid = (_silu(g) * u).astype(BF16)
        ys_ref[...] = jnp.dot(hid, wd_b[...], preferred_element_type=F32)

    @pl.when(jnp.logical_not(used))
    def _():
        ys_ref[...] = jnp.zeros_like(ys_ref)


def _moe(xs, tile_expert, n_used, w_gate, w_up, w_down, tm):
    p, c = xs.shape
    ne, d, de = w_gate.shape
    n_tiles = p // tm

    def x_map(i, te, nu):
        return (jnp.minimum(i, jnp.maximum(nu[0] - 1, 0)), 0)

    def w_map(i, te, nu):
        return (te[i], 0, 0)

    return pl.pallas_call(
        _moe_kernel,
        grid_spec=pltpu.PrefetchScalarGridSpec(
            num_scalar_prefetch=2,
            grid=(n_tiles,),
            in_specs=[pl.BlockSpec((tm, c), x_map),
                      pl.BlockSpec((1, d, de), w_map),
                      pl.BlockSpec((1, d, de), w_map),
                      pl.BlockSpec((1, de, d), w_map)],
            out_specs=pl.BlockSpec((tm, c), lambda i, te, nu: (i, 0)),
            scratch_shapes=[pltpu.VMEM((d, de), BF16), pltpu.VMEM((d, de), BF16),
                            pltpu.VMEM((de, d), BF16)]),
        out_shape=jax.ShapeDtypeStruct((p, c), F32),
        compiler_params=pltpu.CompilerParams(dimension_semantics=("arbitrary",),
                                             vmem_limit_bytes=VMEM_LIMIT),
        name="moe",
    )(tile_expert, n_used, xs, w_gate, w_up, w_down)


def _fin_kernel(pos_ref, x1_ref, route_ref, mod_ref, fg_ref, ys_ref, y_ref, ybuf, sem, *, tp, d):
    def row_copy(r, slot):
        return pltpu.make_async_copy(ys_ref.at[pl.ds(pos_ref[0, slot, r], 1)],
                                     ybuf.at[slot, pl.ds(r, 1)], sem)

    def start(r, carry):
        row_copy(r, 0).start()
        row_copy(r, 1).start()
        return carry

    def wait(r, carry):
        row_copy(r, 0).wait()
        row_copy(r, 1).wait()
        return carry

    lax.fori_loop(0, tp, start, 0)
    lax.fori_loop(0, tp, wait, 0)
    route = route_ref[...]
    moe = route[:, 2:3] * ybuf[0] + route[:, 3:4] * ybuf[1]
    gate2 = mod_ref[0][:, 5 * d:6 * d]
    y_ref[0] = _rms(x1_ref[0] + gate2 * moe, fg_ref[...])


def _finish(x1, route, mod, final_g, ys, pos, row0):
    b, t, d = x1.shape
    tp = min(PERM_TILE, t)
    assert t % tp == 0 and row0 % tp == 0
    nt = t // tp
    pos3 = pos.reshape(-1, tp, 2).transpose(0, 2, 1)
    blk0 = row0 // tp
    if mod.shape[1] == 1:
        mod_spec = pl.BlockSpec((1, 1, 6 * d), lambda i, j: (i, 0, 0))
    else:
        mod_spec = pl.BlockSpec((1, tp, 6 * d), lambda i, j: (i, j, 0))
    return pl.pallas_call(
        functools.partial(_fin_kernel, tp=tp, d=d),
        grid_spec=pltpu.PrefetchScalarGridSpec(
            num_scalar_prefetch=0,
            grid=(b, nt),
            in_specs=[pl.BlockSpec((1, 2, tp), lambda i, j: (blk0 + i * nt + j, 0, 0),
                                   memory_space=pltpu.SMEM),
                      pl.BlockSpec((1, tp, d), lambda i, j: (i, j, 0)),
                      pl.BlockSpec((tp, LANES), lambda i, j: (i * nt + j, 0)),
                      mod_spec,
                      _const_spec((1, d)),
                      pl.BlockSpec(memory_space=pl.ANY)],
            out_specs=pl.BlockSpec((1, tp, d), lambda i, j: (i, j, 0)),
            scratch_shapes=[pltpu.VMEM((2, tp, d), F32), pltpu.SemaphoreType.DMA(())]),
        out_shape=jax.ShapeDtypeStruct((b, t, d), F32),
        compiler_params=pltpu.CompilerParams(dimension_semantics=("arbitrary", "arbitrary"),
                                             vmem_limit_bytes=VMEM_LIMIT),
        name="fin",
    )(pos3, x1, route, mod, final_g.reshape(1, d), ys)


def _plan(expert_ids, tm, n_tiles):
    onehot = (expert_ids[:, None] == jnp.arange(N_EXPERTS, dtype=I32)[None, :]).astype(I32)
    csum = jnp.cumsum(onehot, axis=0)
    counts = csum[-1]
    rank = jnp.sum(csum * onehot, axis=1) - 1
    tiles_per = (counts + tm - 1) // tm
    tile_end = jnp.cumsum(tiles_per)
    seg_start = (tile_end - tiles_per) * tm
    pos = jnp.sum(onehot * seg_start[None, :], axis=1) + rank
    n_used = tile_end[-1]
    tile_ids = jnp.arange(n_tiles, dtype=I32)
    tile_expert = jnp.sum((tile_ids[:, None] >= tile_end[None, :]).astype(I32), axis=1)
    last_used = jnp.sum((n_used - 1 >= tile_end).astype(I32))
    tile_expert = jnp.where(tile_ids < n_used, tile_expert, last_used).astype(I32)
    z_expert = jnp.where((counts % tm) != 0, (tile_end - 1) * tm, -1)
    spare = n_used + jnp.arange(N_EXPERTS, dtype=I32)
    z_spare = jnp.where(spare < n_tiles, spare * tm, -1)
    zoff = jnp.concatenate([z_expert, z_spare]).astype(I32)
    return pos.astype(I32), tile_expert, n_used.reshape(1).astype(I32), zoff


def kernel(x_prompt, x_sample, c_prompt, c_sample, state_mlstm_C, state_mlstm_n, state_mlstm_m,
           cache_conv, w_ada, b_ada, norm1_g, w_in, b_igate, b_fgate, mh_norm_g, w_dw, b_dw,
           conv_ln_g, conv_ln_b, w_out, norm2_g, w_grp_router, b_grp_router, w_exp_router,
           b_exp_router, w_gate, w_up, w_down, final_g):
    depth = w_ada.shape[0]
    assert depth == 1, "one layer per step"
    bp, tp_, d = x_prompt.shape
    bs = x_sample.shape[0]
    assert x_sample.shape[1] == 1
    dk = state_mlstm_C.shape[-1]
    dm = N_HEADS * dk
    dc = d - dm
    l = 0

    win = w_in[l]
    w_qkvo = win[:, 0:4 * dm]
    w_gates = win[:, 4 * dm:4 * dm + 2 * N_HEADS]
    w_glu = win[:, 4 * dm + 2 * N_HEADS:]
    wmain, wmain_lo = _split2(jnp.concatenate([w_qkvo, w_glu], axis=1))
    wout_hi, wout_lo = _split2(w_out[l])
    wkt_hi, wkt_lo = _split2(win[:, dm:2 * dm].T)
    wg_pad = jnp.pad(w_gates, ((0, 0), (0, LANES - 2 * N_HEADS)))
    wg_hi, wg_lo = _split2(wg_pad)
    gbias = jnp.pad(jnp.concatenate([b_igate[l], b_fgate[l]]), (0, LANES - 2 * N_HEADS)).reshape(1, LANES)
    w_r = jnp.concatenate([w_grp_router[l], w_exp_router[l]], axis=1)
    n_r = N_GROUPS + N_EXPERTS
    wr_hi, wr_lo = _split2(jnp.pad(w_r, ((0, 0), (0, LANES - n_r))))
    rbias = jnp.pad(jnp.concatenate([b_grp_router[l], b_exp_router[l]]), (0, LANES - n_r)).reshape(1, LANES)
    wts = dict(dm=dm, g1=norm1_g[l].reshape(1, d), g2=norm2_g[l].reshape(1, d), wmain=wmain,
               wg_hi=wg_hi, wg_lo=wg_lo, gbias=gbias, mhg=mh_norm_g[l],
               wdw=w_dw[l].reshape(CONV_W, dc), bdw=b_dw[l].reshape(1, dc),
               clg=conv_ln_g[l].reshape(1, dc), clb=conv_ln_b[l].reshape(1, dc),
               wout=wout_hi, wr_hi=wr_hi, wr_lo=wr_lo, rbias=rbias, wk_t=wkt_hi,
               wmain_lo=wmain_lo, wout_lo=wout_lo, wk_t_lo=wkt_lo)

    mod = _ada(jnp.concatenate([c_prompt, c_sample], axis=0), w_ada[l], b_ada[l])
    mod_p, mod_s = mod[:bp], mod[bp:]

    n_p = bp * tp_
    n_all = n_p + bs
    x1_p, h2_p, route_p, c_p, n_pr, m_p, cv_p = _mix_prompt(x_prompt, mod_p, wts)
    x1_s, h2_s, route_s, c_s, n_s, m_s, u_s = _mix_sample(
        x_sample.reshape(bs, d), mod_s, state_mlstm_C[l], state_mlstm_n[l], state_mlstm_m[l],
        cache_conv[l], wts)

    route = jnp.concatenate([route_p, route_s], axis=0)
    expert_ids = route[:, 0:2].astype(I32).reshape(-1)
    tm = MOE_TILE
    n_tiles = (2 * n_all) // tm + N_EXPERTS
    pos, tile_expert, n_used, zoff = _plan(expert_ids, tm, n_tiles)
    pos = pos.reshape(n_all, 2)

    xs = _scatter_rows(h2_p, h2_s, pos, zoff, n_tiles * tm, tm)
    ys = _moe(xs, tile_expert, n_used, w_gate[l], w_up[l], w_down[l], tm)

    y_p = _finish(x1_p, route_p, mod_p.reshape(bp, 1, -1), final_g, ys, pos, 0)
    y_s = _finish(x1_s.reshape(1, bs, d), route_s, mod_s.reshape(1, bs, -1), final_g, ys, pos,
                  n_p).reshape(bs, 1, d)

    conv_s = jnp.concatenate([cache_conv[l][:, 1:, :], u_s[:, None, :]], axis=1)
    return (y_p, y_s,
            c_p, n_pr, m_p[:, 0, :N_HEADS].reshape(1, bp, N_HEADS), cv_p,
            c_s[None], n_s.reshape(1, bs, N_HEADS, dk), m_s[:, :N_HEADS].reshape(1, bs, N_HEADS),
            conv_s[None])
```

```python
import functools

import jax
import jax.numpy as jnp
from jax import lax
from jax.experimental import pallas as pl
from jax.experimental.pallas import tpu as pltpu

F32 = jnp.float32
BF16 = jnp.bfloat16
I32 = jnp.int32

EPS = 1e-6
LANES = 128
CHUNK = 128
N_HEADS = 4
N_GROUPS = 4
EXP_PER_GROUP = 8
N_EXPERTS = N_GROUPS * EXP_PER_GROUP
CONV_W = 31
CONV_PAD = 32
CONV_OFF = CONV_PAD - (CONV_W - 1)
MIX_TILE = 256
MOE_TILE = 256
PERM_TILE = 512
VMEM_LIMIT = 56 * 1024 * 1024


def _sigmoid(x):
    return 1.0 / (1.0 + jnp.exp(-x))


def _silu(x):
    return x * _sigmoid(x)


def _log_sigmoid(x):
    return jnp.minimum(x, 0.0) - jnp.log(1.0 + jnp.exp(-jnp.abs(x)))


def _bdot(a, b):
    return jnp.dot(a.astype(BF16), b.astype(BF16), preferred_element_type=F32)


def _bdot_nt(a, b):
    return lax.dot_general(a.astype(BF16), b.astype(BF16), (((1,), (1,)), ((), ())),
                           preferred_element_type=F32)


def _split2(x):
    hi = x.astype(BF16)
    lo = (x - hi.astype(F32)).astype(BF16)
    return hi, lo


def _dot3(a, w_hi, w_lo):
    a_hi, a_lo = _split2(a)
    return (jnp.dot(a_hi, w_hi, preferred_element_type=F32)
            + jnp.dot(a_lo, w_hi, preferred_element_type=F32)
            + jnp.dot(a_hi, w_lo, preferred_element_type=F32))


def _cumsum_rows(tril_bf16, x):
    hi = x.astype(BF16)
    r1 = x - hi.astype(F32)
    mid = r1.astype(BF16)
    lo = (r1 - mid.astype(F32)).astype(BF16)
    return (jnp.dot(tril_bf16, hi, preferred_element_type=F32)
            + jnp.dot(tril_bf16, mid, preferred_element_type=F32)
            + jnp.dot(tril_bf16, lo, preferred_element_type=F32))


def _rms(x, g):
    return x * lax.rsqrt(jnp.mean(x * x, axis=-1, keepdims=True) + EPS) * g


def _layer_norm(x, g, b=None):
    mu = jnp.mean(x, axis=-1, keepdims=True)
    xc = x - mu
    var = jnp.mean(xc * xc, axis=-1, keepdims=True)
    y = xc * lax.rsqrt(var + EPS) * g
    return y if b is None else y + b


def _route(logits):
    lane = lax.broadcasted_iota(I32, logits.shape, 1).astype(F32)
    neg = jnp.float32(-jnp.inf)
    big = jnp.float32(1e9)
    is_g = lane < N_GROUPS
    gl = jnp.where(is_g, logits, neg)
    gmax = jnp.max(gl, axis=1, keepdims=True)
    gsel = jnp.min(jnp.where(gl == gmax, lane, big), axis=1, keepdims=True)
    pg = 1.0 / jnp.sum(jnp.where(is_g, jnp.exp(gl - gmax), 0.0), axis=1, keepdims=True)
    lo = N_GROUPS + EXP_PER_GROUP * gsel
    emask = (lane >= lo) & (lane < lo + EXP_PER_GROUP)
    el = jnp.where(emask, logits, neg)
    v1 = jnp.max(el, axis=1, keepdims=True)
    i1 = jnp.min(jnp.where(el == v1, lane, big), axis=1, keepdims=True)
    el2 = jnp.where(lane == i1, neg, el)
    v2 = jnp.max(el2, axis=1, keepdims=True)
    i2 = jnp.min(jnp.where(el2 == v2, lane, big), axis=1, keepdims=True)
    d = jnp.exp(v2 - v1)
    w1 = pg / (1.0 + d)
    w2 = pg * d / (1.0 + d)
    return jnp.where(lane == 0, i1 - N_GROUPS,
                     jnp.where(lane == 1, i2 - N_GROUPS,
                               jnp.where(lane == 2, w1, jnp.where(lane == 3, w2, 0.0))))


def _ada_kernel(c_ref, w_ref, b_ref, o_ref):
    w_hi, w_lo = _split2(w_ref[...])
    o_ref[...] = _dot3(_silu(c_ref[...]), w_hi, w_lo) + b_ref[...]


def _ada(c_all, w_ada, b_ada):
    rows, d = c_all.shape
    n_out = w_ada.shape[1]
    blk = 1024
    return pl.pallas_call(
        _ada_kernel,
        grid=(n_out // blk,),
        in_specs=[pl.BlockSpec((rows, d), lambda j: (0, 0)),
                  pl.BlockSpec((d, blk), lambda j: (0, j)),
                  pl.BlockSpec((1, blk), lambda j: (0, j))],
        out_specs=pl.BlockSpec((rows, blk), lambda j: (0, j)),
        out_shape=jax.ShapeDtypeStruct((rows, n_out), F32),
        compiler_params=pltpu.CompilerParams(dimension_semantics=("arbitrary",),
                                             vmem_limit_bytes=VMEM_LIMIT),
        name="ada",
    )(c_all, w_ada, b_ada.reshape(1, n_out))


def _in_proj(x, mod, g1, wmain, wg_hi, wg_lo, gbias, d):
    sh1 = mod[:, 0:d]
    sc1 = mod[:, d:2 * d]
    h = _rms(x, g1) * (1.0 + sc1) + sh1
    z = jnp.dot(h.astype(BF16), wmain, preferred_element_type=F32)
    gates = _dot3(h, wg_hi, wg_lo) + gbias
    return z, gates


def _post(x, attn_cat, mod, g2, wout, wr_hi, wr_lo, rbias, d, wout_lo=None):
    gate1 = mod[:, 2 * d:3 * d]
    sh2 = mod[:, 3 * d:4 * d]
    sc2 = mod[:, 4 * d:5 * d]
    if wout_lo is None:
        proj = jnp.dot(attn_cat.astype(BF16), wout, preferred_element_type=F32)
    else:
        proj = _dot3(attn_cat, wout, wout_lo)
    x1 = x + gate1 * proj
    h2 = _rms(x1, g2) * (1.0 + sc2) + sh2
    logits = _dot3(h2, wr_hi, wr_lo) + rbias
    return x1, h2, _route(logits)


def _mix_prompt_kernel(x_ref, mod_ref, g1_ref, g2_ref, wmain_ref, wgh_ref, wgl_ref, gbias_ref,
                       mhg_ref, wdw_ref, bdw_ref, clg_ref, clb_ref, wout_ref, wrh_ref, wrl_ref,
                       rbias_ref,
                       x1_ref, h2_ref, route_ref, c_ref, n_ref, m_ref, cv_ref,
                       ubuf, q_s, hm_s, m_s, *, tt, d, dm, dk):
    t = pl.program_id(1)
    dc = d - dm

    @pl.when(t == 0)
    def _():
        c_ref[...] = jnp.zeros_like(c_ref)
        n_ref[...] = jnp.zeros_like(n_ref)
        m_s[...] = jnp.zeros_like(m_s)
        ubuf[0:CONV_PAD, :] = jnp.zeros((CONV_PAD, dc), F32)
        ubuf[tt + CONV_PAD:tt + CONV_PAD + 8, :] = jnp.zeros((8, dc), F32)

    x = x_ref[0]
    mod = mod_ref[0]
    z, gates = _in_proj(x, mod, g1_ref[...], wmain_ref[...], wgh_ref[...], wgl_ref[...],
                        gbias_ref[...], d)
    q_all = z[:, 0:dm].astype(BF16)
    k_all = z[:, dm:2 * dm] * (dk ** -0.5)
    v_all = z[:, 2 * dm:3 * dm].astype(BF16)
    o_all = z[:, 3 * dm:4 * dm]
    ga = z[:, 4 * dm:4 * dm + dc]
    gb = z[:, 4 * dm + dc:4 * dm + 2 * dc]

    ubuf[CONV_PAD:CONV_PAD + tt, :] = ga * _sigmoid(gb)

    def conv_rows(i, carry):
        r0 = pl.multiple_of(i * 8, 8)
        blocks = [ubuf[pl.ds(pl.multiple_of(r0 + 8 * a, 8), 8), :] for a in range(CONV_PAD // 8 + 1)]
        for s in range(8):
            acc = None
            for a in range(CONV_PAD // 8 + 1):
                j = 8 * a + s - CONV_OFF
                if 0 <= j < CONV_W:
                    term = blocks[a] * wdw_ref[j:j + 1, :]
                    acc = term if acc is None else acc + term
            q_s[s, pl.ds(r0, 8), :] = acc
        return carry

    lax.fori_loop(0, (tt + 8) // 8, conv_rows, 0)
    yc = jnp.broadcast_to(bdw_ref[...], (tt, dc))
    for s in range(8):
        yc = yc + q_s[s, s:s + tt, :]
    hc = _silu(_layer_norm(yc, clg_ref[...], clb_ref[...]))
    cv_ref[0, 0] = ubuf[tt + CONV_PAD - (CONV_W - 1):tt + CONV_PAD, :]
    ubuf[0:CONV_PAD, :] = ubuf[tt:tt + CONV_PAD, :]

    row = lax.broadcasted_iota(I32, (CHUNK, CHUNK), 0)
    col = lax.broadcasted_iota(I32, (CHUNK, CHUNK), 1)
    causal = col <= row
    tril = causal.astype(BF16)
    lane = lax.broadcasted_iota(I32, (CHUNK, LANES), 1)
    neg = jnp.float32(-jnp.inf)
    for c in range(tt // CHUNK):
        r0 = c * CHUNK
        gt = gates[r0:r0 + CHUNK, :]
        bsum = _cumsum_rows(tril, _log_sigmoid(gt))
        pk = jnp.where(lane < N_HEADS, gt - pltpu.roll(bsum, LANES - N_HEADS, 1), bsum)
        pk_t = pk.T
        for hd in range(N_HEADS):
            cs = slice(hd * dk, (hd + 1) * dk)
            qh = q_all[r0:r0 + CHUNK, cs]
            kf = k_all[r0:r0 + CHUNK, cs]
            vh = v_all[r0:r0 + CHUNK, cs]
            g_row = pk_t[hd:hd + 1, :]
            g_col = pk[:, hd:hd + 1]
            b_col = pk[:, N_HEADS + hd:N_HEADS + hd + 1]
            m_prev = m_s[hd:hd + 1, 0:1]
            c_prev = c_ref[0, 0, hd]
            n_prev = n_ref[0, 0, hd:hd + 1, :]
            gm = jnp.where(causal, g_row, neg)
            mt = jnp.maximum(m_prev, jnp.max(gm, axis=1, keepdims=True))
            w = jnp.exp(gm - mt)
            a_int = jnp.exp(m_prev - mt)
            s = _bdot_nt(qh, kf) * w
            num = _bdot(s, vh) + a_int * _bdot(qh, c_prev)
            den = (jnp.sum(s, axis=1, keepdims=True)
                   + a_int * jnp.sum(qh.astype(F32) * n_prev, axis=1, keepdims=True))
            hh = num / jnp.maximum(jnp.abs(den), jnp.exp(-(b_col + mt)))
            hn = _layer_norm(hh, mhg_ref[hd:hd + 1, :])
            hm_s[r0:r0 + CHUNK, cs] = hn * _sigmoid(o_all[r0:r0 + CHUNK, cs])
            mt_l = mt[CHUNK - 1:CHUNK, :]
            a_l = a_int[CHUNK - 1:CHUNK, :]
            kw = kf * jnp.exp(g_col - mt_l)
            c_ref[0, 0, hd] = a_l * c_prev + _bdot(kw.T, vh)
            n_ref[0, 0, hd:hd + 1, :] = a_l * n_prev + jnp.sum(kw, axis=0, keepdims=True)
            m_s[hd:hd + 1, :] = jnp.broadcast_to(b_col[CHUNK - 1:CHUNK, :] + mt_l, (1, LANES))

    lane1 = lax.broadcasted_iota(I32, (1, LANES), 1)
    m_row = jnp.zeros((1, LANES), F32)
    for hd in range(N_HEADS):
        m_row = jnp.where(lane1 == hd, m_s[hd:hd + 1, :], m_row)
    m_ref[0] = m_row

    cat = jnp.concatenate([hm_s[...], hc], axis=1)
    x1, h2, route = _post(x, cat, mod, g2_ref[...], wout_ref[...], wrh_ref[...], wrl_ref[...],
                          rbias_ref[...], d)
    x1_ref[0] = x1
    h2_ref[...] = h2
    route_ref[...] = route


def _const_spec(shape):
    nd = len(shape)
    return pl.BlockSpec(shape, lambda *_: (0,) * nd)


def _mix_prompt(x, mod, wts):
    b, t, d = x.shape
    dm = wts["dm"]
    dk = dm // N_HEADS
    dc = d - dm
    tt = min(MIX_TILE, t)
    assert t % tt == 0 and tt % CHUNK == 0 and tt >= CONV_PAD
    nt = t // tt
    kern = functools.partial(_mix_prompt_kernel, tt=tt, d=d, dm=dm, dk=dk)
    const_names = ["g1", "g2", "wmain", "wg_hi", "wg_lo", "gbias", "mhg", "wdw", "bdw", "clg",
                   "clb", "wout", "wr_hi", "wr_lo", "rbias"]
    consts = [wts[k] for k in const_names]
    in_specs = ([pl.BlockSpec((1, tt, d), lambda i, j: (i, j, 0)),
                 pl.BlockSpec((1, 1, mod.shape[-1]), lambda i, j: (i, 0, 0))]
                + [_const_spec(c.shape) for c in consts])
    out_shape = [
        jax.ShapeDtypeStruct((b, t, d), F32),
        jax.ShapeDtypeStruct((b * t, d), F32),
        jax.ShapeDtypeStruct((b * t, LANES), F32),
        jax.ShapeDtypeStruct((1, b, N_HEADS, dk, dk), F32),
        jax.ShapeDtypeStruct((1, b, N_HEADS, dk), F32),
        jax.ShapeDtypeStruct((b, 1, LANES), F32),
        jax.ShapeDtypeStruct((1, b, CONV_W - 1, dc), F32),
    ]
    out_specs = [
        pl.BlockSpec((1, tt, d), lambda i, j: (i, j, 0)),
        pl.BlockSpec((tt, d), lambda i, j: (i * nt + j, 0)),
        pl.BlockSpec((tt, LANES), lambda i, j: (i * nt + j, 0)),
        pl.BlockSpec((1, 1, N_HEADS, dk, dk), lambda i, j: (0, i, 0, 0, 0)),
        pl.BlockSpec((1, 1, N_HEADS, dk), lambda i, j: (0, i, 0, 0)),
        pl.BlockSpec((1, 1, LANES), lambda i, j: (i, 0, 0)),
        pl.BlockSpec((1, 1, CONV_W - 1, dc), lambda i, j: (0, i, 0, 0)),
    ]
    scratch = [pltpu.VMEM((tt + CONV_PAD + 8, dc), F32),
               pltpu.VMEM((8, tt + 8, dc), F32),
               pltpu.VMEM((tt, dm), F32),
               pltpu.VMEM((8, LANES), F32)]
    return pl.pallas_call(
        kern, grid=(b, nt), in_specs=in_specs, out_specs=out_specs, out_shape=out_shape,
        scratch_shapes=scratch,
        compiler_params=pltpu.CompilerParams(dimension_semantics=("arbitrary", "arbitrary"),
                                             vmem_limit_bytes=VMEM_LIMIT),
        name="mix_p",
    )(x, mod.reshape(b, 1, -1), *consts)


def _s_pre_kernel(x_ref, mod_ref, g1_ref, wmain_ref, wmainlo_ref, wgh_ref, wgl_ref, gbias_ref,
                  wkt_ref, wktlo_ref, wdw_ref, bdw_ref, clg_ref, clb_ref, cache_ref, n0_ref, m0_ref,
                  q_ref, kt_ref, vs_ref, ab_ref, sv_ref, den_ref, eb_ref, o_ref, hc_ref, u_ref,
                  n_ref, m_ref, *, d, dm, dk):
    dc = d - dm
    x = x_ref[...]
    mod = mod_ref[...]
    sh1 = mod[:, 0:d]
    sc1 = mod[:, d:2 * d]
    h = _rms(x, g1_ref[...]) * (1.0 + sc1) + sh1
    z = _dot3(h, wmain_ref[...], wmainlo_ref[...])
    gates = _dot3(h, wgh_ref[...], wgl_ref[...]) + gbias_ref[...]
    scale = dk ** -0.5
    h_hi, h_lo = _split2(h)
    kt = _bdot_nt(wkt_ref[...], h_hi) + _bdot_nt(wktlo_ref[...], h_hi) + _bdot_nt(wkt_ref[...], h_lo)
    kt_ref[...] = (kt * scale).astype(BF16)
    k_all = z[:, dm:2 * dm] * scale
    ga = z[:, 4 * dm:4 * dm + dc]
    gb = z[:, 4 * dm + dc:4 * dm + 2 * dc]
    u = ga * _sigmoid(gb)
    u_ref[...] = u
    acc = jnp.broadcast_to(bdw_ref[...], u.shape) + u * wdw_ref[CONV_W - 1:CONV_W, :]
    for j in range(CONV_W - 1):
        acc = acc + cache_ref[j] * wdw_ref[j:j + 1, :]
    hc_ref[...] = _silu(_layer_norm(acc, clg_ref[...], clb_ref[...]))
    o_ref[...] = z[:, 3 * dm:4 * dm]
    q_ref[...] = z[:, 0:dm]
    m0 = m0_ref[...]
    n0 = n0_ref[...]
    lane1 = lax.broadcasted_iota(I32, (1, LANES), 1)
    m_new = jnp.zeros(m0.shape, F32)
    for hd in range(N_HEADS):
        cs = slice(hd * dk, (hd + 1) * dk)
        ig = gates[:, hd:hd + 1]
        lf = _log_sigmoid(gates[:, N_HEADS + hd:N_HEADS + hd + 1])
        mp = m0[:, hd:hd + 1]
        inter = lf + mp
        mt = jnp.maximum(inter, ig)
        w = jnp.exp(ig - mt)
        a_int = jnp.exp(inter - mt)
        qf = z[:, cs]
        kf = k_all[:, cs]
        vf = z[:, 2 * dm + hd * dk:2 * dm + (hd + 1) * dk]
        s = jnp.sum(qf * kf, axis=1, keepdims=True) * w
        sv_ref[:, cs] = s * vf
        den_ref[:, cs] = jnp.broadcast_to(
            s + a_int * jnp.sum(qf * n0[:, cs], axis=1, keepdims=True), (x.shape[0], dk))
        eb_ref[:, cs] = jnp.broadcast_to(jnp.exp(-mt), (x.shape[0], dk))
        ab_ref[:, cs] = jnp.broadcast_to(a_int, (x.shape[0], dk))
        vs_ref[:, cs] = (vf * w).astype(BF16)
        n_ref[:, cs] = a_int * n0[:, cs] + w * kf
        m_new = jnp.where(lane1 == hd, mt, m_new)
    m_ref[...] = m_new


def _s_state_kernel(q_ref, kt_ref, vs_ref, ab_ref, c0_ref, c_ref, r_ref, *, bb, dk):
    i = pl.program_id(0)
    nb = q_ref.shape[0]
    rows = lax.broadcasted_iota(I32, (nb, dk), 0)

    @pl.when(i == 0)
    def _():
        r_ref[...] = jnp.zeros_like(r_ref)

    a_blk = ab_ref[pl.ds(pl.multiple_of(i * bb, bb), bb), :]
    for j in range(bb):
        sel = rows == i * bb + j
        for hd in range(N_HEADS):
            cs = slice(hd * dk, (hd + 1) * dk)
            c0 = c0_ref[j, hd]
            vmask = jnp.where(sel, vs_ref[:, cs], jnp.zeros((), BF16))
            c_ref[j, hd] = (a_blk[j:j + 1, cs] * c0
                            + jnp.dot(kt_ref[cs, :], vmask, preferred_element_type=F32))
            c_hi, c_lo = _split2(c0)
            q_hi, q_lo = _split2(q_ref[:, cs])
            r = (jnp.dot(q_hi, c_hi, preferred_element_type=F32)
                 + jnp.dot(q_lo, c_hi, preferred_element_type=F32)
                 + jnp.dot(q_hi, c_lo, preferred_element_type=F32))
            r_ref[:, cs] = r_ref[:, cs] + jnp.where(sel, r, 0.0)


def _s_post_kernel(x_ref, mod_ref, g2_ref, mhg_ref, r_ref, ab_ref, sv_ref, den_ref, eb_ref, o_ref,
                   hc_ref, wout_ref, woutlo_ref, wrh_ref, wrl_ref, rbias_ref,
                   x1_ref, h2_ref, route_ref, *, d, dm, dk):
    hm = []
    for hd in range(N_HEADS):
        cs = slice(hd * dk, (hd + 1) * dk)
        num = sv_ref[:, cs] + ab_ref[:, cs] * r_ref[:, cs]
        hh = num / jnp.maximum(jnp.abs(den_ref[:, cs]), eb_ref[:, cs])
        hm.append(_layer_norm(hh, mhg_ref[hd:hd + 1, :]) * _sigmoid(o_ref[:, cs]))
    cat = jnp.concatenate(hm + [hc_ref[...]], axis=1)
    x1, h2, route = _post(x_ref[...], cat, mod_ref[...], g2_ref[...], wout_ref[...], wrh_ref[...],
                          wrl_ref[...], rbias_ref[...], d, wout_lo=woutlo_ref[...])
    x1_ref[...] = x1
    h2_ref[...] = h2
    route_ref[...] = route


def _mix_sample(x, mod, c0, n0, m0, cache, wts):
    nb, d = x.shape
    dm = wts["dm"]
    dk = dm // N_HEADS
    dc = d - dm
    cp = pltpu.CompilerParams(dimension_semantics=("arbitrary",), vmem_limit_bytes=VMEM_LIMIT)
    cache_t = jnp.transpose(cache, (1, 0, 2))
    m0p = jnp.pad(m0, ((0, 0), (0, LANES - N_HEADS)))
    pre_in = [x, mod, wts["g1"], wts["wmain"], wts["wmain_lo"], wts["wg_hi"], wts["wg_lo"],
              wts["gbias"], wts["wk_t"], wts["wk_t_lo"], wts["wdw"], wts["bdw"], wts["clg"],
              wts["clb"], cache_t, n0.reshape(nb, dm), m0p]
    pre_out = [jax.ShapeDtypeStruct((nb, dm), F32),
               jax.ShapeDtypeStruct((dm, nb), BF16),
               jax.ShapeDtypeStruct((nb, dm), BF16),
               jax.ShapeDtypeStruct((nb, dm), F32),
               jax.ShapeDtypeStruct((nb, dm), F32),
               jax.ShapeDtypeStruct((nb, dm), F32),
               jax.ShapeDtypeStruct((nb, dm), F32),
               jax.ShapeDtypeStruct((nb, dm), F32),
               jax.ShapeDtypeStruct((nb, dc), F32),
               jax.ShapeDtypeStruct((nb, dc), F32),
               jax.ShapeDtypeStruct((nb, dm), F32),
               jax.ShapeDtypeStruct((nb, LANES), F32)]
    (q, kt, vs, ab, sv, den, eb, o, hc, u, n1, m1) = pl.pallas_call(
        functools.partial(_s_pre_kernel, d=d, dm=dm, dk=dk),
        grid=(1,),
        in_specs=[_const_spec(a.shape) for a in pre_in],
        out_specs=[_const_spec(s.shape) for s in pre_out],
        out_shape=pre_out, compiler_params=cp, name="s_pre")(*pre_in)

    bb = 8
    assert nb % bb == 0
    c1, r = pl.pallas_call(
        functools.partial(_s_state_kernel, bb=bb, dk=dk),
        grid=(nb // bb,),
        in_specs=[_const_spec(q.shape), _const_spec(kt.shape), _const_spec(vs.shape),
                  _const_spec(ab.shape),
                  pl.BlockSpec((bb, N_HEADS, dk, dk), lambda i: (i, 0, 0, 0))],
        out_specs=[pl.BlockSpec((bb, N_HEADS, dk, dk), lambda i: (i, 0, 0, 0)),
                   _const_spec((nb, dm))],
        out_shape=[jax.ShapeDtypeStruct((nb, N_HEADS, dk, dk), F32),
                   jax.ShapeDtypeStruct((nb, dm), F32)],
        compiler_params=cp, name="s_state")(q, kt, vs, ab, c0)

    post_in = [x, mod, wts["g2"], wts["mhg"], r, ab, sv, den, eb, o, hc, wts["wout"],
               wts["wout_lo"], wts["wr_hi"], wts["wr_lo"], wts["rbias"]]
    post_out = [jax.ShapeDtypeStruct((nb, d), F32),
                jax.ShapeDtypeStruct((nb, d), F32),
                jax.ShapeDtypeStruct((nb, LANES), F32)]
    x1, h2, route = pl.pallas_call(
        functools.partial(_s_post_kernel, d=d, dm=dm, dk=dk),
        grid=(1,),
        in_specs=[_const_spec(a.shape) for a in post_in],
        out_specs=[_const_spec(s.shape) for s in post_out],
        out_shape=post_out, compiler_params=cp, name="s_post")(*post_in)
    return x1, h2, route, c1, n1, m1, u


def _scatter_kernel(pos_ref, zoff_ref, src_ref, *rest, tp, tm, create):
    xs_ref, zbuf, sem, zsem = rest[-4:]
    i = pl.program_id(0)

    if create:
        @pl.when(i == 0)
        def _():
            zbuf[...] = jnp.zeros_like(zbuf)

            def zero_copy(e):
                return pltpu.make_async_copy(
                    zbuf, xs_ref.at[pl.ds(pl.multiple_of(zoff_ref[e], tm), tm)], zsem)

            for e in range(zoff_ref.shape[0]):
                @pl.when(zoff_ref[e] >= 0)
                def _():
                    zero_copy(e).start()
            for e in range(zoff_ref.shape[0]):
                @pl.when(zoff_ref[e] >= 0)
                def _():
                    zero_copy(e).wait()

    def row_start(r, carry):
        for slot in range(2):
            pltpu.make_async_copy(src_ref.at[pl.ds(r, 1)],
                                  xs_ref.at[pl.ds(pos_ref[0, slot, r], 1)], sem).start()
        return carry

    lax.fori_loop(0, tp, row_start, 0, unroll=8)
    for slot in range(2):
        pltpu.make_async_copy(src_ref, xs_ref.at[pl.ds(0, tp)], sem).wait()


def _scatter_rows(h2, pos, zoff, xs_or_rows, tm):
    n, c = h2.shape
    tp = min(PERM_TILE, n)
    assert n % tp == 0
    pos3 = pos.reshape(n // tp, tp, 2).transpose(0, 2, 1)
    create = isinstance(xs_or_rows, int)
    n_sorted = xs_or_rows if create else xs_or_rows.shape[0]
    in_specs = [pl.BlockSpec((1, 2, tp), lambda i: (i, 0, 0), memory_space=pltpu.SMEM),
                pl.BlockSpec(memory_space=pltpu.SMEM),
                pl.BlockSpec((tp, c), lambda i: (i, 0))]
    args = [pos3, zoff, h2]
    if not create:
        in_specs.append(pl.BlockSpec(memory_space=pl.ANY))
        args.append(xs_or_rows)
    return pl.pallas_call(
        functools.partial(_scatter_kernel, tp=tp, tm=tm, create=create),
        grid_spec=pltpu.PrefetchScalarGridSpec(
            num_scalar_prefetch=0,
            grid=(n // tp,),
            in_specs=in_specs,
            out_specs=pl.BlockSpec(memory_space=pl.ANY),
            scratch_shapes=[pltpu.VMEM((tm, c), F32), pltpu.SemaphoreType.DMA(()),
                            pltpu.SemaphoreType.DMA(())]),
        out_shape=jax.ShapeDtypeStruct((n_sorted, c), F32),
        input_output_aliases={} if create else {3: 0},
        compiler_params=pltpu.CompilerParams(dimension_semantics=("arbitrary",),
                                             vmem_limit_bytes=VMEM_LIMIT),
        name="scatter",
    )(*args)


def _moe_kernel(te_ref, nu_ref, xs_ref, wg_ref, wu_ref, wd_ref, ys_ref, wg_b, wu_b, wd_b):
    i = pl.program_id(0)
    used = i < nu_ref[0]

    @pl.when(used & ((i == 0) | (te_ref[i] != te_ref[jnp.maximum(i - 1, 0)])))
    def _():
        wg_b[...] = wg_ref[0].astype(BF16)
        wu_b[...] = wu_ref[0].astype(BF16)
        wd_b[...] = wd_ref[0].astype(BF16)

    @pl.when(used)
    def _():
        xb = xs_ref[...].astype(BF16)
        g = jnp.dot(xb, wg_b[...], preferred_element_type=F32)
        u = jnp.dot(xb, wu_b[...], preferred_element_type=F32)
        hid = (_silu(g) * u).astype(BF16)
        ys_ref[...] = jnp.dot(hid, wd_b[...], preferred_element_type=F32)

    @pl.when(jnp.logical_not(used))
    def _():
        ys_ref[...] = jnp.zeros_like(ys_ref)


def _moe(xs, tile_expert, n_used, w_gate, w_up, w_down, tm):
    p, c = xs.shape
    ne, d, de = w_gate.shape
    n_tiles = p // tm

    def x_map(i, te, nu):
        return (jnp.minimum(i, jnp.maximum(nu[0] - 1, 0)), 0)

    def w_map(i, te, nu):
        return (te[i], 0, 0)

    return pl.pallas_call(
        _moe_kernel,
        grid_spec=pltpu.PrefetchScalarGridSpec(
            num_scalar_prefetch=2,
            grid=(n_tiles,),
            in_specs=[pl.BlockSpec((tm, c), x_map),
                      pl.BlockSpec((1, d, de), w_map),
                      pl.BlockSpec((1, d, de), w_map),
                      pl.BlockSpec((1, de, d), w_map)],
            out_specs=pl.BlockSpec((tm, c), lambda i, te, nu: (i, 0)),
            scratch_shapes=[pltpu.VMEM((d, de), BF16), pltpu.VMEM((d, de), BF16),
                            pltpu.VMEM((de, d), BF16)]),
        out_shape=jax.ShapeDtypeStruct((p, c), F32),
        compiler_params=pltpu.CompilerParams(dimension_semantics=("arbitrary",),
                                             vmem_limit_bytes=VMEM_LIMIT),
        name="moe",
    )(tile_expert, n_used, xs, w_gate, w_up, w_down)


def _fin_kernel(pos_ref, posn_ref, x1_ref, route_ref, mod_ref, fg_ref, ys_ref, y_ref, ybuf, sem,
                *, tp, d, n_steps):
    i = pl.program_id(0)
    cur = i % 2

    def issue(p_ref, buf):
        def row_start(r, carry):
            for slot in range(2):
                pltpu.make_async_copy(ys_ref.at[pl.ds(p_ref[0, slot, r], 1)],
                                      ybuf.at[buf, slot, pl.ds(r, 1)], sem.at[buf]).start()
            return carry
        lax.fori_loop(0, tp, row_start, 0, unroll=8)

    @pl.when(i == 0)
    def _():
        issue(pos_ref, 0)

    @pl.when(i + 1 < n_steps)
    def _():
        issue(posn_ref, 1 - cur)

    for slot in range(2):
        pltpu.make_async_copy(ys_ref.at[pl.ds(0, tp)], ybuf.at[cur, slot], sem.at[cur]).wait()
    route = route_ref[...]
    moe = route[:, 2:3] * ybuf[cur, 0] + route[:, 3:4] * ybuf[cur, 1]
    gate2 = mod_ref[0][:, 5 * d:6 * d]
    y_ref[0] = _rms(x1_ref[0] + gate2 * moe, fg_ref[...])


def _finish(x1, route, mod, final_g, ys, pos):
    b, t, d = x1.shape
    tp = min(PERM_TILE, t)
    assert t % tp == 0
    nt = t // tp
    n_steps = b * nt
    pos3 = pos.reshape(n_steps, tp, 2).transpose(0, 2, 1)
    blk0 = 0
    if mod.shape[1] == 1:
        mod_spec = pl.BlockSpec((1, 1, 6 * d), lambda i: (i // nt, 0, 0))
    else:
        mod_spec = pl.BlockSpec((1, tp, 6 * d), lambda i: (i // nt, i % nt, 0))
    return pl.pallas_call(
        functools.partial(_fin_kernel, tp=tp, d=d, n_steps=n_steps),
        grid_spec=pltpu.PrefetchScalarGridSpec(
            num_scalar_prefetch=0,
            grid=(n_steps,),
            in_specs=[pl.BlockSpec((1, 2, tp), lambda i: (blk0 + i, 0, 0), memory_space=pltpu.SMEM),
                      pl.BlockSpec((1, 2, tp), lambda i: (blk0 + jnp.minimum(i + 1, n_steps - 1), 0, 0),
                                   memory_space=pltpu.SMEM),
                      pl.BlockSpec((1, tp, d), lambda i: (i // nt, i % nt, 0)),
                      pl.BlockSpec((tp, LANES), lambda i: (i, 0)),
                      mod_spec,
                      _const_spec((1, d)),
                      pl.BlockSpec(memory_space=pl.ANY)],
            out_specs=pl.BlockSpec((1, tp, d), lambda i: (i // nt, i % nt, 0)),
            scratch_shapes=[pltpu.VMEM((2, 2, tp, d), F32), pltpu.SemaphoreType.DMA((2,))]),
        out_shape=jax.ShapeDtypeStruct((b, t, d), F32),
        compiler_params=pltpu.CompilerParams(dimension_semantics=("arbitrary",),
                                             vmem_limit_bytes=VMEM_LIMIT),
        name="fin",
    )(pos3, pos3, x1, route, mod, final_g.reshape(1, d), ys)


def _plan(expert_ids, tm, n_tiles):
    onehot = (expert_ids[:, None] == jnp.arange(N_EXPERTS, dtype=I32)[None, :]).astype(I32)
    csum = jnp.cumsum(onehot, axis=0)
    counts = csum[-1]
    rank = jnp.sum(csum * onehot, axis=1) - 1
    tiles_per = (counts + tm - 1) // tm
    tile_end = jnp.cumsum(tiles_per)
    seg_start = (tile_end - tiles_per) * tm
    pos = jnp.sum(onehot * seg_start[None, :], axis=1) + rank
    n_used = tile_end[-1]
    tile_ids = jnp.arange(n_tiles, dtype=I32)
    tile_expert = jnp.sum((tile_ids[:, None] >= tile_end[None, :]).astype(I32), axis=1)
    last_used = jnp.sum((n_used - 1 >= tile_end).astype(I32))
    tile_expert = jnp.where(tile_ids < n_used, tile_expert, last_used).astype(I32)
    z_expert = jnp.where((counts % tm) != 0, (tile_end - 1) * tm, -1)
    spare = n_used + jnp.arange(N_EXPERTS, dtype=I32)
    z_spare = jnp.where(spare < n_tiles, spare * tm, -1)
    zoff = jnp.concatenate([z_expert, z_spare]).astype(I32)
    return pos.astype(I32), tile_expert, n_used.reshape(1).astype(I32), zoff


def kernel(x_prompt, x_sample, c_prompt, c_sample, state_mlstm_C, state_mlstm_n, state_mlstm_m,
           cache_conv, w_ada, b_ada, norm1_g, w_in, b_igate, b_fgate, mh_norm_g, w_dw, b_dw,
           conv_ln_g, conv_ln_b, w_out, norm2_g, w_grp_router, b_grp_router, w_exp_router,
           b_exp_router, w_gate, w_up, w_down, final_g):
    depth = w_ada.shape[0]
    assert depth == 1, "one layer per step"
    bp, tp_, d = x_prompt.shape
    bs = x_sample.shape[0]
    assert x_sample.shape[1] == 1
    dk = state_mlstm_C.shape[-1]
    dm = N_HEADS * dk
    dc = d - dm
    l = 0

    win = w_in[l]
    w_qkvo = win[:, 0:4 * dm]
    w_gates = win[:, 4 * dm:4 * dm + 2 * N_HEADS]
    w_glu = win[:, 4 * dm + 2 * N_HEADS:]
    wmain, wmain_lo = _split2(jnp.concatenate([w_qkvo, w_glu], axis=1))
    wout_hi, wout_lo = _split2(w_out[l])
    wkt_hi, wkt_lo = _split2(win[:, dm:2 * dm].T)
    wg_pad = jnp.pad(w_gates, ((0, 0), (0, LANES - 2 * N_HEADS)))
    wg_hi, wg_lo = _split2(wg_pad)
    gbias = jnp.pad(jnp.concatenate([b_igate[l], b_fgate[l]]), (0, LANES - 2 * N_HEADS)).reshape(1, LANES)
    w_r = jnp.concatenate([w_grp_router[l], w_exp_router[l]], axis=1)
    n_r = N_GROUPS + N_EXPERTS
    wr_hi, wr_lo = _split2(jnp.pad(w_r, ((0, 0), (0, LANES - n_r))))
    rbias = jnp.pad(jnp.concatenate([b_grp_router[l], b_exp_router[l]]), (0, LANES - n_r)).reshape(1, LANES)
    wts = dict(dm=dm, g1=norm1_g[l].reshape(1, d), g2=norm2_g[l].reshape(1, d), wmain=wmain,
               wg_hi=wg_hi, wg_lo=wg_lo, gbias=gbias, mhg=mh_norm_g[l],
               wdw=w_dw[l].reshape(CONV_W, dc), bdw=b_dw[l].reshape(1, dc),
               clg=conv_ln_g[l].reshape(1, dc), clb=conv_ln_b[l].reshape(1, dc),
               wout=wout_hi, wr_hi=wr_hi, wr_lo=wr_lo, rbias=rbias, wk_t=wkt_hi,
               wmain_lo=wmain_lo, wout_lo=wout_lo, wk_t_lo=wkt_lo)

    mod = _ada(jnp.concatenate([c_prompt, c_sample], axis=0), w_ada[l], b_ada[l])
    mod_p, mod_s = mod[:bp], mod[bp:]

    n_p = bp * tp_
    n_all = n_p + bs
    x1_p, h2_p, route_p, c_p, n_pr, m_p, cv_p = _mix_prompt(x_prompt, mod_p, wts)
    x1_s, h2_s, route_s, c_s, n_s, m_s, u_s = _mix_sample(
        x_sample.reshape(bs, d), mod_s, state_mlstm_C[l], state_mlstm_n[l], state_mlstm_m[l],
        cache_conv[l], wts)

    route = jnp.concatenate([route_p, route_s], axis=0)
    expert_ids = route[:, 0:2].astype(I32).reshape(-1)
    tm = MOE_TILE
    n_tiles = (2 * n_all) // tm + N_EXPERTS
    pos, tile_expert, n_used, zoff = _plan(expert_ids, tm, n_tiles)
    pos = pos.reshape(n_all, 2)

    pos_p, pos_s = pos[:n_p], pos[n_p:]

    xs = _scatter_rows(h2_p, pos_p, zoff, n_tiles * tm, tm)
    xs = _scatter_rows(h2_s, pos_s, zoff, xs, tm)
    ys = _moe(xs, tile_expert, n_used, w_gate[l], w_up[l], w_down[l], tm)

    y_p = _finish(x1_p, route_p, mod_p.reshape(bp, 1, -1), final_g, ys, pos_p)
    y_s = _finish(x1_s.reshape(1, bs, d), route_s, mod_s.reshape(1, bs, -1), final_g, ys,
                  pos_s).reshape(bs, 1, d)

    conv_s = jnp.concatenate([cache_conv[l][:, 1:, :], u_s[:, None, :]], axis=1)
    return (y_p, y_s,
            c_p, n_pr, m_p[:, 0, :N_HEADS].reshape(1, bp, N_HEADS), cv_p,
            c_s[None], n_s.reshape(1, bs, N_HEADS, dk), m_s[:, :N_HEADS].reshape(1, bs, N_HEADS),
            conv_s[None])
```

```python
import functools

import jax
import jax.numpy as jnp
from jax import lax
from jax.experimental import pallas as pl
from jax.experimental.pallas import tpu as pltpu

F32 = jnp.float32
BF16 = jnp.bfloat16
I32 = jnp.int32

EPS = 1e-6
LANES = 128
CHUNK = 128
N_HEADS = 4
N_GROUPS = 4
EXP_PER_GROUP = 8
N_EXPERTS = N_GROUPS * EXP_PER_GROUP
CONV_W = 31
CONV_PAD = 32
CONV_OFF = CONV_PAD - (CONV_W - 1)
MIX_TILE = 256
MOE_TILE = 256
PERM_TILE = 512
VMEM_LIMIT = 56 * 1024 * 1024


def _sigmoid(x):
    return 1.0 / (1.0 + jnp.exp(-x))


def _silu(x):
    return x * _sigmoid(x)


def _log_sigmoid(x):
    return jnp.minimum(x, 0.0) - jnp.log(1.0 + jnp.exp(-jnp.abs(x)))


def _bdot(a, b):
    return jnp.dot(a.astype(BF16), b.astype(BF16), preferred_element_type=F32)


def _bdot_nt(a, b):
    return lax.dot_general(a.astype(BF16), b.astype(BF16), (((1,), (1,)), ((), ())),
                           preferred_element_type=F32)


def _split2(x):
    hi = x.astype(BF16)
    lo = (x - hi.astype(F32)).astype(BF16)
    return hi, lo


def _dot3(a, w_hi, w_lo):
    a_hi, a_lo = _split2(a)
    return (jnp.dot(a_hi, w_hi, preferred_element_type=F32)
            + jnp.dot(a_lo, w_hi, preferred_element_type=F32)
            + jnp.dot(a_hi, w_lo, preferred_element_type=F32))


def _cumsum_rows(tril_bf16, x):
    hi = x.astype(BF16)
    r1 = x - hi.astype(F32)
    mid = r1.astype(BF16)
    lo = (r1 - mid.astype(F32)).astype(BF16)
    return (jnp.dot(tril_bf16, hi, preferred_element_type=F32)
            + jnp.dot(tril_bf16, mid, preferred_element_type=F32)
            + jnp.dot(tril_bf16, lo, preferred_element_type=F32))


def _rms(x, g):
    return x * lax.rsqrt(jnp.mean(x * x, axis=-1, keepdims=True) + EPS) * g


def _layer_norm(x, g, b=None):
    mu = jnp.mean(x, axis=-1, keepdims=True)
    xc = x - mu
    var = jnp.mean(xc * xc, axis=-1, keepdims=True)
    y = xc * lax.rsqrt(var + EPS) * g
    return y if b is None else y + b


def _route(logits):
    lane = lax.broadcasted_iota(I32, logits.shape, 1).astype(F32)
    neg = jnp.float32(-jnp.inf)
    big = jnp.float32(1e9)
    is_g = lane < N_GROUPS
    gl = jnp.where(is_g, logits, neg)
    gmax = jnp.max(gl, axis=1, keepdims=True)
    gsel = jnp.min(jnp.where(gl == gmax, lane, big), axis=1, keepdims=True)
    pg = 1.0 / jnp.sum(jnp.where(is_g, jnp.exp(gl - gmax), 0.0), axis=1, keepdims=True)
    lo = N_GROUPS + EXP_PER_GROUP * gsel
    emask = (lane >= lo) & (lane < lo + EXP_PER_GROUP)
    el = jnp.where(emask, logits, neg)
    v1 = jnp.max(el, axis=1, keepdims=True)
    i1 = jnp.min(jnp.where(el == v1, lane, big), axis=1, keepdims=True)
    el2 = jnp.where(lane == i1, neg, el)
    v2 = jnp.max(el2, axis=1, keepdims=True)
    i2 = jnp.min(jnp.where(el2 == v2, lane, big), axis=1, keepdims=True)
    d = jnp.exp(v2 - v1)
    w1 = pg / (1.0 + d)
    w2 = pg * d / (1.0 + d)
    return jnp.where(lane == 0, i1 - N_GROUPS,
                     jnp.where(lane == 1, i2 - N_GROUPS,
                               jnp.where(lane == 2, w1, jnp.where(lane == 3, w2, 0.0))))


def _ada_kernel(c_ref, w_ref, b_ref, o_ref):
    w_hi, w_lo = _split2(w_ref[...])
    o_ref[...] = _dot3(_silu(c_ref[...]), w_hi, w_lo) + b_ref[...]


def _ada(c_all, w_ada, b_ada):
    rows, d = c_all.shape
    n_out = w_ada.shape[1]
    blk = 1024
    return pl.pallas_call(
        _ada_kernel,
        grid=(n_out // blk,),
        in_specs=[pl.BlockSpec((rows, d), lambda j: (0, 0)),
                  pl.BlockSpec((d, blk), lambda j: (0, j)),
                  pl.BlockSpec((1, blk), lambda j: (0, j))],
        out_specs=pl.BlockSpec((rows, blk), lambda j: (0, j)),
        out_shape=jax.ShapeDtypeStruct((rows, n_out), F32),
        compiler_params=pltpu.CompilerParams(dimension_semantics=("arbitrary",),
                                             vmem_limit_bytes=VMEM_LIMIT),
        name="ada",
    )(c_all, w_ada, b_ada.reshape(1, n_out))


def _post(x, attn_cat, mod, g2, wout, wr_hi, wr_lo, rbias, d, wout_lo=None):
    gate1 = mod[:, 2 * d:3 * d]
    sh2 = mod[:, 3 * d:4 * d]
    sc2 = mod[:, 4 * d:5 * d]
    if wout_lo is None:
        proj = jnp.dot(attn_cat.astype(BF16), wout, preferred_element_type=F32)
    else:
        proj = _dot3(attn_cat, wout, wout_lo)
    x1 = x + gate1 * proj
    h2 = _rms(x1, g2) * (1.0 + sc2) + sh2
    logits = _dot3(h2, wr_hi, wr_lo) + rbias
    return x1, h2, _route(logits)


def _mix_prompt_kernel(x_ref, mod_ref, g1_ref, g2_ref, wmain_ref, wgh_ref, wgl_ref, gbias_ref,
                       mhg_ref, wdw_ref, bdw_ref, clg_ref, clb_ref, wout_ref, wrh_ref, wrl_ref,
                       rbias_ref,
                       x1_ref, h2_ref, route_ref, c_ref, n_ref, m_ref, cv_ref,
                       ubuf, yc_s, q_s, k_s, v_s, so_s, hm_s, p_s, u_s, cm_s, nb_s, m_s,
                       *, tt, d, dm, dk):
    t = pl.program_id(1)
    dc = d - dm

    @pl.when(t == 0)
    def _():
        c_ref[...] = jnp.zeros_like(c_ref)
        nb_s[...] = jnp.zeros_like(nb_s)
        m_s[...] = jnp.zeros_like(m_s)
        ubuf[0:CONV_PAD, :] = jnp.zeros((CONV_PAD, dc), F32)
        ubuf[tt + CONV_PAD:tt + CONV_PAD + 8, :] = jnp.zeros((8, dc), F32)

    x = x_ref[0]
    mod = mod_ref[0]
    h = _rms(x, g1_ref[...]) * (1.0 + mod[:, d:2 * d]) + mod[:, 0:d]
    hb = h.astype(BF16)
    gates = _dot3(h, wgh_ref[...], wgl_ref[...]) + gbias_ref[...]

    def proj(lo, hi):
        return jnp.dot(hb, wmain_ref[:, lo:hi], preferred_element_type=F32)

    q_s[...] = proj(0, dm).astype(BF16)
    k_s[...] = proj(dm, 2 * dm) * (dk ** -0.5)
    v_s[...] = proj(2 * dm, 3 * dm).astype(BF16)
    so_s[...] = _sigmoid(proj(3 * dm, 4 * dm))

    ubuf[CONV_PAD:CONV_PAD + tt, :] = proj(4 * dm, 4 * dm + dc) * _sigmoid(proj(4 * dm + dc, 4 * dm + 2 * dc))
    row8 = lax.broadcasted_iota(I32, (8, LANES), 0)
    n_blk = CONV_PAD // 8 + 1

    for lt in range(dc // LANES):
        ls = slice(lt * LANES, (lt + 1) * LANES)
        wrows = [jnp.broadcast_to(wdw_ref[j:j + 1, ls], (8, LANES)) for j in range(CONV_W)]
        bias = jnp.broadcast_to(bdw_ref[:, ls], (8, LANES))

        def partial_sums(r0, ls=ls, wrows=wrows):
            blocks = [ubuf[pl.ds(pl.multiple_of(r0 + 8 * a, 8), 8), ls] for a in range(n_blk)]
            sums = []
            for s in range(8):
                acc = None
                for a in range(n_blk):
                    j = 8 * a + s - CONV_OFF
                    if 0 <= j < CONV_W:
                        term = blocks[a] * wrows[j]
                        acc = term if acc is None else acc + term
                sums.append(acc)
            return tuple(sums)

        def conv_rows(i, q_prev, ls=ls, bias=bias, partial_sums=partial_sums):
            r0 = pl.multiple_of(i * 8, 8)
            q_cur = partial_sums(r0)
            y = bias + q_prev[0]
            for s in range(1, 8):
                merged = jnp.where(row8 < s, q_cur[s], q_prev[s])
                y = y + pltpu.roll(merged, 8 - s, 0)
            yc_s[pl.ds(pl.multiple_of(r0 - 8, 8), 8), ls] = y
            return q_cur

        lax.fori_loop(1, tt // 8 + 1, conv_rows, partial_sums(0))

    hc = _silu(_layer_norm(yc_s[...], clg_ref[...], clb_ref[...]))
    cv_ref[0, 0] = ubuf[tt + CONV_PAD - (CONV_W - 1):tt + CONV_PAD, :]
    ubuf[0:CONV_PAD, :] = ubuf[tt:tt + CONV_PAD, :]

    row = lax.broadcasted_iota(I32, (CHUNK, CHUNK), 0)
    col = lax.broadcasted_iota(I32, (CHUNK, CHUNK), 1)
    causal = col <= row
    tril = causal.astype(BF16)
    lane = lax.broadcasted_iota(I32, (CHUNK, LANES), 1)
    neg = jnp.float32(-jnp.inf)
    ones_b = jnp.ones((CHUNK, dk), BF16)
    n_ch = tt // CHUNK
    b_cols = {}
    for c in range(n_ch):
        r0 = c * CHUNK
        gt = gates[r0:r0 + CHUNK, :]
        bsum = _cumsum_rows(tril, _log_sigmoid(gt))
        pk = jnp.where(lane < N_HEADS, gt - pltpu.roll(bsum, LANES - N_HEADS, 1), bsum)
        pk_t = pk.T
        for hd in range(N_HEADS):
            cs = slice(hd * dk, (hd + 1) * dk)
            idx = c * N_HEADS + hd
            kf = k_s[r0:r0 + CHUNK, cs]
            va = jnp.concatenate([v_s[r0:r0 + CHUNK, cs], ones_b], axis=1)
            gm = jnp.where(causal, pk_t[hd:hd + 1, :], neg)
            cm = jnp.max(gm, axis=1, keepdims=True)
            s = _bdot_nt(q_s[r0:r0 + CHUNK, cs], kf) * jnp.exp(gm - cm)
            p_s[idx] = jnp.dot(s.astype(BF16), va, preferred_element_type=F32)
            cm_s[idx] = jnp.broadcast_to(cm, (CHUNK, LANES))
            kw = kf * jnp.exp(pk[:, hd:hd + 1] - cm[CHUNK - 1:CHUNK, :])
            u_s[idx] = jnp.dot(kw.T.astype(BF16), va, preferred_element_type=F32)
            b_cols[idx] = pk[:, N_HEADS + hd:N_HEADS + hd + 1]

    for hd in range(N_HEADS):
        cs = slice(hd * dk, (hd + 1) * dk)
        ca = jnp.concatenate([c_ref[0, 0, hd], nb_s[hd]], axis=1)
        m_prev = m_s[hd:hd + 1, :]
        for c in range(n_ch):
            r0 = c * CHUNK
            idx = c * N_HEADS + hd
            cm = cm_s[idx]
            b_col = b_cols[idx]
            mt = jnp.maximum(m_prev, cm)
            f_loc = jnp.exp(cm - mt)
            a_int = jnp.exp(m_prev - mt)
            qc = jnp.dot(q_s[r0:r0 + CHUNK, cs], ca.astype(BF16), preferred_element_type=F32)
            p = p_s[idx]
            num = f_loc * p[:, :dk] + a_int * qc[:, :dk]
            den = f_loc * p[:, dk:] + a_int * qc[:, dk:]
            hh = num / jnp.maximum(jnp.abs(den), jnp.exp(-(b_col + mt)))
            hm_s[r0:r0 + CHUNK, cs] = (_layer_norm(hh, mhg_ref[hd:hd + 1, :])
                                       * so_s[r0:r0 + CHUNK, cs])
            mt_l = mt[CHUNK - 1:CHUNK, :]
            u = u_s[idx]
            f_l = f_loc[CHUNK - 1:CHUNK, :]
            a_l = a_int[CHUNK - 1:CHUNK, :]
            ca = jnp.concatenate([a_l * ca[:, :dk] + f_l * u[:, :dk],
                                  a_l * ca[:, dk:] + f_l * u[:, dk:]], axis=1)
            m_prev = b_col[CHUNK - 1:CHUNK, :] + mt_l
        c_ref[0, 0, hd] = ca[:, :dk]
        nb_s[hd] = ca[:, dk:]
        m_s[hd:hd + 1, :] = m_prev

    @pl.when(t == pl.num_programs(1) - 1)
    def _():
        for hd in range(N_HEADS):
            n_ref[0, 0, hd:hd + 1, :] = nb_s[hd].T[0:1, :]

    lane1 = lax.broadcasted_iota(I32, (1, LANES), 1)
    m_row = jnp.zeros((1, LANES), F32)
    for hd in range(N_HEADS):
        m_row = jnp.where(lane1 == hd, m_s[hd:hd + 1, :], m_row)
    m_ref[0] = m_row

    cat = jnp.concatenate([hm_s[...], hc], axis=1)
    x1, h2, route = _post(x, cat, mod, g2_ref[...], wout_ref[...], wrh_ref[...], wrl_ref[...],
                          rbias_ref[...], d)
    x1_ref[0] = x1
    h2_ref[...] = h2
    route_ref[...] = route


def _const_spec(shape):
    nd = len(shape)
    return pl.BlockSpec(shape, lambda *_: (0,) * nd)


def _mix_prompt(x, mod, wts):
    b, t, d = x.shape
    dm = wts["dm"]
    dk = dm // N_HEADS
    dc = d - dm
    tt = min(MIX_TILE, t)
    assert t % tt == 0 and tt % CHUNK == 0 and tt >= CONV_PAD
    nt = t // tt
    kern = functools.partial(_mix_prompt_kernel, tt=tt, d=d, dm=dm, dk=dk)
    const_names = ["g1", "g2", "wmain", "wg_hi", "wg_lo", "gbias", "mhg", "wdw", "bdw", "clg",
                   "clb", "wout", "wr_hi", "wr_lo", "rbias"]
    consts = [wts[k] for k in const_names]
    in_specs = ([pl.BlockSpec((1, tt, d), lambda i, j: (i, j, 0)),
                 pl.BlockSpec((1, 1, mod.shape[-1]), lambda i, j: (i, 0, 0))]
                + [_const_spec(c.shape) for c in consts])
    out_shape = [
        jax.ShapeDtypeStruct((b, t, d), F32),
        jax.ShapeDtypeStruct((b * t, d), F32),
        jax.ShapeDtypeStruct((b * t, LANES), F32),
        jax.ShapeDtypeStruct((1, b, N_HEADS, dk, dk), F32),
        jax.ShapeDtypeStruct((1, b, N_HEADS, dk), F32),
        jax.ShapeDtypeStruct((b, 1, LANES), F32),
        jax.ShapeDtypeStruct((1, b, CONV_W - 1, dc), F32),
    ]
    out_specs = [
        pl.BlockSpec((1, tt, d), lambda i, j: (i, j, 0)),
        pl.BlockSpec((tt, d), lambda i, j: (i * nt + j, 0)),
        pl.BlockSpec((tt, LANES), lambda i, j: (i * nt + j, 0)),
        pl.BlockSpec((1, 1, N_HEADS, dk, dk), lambda i, j: (0, i, 0, 0, 0)),
        pl.BlockSpec((1, 1, N_HEADS, dk), lambda i, j: (0, i, 0, 0)),
        pl.BlockSpec((1, 1, LANES), lambda i, j: (i, 0, 0)),
        pl.BlockSpec((1, 1, CONV_W - 1, dc), lambda i, j: (0, i, 0, 0)),
    ]
    n_hc = (tt // CHUNK) * N_HEADS
    scratch = [pltpu.VMEM((tt + CONV_PAD + 8, dc), F32),
               pltpu.VMEM((tt, dc), F32),
               pltpu.VMEM((tt, dm), BF16),
               pltpu.VMEM((tt, dm), F32),
               pltpu.VMEM((tt, dm), BF16),
               pltpu.VMEM((tt, dm), F32),
               pltpu.VMEM((tt, dm), F32),
               pltpu.VMEM((n_hc, CHUNK, 2 * dk), F32),
               pltpu.VMEM((n_hc, CHUNK, 2 * dk), F32),
               pltpu.VMEM((n_hc, CHUNK, LANES), F32),
               pltpu.VMEM((N_HEADS, dk, LANES), F32),
               pltpu.VMEM((8, LANES), F32)]
    return pl.pallas_call(
        kern, grid=(b, nt), in_specs=in_specs, out_specs=out_specs, out_shape=out_shape,
        scratch_shapes=scratch,
        compiler_params=pltpu.CompilerParams(dimension_semantics=("arbitrary", "arbitrary"),
                                             vmem_limit_bytes=VMEM_LIMIT),
        name="mix_p",
    )(x, mod.reshape(b, 1, -1), *consts)


def _s_pre_kernel(x_ref, mod_ref, g1_ref, wmain_ref, wmainlo_ref, wgh_ref, wgl_ref, gbias_ref,
                  wkt_ref, wktlo_ref, wdw_ref, bdw_ref, clg_ref, clb_ref, cache_ref, n0_ref, m0_ref,
                  q_ref, kt_ref, vs_ref, ab_ref, sv_ref, den_ref, eb_ref, o_ref, hc_ref, u_ref,
                  n_ref, m_ref, *, d, dm, dk):
    dc = d - dm
    x = x_ref[...]
    mod = mod_ref[...]
    sh1 = mod[:, 0:d]
    sc1 = mod[:, d:2 * d]
    h = _rms(x, g1_ref[...]) * (1.0 + sc1) + sh1
    z = _dot3(h, wmain_ref[...], wmainlo_ref[...])
    gates = _dot3(h, wgh_ref[...], wgl_ref[...]) + gbias_ref[...]
    scale = dk ** -0.5
    h_hi, h_lo = _split2(h)
    kt = _bdot_nt(wkt_ref[...], h_hi) + _bdot_nt(wktlo_ref[...], h_hi) + _bdot_nt(wkt_ref[...], h_lo)
    kt_ref[...] = (kt * scale).astype(BF16)
    k_all = z[:, dm:2 * dm] * scale
    ga = z[:, 4 * dm:4 * dm + dc]
    gb = z[:, 4 * dm + dc:4 * dm + 2 * dc]
    u = ga * _sigmoid(gb)
    u_ref[...] = u
    acc = jnp.broadcast_to(bdw_ref[...], u.shape) + u * wdw_ref[CONV_W - 1:CONV_W, :]
    for j in range(CONV_W - 1):
        acc = acc + cache_ref[j] * wdw_ref[j:j + 1, :]
    hc_ref[...] = _silu(_layer_norm(acc, clg_ref[...], clb_ref[...]))
    o_ref[...] = z[:, 3 * dm:4 * dm]
    q_ref[...] = z[:, 0:dm]
    m0 = m0_ref[...]
    n0 = n0_ref[...]
    lane1 = lax.broadcasted_iota(I32, (1, LANES), 1)
    m_new = jnp.zeros(m0.shape, F32)
    for hd in range(N_HEADS):
        cs = slice(hd * dk, (hd + 1) * dk)
        ig = gates[:, hd:hd + 1]
        lf = _log_sigmoid(gates[:, N_HEADS + hd:N_HEADS + hd + 1])
        mp = m0[:, hd:hd + 1]
        inter = lf + mp
        mt = jnp.maximum(inter, ig)
        w = jnp.exp(ig - mt)
        a_int = jnp.exp(inter - mt)
        qf = z[:, cs]
        kf = k_all[:, cs]
        vf = z[:, 2 * dm + hd * dk:2 * dm + (hd + 1) * dk]
        s = jnp.sum(qf * kf, axis=1, keepdims=True) * w
        sv_ref[:, cs] = s * vf
        den_ref[:, cs] = jnp.broadcast_to(
            s + a_int * jnp.sum(qf * n0[:, cs], axis=1, keepdims=True), (x.shape[0], dk))
        eb_ref[:, cs] = jnp.broadcast_to(jnp.exp(-mt), (x.shape[0], dk))
        ab_ref[:, cs] = jnp.broadcast_to(a_int, (x.shape[0], dk))
        vs_ref[:, cs] = (vf * w).astype(BF16)
        n_ref[:, cs] = a_int * n0[:, cs] + w * kf
        m_new = jnp.where(lane1 == hd, mt, m_new)
    m_ref[...] = m_new


def _s_state_kernel(q_ref, kt_ref, vs_ref, ab_ref, c0_ref, c_ref, r_ref, *, bb, dk):
    i = pl.program_id(0)
    nb = q_ref.shape[0]
    rows = lax.broadcasted_iota(I32, (nb, dk), 0)

    @pl.when(i == 0)
    def _():
        r_ref[...] = jnp.zeros_like(r_ref)

    a_blk = ab_ref[pl.ds(pl.multiple_of(i * bb, bb), bb), :]
    for j in range(bb):
        sel = rows == i * bb + j
        for hd in range(N_HEADS):
            cs = slice(hd * dk, (hd + 1) * dk)
            c0 = c0_ref[j, hd]
            vmask = jnp.where(sel, vs_ref[:, cs], jnp.zeros((), BF16))
            c_ref[j, hd] = (a_blk[j:j + 1, cs] * c0
                            + jnp.dot(kt_ref[cs, :], vmask, preferred_element_type=F32))
            c_hi, c_lo = _split2(c0)
            q_hi, q_lo = _split2(q_ref[:, cs])
            r = (jnp.dot(q_hi, c_hi, preferred_element_type=F32)
                 + jnp.dot(q_lo, c_hi, preferred_element_type=F32)
                 + jnp.dot(q_hi, c_lo, preferred_element_type=F32))
            r_ref[:, cs] = r_ref[:, cs] + jnp.where(sel, r, 0.0)


def _s_post_kernel(x_ref, mod_ref, g2_ref, mhg_ref, r_ref, ab_ref, sv_ref, den_ref, eb_ref, o_ref,
                   hc_ref, wout_ref, woutlo_ref, wrh_ref, wrl_ref, rbias_ref,
                   x1_ref, h2_ref, route_ref, *, d, dm, dk):
    hm = []
    for hd in range(N_HEADS):
        cs = slice(hd * dk, (hd + 1) * dk)
        num = sv_ref[:, cs] + ab_ref[:, cs] * r_ref[:, cs]
        hh = num / jnp.maximum(jnp.abs(den_ref[:, cs]), eb_ref[:, cs])
        hm.append(_layer_norm(hh, mhg_ref[hd:hd + 1, :]) * _sigmoid(o_ref[:, cs]))
    cat = jnp.concatenate(hm + [hc_ref[...]], axis=1)
    x1, h2, route = _post(x_ref[...], cat, mod_ref[...], g2_ref[...], wout_ref[...], wrh_ref[...],
                          wrl_ref[...], rbias_ref[...], d, wout_lo=woutlo_ref[...])
    x1_ref[...] = x1
    h2_ref[...] = h2
    route_ref[...] = route


def _mix_sample(x, mod, c0, n0, m0, cache, wts):
    nb, d = x.shape
    dm = wts["dm"]
    dk = dm // N_HEADS
    dc = d - dm
    cp = pltpu.CompilerParams(dimension_semantics=("arbitrary",), vmem_limit_bytes=VMEM_LIMIT)
    cache_t = jnp.transpose(cache, (1, 0, 2))
    m0p = jnp.pad(m0, ((0, 0), (0, LANES - N_HEADS)))
    pre_in = [x, mod, wts["g1"], wts["wmain"], wts["wmain_lo"], wts["wg_hi"], wts["wg_lo"],
              wts["gbias"], wts["wk_t"], wts["wk_t_lo"], wts["wdw"], wts["bdw"], wts["clg"],
              wts["clb"], cache_t, n0.reshape(nb, dm), m0p]
    pre_out = [jax.ShapeDtypeStruct((nb, dm), F32),
               jax.ShapeDtypeStruct((dm, nb), BF16),
               jax.ShapeDtypeStruct((nb, dm), BF16),
               jax.ShapeDtypeStruct((nb, dm), F32),
               jax.ShapeDtypeStruct((nb, dm), F32),
               jax.ShapeDtypeStruct((nb, dm), F32),
               jax.ShapeDtypeStruct((nb, dm), F32),
               jax.ShapeDtypeStruct((nb, dm), F32),
               jax.ShapeDtypeStruct((nb, dc), F32),
               jax.ShapeDtypeStruct((nb, dc), F32),
               jax.ShapeDtypeStruct((nb, dm), F32),
               jax.ShapeDtypeStruct((nb, LANES), F32)]
    (q, kt, vs, ab, sv, den, eb, o, hc, u, n1, m1) = pl.pallas_call(
        functools.partial(_s_pre_kernel, d=d, dm=dm, dk=dk),
        grid=(1,),
        in_specs=[_const_spec(a.shape) for a in pre_in],
        out_specs=[_const_spec(s.shape) for s in pre_out],
        out_shape=pre_out, compiler_params=cp, name="s_pre")(*pre_in)

    bb = 8
    assert nb % bb == 0
    c1, r = pl.pallas_call(
        functools.partial(_s_state_kernel, bb=bb, dk=dk),
        grid=(nb // bb,),
        in_specs=[_const_spec(q.shape), _const_spec(kt.shape), _const_spec(vs.shape),
                  _const_spec(ab.shape),
                  pl.BlockSpec((bb, N_HEADS, dk, dk), lambda i: (i, 0, 0, 0))],
        out_specs=[pl.BlockSpec((bb, N_HEADS, dk, dk), lambda i: (i, 0, 0, 0)),
                   _const_spec((nb, dm))],
        out_shape=[jax.ShapeDtypeStruct((nb, N_HEADS, dk, dk), F32),
                   jax.ShapeDtypeStruct((nb, dm), F32)],
        compiler_params=cp, name="s_state")(q, kt, vs, ab, c0)

    post_in = [x, mod, wts["g2"], wts["mhg"], r, ab, sv, den, eb, o, hc, wts["wout"],
               wts["wout_lo"], wts["wr_hi"], wts["wr_lo"], wts["rbias"]]
    post_out = [jax.ShapeDtypeStruct((nb, d), F32),
                jax.ShapeDtypeStruct((nb, d), F32),
                jax.ShapeDtypeStruct((nb, LANES), F32)]
    x1, h2, route = pl.pallas_call(
        functools.partial(_s_post_kernel, d=d, dm=dm, dk=dk),
        grid=(1,),
        in_specs=[_const_spec(a.shape) for a in post_in],
        out_specs=[_const_spec(s.shape) for s in post_out],
        out_shape=post_out, compiler_params=cp, name="s_post")(*post_in)
    return x1, h2, route, c1, n1, m1, u


def _scatter_kernel(pos_ref, zoff_ref, src_ref, *rest, tp, tm, create):
    xs_ref, zbuf, sem, zsem = rest[-4:]
    i = pl.program_id(0)

    if create:
        @pl.when(i == 0)
        def _():
            zbuf[...] = jnp.zeros_like(zbuf)

            def zero_copy(e):
                return pltpu.make_async_copy(
                    zbuf, xs_ref.at[pl.ds(pl.multiple_of(zoff_ref[e], tm), tm)], zsem)

            for e in range(zoff_ref.shape[0]):
                @pl.when(zoff_ref[e] >= 0)
                def _():
                    zero_copy(e).start()
            for e in range(zoff_ref.shape[0]):
                @pl.when(zoff_ref[e] >= 0)
                def _():
                    zero_copy(e).wait()

    def row_start(r, carry):
        for slot in range(2):
            pltpu.make_async_copy(src_ref.at[pl.ds(r, 1)],
                                  xs_ref.at[pl.ds(pos_ref[0, slot, r], 1)], sem).start()
        return carry

    lax.fori_loop(0, tp, row_start, 0, unroll=8)
    for slot in range(2):
        pltpu.make_async_copy(src_ref, xs_ref.at[pl.ds(0, tp)], sem).wait()


def _scatter_rows(h2, pos, zoff, xs_or_rows, tm):
    n, c = h2.shape
    tp = min(PERM_TILE, n)
    assert n % tp == 0
    pos3 = pos.reshape(n // tp, tp, 2).transpose(0, 2, 1)
    create = isinstance(xs_or_rows, int)
    n_sorted = xs_or_rows if create else xs_or_rows.shape[0]
    in_specs = [pl.BlockSpec((1, 2, tp), lambda i: (i, 0, 0), memory_space=pltpu.SMEM),
                pl.BlockSpec(memory_space=pltpu.SMEM),
                pl.BlockSpec((tp, c), lambda i: (i, 0))]
    args = [pos3, zoff, h2]
    if not create:
        in_specs.append(pl.BlockSpec(memory_space=pl.ANY))
        args.append(xs_or_rows)
    return pl.pallas_call(
        functools.partial(_scatter_kernel, tp=tp, tm=tm, create=create),
        grid_spec=pltpu.PrefetchScalarGridSpec(
            num_scalar_prefetch=0,
            grid=(n // tp,),
            in_specs=in_specs,
            out_specs=pl.BlockSpec(memory_space=pl.ANY),
            scratch_shapes=[pltpu.VMEM((tm, c), F32), pltpu.SemaphoreType.DMA(()),
                            pltpu.SemaphoreType.DMA(())]),
        out_shape=jax.ShapeDtypeStruct((n_sorted, c), F32),
        input_output_aliases={} if create else {3: 0},
        compiler_params=pltpu.CompilerParams(dimension_semantics=("arbitrary",),
                                             vmem_limit_bytes=VMEM_LIMIT),
        name="scatter",
    )(*args)


def _moe_kernel(te_ref, nu_ref, xs_ref, wg_ref, wu_ref, wd_ref, ys_ref, wg_b, wu_b, wd_b):
    i = pl.program_id(0)
    used = i < nu_ref[0]

    @pl.when(used & ((i == 0) | (te_ref[i] != te_ref[jnp.maximum(i - 1, 0)])))
    def _():
        wg_b[...] = wg_ref[0].astype(BF16)
        wu_b[...] = wu_ref[0].astype(BF16)
        wd_b[...] = wd_ref[0].astype(BF16)

    @pl.when(used)
    def _():
        xb = xs_ref[...].astype(BF16)
        g = jnp.dot(xb, wg_b[...], preferred_element_type=F32)
        u = jnp.dot(xb, wu_b[...], preferred_element_type=F32)
        hid = (_silu(g) * u).astype(BF16)
        ys_ref[...] = jnp.dot(hid, wd_b[...], preferred_element_type=F32)

    @pl.when(jnp.logical_not(used))
    def _():
        ys_ref[...] = jnp.zeros_like(ys_ref)


def _moe(xs, tile_expert, n_used, w_gate, w_up, w_down, tm):
    p, c = xs.shape
    ne, d, de = w_gate.shape
    n_tiles = p // tm

    def x_map(i, te, nu):
        return (jnp.minimum(i, jnp.maximum(nu[0] - 1, 0)), 0)

    def w_map(i, te, nu):
        return (te[i], 0, 0)

    return pl.pallas_call(
        _moe_kernel,
        grid_spec=pltpu.PrefetchScalarGridSpec(
            num_scalar_prefetch=2,
            grid=(n_tiles,),
            in_specs=[pl.BlockSpec((tm, c), x_map),
                      pl.BlockSpec((1, d, de), w_map),
                      pl.BlockSpec((1, d, de), w_map),
                      pl.BlockSpec((1, de, d), w_map)],
            out_specs=pl.BlockSpec((tm, c), lambda i, te, nu: (i, 0)),
            scratch_shapes=[pltpu.VMEM((d, de), BF16), pltpu.VMEM((d, de), BF16),
                            pltpu.VMEM((de, d), BF16)]),
        out_shape=jax.ShapeDtypeStruct((p, c), F32),
        compiler_params=pltpu.CompilerParams(dimension_semantics=("arbitrary",),
                                             vmem_limit_bytes=VMEM_LIMIT),
        name="moe",
    )(tile_expert, n_used, xs, w_gate, w_up, w_down)


def _fin_kernel(pos_ref, posn_ref, x1_ref, route_ref, mod_ref, fg_ref, ys_ref, y_ref, ybuf, sem,
                *, tp, d, n_steps):
    i = pl.program_id(0)
    cur = i % 2

    def issue(p_ref, buf):
        def row_start(r, carry):
            for slot in range(2):
                pltpu.make_async_copy(ys_ref.at[pl.ds(p_ref[0, slot, r], 1)],
                                      ybuf.at[buf, slot, pl.ds(r, 1)], sem.at[buf]).start()
            return carry
        lax.fori_loop(0, tp, row_start, 0, unroll=8)

    @pl.when(i == 0)
    def _():
        issue(pos_ref, 0)

    @pl.when(i + 1 < n_steps)
    def _():
        issue(posn_ref, 1 - cur)

    for slot in range(2):
        pltpu.make_async_copy(ys_ref.at[pl.ds(0, tp)], ybuf.at[cur, slot], sem.at[cur]).wait()
    route = route_ref[...]
    moe = route[:, 2:3] * ybuf[cur, 0] + route[:, 3:4] * ybuf[cur, 1]
    gate2 = mod_ref[0][:, 5 * d:6 * d]
    y_ref[0] = _rms(x1_ref[0] + gate2 * moe, fg_ref[...])


def _finish(x1, route, mod, final_g, ys, pos):
    b, t, d = x1.shape
    tp = min(PERM_TILE, t)
    assert t % tp == 0
    nt = t // tp
    n_steps = b * nt
    pos3 = pos.reshape(n_steps, tp, 2).transpose(0, 2, 1)
    blk0 = 0
    if mod.shape[1] == 1:
        mod_spec = pl.BlockSpec((1, 1, 6 * d), lambda i: (i // nt, 0, 0))
    else:
        mod_spec = pl.BlockSpec((1, tp, 6 * d), lambda i: (i // nt, i % nt, 0))
    return pl.pallas_call(
        functools.partial(_fin_kernel, tp=tp, d=d, n_steps=n_steps),
        grid_spec=pltpu.PrefetchScalarGridSpec(
            num_scalar_prefetch=0,
            grid=(n_steps,),
            in_specs=[pl.BlockSpec((1, 2, tp), lambda i: (blk0 + i, 0, 0), memory_space=pltpu.SMEM),
                      pl.BlockSpec((1, 2, tp), lambda i: (blk0 + jnp.minimum(i + 1, n_steps - 1), 0, 0),
                                   memory_space=pltpu.SMEM),
                      pl.BlockSpec((1, tp, d), lambda i: (i // nt, i % nt, 0)),
                      pl.BlockSpec((tp, LANES), lambda i: (i, 0)),
                      mod_spec,
                      _const_spec((1, d)),
                      pl.BlockSpec(memory_space=pl.ANY)],
            out_specs=pl.BlockSpec((1, tp, d), lambda i: (i // nt, i % nt, 0)),
            scratch_shapes=[pltpu.VMEM((2, 2, tp, d), F32), pltpu.SemaphoreType.DMA((2,))]),
        out_shape=jax.ShapeDtypeStruct((b, t, d), F32),
        compiler_params=pltpu.CompilerParams(dimension_semantics=("arbitrary",),
                                             vmem_limit_bytes=VMEM_LIMIT),
        name="fin",
    )(pos3, pos3, x1, route, mod, final_g.reshape(1, d), ys)


def _plan(expert_ids, tm, n_tiles):
    onehot = (expert_ids[:, None] == jnp.arange(N_EXPERTS, dtype=I32)[None, :]).astype(I32)
    csum = jnp.cumsum(onehot, axis=0)
    counts = csum[-1]
    rank = jnp.sum(csum * onehot, axis=1) - 1
    tiles_per = (counts + tm - 1) // tm
    tile_end = jnp.cumsum(tiles_per)
    seg_start = (tile_end - tiles_per) * tm
    pos = jnp.sum(onehot * seg_start[None, :], axis=1) + rank
    n_used = tile_end[-1]
    tile_ids = jnp.arange(n_tiles, dtype=I32)
    tile_expert = jnp.sum((tile_ids[:, None] >= tile_end[None, :]).astype(I32), axis=1)
    last_used = jnp.sum((n_used - 1 >= tile_end).astype(I32))
    tile_expert = jnp.where(tile_ids < n_used, tile_expert, last_used).astype(I32)
    z_expert = jnp.where((counts % tm) != 0, (tile_end - 1) * tm, -1)
    spare = n_used + jnp.arange(N_EXPERTS, dtype=I32)
    z_spare = jnp.where(spare < n_tiles, spare * tm, -1)
    zoff = jnp.concatenate([z_expert, z_spare]).astype(I32)
    return pos.astype(I32), tile_expert, n_used.reshape(1).astype(I32), zoff


def kernel(x_prompt, x_sample, c_prompt, c_sample, state_mlstm_C, state_mlstm_n, state_mlstm_m,
           cache_conv, w_ada, b_ada, norm1_g, w_in, b_igate, b_fgate, mh_norm_g, w_dw, b_dw,
           conv_ln_g, conv_ln_b, w_out, norm2_g, w_grp_router, b_grp_router, w_exp_router,
           b_exp_router, w_gate, w_up, w_down, final_g):
    depth = w_ada.shape[0]
    assert depth == 1, "one layer per step"
    bp, tp_, d = x_prompt.shape
    bs = x_sample.shape[0]
    assert x_sample.shape[1] == 1
    dk = state_mlstm_C.shape[-1]
    dm = N_HEADS * dk
    dc = d - dm
    l = 0

    win = w_in[l]
    w_qkvo = win[:, 0:4 * dm]
    w_gates = win[:, 4 * dm:4 * dm + 2 * N_HEADS]
    w_glu = win[:, 4 * dm + 2 * N_HEADS:]
    wmain, wmain_lo = _split2(jnp.concatenate([w_qkvo, w_glu], axis=1))
    wout_hi, wout_lo = _split2(w_out[l])
    wkt_hi, wkt_lo = _split2(win[:, dm:2 * dm].T)
    wg_pad = jnp.pad(w_gates, ((0, 0), (0, LANES - 2 * N_HEADS)))
    wg_hi, wg_lo = _split2(wg_pad)
    gbias = jnp.pad(jnp.concatenate([b_igate[l], b_fgate[l]]), (0, LANES - 2 * N_HEADS)).reshape(1, LANES)
    w_r = jnp.concatenate([w_grp_router[l], w_exp_router[l]], axis=1)
    n_r = N_GROUPS + N_EXPERTS
    wr_hi, wr_lo = _split2(jnp.pad(w_r, ((0, 0), (0, LANES - n_r))))
    rbias = jnp.pad(jnp.concatenate([b_grp_router[l], b_exp_router[l]]), (0, LANES - n_r)).reshape(1, LANES)
    wts = dict(dm=dm, g1=norm1_g[l].reshape(1, d), g2=norm2_g[l].reshape(1, d), wmain=wmain,
               wg_hi=wg_hi, wg_lo=wg_lo, gbias=gbias, mhg=mh_norm_g[l],
               wdw=w_dw[l].reshape(CONV_W, dc), bdw=b_dw[l].reshape(1, dc),
               clg=conv_ln_g[l].reshape(1, dc), clb=conv_ln_b[l].reshape(1, dc),
               wout=wout_hi, wr_hi=wr_hi, wr_lo=wr_lo, rbias=rbias, wk_t=wkt_hi,
               wmain_lo=wmain_lo, wout_lo=wout_lo, wk_t_lo=wkt_lo)

    mod = _ada(jnp.concatenate([c_prompt, c_sample], axis=0), w_ada[l], b_ada[l])
    mod_p, mod_s = mod[:bp], mod[bp:]

    n_p = bp * tp_
    n_all = n_p + bs
    x1_p, h2_p, route_p, c_p, n_pr, m_p, cv_p = _mix_prompt(x_prompt, mod_p, wts)
    x1_s, h2_s, route_s, c_s, n_s, m_s, u_s = _mix_sample(
        x_sample.reshape(bs, d), mod_s, state_mlstm_C[l], state_mlstm_n[l], state_mlstm_m[l],
        cache_conv[l], wts)

    route = jnp.concatenate([route_p, route_s], axis=0)
    expert_ids = route[:, 0:2].astype(I32).reshape(-1)
    tm = MOE_TILE
    n_tiles = (2 * n_all) // tm + N_EXPERTS
    pos, tile_expert, n_used, zoff = _plan(expert_ids, tm, n_tiles)
    pos = pos.reshape(n_all, 2)

    pos_p, pos_s = pos[:n_p], pos[n_p:]

    xs = _scatter_rows(h2_p, pos_p, zoff, n_tiles * tm, tm)
    xs = _scatter_rows(h2_s, pos_s, zoff, xs, tm)
    ys = _moe(xs, tile_expert, n_used, w_gate[l], w_up[l], w_down[l], tm)

    y_p = _finish(x1_p, route_p, mod_p.reshape(bp, 1, -1), final_g, ys, pos_p)
    y_s = _finish(x1_s.reshape(1, bs, d), route_s, mod_s.reshape(1, bs, -1), final_g, ys,
                  pos_s).reshape(bs, 1, d)

    conv_s = jnp.concatenate([cache_conv[l][:, 1:, :], u_s[:, None, :]], axis=1)
    return (y_p, y_s,
            c_p, n_pr, m_p[:, 0, :N_HEADS].reshape(1, bp, N_HEADS), cv_p,
            c_s[None], n_s.reshape(1, bs, N_HEADS, dk), m_s[:, :N_HEADS].reshape(1, bs, N_HEADS),
            conv_s[None])
```

```python
import functools

import jax
import jax.numpy as jnp
from jax import lax
from jax.experimental import pallas as pl
from jax.experimental.pallas import tpu as pltpu

F32 = jnp.float32
BF16 = jnp.bfloat16
I32 = jnp.int32

EPS = 1e-6
LANES = 128
ROW_SUB = 8
CHUNK = 128
N_HEADS = 4
N_GROUPS = 4
EXP_PER_GROUP = 8
N_EXPERTS = N_GROUPS * EXP_PER_GROUP
CONV_W = 31
CONV_PAD = 32
CONV_OFF = CONV_PAD - (CONV_W - 1)
MIX_TILE = 256
MOE_TILE = 256
PERM_TILE = 512
VMEM_LIMIT = 56 * 1024 * 1024


def _sigmoid(x):
    return 1.0 / (1.0 + jnp.exp(-x))


def _silu(x):
    return x * _sigmoid(x)


def _log_sigmoid(x):
    return jnp.minimum(x, 0.0) - jnp.log(1.0 + jnp.exp(-jnp.abs(x)))


def _bdot(a, b):
    return jnp.dot(a.astype(BF16), b.astype(BF16), preferred_element_type=F32)


def _bdot_nt(a, b):
    return lax.dot_general(a.astype(BF16), b.astype(BF16), (((1,), (1,)), ((), ())),
                           preferred_element_type=F32)


def _split2(x):
    hi = x.astype(BF16)
    lo = (x - hi.astype(F32)).astype(BF16)
    return hi, lo


def _dot3(a, w_hi, w_lo):
    a_hi, a_lo = _split2(a)
    return (jnp.dot(a_hi, w_hi, preferred_element_type=F32)
            + jnp.dot(a_lo, w_hi, preferred_element_type=F32)
            + jnp.dot(a_hi, w_lo, preferred_element_type=F32))


def _cumsum_rows(tril_bf16, x):
    hi = x.astype(BF16)
    r1 = x - hi.astype(F32)
    mid = r1.astype(BF16)
    lo = (r1 - mid.astype(F32)).astype(BF16)
    return (jnp.dot(tril_bf16, hi, preferred_element_type=F32)
            + jnp.dot(tril_bf16, mid, preferred_element_type=F32)
            + jnp.dot(tril_bf16, lo, preferred_element_type=F32))


def _rms(x, g):
    return x * lax.rsqrt(jnp.mean(x * x, axis=-1, keepdims=True) + EPS) * g


def _layer_norm(x, g, b=None):
    mu = jnp.mean(x, axis=-1, keepdims=True)
    xc = x - mu
    var = jnp.mean(xc * xc, axis=-1, keepdims=True)
    y = xc * lax.rsqrt(var + EPS) * g
    return y if b is None else y + b


def _store_rows(ref, x):
    r = x.shape[0]
    for k in range(ROW_SUB):
        ref[pl.ds(k, r, stride=ROW_SUB), :] = x[:, k * LANES:(k + 1) * LANES]


def _load_rows(ref, r):
    return jnp.concatenate([ref[pl.ds(k, r, stride=ROW_SUB), :] for k in range(ROW_SUB)], axis=1)


def _route(logits):
    lane = lax.broadcasted_iota(I32, logits.shape, 1).astype(F32)
    neg = jnp.float32(-jnp.inf)
    big = jnp.float32(1e9)
    is_g = lane < N_GROUPS
    gl = jnp.where(is_g, logits, neg)
    gmax = jnp.max(gl, axis=1, keepdims=True)
    gsel = jnp.min(jnp.where(gl == gmax, lane, big), axis=1, keepdims=True)
    pg = 1.0 / jnp.sum(jnp.where(is_g, jnp.exp(gl - gmax), 0.0), axis=1, keepdims=True)
    lo = N_GROUPS + EXP_PER_GROUP * gsel
    emask = (lane >= lo) & (lane < lo + EXP_PER_GROUP)
    el = jnp.where(emask, logits, neg)
    v1 = jnp.max(el, axis=1, keepdims=True)
    i1 = jnp.min(jnp.where(el == v1, lane, big), axis=1, keepdims=True)
    el2 = jnp.where(lane == i1, neg, el)
    v2 = jnp.max(el2, axis=1, keepdims=True)
    i2 = jnp.min(jnp.where(el2 == v2, lane, big), axis=1, keepdims=True)
    d = jnp.exp(v2 - v1)
    w1 = pg / (1.0 + d)
    w2 = pg * d / (1.0 + d)
    return jnp.where(lane == 0, i1 - N_GROUPS,
                     jnp.where(lane == 1, i2 - N_GROUPS,
                               jnp.where(lane == 2, w1, jnp.where(lane == 3, w2, 0.0))))


def _ada_kernel(c_ref, w_ref, b_ref, o_ref):
    w_hi, w_lo = _split2(w_ref[...])
    o_ref[...] = _dot3(_silu(c_ref[...]), w_hi, w_lo) + b_ref[...]


def _ada(c_all, w_ada, b_ada):
    rows, d = c_all.shape
    n_out = w_ada.shape[1]
    blk = 1024
    return pl.pallas_call(
        _ada_kernel,
        grid=(n_out // blk,),
        in_specs=[pl.BlockSpec((rows, d), lambda j: (0, 0)),
                  pl.BlockSpec((d, blk), lambda j: (0, j)),
                  pl.BlockSpec((1, blk), lambda j: (0, j))],
        out_specs=pl.BlockSpec((rows, blk), lambda j: (0, j)),
        out_shape=jax.ShapeDtypeStruct((rows, n_out), F32),
        compiler_params=pltpu.CompilerParams(dimension_semantics=("arbitrary",),
                                             vmem_limit_bytes=VMEM_LIMIT),
        name="ada",
    )(c_all, w_ada, b_ada.reshape(1, n_out))


def _post(x, attn_cat, mod, g2, wout, wr_hi, wr_lo, rbias, d, wout_lo=None):
    gate1 = mod[:, 2 * d:3 * d]
    sh2 = mod[:, 3 * d:4 * d]
    sc2 = mod[:, 4 * d:5 * d]
    if wout_lo is None:
        proj = jnp.dot(attn_cat.astype(BF16), wout, preferred_element_type=F32)
    else:
        proj = _dot3(attn_cat, wout, wout_lo)
    x1 = x + gate1 * proj
    h2 = _rms(x1, g2) * (1.0 + sc2) + sh2
    logits = _dot3(h2, wr_hi, wr_lo) + rbias
    return x1, h2, _route(logits)


def _mix_prompt_kernel(x_ref, mod_ref, g1_ref, g2_ref, wmain_ref, wgh_ref, wgl_ref, gbias_ref,
                       mhg_ref, wdw_ref, bdw_ref, clg_ref, clb_ref, wout_ref, wrh_ref, wrl_ref,
                       rbias_ref,
                       x1_ref, h2_ref, route_ref, c_ref, n_ref, m_ref, cv_ref,
                       ubuf, yc_s, q_s, k_s, v_s, so_s, hm_s, p_s, u_s, cm_s, nb_s, m_s,
                       *, tt, d, dm, dk):
    t = pl.program_id(1)
    dc = d - dm

    @pl.when(t == 0)
    def _():
        c_ref[...] = jnp.zeros_like(c_ref)
        nb_s[...] = jnp.zeros_like(nb_s)
        m_s[...] = jnp.zeros_like(m_s)
        ubuf[0:CONV_PAD, :] = jnp.zeros((CONV_PAD, dc), F32)
        ubuf[tt + CONV_PAD:tt + CONV_PAD + 8, :] = jnp.zeros((8, dc), F32)

    x = x_ref[0]
    mod = mod_ref[0]
    h = _rms(x, g1_ref[...]) * (1.0 + mod[:, d:2 * d]) + mod[:, 0:d]
    hb = h.astype(BF16)
    gates = _dot3(h, wgh_ref[...], wgl_ref[...]) + gbias_ref[...]

    def proj(lo, hi):
        return jnp.dot(hb, wmain_ref[:, lo:hi], preferred_element_type=F32)

    q_s[...] = proj(0, dm).astype(BF16)
    k_s[...] = proj(dm, 2 * dm) * (dk ** -0.5)
    v_s[...] = proj(2 * dm, 3 * dm).astype(BF16)
    so_s[...] = _sigmoid(proj(3 * dm, 4 * dm))

    ubuf[CONV_PAD:CONV_PAD + tt, :] = proj(4 * dm, 4 * dm + dc) * _sigmoid(proj(4 * dm + dc, 4 * dm + 2 * dc))
    row8 = lax.broadcasted_iota(I32, (8, LANES), 0)
    n_blk = CONV_PAD // 8 + 1

    for lt in range(dc // LANES):
        ls = slice(lt * LANES, (lt + 1) * LANES)
        wrows = [jnp.broadcast_to(wdw_ref[j:j + 1, ls], (8, LANES)) for j in range(CONV_W)]
        bias = jnp.broadcast_to(bdw_ref[:, ls], (8, LANES))

        def partial_sums(r0, ls=ls, wrows=wrows):
            blocks = [ubuf[pl.ds(pl.multiple_of(r0 + 8 * a, 8), 8), ls] for a in range(n_blk)]
            sums = []
            for s in range(8):
                acc = None
                for a in range(n_blk):
                    j = 8 * a + s - CONV_OFF
                    if 0 <= j < CONV_W:
                        term = blocks[a] * wrows[j]
                        acc = term if acc is None else acc + term
                sums.append(acc)
            return tuple(sums)

        def conv_rows(i, q_prev, ls=ls, bias=bias, partial_sums=partial_sums):
            r0 = pl.multiple_of(i * 8, 8)
            q_cur = partial_sums(r0)
            y = bias + q_prev[0]
            for s in range(1, 8):
                merged = jnp.where(row8 < s, q_cur[s], q_prev[s])
                y = y + pltpu.roll(merged, 8 - s, 0)
            yc_s[pl.ds(pl.multiple_of(r0 - 8, 8), 8), ls] = y
            return q_cur

        lax.fori_loop(1, tt // 8 + 1, conv_rows, partial_sums(0))

    hc = _silu(_layer_norm(yc_s[...], clg_ref[...], clb_ref[...]))
    cv_ref[0, 0] = ubuf[tt + CONV_PAD - (CONV_W - 1):tt + CONV_PAD, :]
    ubuf[0:CONV_PAD, :] = ubuf[tt:tt + CONV_PAD, :]

    row = lax.broadcasted_iota(I32, (CHUNK, CHUNK), 0)
    col = lax.broadcasted_iota(I32, (CHUNK, CHUNK), 1)
    causal = col <= row
    tril = causal.astype(BF16)
    lane = lax.broadcasted_iota(I32, (CHUNK, LANES), 1)
    neg = jnp.float32(-jnp.inf)
    ones_b = jnp.ones((CHUNK, dk), BF16)
    n_ch = tt // CHUNK
    b_cols = {}
    for c in range(n_ch):
        r0 = c * CHUNK
        gt = gates[r0:r0 + CHUNK, :]
        bsum = _cumsum_rows(tril, _log_sigmoid(gt))
        pk = jnp.where(lane < N_HEADS, gt - pltpu.roll(bsum, LANES - N_HEADS, 1), bsum)
        pk_t = pk.T
        for hd in range(N_HEADS):
            cs = slice(hd * dk, (hd + 1) * dk)
            idx = c * N_HEADS + hd
            kf = k_s[r0:r0 + CHUNK, cs]
            va = jnp.concatenate([v_s[r0:r0 + CHUNK, cs], ones_b], axis=1)
            gm = jnp.where(causal, pk_t[hd:hd + 1, :], neg)
            cm = jnp.max(gm, axis=1, keepdims=True)
            s = _bdot_nt(q_s[r0:r0 + CHUNK, cs], kf) * jnp.exp(gm - cm)
            p_s[idx] = jnp.dot(s.astype(BF16), va, preferred_element_type=F32)
            cm_s[idx] = jnp.broadcast_to(cm, (CHUNK, LANES))
            kw = kf * jnp.exp(pk[:, hd:hd + 1] - cm[CHUNK - 1:CHUNK, :])
            u_s[idx] = jnp.dot(kw.T.astype(BF16), va, preferred_element_type=F32)
            b_cols[idx] = pk[:, N_HEADS + hd:N_HEADS + hd + 1]

    for hd in range(N_HEADS):
        cs = slice(hd * dk, (hd + 1) * dk)
        ca = jnp.concatenate([c_ref[0, 0, hd], nb_s[hd]], axis=1)
        m_prev = m_s[hd:hd + 1, :]
        for c in range(n_ch):
            r0 = c * CHUNK
            idx = c * N_HEADS + hd
            cm = cm_s[idx]
            b_col = b_cols[idx]
            mt = jnp.maximum(m_prev, cm)
            f_loc = jnp.exp(cm - mt)
            a_int = jnp.exp(m_prev - mt)
            qc = jnp.dot(q_s[r0:r0 + CHUNK, cs], ca.astype(BF16), preferred_element_type=F32)
            p = p_s[idx]
            num = f_loc * p[:, :dk] + a_int * qc[:, :dk]
            den = f_loc * p[:, dk:] + a_int * qc[:, dk:]
            hh = num / jnp.maximum(jnp.abs(den), jnp.exp(-(b_col + mt)))
            hm_s[r0:r0 + CHUNK, cs] = (_layer_norm(hh, mhg_ref[hd:hd + 1, :])
                                       * so_s[r0:r0 + CHUNK, cs])
            mt_l = mt[CHUNK - 1:CHUNK, :]
            u = u_s[idx]
            f_l = f_loc[CHUNK - 1:CHUNK, :]
            a_l = a_int[CHUNK - 1:CHUNK, :]
            ca = jnp.concatenate([a_l * ca[:, :dk] + f_l * u[:, :dk],
                                  a_l * ca[:, dk:] + f_l * u[:, dk:]], axis=1)
            m_prev = b_col[CHUNK - 1:CHUNK, :] + mt_l
        c_ref[0, 0, hd] = ca[:, :dk]
        nb_s[hd] = ca[:, dk:]
        m_s[hd:hd + 1, :] = m_prev

    @pl.when(t == pl.num_programs(1) - 1)
    def _():
        for hd in range(N_HEADS):
            n_ref[0, 0, hd:hd + 1, :] = nb_s[hd].T[0:1, :]

    lane1 = lax.broadcasted_iota(I32, (1, LANES), 1)
    m_row = jnp.zeros((1, LANES), F32)
    for hd in range(N_HEADS):
        m_row = jnp.where(lane1 == hd, m_s[hd:hd + 1, :], m_row)
    m_ref[0] = m_row

    cat = jnp.concatenate([hm_s[...], hc], axis=1)
    x1, h2, route = _post(x, cat, mod, g2_ref[...], wout_ref[...], wrh_ref[...], wrl_ref[...],
                          rbias_ref[...], d)
    x1_ref[0] = x1
    _store_rows(h2_ref, h2)
    route_ref[...] = route


def _const_spec(shape):
    nd = len(shape)
    return pl.BlockSpec(shape, lambda *_: (0,) * nd)


def _mix_prompt(x, mod, wts):
    b, t, d = x.shape
    dm = wts["dm"]
    dk = dm // N_HEADS
    dc = d - dm
    tt = min(MIX_TILE, t)
    assert t % tt == 0 and tt % CHUNK == 0 and tt >= CONV_PAD
    nt = t // tt
    kern = functools.partial(_mix_prompt_kernel, tt=tt, d=d, dm=dm, dk=dk)
    const_names = ["g1", "g2", "wmain", "wg_hi", "wg_lo", "gbias", "mhg", "wdw", "bdw", "clg",
                   "clb", "wout", "wr_hi", "wr_lo", "rbias"]
    consts = [wts[k] for k in const_names]
    in_specs = ([pl.BlockSpec((1, tt, d), lambda i, j: (i, j, 0)),
                 pl.BlockSpec((1, 1, mod.shape[-1]), lambda i, j: (i, 0, 0))]
                + [_const_spec(c.shape) for c in consts])
    out_shape = [
        jax.ShapeDtypeStruct((b, t, d), F32),
        jax.ShapeDtypeStruct((b * t * ROW_SUB, LANES), F32),
        jax.ShapeDtypeStruct((b * t, LANES), F32),
        jax.ShapeDtypeStruct((1, b, N_HEADS, dk, dk), F32),
        jax.ShapeDtypeStruct((1, b, N_HEADS, dk), F32),
        jax.ShapeDtypeStruct((b, 1, LANES), F32),
        jax.ShapeDtypeStruct((1, b, CONV_W - 1, dc), F32),
    ]
    out_specs = [
        pl.BlockSpec((1, tt, d), lambda i, j: (i, j, 0)),
        pl.BlockSpec((tt * ROW_SUB, LANES), lambda i, j: (i * nt + j, 0)),
        pl.BlockSpec((tt, LANES), lambda i, j: (i * nt + j, 0)),
        pl.BlockSpec((1, 1, N_HEADS, dk, dk), lambda i, j: (0, i, 0, 0, 0)),
        pl.BlockSpec((1, 1, N_HEADS, dk), lambda i, j: (0, i, 0, 0)),
        pl.BlockSpec((1, 1, LANES), lambda i, j: (i, 0, 0)),
        pl.BlockSpec((1, 1, CONV_W - 1, dc), lambda i, j: (0, i, 0, 0)),
    ]
    n_hc = (tt // CHUNK) * N_HEADS
    scratch = [pltpu.VMEM((tt + CONV_PAD + 8, dc), F32),
               pltpu.VMEM((tt, dc), F32),
               pltpu.VMEM((tt, dm), BF16),
               pltpu.VMEM((tt, dm), F32),
               pltpu.VMEM((tt, dm), BF16),
               pltpu.VMEM((tt, dm), F32),
               pltpu.VMEM((tt, dm), F32),
               pltpu.VMEM((n_hc, CHUNK, 2 * dk), F32),
               pltpu.VMEM((n_hc, CHUNK, 2 * dk), F32),
               pltpu.VMEM((n_hc, CHUNK, LANES), F32),
               pltpu.VMEM((N_HEADS, dk, LANES), F32),
               pltpu.VMEM((8, LANES), F32)]
    return pl.pallas_call(
        kern, grid=(b, nt), in_specs=in_specs, out_specs=out_specs, out_shape=out_shape,
        scratch_shapes=scratch,
        compiler_params=pltpu.CompilerParams(dimension_semantics=("arbitrary", "arbitrary"),
                                             vmem_limit_bytes=VMEM_LIMIT),
        name="mix_p",
    )(x, mod.reshape(b, 1, -1), *consts)


def _s_pre_kernel(x_ref, mod_ref, g1_ref, wmain_ref, wmainlo_ref, wgh_ref, wgl_ref, gbias_ref,
                  wkt_ref, wktlo_ref, wdw_ref, bdw_ref, clg_ref, clb_ref, cache_ref, n0_ref, m0_ref,
                  q_ref, kt_ref, vs_ref, ab_ref, sv_ref, den_ref, eb_ref, o_ref, hc_ref, u_ref,
                  n_ref, m_ref, *, d, dm, dk):
    dc = d - dm
    x = x_ref[...]
    mod = mod_ref[...]
    sh1 = mod[:, 0:d]
    sc1 = mod[:, d:2 * d]
    h = _rms(x, g1_ref[...]) * (1.0 + sc1) + sh1
    z = _dot3(h, wmain_ref[...], wmainlo_ref[...])
    gates = _dot3(h, wgh_ref[...], wgl_ref[...]) + gbias_ref[...]
    scale = dk ** -0.5
    h_hi, h_lo = _split2(h)
    kt = _bdot_nt(wkt_ref[...], h_hi) + _bdot_nt(wktlo_ref[...], h_hi) + _bdot_nt(wkt_ref[...], h_lo)
    kt_ref[...] = (kt * scale).astype(BF16)
    k_all = z[:, dm:2 * dm] * scale
    ga = z[:, 4 * dm:4 * dm + dc]
    gb = z[:, 4 * dm + dc:4 * dm + 2 * dc]
    u = ga * _sigmoid(gb)
    u_ref[...] = u
    acc = jnp.broadcast_to(bdw_ref[...], u.shape) + u * wdw_ref[CONV_W - 1:CONV_W, :]
    for j in range(CONV_W - 1):
        acc = acc + cache_ref[j] * wdw_ref[j:j + 1, :]
    hc_ref[...] = _silu(_layer_norm(acc, clg_ref[...], clb_ref[...]))
    o_ref[...] = z[:, 3 * dm:4 * dm]
    q_ref[...] = z[:, 0:dm]
    m0 = m0_ref[...]
    n0 = n0_ref[...]
    lane1 = lax.broadcasted_iota(I32, (1, LANES), 1)
    m_new = jnp.zeros(m0.shape, F32)
    for hd in range(N_HEADS):
        cs = slice(hd * dk, (hd + 1) * dk)
        ig = gates[:, hd:hd + 1]
        lf = _log_sigmoid(gates[:, N_HEADS + hd:N_HEADS + hd + 1])
        mp = m0[:, hd:hd + 1]
        inter = lf + mp
        mt = jnp.maximum(inter, ig)
        w = jnp.exp(ig - mt)
        a_int = jnp.exp(inter - mt)
        qf = z[:, cs]
        kf = k_all[:, cs]
        vf = z[:, 2 * dm + hd * dk:2 * dm + (hd + 1) * dk]
        s = jnp.sum(qf * kf, axis=1, keepdims=True) * w
        sv_ref[:, cs] = s * vf
        den_ref[:, cs] = jnp.broadcast_to(
            s + a_int * jnp.sum(qf * n0[:, cs], axis=1, keepdims=True), (x.shape[0], dk))
        eb_ref[:, cs] = jnp.broadcast_to(jnp.exp(-mt), (x.shape[0], dk))
        ab_ref[:, cs] = jnp.broadcast_to(a_int, (x.shape[0], dk))
        vs_ref[:, cs] = (vf * w).astype(BF16)
        n_ref[:, cs] = a_int * n0[:, cs] + w * kf
        m_new = jnp.where(lane1 == hd, mt, m_new)
    m_ref[...] = m_new


def _s_state_kernel(q_ref, kt_ref, vs_ref, ab_ref, c0_ref, c_ref, r_ref, *, bb, dk):
    i = pl.program_id(0)
    nb = q_ref.shape[0]
    rows = lax.broadcasted_iota(I32, (nb, dk), 0)

    @pl.when(i == 0)
    def _():
        r_ref[...] = jnp.zeros_like(r_ref)

    a_blk = ab_ref[pl.ds(pl.multiple_of(i * bb, bb), bb), :]
    for j in range(bb):
        sel = rows == i * bb + j
        for hd in range(N_HEADS):
            cs = slice(hd * dk, (hd + 1) * dk)
            c0 = c0_ref[j, hd]
            vmask = jnp.where(sel, vs_ref[:, cs], jnp.zeros((), BF16))
            c_ref[j, hd] = (a_blk[j:j + 1, cs] * c0
                            + jnp.dot(kt_ref[cs, :], vmask, preferred_element_type=F32))
            c_hi, c_lo = _split2(c0)
            q_hi, q_lo = _split2(q_ref[:, cs])
            r = (jnp.dot(q_hi, c_hi, preferred_element_type=F32)
                 + jnp.dot(q_lo, c_hi, preferred_element_type=F32)
                 + jnp.dot(q_hi, c_lo, preferred_element_type=F32))
            r_ref[:, cs] = r_ref[:, cs] + jnp.where(sel, r, 0.0)


def _s_post_kernel(x_ref, mod_ref, g2_ref, mhg_ref, r_ref, ab_ref, sv_ref, den_ref, eb_ref, o_ref,
                   hc_ref, wout_ref, woutlo_ref, wrh_ref, wrl_ref, rbias_ref,
                   x1_ref, h2_ref, route_ref, *, d, dm, dk):
    hm = []
    for hd in range(N_HEADS):
        cs = slice(hd * dk, (hd + 1) * dk)
        num = sv_ref[:, cs] + ab_ref[:, cs] * r_ref[:, cs]
        hh = num / jnp.maximum(jnp.abs(den_ref[:, cs]), eb_ref[:, cs])
        hm.append(_layer_norm(hh, mhg_ref[hd:hd + 1, :]) * _sigmoid(o_ref[:, cs]))
    cat = jnp.concatenate(hm + [hc_ref[...]], axis=1)
    x1, h2, route = _post(x_ref[...], cat, mod_ref[...], g2_ref[...], wout_ref[...], wrh_ref[...],
                          wrl_ref[...], rbias_ref[...], d, wout_lo=woutlo_ref[...])
    x1_ref[...] = x1
    _store_rows(h2_ref, h2)
    route_ref[...] = route


def _mix_sample(x, mod, c0, n0, m0, cache, wts):
    nb, d = x.shape
    dm = wts["dm"]
    dk = dm // N_HEADS
    dc = d - dm
    cp = pltpu.CompilerParams(dimension_semantics=("arbitrary",), vmem_limit_bytes=VMEM_LIMIT)
    cache_t = jnp.transpose(cache, (1, 0, 2))
    m0p = jnp.pad(m0, ((0, 0), (0, LANES - N_HEADS)))
    pre_in = [x, mod, wts["g1"], wts["wmain"], wts["wmain_lo"], wts["wg_hi"], wts["wg_lo"],
              wts["gbias"], wts["wk_t"], wts["wk_t_lo"], wts["wdw"], wts["bdw"], wts["clg"],
              wts["clb"], cache_t, n0.reshape(nb, dm), m0p]
    pre_out = [jax.ShapeDtypeStruct((nb, dm), F32),
               jax.ShapeDtypeStruct((dm, nb), BF16),
               jax.ShapeDtypeStruct((nb, dm), BF16),
               jax.ShapeDtypeStruct((nb, dm), F32),
               jax.ShapeDtypeStruct((nb, dm), F32),
               jax.ShapeDtypeStruct((nb, dm), F32),
               jax.ShapeDtypeStruct((nb, dm), F32),
               jax.ShapeDtypeStruct((nb, dm), F32),
               jax.ShapeDtypeStruct((nb, dc), F32),
               jax.ShapeDtypeStruct((nb, dc), F32),
               jax.ShapeDtypeStruct((nb, dm), F32),
               jax.ShapeDtypeStruct((nb, LANES), F32)]
    (q, kt, vs, ab, sv, den, eb, o, hc, u, n1, m1) = pl.pallas_call(
        functools.partial(_s_pre_kernel, d=d, dm=dm, dk=dk),
        grid=(1,),
        in_specs=[_const_spec(a.shape) for a in pre_in],
        out_specs=[_const_spec(s.shape) for s in pre_out],
        out_shape=pre_out, compiler_params=cp, name="s_pre")(*pre_in)

    bb = 8
    assert nb % bb == 0
    c1, r = pl.pallas_call(
        functools.partial(_s_state_kernel, bb=bb, dk=dk),
        grid=(nb // bb,),
        in_specs=[_const_spec(q.shape), _const_spec(kt.shape), _const_spec(vs.shape),
                  _const_spec(ab.shape),
                  pl.BlockSpec((bb, N_HEADS, dk, dk), lambda i: (i, 0, 0, 0))],
        out_specs=[pl.BlockSpec((bb, N_HEADS, dk, dk), lambda i: (i, 0, 0, 0)),
                   _const_spec((nb, dm))],
        out_shape=[jax.ShapeDtypeStruct((nb, N_HEADS, dk, dk), F32),
                   jax.ShapeDtypeStruct((nb, dm), F32)],
        compiler_params=cp, name="s_state")(q, kt, vs, ab, c0)

    post_in = [x, mod, wts["g2"], wts["mhg"], r, ab, sv, den, eb, o, hc, wts["wout"],
               wts["wout_lo"], wts["wr_hi"], wts["wr_lo"], wts["rbias"]]
    post_out = [jax.ShapeDtypeStruct((nb, d), F32),
                jax.ShapeDtypeStruct((nb * ROW_SUB, LANES), F32),
                jax.ShapeDtypeStruct((nb, LANES), F32)]
    x1, h2, route = pl.pallas_call(
        functools.partial(_s_post_kernel, d=d, dm=dm, dk=dk),
        grid=(1,),
        in_specs=[_const_spec(a.shape) for a in post_in],
        out_specs=[_const_spec(s.shape) for s in post_out],
        out_shape=post_out, compiler_params=cp, name="s_post")(*post_in)
    return x1, h2, route, c1, n1, m1, u


def _scatter_kernel(pos_ref, zoff_ref, src_ref, *rest, tp, tm, create):
    xs_ref, zbuf, sem, zsem = rest[-4:]
    i = pl.program_id(0)

    if create:
        @pl.when(i == 0)
        def _():
            zbuf[...] = jnp.zeros_like(zbuf)

            def zero_copy(e):
                start = pl.multiple_of(zoff_ref[e] * ROW_SUB, tm * ROW_SUB)
                return pltpu.make_async_copy(zbuf, xs_ref.at[pl.ds(start, tm * ROW_SUB)], zsem)

            for e in range(zoff_ref.shape[0]):
                @pl.when(zoff_ref[e] >= 0)
                def _():
                    zero_copy(e).start()
            for e in range(zoff_ref.shape[0]):
                @pl.when(zoff_ref[e] >= 0)
                def _():
                    zero_copy(e).wait()

    def row_start(r, carry):
        src = src_ref.at[pl.ds(pl.multiple_of(r * ROW_SUB, ROW_SUB), ROW_SUB)]
        for slot in range(2):
            dst = pl.multiple_of(pos_ref[0, slot, r] * ROW_SUB, ROW_SUB)
            pltpu.make_async_copy(src, xs_ref.at[pl.ds(dst, ROW_SUB)], sem).start()
        return carry

    lax.fori_loop(0, tp, row_start, 0, unroll=8)
    for slot in range(2):
        pltpu.make_async_copy(src_ref, xs_ref.at[pl.ds(0, tp * ROW_SUB)], sem).wait()


def _scatter_rows(h2, pos, zoff, xs_or_rows, tm):
    n, c = h2.shape[0] // ROW_SUB, LANES
    tp = min(PERM_TILE, n)
    assert n % tp == 0
    pos3 = pos.reshape(n // tp, tp, 2).transpose(0, 2, 1)
    create = isinstance(xs_or_rows, int)
    n_sorted = xs_or_rows * ROW_SUB if create else xs_or_rows.shape[0]
    in_specs = [pl.BlockSpec((1, 2, tp), lambda i: (i, 0, 0), memory_space=pltpu.SMEM),
                pl.BlockSpec(memory_space=pltpu.SMEM),
                pl.BlockSpec((tp * ROW_SUB, c), lambda i: (i, 0))]
    args = [pos3, zoff, h2]
    if not create:
        in_specs.append(pl.BlockSpec(memory_space=pl.ANY))
        args.append(xs_or_rows)
    return pl.pallas_call(
        functools.partial(_scatter_kernel, tp=tp, tm=tm, create=create),
        grid_spec=pltpu.PrefetchScalarGridSpec(
            num_scalar_prefetch=0,
            grid=(n // tp,),
            in_specs=in_specs,
            out_specs=pl.BlockSpec(memory_space=pl.ANY),
            scratch_shapes=[pltpu.VMEM((tm * ROW_SUB, c), F32), pltpu.SemaphoreType.DMA(()),
                            pltpu.SemaphoreType.DMA(())]),
        out_shape=jax.ShapeDtypeStruct((n_sorted, c), F32),
        input_output_aliases={} if create else {3: 0},
        compiler_params=pltpu.CompilerParams(dimension_semantics=("arbitrary",),
                                             vmem_limit_bytes=VMEM_LIMIT),
        name="scatter",
    )(*args)


def _moe_kernel(te_ref, nu_ref, xs_ref, wg_ref, wu_ref, wd_ref, ys_ref, wg_b, wu_b, wd_b):
    i = pl.program_id(0)
    used = i < nu_ref[0]

    @pl.when(used & ((i == 0) | (te_ref[i] != te_ref[jnp.maximum(i - 1, 0)])))
    def _():
        wg_b[...] = wg_ref[0].astype(BF16)
        wu_b[...] = wu_ref[0].astype(BF16)
        wd_b[...] = wd_ref[0].astype(BF16)

    @pl.when(used)
    def _():
        xb = _load_rows(xs_ref, xs_ref.shape[0] // ROW_SUB).astype(BF16)
        g = jnp.dot(xb, wg_b[...], preferred_element_type=F32)
        u = jnp.dot(xb, wu_b[...], preferred_element_type=F32)
        hid = (_silu(g) * u).astype(BF16)
        _store_rows(ys_ref, jnp.dot(hid, wd_b[...], preferred_element_type=F32))

    @pl.when(jnp.logical_not(used))
    def _():
        ys_ref[...] = jnp.zeros_like(ys_ref)


def _moe(xs, tile_expert, n_used, w_gate, w_up, w_down, tm):
    p, c = xs.shape[0] // ROW_SUB, LANES
    ne, d, de = w_gate.shape
    n_tiles = p // tm

    def x_map(i, te, nu):
        return (jnp.minimum(i, jnp.maximum(nu[0] - 1, 0)), 0)

    def w_map(i, te, nu):
        return (te[i], 0, 0)

    return pl.pallas_call(
        _moe_kernel,
        grid_spec=pltpu.PrefetchScalarGridSpec(
            num_scalar_prefetch=2,
            grid=(n_tiles,),
            in_specs=[pl.BlockSpec((tm * ROW_SUB, c), x_map),
                      pl.BlockSpec((1, d, de), w_map),
                      pl.BlockSpec((1, d, de), w_map),
                      pl.BlockSpec((1, de, d), w_map)],
            out_specs=pl.BlockSpec((tm * ROW_SUB, c), lambda i, te, nu: (i, 0)),
            scratch_shapes=[pltpu.VMEM((d, de), BF16), pltpu.VMEM((d, de), BF16),
                            pltpu.VMEM((de, d), BF16)]),
        out_shape=jax.ShapeDtypeStruct((p * ROW_SUB, c), F32),
        compiler_params=pltpu.CompilerParams(dimension_semantics=("arbitrary",),
                                             vmem_limit_bytes=VMEM_LIMIT),
        name="moe",
    )(tile_expert, n_used, xs, w_gate, w_up, w_down)


def _fin_kernel(pos_ref, posn_ref, x1_ref, route_ref, mod_ref, fg_ref, ys_ref, y_ref, ybuf, sem,
                *, tp, d, n_steps):
    i = pl.program_id(0)
    cur = i % 2

    def issue(p_ref, buf):
        def row_start(r, carry):
            dst = pl.ds(pl.multiple_of(r * ROW_SUB, ROW_SUB), ROW_SUB)
            for slot in range(2):
                src = pl.multiple_of(p_ref[0, slot, r] * ROW_SUB, ROW_SUB)
                pltpu.make_async_copy(ys_ref.at[pl.ds(src, ROW_SUB)],
                                      ybuf.at[buf, slot, dst], sem.at[buf]).start()
            return carry
        lax.fori_loop(0, tp, row_start, 0, unroll=8)

    @pl.when(i == 0)
    def _():
        issue(pos_ref, 0)

    @pl.when(i + 1 < n_steps)
    def _():
        issue(posn_ref, 1 - cur)

    for slot in range(2):
        pltpu.make_async_copy(ys_ref.at[pl.ds(0, tp * ROW_SUB)], ybuf.at[cur, slot],
                              sem.at[cur]).wait()
    route = route_ref[...]
    moe = (route[:, 2:3] * _load_rows(ybuf.at[cur, 0], tp)
           + route[:, 3:4] * _load_rows(ybuf.at[cur, 1], tp))
    gate2 = mod_ref[0][:, 5 * d:6 * d]
    y_ref[0] = _rms(x1_ref[0] + gate2 * moe, fg_ref[...])


def _finish(x1, route, mod, final_g, ys, pos):
    b, t, d = x1.shape
    tp = min(PERM_TILE, t)
    assert t % tp == 0
    nt = t // tp
    n_steps = b * nt
    pos3 = pos.reshape(n_steps, tp, 2).transpose(0, 2, 1)
    blk0 = 0
    if mod.shape[1] == 1:
        mod_spec = pl.BlockSpec((1, 1, 6 * d), lambda i: (i // nt, 0, 0))
    else:
        mod_spec = pl.BlockSpec((1, tp, 6 * d), lambda i: (i // nt, i % nt, 0))
    return pl.pallas_call(
        functools.partial(_fin_kernel, tp=tp, d=d, n_steps=n_steps),
        grid_spec=pltpu.PrefetchScalarGridSpec(
            num_scalar_prefetch=0,
            grid=(n_steps,),
            in_specs=[pl.BlockSpec((1, 2, tp), lambda i: (blk0 + i, 0, 0), memory_space=pltpu.SMEM),
                      pl.BlockSpec((1, 2, tp), lambda i: (blk0 + jnp.minimum(i + 1, n_steps - 1), 0, 0),
                                   memory_space=pltpu.SMEM),
                      pl.BlockSpec((1, tp, d), lambda i: (i // nt, i % nt, 0)),
                      pl.BlockSpec((tp, LANES), lambda i: (i, 0)),
                      mod_spec,
                      _const_spec((1, d)),
                      pl.BlockSpec(memory_space=pl.ANY)],
            out_specs=pl.BlockSpec((1, tp, d), lambda i: (i // nt, i % nt, 0)),
            scratch_shapes=[pltpu.VMEM((2, 2, tp * ROW_SUB, LANES), F32),
                            pltpu.SemaphoreType.DMA((2,))]),
        out_shape=jax.ShapeDtypeStruct((b, t, d), F32),
        compiler_params=pltpu.CompilerParams(dimension_semantics=("arbitrary",),
                                             vmem_limit_bytes=VMEM_LIMIT),
        name="fin",
    )(pos3, pos3, x1, route, mod, final_g.reshape(1, d), ys)


def _plan(expert_ids, tm, n_tiles):
    onehot = (expert_ids[:, None] == jnp.arange(N_EXPERTS, dtype=I32)[None, :]).astype(I32)
    csum = jnp.cumsum(onehot, axis=0)
    counts = csum[-1]
    rank = jnp.sum(csum * onehot, axis=1) - 1
    tiles_per = (counts + tm - 1) // tm
    tile_end = jnp.cumsum(tiles_per)
    seg_start = (tile_end - tiles_per) * tm
    pos = jnp.sum(onehot * seg_start[None, :], axis=1) + rank
    n_used = tile_end[-1]
    tile_ids = jnp.arange(n_tiles, dtype=I32)
    tile_expert = jnp.sum((tile_ids[:, None] >= tile_end[None, :]).astype(I32), axis=1)
    last_used = jnp.sum((n_used - 1 >= tile_end).astype(I32))
    tile_expert = jnp.where(tile_ids < n_used, tile_expert, last_used).astype(I32)
    z_expert = jnp.where((counts % tm) != 0, (tile_end - 1) * tm, -1)
    spare = n_used + jnp.arange(N_EXPERTS, dtype=I32)
    z_spare = jnp.where(spare < n_tiles, spare * tm, -1)
    zoff = jnp.concatenate([z_expert, z_spare]).astype(I32)
    return pos.astype(I32), tile_expert, n_used.reshape(1).astype(I32), zoff


def kernel(x_prompt, x_sample, c_prompt, c_sample, state_mlstm_C, state_mlstm_n, state_mlstm_m,
           cache_conv, w_ada, b_ada, norm1_g, w_in, b_igate, b_fgate, mh_norm_g, w_dw, b_dw,
           conv_ln_g, conv_ln_b, w_out, norm2_g, w_grp_router, b_grp_router, w_exp_router,
           b_exp_router, w_gate, w_up, w_down, final_g):
    depth = w_ada.shape[0]
    assert depth == 1, "one layer per step"
    bp, tp_, d = x_prompt.shape
    bs = x_sample.shape[0]
    assert x_sample.shape[1] == 1 and d == ROW_SUB * LANES
    dk = state_mlstm_C.shape[-1]
    dm = N_HEADS * dk
    dc = d - dm
    l = 0

    win = w_in[l]
    w_qkvo = win[:, 0:4 * dm]
    w_gates = win[:, 4 * dm:4 * dm + 2 * N_HEADS]
    w_glu = win[:, 4 * dm + 2 * N_HEADS:]
    wmain, wmain_lo = _split2(jnp.concatenate([w_qkvo, w_glu], axis=1))
    wout_hi, wout_lo = _split2(w_out[l])
    wkt_hi, wkt_lo = _split2(win[:, dm:2 * dm].T)
    wg_pad = jnp.pad(w_gates, ((0, 0), (0, LANES - 2 * N_HEADS)))
    wg_hi, wg_lo = _split2(wg_pad)
    gbias = jnp.pad(jnp.concatenate([b_igate[l], b_fgate[l]]), (0, LANES - 2 * N_HEADS)).reshape(1, LANES)
    w_r = jnp.concatenate([w_grp_router[l], w_exp_router[l]], axis=1)
    n_r = N_GROUPS + N_EXPERTS
    wr_hi, wr_lo = _split2(jnp.pad(w_r, ((0, 0), (0, LANES - n_r))))
    rbias = jnp.pad(jnp.concatenate([b_grp_router[l], b_exp_router[l]]), (0, LANES - n_r)).reshape(1, LANES)
    wts = dict(dm=dm, g1=norm1_g[l].reshape(1, d), g2=norm2_g[l].reshape(1, d), wmain=wmain,
               wg_hi=wg_hi, wg_lo=wg_lo, gbias=gbias, mhg=mh_norm_g[l],
               wdw=w_dw[l].reshape(CONV_W, dc), bdw=b_dw[l].reshape(1, dc),
               clg=conv_ln_g[l].reshape(1, dc), clb=conv_ln_b[l].reshape(1, dc),
               wout=wout_hi, wr_hi=wr_hi, wr_lo=wr_lo, rbias=rbias, wk_t=wkt_hi,
               wmain_lo=wmain_lo, wout_lo=wout_lo, wk_t_lo=wkt_lo)

    mod = _ada(jnp.concatenate([c_prompt, c_sample], axis=0), w_ada[l], b_ada[l])
    mod_p, mod_s = mod[:bp], mod[bp:]

    n_p = bp * tp_
    n_all = n_p + bs
    x1_p, h2_p, route_p, c_p, n_pr, m_p, cv_p = _mix_prompt(x_prompt, mod_p, wts)
    x1_s, h2_s, route_s, c_s, n_s, m_s, u_s = _mix_sample(
        x_sample.reshape(bs, d), mod_s, state_mlstm_C[l], state_mlstm_n[l], state_mlstm_m[l],
        cache_conv[l], wts)

    route = jnp.concatenate([route_p, route_s], axis=0)
    expert_ids = route[:, 0:2].astype(I32).reshape(-1)
    tm = MOE_TILE
    n_tiles = (2 * n_all) // tm + N_EXPERTS
    pos, tile_expert, n_used, zoff = _plan(expert_ids, tm, n_tiles)
    pos = pos.reshape(n_all, 2)

    pos_p, pos_s = pos[:n_p], pos[n_p:]

    xs = _scatter_rows(h2_p, pos_p, zoff, n_tiles * tm, tm)
    xs = _scatter_rows(h2_s, pos_s, zoff, xs, tm)
    ys = _moe(xs, tile_expert, n_used, w_gate[l], w_up[l], w_down[l], tm)

    y_p = _finish(x1_p, route_p, mod_p.reshape(bp, 1, -1), final_g, ys, pos_p)
    y_s = _finish(x1_s.reshape(1, bs, d), route_s, mod_s.reshape(1, bs, -1), final_g, ys,
                  pos_s).reshape(bs, 1, d)

    conv_s = jnp.concatenate([cache_conv[l][:, 1:, :], u_s[:, None, :]], axis=1)
    return (y_p, y_s,
            c_p, n_pr, m_p[:, 0, :N_HEADS].reshape(1, bp, N_HEADS), cv_p,
            c_s[None], n_s.reshape(1, bs, N_HEADS, dk), m_s[:, :N_HEADS].reshape(1, bs, N_HEADS),
            conv_s[None])
```

```python
import functools

import jax
import jax.numpy as jnp
from jax import lax
from jax.experimental import pallas as pl
from jax.experimental.pallas import tpu as pltpu

F32 = jnp.float32
BF16 = jnp.bfloat16
I32 = jnp.int32

EPS = 1e-6
LANES = 128
ROW_SUB = 8
CHUNK = 128
N_HEADS = 4
N_GROUPS = 4
EXP_PER_GROUP = 8
N_EXPERTS = N_GROUPS * EXP_PER_GROUP
CONV_W = 31
CONV_PAD = 32
CONV_OFF = CONV_PAD - (CONV_W - 1)
MIX_TILE = 512
MOE_TILE = 256
PERM_TILE = 512
VMEM_LIMIT = 56 * 1024 * 1024


def _sigmoid(x):
    return 1.0 / (1.0 + jnp.exp(-x))


def _silu(x):
    return x * _sigmoid(x)


def _log_sigmoid(x):
    return jnp.minimum(x, 0.0) - jnp.log(1.0 + jnp.exp(-jnp.abs(x)))


def _bdot(a, b):
    return jnp.dot(a.astype(BF16), b.astype(BF16), preferred_element_type=F32)


def _bdot_nt(a, b):
    return lax.dot_general(a.astype(BF16), b.astype(BF16), (((1,), (1,)), ((), ())),
                           preferred_element_type=F32)


def _split2(x):
    hi = x.astype(BF16)
    lo = (x - hi.astype(F32)).astype(BF16)
    return hi, lo


def _dot3(a, w_hi, w_lo):
    a_hi, a_lo = _split2(a)
    return (jnp.dot(a_hi, w_hi, preferred_element_type=F32)
            + jnp.dot(a_lo, w_hi, preferred_element_type=F32)
            + jnp.dot(a_hi, w_lo, preferred_element_type=F32))


def _cumsum_rows(tril_bf16, x):
    hi = x.astype(BF16)
    r1 = x - hi.astype(F32)
    mid = r1.astype(BF16)
    lo = (r1 - mid.astype(F32)).astype(BF16)
    return (jnp.dot(tril_bf16, hi, preferred_element_type=F32)
            + jnp.dot(tril_bf16, mid, preferred_element_type=F32)
            + jnp.dot(tril_bf16, lo, preferred_element_type=F32))


def _rms(x, g):
    return x * lax.rsqrt(jnp.mean(x * x, axis=-1, keepdims=True) + EPS) * g


def _layer_norm(x, g, b=None):
    mu = jnp.mean(x, axis=-1, keepdims=True)
    xc = x - mu
    var = jnp.mean(xc * xc, axis=-1, keepdims=True)
    y = xc * lax.rsqrt(var + EPS) * g
    return y if b is None else y + b


def _store_rows(ref, x):
    r = x.shape[0]
    for k in range(ROW_SUB):
        ref[pl.ds(k, r, stride=ROW_SUB), :] = x[:, k * LANES:(k + 1) * LANES]


def _load_rows(ref, r):
    return jnp.concatenate([ref[pl.ds(k, r, stride=ROW_SUB), :] for k in range(ROW_SUB)], axis=1)


def _route(logits):
    lane = lax.broadcasted_iota(I32, logits.shape, 1).astype(F32)
    neg = jnp.float32(-jnp.inf)
    big = jnp.float32(1e9)
    is_g = lane < N_GROUPS
    gl = jnp.where(is_g, logits, neg)
    gmax = jnp.max(gl, axis=1, keepdims=True)
    gsel = jnp.min(jnp.where(gl == gmax, lane, big), axis=1, keepdims=True)
    pg = 1.0 / jnp.sum(jnp.where(is_g, jnp.exp(gl - gmax), 0.0), axis=1, keepdims=True)
    lo = N_GROUPS + EXP_PER_GROUP * gsel
    emask = (lane >= lo) & (lane < lo + EXP_PER_GROUP)
    el = jnp.where(emask, logits, neg)
    v1 = jnp.max(el, axis=1, keepdims=True)
    i1 = jnp.min(jnp.where(el == v1, lane, big), axis=1, keepdims=True)
    el2 = jnp.where(lane == i1, neg, el)
    v2 = jnp.max(el2, axis=1, keepdims=True)
    i2 = jnp.min(jnp.where(el2 == v2, lane, big), axis=1, keepdims=True)
    d = jnp.exp(v2 - v1)
    w1 = pg / (1.0 + d)
    w2 = pg * d / (1.0 + d)
    return jnp.where(lane == 0, i1 - N_GROUPS,
                     jnp.where(lane == 1, i2 - N_GROUPS,
                               jnp.where(lane == 2, w1, jnp.where(lane == 3, w2, 0.0))))


def _ada_kernel(c_ref, w_ref, b_ref, o_ref):
    w_hi, w_lo = _split2(w_ref[...])
    o_ref[...] = _dot3(_silu(c_ref[...]), w_hi, w_lo) + b_ref[...]


def _ada(c_all, w_ada, b_ada):
    rows, d = c_all.shape
    n_out = w_ada.shape[1]
    blk = 1024
    return pl.pallas_call(
        _ada_kernel,
        grid=(n_out // blk,),
        in_specs=[pl.BlockSpec((rows, d), lambda j: (0, 0)),
                  pl.BlockSpec((d, blk), lambda j: (0, j)),
                  pl.BlockSpec((1, blk), lambda j: (0, j))],
        out_specs=pl.BlockSpec((rows, blk), lambda j: (0, j)),
        out_shape=jax.ShapeDtypeStruct((rows, n_out), F32),
        compiler_params=pltpu.CompilerParams(dimension_semantics=("arbitrary",),
                                             vmem_limit_bytes=VMEM_LIMIT),
        name="ada",
    )(c_all, w_ada, b_ada.reshape(1, n_out))


def _post(x, attn_cat, mod, g2, wout, wr_hi, wr_lo, rbias, d, wout_lo=None):
    gate1 = mod[:, 2 * d:3 * d]
    sh2 = mod[:, 3 * d:4 * d]
    sc2 = mod[:, 4 * d:5 * d]
    if wout_lo is None:
        proj = jnp.dot(attn_cat.astype(BF16), wout, preferred_element_type=F32)
    else:
        proj = _dot3(attn_cat, wout, wout_lo)
    x1 = x + gate1 * proj
    h2 = _rms(x1, g2) * (1.0 + sc2) + sh2
    logits = _dot3(h2, wr_hi, wr_lo) + rbias
    return x1, h2, _route(logits)


def _mix_prompt_kernel(x_ref, mod_ref, g1_ref, g2_ref, wmain_ref, wgh_ref, wgl_ref, gbias_ref,
                       mhg_ref, wdw_ref, bdw_ref, clg_ref, clb_ref, wout_ref, wrh_ref, wrl_ref,
                       rbias_ref,
                       x1_ref, h2_ref, route_ref, c_ref, n_ref, m_ref, cv_ref,
                       ubuf, yc_s, q_s, k_s, v_s, so_s, hm_s, p_s, u_s, cm_s, nb_s, m_s,
                       *, tt, d, dm, dk):
    t = pl.program_id(1)
    dc = d - dm

    @pl.when(t == 0)
    def _():
        c_ref[...] = jnp.zeros_like(c_ref)
        nb_s[...] = jnp.zeros_like(nb_s)
        m_s[...] = jnp.zeros_like(m_s)
        ubuf[0:CONV_PAD, :] = jnp.zeros((CONV_PAD, dc), F32)
        ubuf[tt + CONV_PAD:tt + CONV_PAD + 8, :] = jnp.zeros((8, dc), F32)

    x = x_ref[0]
    mod = mod_ref[0]
    h = _rms(x, g1_ref[...]) * (1.0 + mod[:, d:2 * d]) + mod[:, 0:d]
    hb = h.astype(BF16)
    gates = _dot3(h, wgh_ref[...], wgl_ref[...]) + gbias_ref[...]

    def proj(lo, hi):
        return jnp.dot(hb, wmain_ref[:, lo:hi], preferred_element_type=F32)

    q_s[...] = proj(0, dm).astype(BF16)
    k_s[...] = proj(dm, 2 * dm) * (dk ** -0.5)
    v_s[...] = proj(2 * dm, 3 * dm).astype(BF16)
    so_s[...] = _sigmoid(proj(3 * dm, 4 * dm))

    ubuf[CONV_PAD:CONV_PAD + tt, :] = proj(4 * dm, 4 * dm + dc) * _sigmoid(proj(4 * dm + dc, 4 * dm + 2 * dc))
    row8 = lax.broadcasted_iota(I32, (8, LANES), 0)
    n_blk = CONV_PAD // 8 + 1

    for lt in range(dc // LANES):
        ls = slice(lt * LANES, (lt + 1) * LANES)
        wrows = [jnp.broadcast_to(wdw_ref[j:j + 1, ls], (8, LANES)) for j in range(CONV_W)]
        bias = jnp.broadcast_to(bdw_ref[:, ls], (8, LANES))

        def partial_sums(r0, ls=ls, wrows=wrows):
            blocks = [ubuf[r0 + 8 * a:r0 + 8 * a + 8, ls] for a in range(n_blk)]
            sums = []
            for s in range(8):
                acc = None
                for a in range(n_blk):
                    j = 8 * a + s - CONV_OFF
                    if 0 <= j < CONV_W:
                        term = blocks[a] * wrows[j]
                        acc = term if acc is None else acc + term
                sums.append(acc)
            return tuple(sums)

        q_prev = partial_sums(0)
        for i in range(1, tt // 8 + 1):
            q_cur = partial_sums(i * 8)
            y = bias + q_prev[0]
            for s in range(1, 8):
                merged = jnp.where(row8 < s, q_cur[s], q_prev[s])
                y = y + pltpu.roll(merged, 8 - s, 0)
            yc_s[(i - 1) * 8:i * 8, ls] = y
            q_prev = q_cur

    hc = _silu(_layer_norm(yc_s[...], clg_ref[...], clb_ref[...]))
    cv_ref[0, 0] = ubuf[tt + CONV_PAD - (CONV_W - 1):tt + CONV_PAD, :]
    ubuf[0:CONV_PAD, :] = ubuf[tt:tt + CONV_PAD, :]

    row = lax.broadcasted_iota(I32, (CHUNK, CHUNK), 0)
    col = lax.broadcasted_iota(I32, (CHUNK, CHUNK), 1)
    causal = col <= row
    tril = causal.astype(BF16)
    lane = lax.broadcasted_iota(I32, (CHUNK, LANES), 1)
    neg = jnp.float32(-jnp.inf)
    ones_b = jnp.ones((CHUNK, dk), BF16)
    n_ch = tt // CHUNK
    b_cols = {}
    for c in range(n_ch):
        r0 = c * CHUNK
        gt = gates[r0:r0 + CHUNK, :]
        bsum = _cumsum_rows(tril, _log_sigmoid(gt))
        pk = jnp.where(lane < N_HEADS, gt - pltpu.roll(bsum, LANES - N_HEADS, 1), bsum)
        pk_t = pk.T
        for hd in range(N_HEADS):
            cs = slice(hd * dk, (hd + 1) * dk)
            idx = c * N_HEADS + hd
            kf = k_s[r0:r0 + CHUNK, cs]
            va = jnp.concatenate([v_s[r0:r0 + CHUNK, cs], ones_b], axis=1)
            gm = jnp.where(causal, pk_t[hd:hd + 1, :], neg)
            cm = jnp.max(gm, axis=1, keepdims=True)
            s = _bdot_nt(q_s[r0:r0 + CHUNK, cs], kf) * jnp.exp(gm - cm)
            p_s[idx] = jnp.dot(s.astype(BF16), va, preferred_element_type=F32)
            cm_s[idx] = jnp.broadcast_to(cm, (CHUNK, LANES))
            kw = kf * jnp.exp(pk[:, hd:hd + 1] - cm[CHUNK - 1:CHUNK, :])
            u_s[idx] = jnp.dot(kw.T.astype(BF16), va, preferred_element_type=F32)
            b_cols[idx] = pk[:, N_HEADS + hd:N_HEADS + hd + 1]

    for hd in range(N_HEADS):
        cs = slice(hd * dk, (hd + 1) * dk)
        ca = jnp.concatenate([c_ref[0, 0, hd], nb_s[hd]], axis=1)
        m_prev = m_s[hd:hd + 1, :]
        for c in range(n_ch):
            r0 = c * CHUNK
            idx = c * N_HEADS + hd
            cm = cm_s[idx]
            b_col = b_cols[idx]
            mt = jnp.maximum(m_prev, cm)
            f_loc = jnp.exp(cm - mt)
            a_int = jnp.exp(m_prev - mt)
            qc = jnp.dot(q_s[r0:r0 + CHUNK, cs], ca.astype(BF16), preferred_element_type=F32)
            p = p_s[idx]
            num = f_loc * p[:, :dk] + a_int * qc[:, :dk]
            den = f_loc * p[:, dk:] + a_int * qc[:, dk:]
            hh = num / jnp.maximum(jnp.abs(den), jnp.exp(-(b_col + mt)))
            hm_s[r0:r0 + CHUNK, cs] = (_layer_norm(hh, mhg_ref[hd:hd + 1, :])
                                       * so_s[r0:r0 + CHUNK, cs])
            mt_l = mt[CHUNK - 1:CHUNK, :]
            u = u_s[idx]
            f_l = f_loc[CHUNK - 1:CHUNK, :]
            a_l = a_int[CHUNK - 1:CHUNK, :]
            ca = jnp.concatenate([a_l * ca[:, :dk] + f_l * u[:, :dk],
                                  a_l * ca[:, dk:] + f_l * u[:, dk:]], axis=1)
            m_prev = b_col[CHUNK - 1:CHUNK, :] + mt_l
        c_ref[0, 0, hd] = ca[:, :dk]
        nb_s[hd] = ca[:, dk:]
        m_s[hd:hd + 1, :] = m_prev

    @pl.when(t == pl.num_programs(1) - 1)
    def _():
        for hd in range(N_HEADS):
            n_ref[0, 0, hd:hd + 1, :] = nb_s[hd].T[0:1, :]

    lane1 = lax.broadcasted_iota(I32, (1, LANES), 1)
    m_row = jnp.zeros((1, LANES), F32)
    for hd in range(N_HEADS):
        m_row = jnp.where(lane1 == hd, m_s[hd:hd + 1, :], m_row)
    m_ref[0] = m_row

    cat = jnp.concatenate([hm_s[...], hc], axis=1)
    x1, h2, route = _post(x, cat, mod, g2_ref[...], wout_ref[...], wrh_ref[...], wrl_ref[...],
                          rbias_ref[...], d)
    x1_ref[0] = x1
    _store_rows(h2_ref, h2)
    route_ref[...] = route


def _const_spec(shape):
    nd = len(shape)
    return pl.BlockSpec(shape, lambda *_: (0,) * nd)


def _mix_prompt(x, mod, wts):
    b, t, d = x.shape
    dm = wts["dm"]
    dk = dm // N_HEADS
    dc = d - dm
    tt = min(MIX_TILE, t)
    assert t % tt == 0 and tt % CHUNK == 0 and tt >= CONV_PAD
    nt = t // tt
    kern = functools.partial(_mix_prompt_kernel, tt=tt, d=d, dm=dm, dk=dk)
    const_names = ["g1", "g2", "wmain", "wg_hi", "wg_lo", "gbias", "mhg", "wdw", "bdw", "clg",
                   "clb", "wout", "wr_hi", "wr_lo", "rbias"]
    consts = [wts[k] for k in const_names]
    in_specs = ([pl.BlockSpec((1, tt, d), lambda i, j: (i, j, 0)),
                 pl.BlockSpec((1, 1, mod.shape[-1]), lambda i, j: (i, 0, 0))]
                + [_const_spec(c.shape) for c in consts])
    out_shape = [
        jax.ShapeDtypeStruct((b, t, d), F32),
        jax.ShapeDtypeStruct((b * t * ROW_SUB, LANES), F32),
        jax.ShapeDtypeStruct((b * t, LANES), F32),
        jax.ShapeDtypeStruct((1, b, N_HEADS, dk, dk), F32),
        jax.ShapeDtypeStruct((1, b, N_HEADS, dk), F32),
        jax.ShapeDtypeStruct((b, 1, LANES), F32),
        jax.ShapeDtypeStruct((1, b, CONV_W - 1, dc), F32),
    ]
    out_specs = [
        pl.BlockSpec((1, tt, d), lambda i, j: (i, j, 0)),
        pl.BlockSpec((tt * ROW_SUB, LANES), lambda i, j: (i * nt + j, 0)),
        pl.BlockSpec((tt, LANES), lambda i, j: (i * nt + j, 0)),
        pl.BlockSpec((1, 1, N_HEADS, dk, dk), lambda i, j: (0, i, 0, 0, 0)),
        pl.BlockSpec((1, 1, N_HEADS, dk), lambda i, j: (0, i, 0, 0)),
        pl.BlockSpec((1, 1, LANES), lambda i, j: (i, 0, 0)),
        pl.BlockSpec((1, 1, CONV_W - 1, dc), lambda i, j: (0, i, 0, 0)),
    ]
    n_hc = (tt // CHUNK) * N_HEADS
    scratch = [pltpu.VMEM((tt + CONV_PAD + 8, dc), F32),
               pltpu.VMEM((tt, dc), F32),
               pltpu.VMEM((tt, dm), BF16),
               pltpu.VMEM((tt, dm), F32),
               pltpu.VMEM((tt, dm), BF16),
               pltpu.VMEM((tt, dm), F32),
               pltpu.VMEM((tt, dm), F32),
               pltpu.VMEM((n_hc, CHUNK, 2 * dk), F32),
               pltpu.VMEM((n_hc, CHUNK, 2 * dk), F32),
               pltpu.VMEM((n_hc, CHUNK, LANES), F32),
               pltpu.VMEM((N_HEADS, dk, LANES), F32),
               pltpu.VMEM((8, LANES), F32)]
    return pl.pallas_call(
        kern, grid=(b, nt), in_specs=in_specs, out_specs=out_specs, out_shape=out_shape,
        scratch_shapes=scratch,
        compiler_params=pltpu.CompilerParams(dimension_semantics=("arbitrary", "arbitrary"),
                                             vmem_limit_bytes=VMEM_LIMIT),
        name="mix_p",
    )(x, mod.reshape(b, 1, -1), *consts)


def _s_pre_kernel(x_ref, mod_ref, g1_ref, wmain_ref, wmainlo_ref, wgh_ref, wgl_ref, gbias_ref,
                  wkt_ref, wktlo_ref, wdw_ref, bdw_ref, clg_ref, clb_ref, cache_ref, n0_ref, m0_ref,
                  q_ref, kt_ref, vs_ref, ab_ref, sv_ref, den_ref, eb_ref, o_ref, hc_ref, u_ref,
                  n_ref, m_ref, *, d, dm, dk):
    dc = d - dm
    x = x_ref[...]
    mod = mod_ref[...]
    sh1 = mod[:, 0:d]
    sc1 = mod[:, d:2 * d]
    h = _rms(x, g1_ref[...]) * (1.0 + sc1) + sh1
    z = _dot3(h, wmain_ref[...], wmainlo_ref[...])
    gates = _dot3(h, wgh_ref[...], wgl_ref[...]) + gbias_ref[...]
    scale = dk ** -0.5
    h_hi, h_lo = _split2(h)
    kt = _bdot_nt(wkt_ref[...], h_hi) + _bdot_nt(wktlo_ref[...], h_hi) + _bdot_nt(wkt_ref[...], h_lo)
    kt_ref[...] = (kt * scale).astype(BF16)
    k_all = z[:, dm:2 * dm] * scale
    ga = z[:, 4 * dm:4 * dm + dc]
    gb = z[:, 4 * dm + dc:4 * dm + 2 * dc]
    u = ga * _sigmoid(gb)
    u_ref[...] = u
    acc = jnp.broadcast_to(bdw_ref[...], u.shape) + u * wdw_ref[CONV_W - 1:CONV_W, :]
    for j in range(CONV_W - 1):
        acc = acc + cache_ref[j] * wdw_ref[j:j + 1, :]
    hc_ref[...] = _silu(_layer_norm(acc, clg_ref[...], clb_ref[...]))
    o_ref[...] = z[:, 3 * dm:4 * dm]
    q_ref[...] = z[:, 0:dm]
    m0 = m0_ref[...]
    n0 = n0_ref[...]
    lane1 = lax.broadcasted_iota(I32, (1, LANES), 1)
    m_new = jnp.zeros(m0.shape, F32)
    for hd in range(N_HEADS):
        cs = slice(hd * dk, (hd + 1) * dk)
        ig = gates[:, hd:hd + 1]
        lf = _log_sigmoid(gates[:, N_HEADS + hd:N_HEADS + hd + 1])
        mp = m0[:, hd:hd + 1]
        inter = lf + mp
        mt = jnp.maximum(inter, ig)
        w = jnp.exp(ig - mt)
        a_int = jnp.exp(inter - mt)
        qf = z[:, cs]
        kf = k_all[:, cs]
        vf = z[:, 2 * dm + hd * dk:2 * dm + (hd + 1) * dk]
        s = jnp.sum(qf * kf, axis=1, keepdims=True) * w
        sv_ref[:, cs] = s * vf
        den_ref[:, cs] = jnp.broadcast_to(
            s + a_int * jnp.sum(qf * n0[:, cs], axis=1, keepdims=True), (x.shape[0], dk))
        eb_ref[:, cs] = jnp.broadcast_to(jnp.exp(-mt), (x.shape[0], dk))
        ab_ref[:, cs] = jnp.broadcast_to(a_int, (x.shape[0], dk))
        vs_ref[:, cs] = (vf * w).astype(BF16)
        n_ref[:, cs] = a_int * n0[:, cs] + w * kf
        m_new = jnp.where(lane1 == hd, mt, m_new)
    m_ref[...] = m_new


def _s_state_kernel(q_ref, kt_ref, vs_ref, ab_ref, c0_ref, c_ref, r_ref, *, bb, dk):
    i = pl.program_id(0)
    nb = q_ref.shape[0]
    rows = lax.broadcasted_iota(I32, (nb, dk), 0)

    @pl.when(i == 0)
    def _():
        r_ref[...] = jnp.zeros_like(r_ref)

    a_blk = ab_ref[pl.ds(pl.multiple_of(i * bb, bb), bb), :]
    for j in range(bb):
        sel = rows == i * bb + j
        for hd in range(N_HEADS):
            cs = slice(hd * dk, (hd + 1) * dk)
            c0 = c0_ref[j, hd]
            vmask = jnp.where(sel, vs_ref[:, cs], jnp.zeros((), BF16))
            c_ref[j, hd] = (a_blk[j:j + 1, cs] * c0
                            + jnp.dot(kt_ref[cs, :], vmask, preferred_element_type=F32))
            c_hi, c_lo = _split2(c0)
            q_hi, q_lo = _split2(q_ref[:, cs])
            r = (jnp.dot(q_hi, c_hi, preferred_element_type=F32)
                 + jnp.dot(q_lo, c_hi, preferred_element_type=F32)
                 + jnp.dot(q_hi, c_lo, preferred_element_type=F32))
            r_ref[:, cs] = r_ref[:, cs] + jnp.where(sel, r, 0.0)


def _s_post_kernel(x_ref, mod_ref, g2_ref, mhg_ref, r_ref, ab_ref, sv_ref, den_ref, eb_ref, o_ref,
                   hc_ref, wout_ref, woutlo_ref, wrh_ref, wrl_ref, rbias_ref,
                   x1_ref, h2_ref, route_ref, *, d, dm, dk):
    hm = []
    for hd in range(N_HEADS):
        cs = slice(hd * dk, (hd + 1) * dk)
        num = sv_ref[:, cs] + ab_ref[:, cs] * r_ref[:, cs]
        hh = num / jnp.maximum(jnp.abs(den_ref[:, cs]), eb_ref[:, cs])
        hm.append(_layer_norm(hh, mhg_ref[hd:hd + 1, :]) * _sigmoid(o_ref[:, cs]))
    cat = jnp.concatenate(hm + [hc_ref[...]], axis=1)
    x1, h2, route = _post(x_ref[...], cat, mod_ref[...], g2_ref[...], wout_ref[...], wrh_ref[...],
                          wrl_ref[...], rbias_ref[...], d, wout_lo=woutlo_ref[...])
    x1_ref[...] = x1
    _store_rows(h2_ref, h2)
    route_ref[...] = route


def _mix_sample(x, mod, c0, n0, m0, cache, wts):
    nb, d = x.shape
    dm = wts["dm"]
    dk = dm // N_HEADS
    dc = d - dm
    cp = pltpu.CompilerParams(dimension_semantics=("arbitrary",), vmem_limit_bytes=VMEM_LIMIT)
    cache_t = jnp.transpose(cache, (1, 0, 2))
    m0p = jnp.pad(m0, ((0, 0), (0, LANES - N_HEADS)))
    pre_in = [x, mod, wts["g1"], wts["wmain"], wts["wmain_lo"], wts["wg_hi"], wts["wg_lo"],
              wts["gbias"], wts["wk_t"], wts["wk_t_lo"], wts["wdw"], wts["bdw"], wts["clg"],
              wts["clb"], cache_t, n0.reshape(nb, dm), m0p]
    pre_out = [jax.ShapeDtypeStruct((nb, dm), F32),
               jax.ShapeDtypeStruct((dm, nb), BF16),
               jax.ShapeDtypeStruct((nb, dm), BF16),
               jax.ShapeDtypeStruct((nb, dm), F32),
               jax.ShapeDtypeStruct((nb, dm), F32),
               jax.ShapeDtypeStruct((nb, dm), F32),
               jax.ShapeDtypeStruct((nb, dm), F32),
               jax.ShapeDtypeStruct((nb, dm), F32),
               jax.ShapeDtypeStruct((nb, dc), F32),
               jax.ShapeDtypeStruct((nb, dc), F32),
               jax.ShapeDtypeStruct((nb, dm), F32),
               jax.ShapeDtypeStruct((nb, LANES), F32)]
    (q, kt, vs, ab, sv, den, eb, o, hc, u, n1, m1) = pl.pallas_call(
        functools.partial(_s_pre_kernel, d=d, dm=dm, dk=dk),
        grid=(1,),
        in_specs=[_const_spec(a.shape) for a in pre_in],
        out_specs=[_const_spec(s.shape) for s in pre_out],
        out_shape=pre_out, compiler_params=cp, name="s_pre")(*pre_in)

    bb = 8
    assert nb % bb == 0
    c1, r = pl.pallas_call(
        functools.partial(_s_state_kernel, bb=bb, dk=dk),
        grid=(nb // bb,),
        in_specs=[_const_spec(q.shape), _const_spec(kt.shape), _const_spec(vs.shape),
                  _const_spec(ab.shape),
                  pl.BlockSpec((bb, N_HEADS, dk, dk), lambda i: (i, 0, 0, 0))],
        out_specs=[pl.BlockSpec((bb, N_HEADS, dk, dk), lambda i: (i, 0, 0, 0)),
                   _const_spec((nb, dm))],
        out_shape=[jax.ShapeDtypeStruct((nb, N_HEADS, dk, dk), F32),
                   jax.ShapeDtypeStruct((nb, dm), F32)],
        compiler_params=cp, name="s_state")(q, kt, vs, ab, c0)

    post_in = [x, mod, wts["g2"], wts["mhg"], r, ab, sv, den, eb, o, hc, wts["wout"],
               wts["wout_lo"], wts["wr_hi"], wts["wr_lo"], wts["rbias"]]
    post_out = [jax.ShapeDtypeStruct((nb, d), F32),
                jax.ShapeDtypeStruct((nb * ROW_SUB, LANES), F32),
                jax.ShapeDtypeStruct((nb, LANES), F32)]
    x1, h2, route = pl.pallas_call(
        functools.partial(_s_post_kernel, d=d, dm=dm, dk=dk),
        grid=(1,),
        in_specs=[_const_spec(a.shape) for a in post_in],
        out_specs=[_const_spec(s.shape) for s in post_out],
        out_shape=post_out, compiler_params=cp, name="s_post")(*post_in)
    return x1, h2, route, c1, n1, m1, u


def _scatter_kernel(pos_ref, zoff_ref, src_ref, *rest, tp, tm, create):
    xs_ref, zbuf, sem, zsem = rest[-4:]
    i = pl.program_id(0)

    if create:
        @pl.when(i == 0)
        def _():
            zbuf[...] = jnp.zeros_like(zbuf)

            def zero_copy(e):
                start = pl.multiple_of(zoff_ref[e] * ROW_SUB, tm * ROW_SUB)
                return pltpu.make_async_copy(zbuf, xs_ref.at[pl.ds(start, tm * ROW_SUB)], zsem)

            for e in range(zoff_ref.shape[0]):
                @pl.when(zoff_ref[e] >= 0)
                def _():
                    zero_copy(e).start()
            for e in range(zoff_ref.shape[0]):
                @pl.when(zoff_ref[e] >= 0)
                def _():
                    zero_copy(e).wait()

    def row_start(r, carry):
        src = src_ref.at[pl.ds(pl.multiple_of(r * ROW_SUB, ROW_SUB), ROW_SUB)]
        for slot in range(2):
            dst = pl.multiple_of(pos_ref[0, slot, r] * ROW_SUB, ROW_SUB)
            pltpu.make_async_copy(src, xs_ref.at[pl.ds(dst, ROW_SUB)], sem).start(priority=slot)
        return carry

    lax.fori_loop(0, tp, row_start, 0, unroll=8)
    for slot in range(2):
        pltpu.make_async_copy(src_ref, xs_ref.at[pl.ds(0, tp * ROW_SUB)], sem).wait()


def _scatter_rows(h2, pos, zoff, xs_or_rows, tm):
    n, c = h2.shape[0] // ROW_SUB, LANES
    tp = min(PERM_TILE, n)
    assert n % tp == 0
    pos3 = pos.reshape(n // tp, tp, 2).transpose(0, 2, 1)
    create = isinstance(xs_or_rows, int)
    n_sorted = xs_or_rows * ROW_SUB if create else xs_or_rows.shape[0]
    in_specs = [pl.BlockSpec((1, 2, tp), lambda i: (i, 0, 0), memory_space=pltpu.SMEM),
                pl.BlockSpec(memory_space=pltpu.SMEM),
                pl.BlockSpec((tp * ROW_SUB, c), lambda i: (i, 0))]
    args = [pos3, zoff, h2]
    if not create:
        in_specs.append(pl.BlockSpec(memory_space=pl.ANY))
        args.append(xs_or_rows)
    return pl.pallas_call(
        functools.partial(_scatter_kernel, tp=tp, tm=tm, create=create),
        grid_spec=pltpu.PrefetchScalarGridSpec(
            num_scalar_prefetch=0,
            grid=(n // tp,),
            in_specs=in_specs,
            out_specs=pl.BlockSpec(memory_space=pl.ANY),
            scratch_shapes=[pltpu.VMEM((tm * ROW_SUB, c), F32), pltpu.SemaphoreType.DMA(()),
                            pltpu.SemaphoreType.DMA(())]),
        out_shape=jax.ShapeDtypeStruct((n_sorted, c), F32),
        input_output_aliases={} if create else {3: 0},
        compiler_params=pltpu.CompilerParams(dimension_semantics=("arbitrary",),
                                             vmem_limit_bytes=VMEM_LIMIT),
        name="scatter",
    )(*args)


def _moe_kernel(te_ref, nu_ref, xs_ref, wg_ref, wu_ref, wd_ref, ys_ref, wg_b, wu_b, wd_b):
    i = pl.program_id(0)
    used = i < nu_ref[0]

    @pl.when(used & ((i == 0) | (te_ref[i] != te_ref[jnp.maximum(i - 1, 0)])))
    def _():
        wg_b[...] = wg_ref[0].astype(BF16)
        wu_b[...] = wu_ref[0].astype(BF16)
        wd_b[...] = wd_ref[0].astype(BF16)

    @pl.when(used)
    def _():
        xb = _load_rows(xs_ref, xs_ref.shape[0] // ROW_SUB).astype(BF16)
        g = jnp.dot(xb, wg_b[...], preferred_element_type=F32)
        u = jnp.dot(xb, wu_b[...], preferred_element_type=F32)
        hid = (_silu(g) * u).astype(BF16)
        _store_rows(ys_ref, jnp.dot(hid, wd_b[...], preferred_element_type=F32))

    @pl.when(jnp.logical_not(used))
    def _():
        ys_ref[...] = jnp.zeros_like(ys_ref)


def _moe(xs, tile_expert, n_used, w_gate, w_up, w_down, tm):
    p, c = xs.shape[0] // ROW_SUB, LANES
    ne, d, de = w_gate.shape
    n_tiles = p // tm

    def x_map(i, te, nu):
        return (jnp.minimum(i, jnp.maximum(nu[0] - 1, 0)), 0)

    def w_map(i, te, nu):
        return (te[i], 0, 0)

    return pl.pallas_call(
        _moe_kernel,
        grid_spec=pltpu.PrefetchScalarGridSpec(
            num_scalar_prefetch=2,
            grid=(n_tiles,),
            in_specs=[pl.BlockSpec((tm * ROW_SUB, c), x_map),
                      pl.BlockSpec((1, d, de), w_map),
                      pl.BlockSpec((1, d, de), w_map),
                      pl.BlockSpec((1, de, d), w_map)],
            out_specs=pl.BlockSpec((tm * ROW_SUB, c), lambda i, te, nu: (i, 0)),
            scratch_shapes=[pltpu.VMEM((d, de), BF16), pltpu.VMEM((d, de), BF16),
                            pltpu.VMEM((de, d), BF16)]),
        out_shape=jax.ShapeDtypeStruct((p * ROW_SUB, c), F32),
        compiler_params=pltpu.CompilerParams(dimension_semantics=("arbitrary",),
                                             vmem_limit_bytes=VMEM_LIMIT),
        name="moe",
    )(tile_expert, n_used, xs, w_gate, w_up, w_down)


def _fin_kernel(pos_ref, posn_ref, x1_ref, route_ref, mod_ref, fg_ref, ys_ref, y_ref, ybuf, sem,
                *, tp, d, n_steps):
    i = pl.program_id(0)
    cur = i % 2

    def issue(p_ref, buf):
        def row_start(r, carry):
            dst = pl.ds(pl.multiple_of(r * ROW_SUB, ROW_SUB), ROW_SUB)
            for slot in range(2):
                src = pl.multiple_of(p_ref[0, slot, r] * ROW_SUB, ROW_SUB)
                pltpu.make_async_copy(ys_ref.at[pl.ds(src, ROW_SUB)],
                                      ybuf.at[buf, slot, dst], sem.at[buf]).start(priority=slot)
            return carry
        lax.fori_loop(0, tp, row_start, 0, unroll=8)

    @pl.when(i == 0)
    def _():
        issue(pos_ref, 0)

    @pl.when(i + 1 < n_steps)
    def _():
        issue(posn_ref, 1 - cur)

    for slot in range(2):
        pltpu.make_async_copy(ys_ref.at[pl.ds(0, tp * ROW_SUB)], ybuf.at[cur, slot],
                              sem.at[cur]).wait()
    route = route_ref[...]
    moe = (route[:, 2:3] * _load_rows(ybuf.at[cur, 0], tp)
           + route[:, 3:4] * _load_rows(ybuf.at[cur, 1], tp))
    gate2 = mod_ref[0][:, 5 * d:6 * d]
    y_ref[0] = _rms(x1_ref[0] + gate2 * moe, fg_ref[...])


def _finish(x1, route, mod, final_g, ys, pos):
    b, t, d = x1.shape
    tp = min(PERM_TILE, t)
    assert t % tp == 0
    nt = t // tp
    n_steps = b * nt
    pos3 = pos.reshape(n_steps, tp, 2).transpose(0, 2, 1)
    blk0 = 0
    if mod.shape[1] == 1:
        mod_spec = pl.BlockSpec((1, 1, 6 * d), lambda i: (i // nt, 0, 0))
    else:
        mod_spec = pl.BlockSpec((1, tp, 6 * d), lambda i: (i // nt, i % nt, 0))
    return pl.pallas_call(
        functools.partial(_fin_kernel, tp=tp, d=d, n_steps=n_steps),
        grid_spec=pltpu.PrefetchScalarGridSpec(
            num_scalar_prefetch=0,
            grid=(n_steps,),
            in_specs=[pl.BlockSpec((1, 2, tp), lambda i: (blk0 + i, 0, 0), memory_space=pltpu.SMEM),
                      pl.BlockSpec((1, 2, tp), lambda i: (blk0 + jnp.minimum(i + 1, n_steps - 1), 0, 0),
                                   memory_space=pltpu.SMEM),
                      pl.BlockSpec((1, tp, d), lambda i: (i // nt, i % nt, 0)),
                      pl.BlockSpec((tp, LANES), lambda i: (i, 0)),
                      mod_spec,
                      _const_spec((1, d)),
                      pl.BlockSpec(memory_space=pl.ANY)],
            out_specs=pl.BlockSpec((1, tp, d), lambda i: (i // nt, i % nt, 0)),
            scratch_shapes=[pltpu.VMEM((2, 2, tp * ROW_SUB, LANES), F32),
                            pltpu.SemaphoreType.DMA((2,))]),
        out_shape=jax.ShapeDtypeStruct((b, t, d), F32),
        compiler_params=pltpu.CompilerParams(dimension_semantics=("arbitrary",),
                                             vmem_limit_bytes=VMEM_LIMIT),
        name="fin",
    )(pos3, pos3, x1, route, mod, final_g.reshape(1, d), ys)


def _plan(expert_ids, tm, n_tiles):
    onehot = (expert_ids[:, None] == jnp.arange(N_EXPERTS, dtype=I32)[None, :]).astype(I32)
    csum = jnp.cumsum(onehot, axis=0)
    counts = csum[-1]
    rank = jnp.sum(csum * onehot, axis=1) - 1
    tiles_per = (counts + tm - 1) // tm
    tile_end = jnp.cumsum(tiles_per)
    seg_start = (tile_end - tiles_per) * tm
    pos = jnp.sum(onehot * seg_start[None, :], axis=1) + rank
    n_used = tile_end[-1]
    tile_ids = jnp.arange(n_tiles, dtype=I32)
    tile_expert = jnp.sum((tile_ids[:, None] >= tile_end[None, :]).astype(I32), axis=1)
    last_used = jnp.sum((n_used - 1 >= tile_end).astype(I32))
    tile_expert = jnp.where(tile_ids < n_used, tile_expert, last_used).astype(I32)
    z_expert = jnp.where((counts % tm) != 0, (tile_end - 1) * tm, -1)
    spare = n_used + jnp.arange(N_EXPERTS, dtype=I32)
    z_spare = jnp.where(spare < n_tiles, spare * tm, -1)
    zoff = jnp.concatenate([z_expert, z_spare]).astype(I32)
    return pos.astype(I32), tile_expert, n_used.reshape(1).astype(I32), zoff


def kernel(x_prompt, x_sample, c_prompt, c_sample, state_mlstm_C, state_mlstm_n, state_mlstm_m,
           cache_conv, w_ada, b_ada, norm1_g, w_in, b_igate, b_fgate, mh_norm_g, w_dw, b_dw,
           conv_ln_g, conv_ln_b, w_out, norm2_g, w_grp_router, b_grp_router, w_exp_router,
           b_exp_router, w_gate, w_up, w_down, final_g):
    depth = w_ada.shape[0]
    assert depth == 1, "one layer per step"
    bp, tp_, d = x_prompt.shape
    bs = x_sample.shape[0]
    assert x_sample.shape[1] == 1 and d == ROW_SUB * LANES
    dk = state_mlstm_C.shape[-1]
    dm = N_HEADS * dk
    dc = d - dm
    l = 0

    win = w_in[l]
    w_qkvo = win[:, 0:4 * dm]
    w_gates = win[:, 4 * dm:4 * dm + 2 * N_HEADS]
    w_glu = win[:, 4 * dm + 2 * N_HEADS:]
    wmain, wmain_lo = _split2(jnp.concatenate([w_qkvo, w_glu], axis=1))
    wout_hi, wout_lo = _split2(w_out[l])
    wkt_hi, wkt_lo = _split2(win[:, dm:2 * dm].T)
    wg_pad = jnp.pad(w_gates, ((0, 0), (0, LANES - 2 * N_HEADS)))
    wg_hi, wg_lo = _split2(wg_pad)
    gbias = jnp.pad(jnp.concatenate([b_igate[l], b_fgate[l]]), (0, LANES - 2 * N_HEADS)).reshape(1, LANES)
    w_r = jnp.concatenate([w_grp_router[l], w_exp_router[l]], axis=1)
    n_r = N_GROUPS + N_EXPERTS
    wr_hi, wr_lo = _split2(jnp.pad(w_r, ((0, 0), (0, LANES - n_r))))
    rbias = jnp.pad(jnp.concatenate([b_grp_router[l], b_exp_router[l]]), (0, LANES - n_r)).reshape(1, LANES)
    wts = dict(dm=dm, g1=norm1_g[l].reshape(1, d), g2=norm2_g[l].reshape(1, d), wmain=wmain,
               wg_hi=wg_hi, wg_lo=wg_lo, gbias=gbias, mhg=mh_norm_g[l],
               wdw=w_dw[l].reshape(CONV_W, dc), bdw=b_dw[l].reshape(1, dc),
               clg=conv_ln_g[l].reshape(1, dc), clb=conv_ln_b[l].reshape(1, dc),
               wout=wout_hi, wr_hi=wr_hi, wr_lo=wr_lo, rbias=rbias, wk_t=wkt_hi,
               wmain_lo=wmain_lo, wout_lo=wout_lo, wk_t_lo=wkt_lo)

    mod = _ada(jnp.concatenate([c_prompt, c_sample], axis=0), w_ada[l], b_ada[l])
    mod_p, mod_s = mod[:bp], mod[bp:]

    n_p = bp * tp_
    n_all = n_p + bs
    x1_p, h2_p, route_p, c_p, n_pr, m_p, cv_p = _mix_prompt(x_prompt, mod_p, wts)
    x1_s, h2_s, route_s, c_s, n_s, m_s, u_s = _mix_sample(
        x_sample.reshape(bs, d), mod_s, state_mlstm_C[l], state_mlstm_n[l], state_mlstm_m[l],
        cache_conv[l], wts)

    route = jnp.concatenate([route_p, route_s], axis=0)
    expert_ids = route[:, 0:2].astype(I32).reshape(-1)
    tm = MOE_TILE
    n_tiles = (2 * n_all) // tm + N_EXPERTS
    pos, tile_expert, n_used, zoff = _plan(expert_ids, tm, n_tiles)
    pos = pos.reshape(n_all, 2)

    pos_p, pos_s = pos[:n_p], pos[n_p:]

    xs = _scatter_rows(h2_p, pos_p, zoff, n_tiles * tm, tm)
    xs = _scatter_rows(h2_s, pos_s, zoff, xs, tm)
    ys = _moe(xs, tile_expert, n_used, w_gate[l], w_up[l], w_down[l], tm)

    y_p = _finish(x1_p, route_p, mod_p.reshape(bp, 1, -1), final_g, ys, pos_p)
    y_s = _finish(x1_s.reshape(1, bs, d), route_s, mod_s.reshape(1, bs, -1), final_g, ys,
                  pos_s).reshape(bs, 1, d)

    conv_s = jnp.concatenate([cache_conv[l][:, 1:, :], u_s[:, None, :]], axis=1)
    return (y_p, y_s,
            c_p, n_pr, m_p[:, 0, :N_HEADS].reshape(1, bp, N_HEADS), cv_p,
            c_s[None], n_s.reshape(1, bs, N_HEADS, dk), m_s[:, :N_HEADS].reshape(1, bs, N_HEADS),
            conv_s[None])
```

```python
import functools

import jax
import jax.numpy as jnp
from jax import lax
from jax.experimental import pallas as pl
from jax.experimental.pallas import tpu as pltpu

F32 = jnp.float32
BF16 = jnp.bfloat16
I32 = jnp.int32

EPS = 1e-6
LANES = 128
ROW_SUB = 8
CHUNK = 128
N_HEADS = 4
N_GROUPS = 4
EXP_PER_GROUP = 8
N_EXPERTS = N_GROUPS * EXP_PER_GROUP
CONV_W = 31
CONV_PAD = 32
CONV_OFF = CONV_PAD - (CONV_W - 1)
MIX_TILE = 512
MIX_SUB = 256
MOE_TILE = 256
PERM_TILE = 512
VMEM_LIMIT = 56 * 1024 * 1024


def _sigmoid(x):
    return 1.0 / (1.0 + jnp.exp(-x))


def _silu(x):
    return x * _sigmoid(x)


def _log_sigmoid(x):
    return jnp.minimum(x, 0.0) - jnp.log(1.0 + jnp.exp(-jnp.abs(x)))


def _bdot(a, b):
    return jnp.dot(a.astype(BF16), b.astype(BF16), preferred_element_type=F32)


def _bdot_nt(a, b):
    return lax.dot_general(a.astype(BF16), b.astype(BF16), (((1,), (1,)), ((), ())),
                           preferred_element_type=F32)


def _split2(x):
    hi = x.astype(BF16)
    lo = (x - hi.astype(F32)).astype(BF16)
    return hi, lo


def _split2_glue(x):
    hi = lax.reduce_precision(x, exponent_bits=8, mantissa_bits=7)
    return hi.astype(BF16), (x - hi).astype(BF16)


def _dot3(a, w_hi, w_lo):
    a_hi, a_lo = _split2(a)
    return (jnp.dot(a_hi, w_hi, preferred_element_type=F32)
            + jnp.dot(a_lo, w_hi, preferred_element_type=F32)
            + jnp.dot(a_hi, w_lo, preferred_element_type=F32))


def _cumsum_lanes(triu_bf16, x):
    hi = x.astype(BF16)
    r1 = x - hi.astype(F32)
    mid = r1.astype(BF16)
    lo = (r1 - mid.astype(F32)).astype(BF16)
    return (jnp.dot(hi, triu_bf16, preferred_element_type=F32)
            + jnp.dot(mid, triu_bf16, preferred_element_type=F32)
            + jnp.dot(lo, triu_bf16, preferred_element_type=F32))


def _rms(x, g):
    return x * lax.rsqrt(jnp.mean(x * x, axis=-1, keepdims=True) + EPS) * g


def _layer_norm(x, g, b=None):
    mu = jnp.mean(x, axis=-1, keepdims=True)
    xc = x - mu
    var = jnp.mean(xc * xc, axis=-1, keepdims=True)
    y = xc * lax.rsqrt(var + EPS) * g
    return y if b is None else y + b


def _store_rows(ref, x, row0=0):
    r = x.shape[0]
    for k in range(ROW_SUB):
        ref[pl.ds(row0 * ROW_SUB + k, r, stride=ROW_SUB), :] = x[:, k * LANES:(k + 1) * LANES]


def _load_rows(ref, r):
    return jnp.concatenate([ref[pl.ds(k, r, stride=ROW_SUB), :] for k in range(ROW_SUB)], axis=1)


def _route(logits):
    lane = lax.broadcasted_iota(I32, logits.shape, 1).astype(F32)
    neg = jnp.float32(-jnp.inf)
    big = jnp.float32(1e9)
    is_g = lane < N_GROUPS
    gl = jnp.where(is_g, logits, neg)
    gmax = jnp.max(gl, axis=1, keepdims=True)
    gsel = jnp.min(jnp.where(gl == gmax, lane, big), axis=1, keepdims=True)
    pg = 1.0 / jnp.sum(jnp.where(is_g, jnp.exp(gl - gmax), 0.0), axis=1, keepdims=True)
    lo = N_GROUPS + EXP_PER_GROUP * gsel
    emask = (lane >= lo) & (lane < lo + EXP_PER_GROUP)
    el = jnp.where(emask, logits, neg)
    v1 = jnp.max(el, axis=1, keepdims=True)
    i1 = jnp.min(jnp.where(el == v1, lane, big), axis=1, keepdims=True)
    el2 = jnp.where(lane == i1, neg, el)
    v2 = jnp.max(el2, axis=1, keepdims=True)
    i2 = jnp.min(jnp.where(el2 == v2, lane, big), axis=1, keepdims=True)
    d = jnp.exp(v2 - v1)
    w1 = pg / (1.0 + d)
    w2 = pg * d / (1.0 + d)
    return jnp.where(lane == 0, i1 - N_GROUPS,
                     jnp.where(lane == 1, i2 - N_GROUPS,
                               jnp.where(lane == 2, w1, jnp.where(lane == 3, w2, 0.0))))


def _ada_kernel(c_ref, w_ref, b_ref, o_ref):
    w_hi, w_lo = _split2(w_ref[...])
    o_ref[...] = _dot3(_silu(c_ref[...]), w_hi, w_lo) + b_ref[...]


def _ada(c_all, w_ada, b_ada):
    rows, d = c_all.shape
    n_out = w_ada.shape[1]
    blk = 1024
    return pl.pallas_call(
        _ada_kernel,
        grid=(n_out // blk,),
        in_specs=[pl.BlockSpec((rows, d), lambda j: (0, 0)),
                  pl.BlockSpec((d, blk), lambda j: (0, j)),
                  pl.BlockSpec((1, blk), lambda j: (0, j))],
        out_specs=pl.BlockSpec((rows, blk), lambda j: (0, j)),
        out_shape=jax.ShapeDtypeStruct((rows, n_out), F32),
        compiler_params=pltpu.CompilerParams(dimension_semantics=("arbitrary",),
                                             vmem_limit_bytes=VMEM_LIMIT),
        name="ada",
    )(c_all, w_ada, b_ada.reshape(1, n_out))


def _post(x, attn_cat, mod, g2, wout, wr_hi, wr_lo, rbias, d, wout_lo=None):
    gate1 = mod[:, 2 * d:3 * d]
    sh2 = mod[:, 3 * d:4 * d]
    sc2 = mod[:, 4 * d:5 * d]
    if wout_lo is None:
        proj = jnp.dot(attn_cat.astype(BF16), wout, preferred_element_type=F32)
    else:
        proj = _dot3(attn_cat, wout, wout_lo)
    x1 = x + gate1 * proj
    h2 = _rms(x1, g2) * (1.0 + sc2) + sh2
    logits = _dot3(h2, wr_hi, wr_lo) + rbias
    return x1, h2, _route(logits)


def _mix_prompt_kernel(x_ref, mod_ref, g1_ref, g2_ref, wmain_ref, wgh_ref, wgl_ref, gbias_ref,
                       mhg_ref, wdw_ref, bdw_ref, clg_ref, clb_ref, wout_ref, wrh_ref, wrl_ref,
                       rbias_ref,
                       x1_ref, h2_ref, route_ref, c_ref, n_ref, m_ref, cv_ref,
                       ubuf, yc_s, q_s, k_s, v_s, so_s, hm_s, p_s, u_s, cm_s, nb_s, m_s,
                       *, tt, d, dm, dk):
    t = pl.program_id(1)
    dc = d - dm
    n_lt = dc // LANES
    sub = min(MIX_SUB, tt)

    @pl.when(t == 0)
    def _():
        c_ref[...] = jnp.zeros_like(c_ref)
        nb_s[...] = jnp.zeros_like(nb_s)
        m_s[...] = jnp.zeros_like(m_s)
        ubuf[0:CONV_PAD, :] = jnp.zeros((CONV_PAD, dc), F32)

    mod = mod_ref[0]
    row8 = lax.broadcasted_iota(I32, (8, LANES), 0)
    row = lax.broadcasted_iota(I32, (CHUNK, CHUNK), 0)
    col = lax.broadcasted_iota(I32, (CHUNK, CHUNK), 1)
    causal = col <= row
    triu = (row <= col).astype(BF16)
    neg = jnp.float32(-jnp.inf)
    ones_b = jnp.ones((CHUNK, dk), BF16)
    pad_rows = jnp.zeros((CHUNK - 8, LANES), F32)

    lanes_of = [slice(lt * LANES, (lt + 1) * LANES) for lt in range(n_lt)]
    wrows = [[jnp.broadcast_to(wdw_ref[j:j + 1, ls], (8, LANES)) for j in range(CONV_W)]
             for ls in lanes_of]
    bias = [jnp.broadcast_to(bdw_ref[:, ls], (8, LANES)) for ls in lanes_of]

    def partial_sums(r0, lt):
        blocks = [ubuf[r0 + 8 * a:r0 + 8 * a + 8, lanes_of[lt]] for a in range(CONV_PAD // 8)]
        sums = []
        for s in range(8):
            acc = None
            for a in range(CONV_PAD // 8):
                j = 8 * a + s - CONV_OFF
                if 0 <= j < CONV_W:
                    term = blocks[a] * wrows[lt][j]
                    acc = term if acc is None else acc + term
            sums.append(acc)
        return tuple(sums)

    q_prev = [None] * n_lt
    ca = [jnp.concatenate([c_ref[0, 0, hd], nb_s[hd]], axis=1) for hd in range(N_HEADS)]
    m_prev = [m_s[hd:hd + 1, :] for hd in range(N_HEADS)]

    n_sub = tt // sub
    gates_of = {}
    hc_of = {}

    def rows_of(sb):
        return slice(sb * sub, (sb + 1) * sub)

    def proj_items(sb):
        rs = rows_of(sb)
        st = {}

        def head():
            x = x_ref[0, rs, :]
            h = _rms(x, g1_ref[...]) * (1.0 + mod[:, d:2 * d]) + mod[:, 0:d]
            st["hb"] = h.astype(BF16)
            gates_of[sb] = _dot3(h, wgh_ref[...], wgl_ref[...]) + gbias_ref[...]

        def proj(lo, hi):
            return jnp.dot(st["hb"], wmain_ref[:, lo:hi], preferred_element_type=F32)

        def glu():
            ubuf[CONV_PAD + sb * sub:CONV_PAD + (sb + 1) * sub, :] = (
                proj(4 * dm, 4 * dm + dc) * _sigmoid(proj(4 * dm + dc, 4 * dm + 2 * dc)))

        def q():
            q_s[rs, :] = proj(0, dm).astype(BF16)

        def k():
            k_s[rs, :] = proj(dm, 2 * dm) * (dk ** -0.5)

        def v():
            v_s[rs, :] = proj(2 * dm, 3 * dm).astype(BF16)

        def o():
            so_s[rs, :] = _sigmoid(proj(3 * dm, 4 * dm))

        return [head, glu, q, k, v, o]

    def conv_items(sb):
        items = []
        for lt in range(n_lt):
            for i in range(sb * sub // 8 + 1, (sb + 1) * sub // 8 + 1):
                def block(lt=lt, i=i):
                    if i == 1:
                        q_prev[lt] = partial_sums(0, lt)
                    q_cur = partial_sums(i * 8, lt)
                    cur = ubuf[(i - 1) * 8 + CONV_PAD:i * 8 + CONV_PAD, lanes_of[lt]]
                    y = bias[lt] + q_prev[lt][0] + cur * wrows[lt][CONV_W - 1]
                    for s in range(1, 8):
                        merged = jnp.where(row8 < s, q_cur[s], q_prev[lt][s])
                        y = y + pltpu.roll(merged, 8 - s, 0)
                    yc_s[(i - 1) * 8:i * 8, lanes_of[lt]] = y
                    q_prev[lt] = q_cur
                items.append(block)
        return items

    def post_items(sb):
        def post():
            rs = rows_of(sb)
            cat = jnp.concatenate([hm_s[rs, :], hc_of[sb]], axis=1)
            x1, h2, route = _post(x_ref[0, rs, :], cat, mod, g2_ref[...], wout_ref[...],
                                  wrh_ref[...], wrl_ref[...], rbias_ref[...], d)
            x1_ref[0, rs, :] = x1
            _store_rows(h2_ref, h2, sb * sub)
            route_ref[rs, :] = route
        return [post]

    def interleave(main, side):
        gap = len(main) / (len(side) + 1)
        due, done = gap, 0
        for n, item in enumerate(main):
            item()
            while done < len(side) and n + 1 >= due:
                side[done]()
                done += 1
                due += gap
        for item in side[done:]:
            item()

    for item in proj_items(0):
        item()
    for sb in range(n_sub):
        r_lo = sb * sub
        rs = rows_of(sb)
        side = post_items(sb - 1) if sb > 0 else []
        if sb + 1 < n_sub:
            side = side + proj_items(sb + 1)
        interleave(conv_items(sb), side)
        hc_of[sb] = _silu(_layer_norm(yc_s[rs, :], clg_ref[...], clb_ref[...]))
        gates = gates_of[sb]

        chunks = range(r_lo // CHUNK, (r_lo + sub) // CHUNK)
        b_cols = {}
        for c in chunks:
            r0 = c * CHUNK
            g8 = gates[r0 - r_lo:r0 - r_lo + CHUNK, :].T[0:8, :]
            b8 = _cumsum_lanes(triu, _log_sigmoid(g8))
            pk8 = jnp.where(row8 < N_HEADS, g8 - pltpu.roll(b8, N_HEADS, 0), b8)
            pk = jnp.concatenate([pk8, pad_rows], axis=0).T
            for hd in range(N_HEADS):
                cs = slice(hd * dk, (hd + 1) * dk)
                idx = c * N_HEADS + hd
                kf = k_s[r0:r0 + CHUNK, cs]
                va = jnp.concatenate([v_s[r0:r0 + CHUNK, cs], ones_b], axis=1)
                gm = jnp.where(causal, pk8[hd:hd + 1, :], neg)
                cm = jnp.max(gm, axis=1, keepdims=True)
                s = _bdot_nt(q_s[r0:r0 + CHUNK, cs], kf) * jnp.exp(gm - cm)
                p_s[idx] = jnp.dot(s.astype(BF16), va, preferred_element_type=F32)
                cm_s[idx] = jnp.broadcast_to(cm, (CHUNK, LANES))
                kw = kf * jnp.exp(pk[:, hd:hd + 1] - cm[CHUNK - 1:CHUNK, :])
                u_s[idx] = jnp.dot(kw.T.astype(BF16), va, preferred_element_type=F32)
                b_cols[idx] = pk[:, N_HEADS + hd:N_HEADS + hd + 1]

        for hd in range(N_HEADS):
            cs = slice(hd * dk, (hd + 1) * dk)
            for c in chunks:
                r0 = c * CHUNK
                idx = c * N_HEADS + hd
                cm = cm_s[idx]
                b_col = b_cols[idx]
                mt = jnp.maximum(m_prev[hd], cm)
                f_loc = jnp.exp(cm - mt)
                a_int = jnp.exp(m_prev[hd] - mt)
                qc = jnp.dot(q_s[r0:r0 + CHUNK, cs], ca[hd].astype(BF16),
                             preferred_element_type=F32)
                p = p_s[idx]
                num = f_loc * p[:, :dk] + a_int * qc[:, :dk]
                den = f_loc * p[:, dk:] + a_int * qc[:, dk:]
                hh = num / jnp.maximum(jnp.abs(den), jnp.exp(-(b_col + mt)))
                hm_s[r0:r0 + CHUNK, cs] = (_layer_norm(hh, mhg_ref[hd:hd + 1, :])
                                           * so_s[r0:r0 + CHUNK, cs])
                mt_l = mt[CHUNK - 1:CHUNK, :]
                u = u_s[idx]
                f_l = f_loc[CHUNK - 1:CHUNK, :]
                a_l = a_int[CHUNK - 1:CHUNK, :]
                ca[hd] = jnp.concatenate([a_l * ca[hd][:, :dk] + f_l * u[:, :dk],
                                          a_l * ca[hd][:, dk:] + f_l * u[:, dk:]], axis=1)
                m_prev[hd] = b_col[CHUNK - 1:CHUNK, :] + mt_l

    for item in post_items(n_sub - 1):
        item()

    for hd in range(N_HEADS):
        c_ref[0, 0, hd] = ca[hd][:, :dk]
        nb_s[hd] = ca[hd][:, dk:]
        m_s[hd:hd + 1, :] = m_prev[hd]
    cv_ref[0, 0] = ubuf[tt + CONV_PAD - (CONV_W - 1):tt + CONV_PAD, :]
    ubuf[0:CONV_PAD, :] = ubuf[tt:tt + CONV_PAD, :]

    @pl.when(t == pl.num_programs(1) - 1)
    def _():
        for hd in range(N_HEADS):
            n_ref[0, 0, hd:hd + 1, :] = nb_s[hd].T[0:1, :]

    lane1 = lax.broadcasted_iota(I32, (1, LANES), 1)
    m_row = jnp.zeros((1, LANES), F32)
    for hd in range(N_HEADS):
        m_row = jnp.where(lane1 == hd, m_s[hd:hd + 1, :], m_row)
    m_ref[0] = m_row


def _const_spec(shape):
    nd = len(shape)
    return pl.BlockSpec(shape, lambda *_: (0,) * nd)


def _mix_prompt(x, mod, wts):
    b, t, d = x.shape
    dm = wts["dm"]
    dk = dm // N_HEADS
    dc = d - dm
    tt = min(MIX_TILE, t)
    assert t % tt == 0 and tt % CHUNK == 0 and tt >= CONV_PAD
    nt = t // tt
    kern = functools.partial(_mix_prompt_kernel, tt=tt, d=d, dm=dm, dk=dk)
    const_names = ["g1", "g2", "wmain", "wg_hi", "wg_lo", "gbias", "mhg", "wdw", "bdw", "clg",
                   "clb", "wout", "wr_hi", "wr_lo", "rbias"]
    consts = [wts[k] for k in const_names]
    in_specs = ([pl.BlockSpec((1, tt, d), lambda i, j: (i, j, 0)),
                 pl.BlockSpec((1, 1, mod.shape[-1]), lambda i, j: (i, 0, 0))]
                + [_const_spec(c.shape) for c in consts])
    out_shape = [
        jax.ShapeDtypeStruct((b, t, d), F32),
        jax.ShapeDtypeStruct((b * t * ROW_SUB, LANES), F32),
        jax.ShapeDtypeStruct((b * t, LANES), F32),
        jax.ShapeDtypeStruct((1, b, N_HEADS, dk, dk), F32),
        jax.ShapeDtypeStruct((1, b, N_HEADS, dk), F32),
        jax.ShapeDtypeStruct((b, 1, LANES), F32),
        jax.ShapeDtypeStruct((1, b, CONV_W - 1, dc), F32),
    ]
    out_specs = [
        pl.BlockSpec((1, tt, d), lambda i, j: (i, j, 0)),
        pl.BlockSpec((tt * ROW_SUB, LANES), lambda i, j: (i * nt + j, 0)),
        pl.BlockSpec((tt, LANES), lambda i, j: (i * nt + j, 0)),
        pl.BlockSpec((1, 1, N_HEADS, dk, dk), lambda i, j: (0, i, 0, 0, 0)),
        pl.BlockSpec((1, 1, N_HEADS, dk), lambda i, j: (0, i, 0, 0)),
        pl.BlockSpec((1, 1, LANES), lambda i, j: (i, 0, 0)),
        pl.BlockSpec((1, 1, CONV_W - 1, dc), lambda i, j: (0, i, 0, 0)),
    ]
    n_hc = (tt // CHUNK) * N_HEADS
    scratch = [pltpu.VMEM((tt + CONV_PAD, dc), F32),
               pltpu.VMEM((tt, dc), F32),
               pltpu.VMEM((tt, dm), BF16),
               pltpu.VMEM((tt, dm), F32),
               pltpu.VMEM((tt, dm), BF16),
               pltpu.VMEM((tt, dm), F32),
               pltpu.VMEM((tt, dm), F32),
               pltpu.VMEM((n_hc, CHUNK, 2 * dk), F32),
               pltpu.VMEM((n_hc, CHUNK, 2 * dk), F32),
               pltpu.VMEM((n_hc, CHUNK, LANES), F32),
               pltpu.VMEM((N_HEADS, dk, LANES), F32),
               pltpu.VMEM((8, LANES), F32)]
    return pl.pallas_call(
        kern, grid=(b, nt), in_specs=in_specs, out_specs=out_specs, out_shape=out_shape,
        scratch_shapes=scratch,
        compiler_params=pltpu.CompilerParams(dimension_semantics=("arbitrary", "arbitrary"),
                                             vmem_limit_bytes=VMEM_LIMIT),
        name="mix_p",
    )(x, mod.reshape(b, 1, -1), *consts)


def _s_pre_kernel(x_ref, mod_ref, g1_ref, wmain_ref, wmainlo_ref, wgh_ref, wgl_ref, gbias_ref,
                  wkt_ref, wktlo_ref, wdw_ref, bdw_ref, clg_ref, clb_ref, cache_ref, n0_ref, m0_ref,
                  q_ref, kt_ref, vs_ref, ab_ref, sv_ref, den_ref, eb_ref, o_ref, hc_ref, u_ref,
                  n_ref, m_ref, *, d, dm, dk):
    dc = d - dm
    x = x_ref[...]
    mod = mod_ref[...]
    sh1 = mod[:, 0:d]
    sc1 = mod[:, d:2 * d]
    h = _rms(x, g1_ref[...]) * (1.0 + sc1) + sh1
    z = _dot3(h, wmain_ref[...], wmainlo_ref[...])
    gates = _dot3(h, wgh_ref[...], wgl_ref[...]) + gbias_ref[...]
    scale = dk ** -0.5
    h_hi, h_lo = _split2(h)
    kt = _bdot_nt(wkt_ref[...], h_hi) + _bdot_nt(wktlo_ref[...], h_hi) + _bdot_nt(wkt_ref[...], h_lo)
    kt_ref[...] = (kt * scale).astype(BF16)
    k_all = z[:, dm:2 * dm] * scale
    ga = z[:, 4 * dm:4 * dm + dc]
    gb = z[:, 4 * dm + dc:4 * dm + 2 * dc]
    u = ga * _sigmoid(gb)
    u_ref[...] = u
    acc = jnp.broadcast_to(bdw_ref[...], u.shape) + u * wdw_ref[CONV_W - 1:CONV_W, :]
    for j in range(CONV_W - 1):
        acc = acc + cache_ref[j] * wdw_ref[j:j + 1, :]
    hc_ref[...] = _silu(_layer_norm(acc, clg_ref[...], clb_ref[...]))
    o_ref[...] = z[:, 3 * dm:4 * dm]
    q_ref[...] = z[:, 0:dm]
    m0 = m0_ref[...]
    n0 = n0_ref[...]
    lane1 = lax.broadcasted_iota(I32, (1, LANES), 1)
    m_new = jnp.zeros(m0.shape, F32)
    for hd in range(N_HEADS):
        cs = slice(hd * dk, (hd + 1) * dk)
        ig = gates[:, hd:hd + 1]
        lf = _log_sigmoid(gates[:, N_HEADS + hd:N_HEADS + hd + 1])
        mp = m0[:, hd:hd + 1]
        inter = lf + mp
        mt = jnp.maximum(inter, ig)
        w = jnp.exp(ig - mt)
        a_int = jnp.exp(inter - mt)
        qf = z[:, cs]
        kf = k_all[:, cs]
        vf = z[:, 2 * dm + hd * dk:2 * dm + (hd + 1) * dk]
        s = jnp.sum(qf * kf, axis=1, keepdims=True) * w
        sv_ref[:, cs] = s * vf
        den_ref[:, cs] = jnp.broadcast_to(
            s + a_int * jnp.sum(qf * n0[:, cs], axis=1, keepdims=True), (x.shape[0], dk))
        eb_ref[:, cs] = jnp.broadcast_to(jnp.exp(-mt), (x.shape[0], dk))
        ab_ref[:, cs] = jnp.broadcast_to(a_int, (x.shape[0], dk))
        vs_ref[:, cs] = (vf * w).astype(BF16)
        n_ref[:, cs] = a_int * n0[:, cs] + w * kf
        m_new = jnp.where(lane1 == hd, mt, m_new)
    m_ref[...] = m_new


def _s_state_kernel(q_ref, kt_ref, vs_ref, ab_ref, c0_ref, c_ref, r_ref, *, bb, dk):
    i = pl.program_id(0)
    nb = q_ref.shape[0]
    rows = lax.broadcasted_iota(I32, (nb, dk), 0)

    @pl.when(i == 0)
    def _():
        r_ref[...] = jnp.zeros_like(r_ref)

    a_blk = ab_ref[pl.ds(pl.multiple_of(i * bb, bb), bb), :]
    for j in range(bb):
        sel = rows == i * bb + j
        for hd in range(N_HEADS):
            cs = slice(hd * dk, (hd + 1) * dk)
            c0 = c0_ref[j, hd]
            vmask = jnp.where(sel, vs_ref[:, cs], jnp.zeros((), BF16))
            c_ref[j, hd] = (a_blk[j:j + 1, cs] * c0
                            + jnp.dot(kt_ref[cs, :], vmask, preferred_element_type=F32))
            c_hi, c_lo = _split2(c0)
            q_hi, q_lo = _split2(q_ref[:, cs])
            r = (jnp.dot(q_hi, c_hi, preferred_element_type=F32)
                 + jnp.dot(q_lo, c_hi, preferred_element_type=F32)
                 + jnp.dot(q_hi, c_lo, preferred_element_type=F32))
            r_ref[:, cs] = r_ref[:, cs] + jnp.where(sel, r, 0.0)


def _s_post_kernel(x_ref, mod_ref, g2_ref, mhg_ref, r_ref, ab_ref, sv_ref, den_ref, eb_ref, o_ref,
                   hc_ref, wout_ref, woutlo_ref, wrh_ref, wrl_ref, rbias_ref,
                   x1_ref, h2_ref, route_ref, *, d, dm, dk):
    hm = []
    for hd in range(N_HEADS):
        cs = slice(hd * dk, (hd + 1) * dk)
        num = sv_ref[:, cs] + ab_ref[:, cs] * r_ref[:, cs]
        hh = num / jnp.maximum(jnp.abs(den_ref[:, cs]), eb_ref[:, cs])
        hm.append(_layer_norm(hh, mhg_ref[hd:hd + 1, :]) * _sigmoid(o_ref[:, cs]))
    cat = jnp.concatenate(hm + [hc_ref[...]], axis=1)
    x1, h2, route = _post(x_ref[...], cat, mod_ref[...], g2_ref[...], wout_ref[...], wrh_ref[...],
                          wrl_ref[...], rbias_ref[...], d, wout_lo=woutlo_ref[...])
    x1_ref[...] = x1
    _store_rows(h2_ref, h2)
    route_ref[...] = route


def _mix_sample(x, mod, c0, n0, m0, cache, wts):
    nb, d = x.shape
    dm = wts["dm"]
    dk = dm // N_HEADS
    dc = d - dm
    cp = pltpu.CompilerParams(dimension_semantics=("arbitrary",), vmem_limit_bytes=VMEM_LIMIT)
    cache_t = jnp.transpose(cache, (1, 0, 2))
    m0p = jnp.pad(m0, ((0, 0), (0, LANES - N_HEADS)))
    pre_in = [x, mod, wts["g1"], wts["wmain"], wts["wmain_lo"], wts["wg_hi"], wts["wg_lo"],
              wts["gbias"], wts["wk_t"], wts["wk_t_lo"], wts["wdw"], wts["bdw"], wts["clg"],
              wts["clb"], cache_t, n0.reshape(nb, dm), m0p]
    pre_out = [jax.ShapeDtypeStruct((nb, dm), F32),
               jax.ShapeDtypeStruct((dm, nb), BF16),
               jax.ShapeDtypeStruct((nb, dm), BF16),
               jax.ShapeDtypeStruct((nb, dm), F32),
               jax.ShapeDtypeStruct((nb, dm), F32),
               jax.ShapeDtypeStruct((nb, dm), F32),
               jax.ShapeDtypeStruct((nb, dm), F32),
               jax.ShapeDtypeStruct((nb, dm), F32),
               jax.ShapeDtypeStruct((nb, dc), F32),
               jax.ShapeDtypeStruct((nb, dc), F32),
               jax.ShapeDtypeStruct((nb, dm), F32),
               jax.ShapeDtypeStruct((nb, LANES), F32)]
    (q, kt, vs, ab, sv, den, eb, o, hc, u, n1, m1) = pl.pallas_call(
        functools.partial(_s_pre_kernel, d=d, dm=dm, dk=dk),
        grid=(1,),
        in_specs=[_const_spec(a.shape) for a in pre_in],
        out_specs=[_const_spec(s.shape) for s in pre_out],
        out_shape=pre_out, compiler_params=cp, name="s_pre")(*pre_in)

    bb = 8
    assert nb % bb == 0
    c1, r = pl.pallas_call(
        functools.partial(_s_state_kernel, bb=bb, dk=dk),
        grid=(nb // bb,),
        in_specs=[_const_spec(q.shape), _const_spec(kt.shape), _const_spec(vs.shape),
                  _const_spec(ab.shape),
                  pl.BlockSpec((bb, N_HEADS, dk, dk), lambda i: (i, 0, 0, 0))],
        out_specs=[pl.BlockSpec((bb, N_HEADS, dk, dk), lambda i: (i, 0, 0, 0)),
                   _const_spec((nb, dm))],
        out_shape=[jax.ShapeDtypeStruct((nb, N_HEADS, dk, dk), F32),
                   jax.ShapeDtypeStruct((nb, dm), F32)],
        compiler_params=cp, name="s_state")(q, kt, vs, ab, c0)

    post_in = [x, mod, wts["g2"], wts["mhg"], r, ab, sv, den, eb, o, hc, wts["wout"],
               wts["wout_lo"], wts["wr_hi"], wts["wr_lo"], wts["rbias"]]
    post_out = [jax.ShapeDtypeStruct((nb, d), F32),
                jax.ShapeDtypeStruct((nb * ROW_SUB, LANES), F32),
                jax.ShapeDtypeStruct((nb, LANES), F32)]
    x1, h2, route = pl.pallas_call(
        functools.partial(_s_post_kernel, d=d, dm=dm, dk=dk),
        grid=(1,),
        in_specs=[_const_spec(a.shape) for a in post_in],
        out_specs=[_const_spec(s.shape) for s in post_out],
        out_shape=post_out, compiler_params=cp, name="s_post")(*post_in)
    return x1, h2, route, c1, n1, m1, u


def _scatter_kernel(pos_ref, zoff_ref, src_ref, *rest, tp, tm, create):
    xs_ref, zbuf, sem, zsem = rest[-4:]
    i = pl.program_id(0)

    if create:
        @pl.when(i == 0)
        def _():
            zbuf[...] = jnp.zeros_like(zbuf)

            def zero_copy(e):
                start = pl.multiple_of(zoff_ref[e] * ROW_SUB, tm * ROW_SUB)
                return pltpu.make_async_copy(zbuf, xs_ref.at[pl.ds(start, tm * ROW_SUB)], zsem)

            for e in range(zoff_ref.shape[0]):
                @pl.when(zoff_ref[e] >= 0)
                def _():
                    zero_copy(e).start()
            for e in range(zoff_ref.shape[0]):
                @pl.when(zoff_ref[e] >= 0)
                def _():
                    zero_copy(e).wait()

    def row_start(r, carry):
        src = src_ref.at[pl.ds(pl.multiple_of(r * ROW_SUB, ROW_SUB), ROW_SUB)]
        for slot in range(2):
            dst = pl.multiple_of(pos_ref[0, slot, r] * ROW_SUB, ROW_SUB)
            pltpu.make_async_copy(src, xs_ref.at[pl.ds(dst, ROW_SUB)], sem).start(priority=slot)
        return carry

    lax.fori_loop(0, tp, row_start, 0, unroll=8)
    for slot in range(2):
        pltpu.make_async_copy(src_ref, xs_ref.at[pl.ds(0, tp * ROW_SUB)], sem).wait()


def _scatter_rows(h2, pos, zoff, xs_or_rows, tm):
    n, c = h2.shape[0] // ROW_SUB, LANES
    tp = min(PERM_TILE, n)
    assert n % tp == 0
    pos3 = pos.reshape(n // tp, tp, 2).transpose(0, 2, 1)
    create = isinstance(xs_or_rows, int)
    n_sorted = xs_or_rows * ROW_SUB if create else xs_or_rows.shape[0]
    in_specs = [pl.BlockSpec((1, 2, tp), lambda i: (i, 0, 0), memory_space=pltpu.SMEM),
                pl.BlockSpec(memory_space=pltpu.SMEM),
                pl.BlockSpec((tp * ROW_SUB, c), lambda i: (i, 0))]
    args = [pos3, zoff, h2]
    if not create:
        in_specs.append(pl.BlockSpec(memory_space=pl.ANY))
        args.append(xs_or_rows)
    return pl.pallas_call(
        functools.partial(_scatter_kernel, tp=tp, tm=tm, create=create),
        grid_spec=pltpu.PrefetchScalarGridSpec(
            num_scalar_prefetch=0,
            grid=(n // tp,),
            in_specs=in_specs,
            out_specs=pl.BlockSpec(memory_space=pl.ANY),
            scratch_shapes=[pltpu.VMEM((tm * ROW_SUB, c), F32), pltpu.SemaphoreType.DMA(()),
                            pltpu.SemaphoreType.DMA(())]),
        out_shape=jax.ShapeDtypeStruct((n_sorted, c), F32),
        input_output_aliases={} if create else {3: 0},
        compiler_params=pltpu.CompilerParams(dimension_semantics=("arbitrary",),
                                             vmem_limit_bytes=VMEM_LIMIT),
        name="scatter",
    )(*args)


def _moe_kernel(te_ref, nu_ref, xs_ref, wg_ref, wu_ref, wd_ref, ys_ref, wg_b, wu_b, wd_b):
    i = pl.program_id(0)
    used = i < nu_ref[0]

    @pl.when(used & ((i == 0) | (te_ref[i] != te_ref[jnp.maximum(i - 1, 0)])))
    def _():
        wg_b[...] = wg_ref[0].astype(BF16)
        wu_b[...] = wu_ref[0].astype(BF16)
        wd_b[...] = wd_ref[0].astype(BF16)

    @pl.when(used)
    def _():
        xb = _load_rows(xs_ref, xs_ref.shape[0] // ROW_SUB).astype(BF16)
        g = jnp.dot(xb, wg_b[...], preferred_element_type=F32)
        u = jnp.dot(xb, wu_b[...], preferred_element_type=F32)
        hid = (_silu(g) * u).astype(BF16)
        _store_rows(ys_ref, jnp.dot(hid, wd_b[...], preferred_element_type=F32))

    @pl.when(jnp.logical_not(used))
    def _():
        ys_ref[...] = jnp.zeros_like(ys_ref)


def _moe(xs, tile_expert, n_used, w_gate, w_up, w_down, tm):
    p, c = xs.shape[0] // ROW_SUB, LANES
    ne, d, de = w_gate.shape
    n_tiles = p // tm

    def x_map(i, te, nu):
        return (jnp.minimum(i, jnp.maximum(nu[0] - 1, 0)), 0)

    def w_map(i, te, nu):
        return (te[i], 0, 0)

    return pl.pallas_call(
        _moe_kernel,
        grid_spec=pltpu.PrefetchScalarGridSpec(
            num_scalar_prefetch=2,
            grid=(n_tiles,),
            in_specs=[pl.BlockSpec((tm * ROW_SUB, c), x_map),
                      pl.BlockSpec((1, d, de), w_map),
                      pl.BlockSpec((1, d, de), w_map),
                      pl.BlockSpec((1, de, d), w_map)],
            out_specs=pl.BlockSpec((tm * ROW_SUB, c), lambda i, te, nu: (i, 0)),
            scratch_shapes=[pltpu.VMEM((d, de), BF16), pltpu.VMEM((d, de), BF16),
                            pltpu.VMEM((de, d), BF16)]),
        out_shape=jax.ShapeDtypeStruct((p * ROW_SUB, c), F32),
        compiler_params=pltpu.CompilerParams(dimension_semantics=("arbitrary",),
                                             vmem_limit_bytes=VMEM_LIMIT),
        name="moe",
    )(tile_expert, n_used, xs, w_gate, w_up, w_down)


def _fin_kernel(pos_ref, posn_ref, x1_ref, route_ref, mod_ref, fg_ref, ys_ref, y_ref, ybuf, sem,
                *, tp, d, n_steps):
    i = pl.program_id(0)
    cur = i % 2

    def issue(p_ref, buf):
        def row_start(r, carry):
            dst = pl.ds(pl.multiple_of(r * ROW_SUB, ROW_SUB), ROW_SUB)
            for slot in range(2):
                src = pl.multiple_of(p_ref[0, slot, r] * ROW_SUB, ROW_SUB)
                pltpu.make_async_copy(ys_ref.at[pl.ds(src, ROW_SUB)],
                                      ybuf.at[buf, slot, dst], sem.at[buf]).start(priority=slot)
            return carry
        lax.fori_loop(0, tp, row_start, 0, unroll=8)

    @pl.when(i == 0)
    def _():
        issue(pos_ref, 0)

    @pl.when(i + 1 < n_steps)
    def _():
        issue(posn_ref, 1 - cur)

    for slot in range(2):
        pltpu.make_async_copy(ys_ref.at[pl.ds(0, tp * ROW_SUB)], ybuf.at[cur, slot],
                              sem.at[cur]).wait()
    route = route_ref[...]
    moe = (route[:, 2:3] * _load_rows(ybuf.at[cur, 0], tp)
           + route[:, 3:4] * _load_rows(ybuf.at[cur, 1], tp))
    gate2 = mod_ref[0][:, 5 * d:6 * d]
    y_ref[0] = _rms(x1_ref[0] + gate2 * moe, fg_ref[...])


def _finish(x1, route, mod, final_g, ys, pos):
    b, t, d = x1.shape
    tp = min(PERM_TILE, t)
    assert t % tp == 0
    nt = t // tp
    n_steps = b * nt
    pos3 = pos.reshape(n_steps, tp, 2).transpose(0, 2, 1)
    blk0 = 0
    if mod.shape[1] == 1:
        mod_spec = pl.BlockSpec((1, 1, 6 * d), lambda i: (i // nt, 0, 0))
    else:
        mod_spec = pl.BlockSpec((1, tp, 6 * d), lambda i: (i // nt, i % nt, 0))
    return pl.pallas_call(
        functools.partial(_fin_kernel, tp=tp, d=d, n_steps=n_steps),
        grid_spec=pltpu.PrefetchScalarGridSpec(
            num_scalar_prefetch=0,
            grid=(n_steps,),
            in_specs=[pl.BlockSpec((1, 2, tp), lambda i: (blk0 + i, 0, 0), memory_space=pltpu.SMEM),
                      pl.BlockSpec((1, 2, tp), lambda i: (blk0 + jnp.minimum(i + 1, n_steps - 1), 0, 0),
                                   memory_space=pltpu.SMEM),
                      pl.BlockSpec((1, tp, d), lambda i: (i // nt, i % nt, 0)),
                      pl.BlockSpec((tp, LANES), lambda i: (i, 0)),
                      mod_spec,
                      _const_spec((1, d)),
                      pl.BlockSpec(memory_space=pl.ANY)],
            out_specs=pl.BlockSpec((1, tp, d), lambda i: (i // nt, i % nt, 0)),
            scratch_shapes=[pltpu.VMEM((2, 2, tp * ROW_SUB, LANES), F32),
                            pltpu.SemaphoreType.DMA((2,))]),
        out_shape=jax.ShapeDtypeStruct((b, t, d), F32),
        compiler_params=pltpu.CompilerParams(dimension_semantics=("arbitrary",),
                                             vmem_limit_bytes=VMEM_LIMIT),
        name="fin",
    )(pos3, pos3, x1, route, mod, final_g.reshape(1, d), ys)


def _plan(expert_ids, tm, n_tiles):
    onehot = (expert_ids[:, None] == jnp.arange(N_EXPERTS, dtype=I32)[None, :]).astype(I32)
    csum = jnp.cumsum(onehot, axis=0)
    counts = csum[-1]
    rank = jnp.sum(csum * onehot, axis=1) - 1
    tiles_per = (counts + tm - 1) // tm
    tile_end = jnp.cumsum(tiles_per)
    seg_start = (tile_end - tiles_per) * tm
    pos = jnp.sum(onehot * seg_start[None, :], axis=1) + rank
    n_used = tile_end[-1]
    tile_ids = jnp.arange(n_tiles, dtype=I32)
    tile_expert = jnp.sum((tile_ids[:, None] >= tile_end[None, :]).astype(I32), axis=1)
    last_used = jnp.sum((n_used - 1 >= tile_end).astype(I32))
    tile_expert = jnp.where(tile_ids < n_used, tile_expert, last_used).astype(I32)
    z_expert = jnp.where((counts % tm) != 0, (tile_end - 1) * tm, -1)
    spare = n_used + jnp.arange(N_EXPERTS, dtype=I32)
    z_spare = jnp.where(spare < n_tiles, spare * tm, -1)
    zoff = jnp.concatenate([z_expert, z_spare]).astype(I32)
    return pos.astype(I32), tile_expert, n_used.reshape(1).astype(I32), zoff


def kernel(x_prompt, x_sample, c_prompt, c_sample, state_mlstm_C, state_mlstm_n, state_mlstm_m,
           cache_conv, w_ada, b_ada, norm1_g, w_in, b_igate, b_fgate, mh_norm_g, w_dw, b_dw,
           conv_ln_g, conv_ln_b, w_out, norm2_g, w_grp_router, b_grp_router, w_exp_router,
           b_exp_router, w_gate, w_up, w_down, final_g):
    depth = w_ada.shape[0]
    assert depth == 1, "one layer per step"
    bp, tp_, d = x_prompt.shape
    bs = x_sample.shape[0]
    assert x_sample.shape[1] == 1 and d == ROW_SUB * LANES
    dk = state_mlstm_C.shape[-1]
    dm = N_HEADS * dk
    dc = d - dm
    l = 0

    win = w_in[l]
    w_qkvo = win[:, 0:4 * dm]
    w_gates = win[:, 4 * dm:4 * dm + 2 * N_HEADS]
    w_glu = win[:, 4 * dm + 2 * N_HEADS:]
    wmain, wmain_lo = _split2_glue(jnp.concatenate([w_qkvo, w_glu], axis=1))
    wout_hi, wout_lo = _split2_glue(w_out[l])
    wkt_hi, wkt_lo = _split2_glue(win[:, dm:2 * dm].T)
    wg_pad = jnp.pad(w_gates, ((0, 0), (0, LANES - 2 * N_HEADS)))
    wg_hi, wg_lo = _split2_glue(wg_pad)
    gbias = jnp.pad(jnp.concatenate([b_igate[l], b_fgate[l]]), (0, LANES - 2 * N_HEADS)).reshape(1, LANES)
    w_r = jnp.concatenate([w_grp_router[l], w_exp_router[l]], axis=1)
    n_r = N_GROUPS + N_EXPERTS
    wr_hi, wr_lo = _split2_glue(jnp.pad(w_r, ((0, 0), (0, LANES - n_r))))
    rbias = jnp.pad(jnp.concatenate([b_grp_router[l], b_exp_router[l]]), (0, LANES - n_r)).reshape(1, LANES)
    wts = dict(dm=dm, g1=norm1_g[l].reshape(1, d), g2=norm2_g[l].reshape(1, d), wmain=wmain,
               wg_hi=wg_hi, wg_lo=wg_lo, gbias=gbias, mhg=mh_norm_g[l],
               wdw=w_dw[l].reshape(CONV_W, dc), bdw=b_dw[l].reshape(1, dc),
               clg=conv_ln_g[l].reshape(1, dc), clb=conv_ln_b[l].reshape(1, dc),
               wout=wout_hi, wr_hi=wr_hi, wr_lo=wr_lo, rbias=rbias, wk_t=wkt_hi,
               wmain_lo=wmain_lo, wout_lo=wout_lo, wk_t_lo=wkt_lo)

    mod = _ada(jnp.concatenate([c_prompt, c_sample], axis=0), w_ada[l], b_ada[l])
    mod_p, mod_s = mod[:bp], mod[bp:]

    n_p = bp * tp_
    n_all = n_p + bs
    x1_p, h2_p, route_p, c_p, n_pr, m_p, cv_p = _mix_prompt(x_prompt, mod_p, wts)
    x1_s, h2_s, route_s, c_s, n_s, m_s, u_s = _mix_sample(
        x_sample.reshape(bs, d), mod_s, state_mlstm_C[l], state_mlstm_n[l], state_mlstm_m[l],
        cache_conv[l], wts)

    route = jnp.concatenate([route_p, route_s], axis=0)
    expert_ids = route[:, 0:2].astype(I32).reshape(-1)
    tm = MOE_TILE
    n_tiles = (2 * n_all) // tm + N_EXPERTS
    pos, tile_expert, n_used, zoff = _plan(expert_ids, tm, n_tiles)
    pos = pos.reshape(n_all, 2)

    pos_p, pos_s = pos[:n_p], pos[n_p:]

    xs = _scatter_rows(h2_p, pos_p, zoff, n_tiles * tm, tm)
    xs = _scatter_rows(h2_s, pos_s, zoff, xs, tm)
    ys = _moe(xs, tile_expert, n_used, w_gate[l], w_up[l], w_down[l], tm)

    y_p = _finish(x1_p, route_p, mod_p.reshape(bp, 1, -1), final_g, ys, pos_p)
    y_s = _finish(x1_s.reshape(1, bs, d), route_s, mod_s.reshape(1, bs, -1), final_g, ys,
                  pos_s).reshape(bs, 1, d)

    conv_s = jnp.concatenate([cache_conv[l][:, 1:, :], u_s[:, None, :]], axis=1)
    return (y_p, y_s,
            c_p, n_pr, m_p[:, 0, :N_HEADS].reshape(1, bp, N_HEADS), cv_p,
            c_s[None], n_s.reshape(1, bs, N_HEADS, dk), m_s[:, :N_HEADS].reshape(1, bs, N_HEADS),
            conv_s[None])
```

```python
import functools

import jax
import jax.numpy as jnp
from jax import lax
from jax.experimental import pallas as pl
from jax.experimental.pallas import tpu as pltpu

F32 = jnp.float32
BF16 = jnp.bfloat16
I32 = jnp.int32

EPS = 1e-6
LANES = 128
ROW_SUB = 8
CHUNK = 128
N_HEADS = 4
N_GROUPS = 4
EXP_PER_GROUP = 8
N_EXPERTS = N_GROUPS * EXP_PER_GROUP
CONV_W = 31
CONV_PAD = 32
CONV_OFF = CONV_PAD - (CONV_W - 1)
MIX_TILE = 512
MIX_SUB = 256
MOE_TILE = 256
PERM_TILE = 512
VMEM_LIMIT = 56 * 1024 * 1024


def _sigmoid(x):
    return 1.0 / (1.0 + jnp.exp(-x))


def _silu(x):
    return x * _sigmoid(x)


def _log_sigmoid(x):
    return jnp.minimum(x, 0.0) - jnp.log(1.0 + jnp.exp(-jnp.abs(x)))


def _bdot(a, b):
    return jnp.dot(a.astype(BF16), b.astype(BF16), preferred_element_type=F32)


def _bdot_nt(a, b):
    return lax.dot_general(a.astype(BF16), b.astype(BF16), (((1,), (1,)), ((), ())),
                           preferred_element_type=F32)


def _split2(x):
    hi = x.astype(BF16)
    lo = (x - hi.astype(F32)).astype(BF16)
    return hi, lo


def _split_kernel(w_ref, hi_ref, lo_ref):
    hi, lo = _split2(w_ref[...])
    hi_ref[...] = hi
    lo_ref[...] = lo


def _split_weights(w):
    rows, cols = w.shape
    blk = min(cols, 512)
    assert cols % blk == 0
    spec = pl.BlockSpec((rows, blk), lambda j: (0, j))
    return pl.pallas_call(
        _split_kernel, grid=(cols // blk,), in_specs=[spec], out_specs=[spec, spec],
        out_shape=[jax.ShapeDtypeStruct(w.shape, BF16)] * 2,
        compiler_params=pltpu.CompilerParams(dimension_semantics=("arbitrary",),
                                             vmem_limit_bytes=VMEM_LIMIT),
        name="split",
    )(w)


def _dot3(a, w_hi, w_lo):
    a_hi, a_lo = _split2(a)
    return (jnp.dot(a_hi, w_hi, preferred_element_type=F32)
            + jnp.dot(a_lo, w_hi, preferred_element_type=F32)
            + jnp.dot(a_hi, w_lo, preferred_element_type=F32))


def _cumsum_lanes(triu_bf16, x):
    hi = x.astype(BF16)
    r1 = x - hi.astype(F32)
    mid = r1.astype(BF16)
    lo = (r1 - mid.astype(F32)).astype(BF16)
    return (jnp.dot(hi, triu_bf16, preferred_element_type=F32)
            + jnp.dot(mid, triu_bf16, preferred_element_type=F32)
            + jnp.dot(lo, triu_bf16, preferred_element_type=F32))


def _rms(x, g):
    return x * lax.rsqrt(jnp.mean(x * x, axis=-1, keepdims=True) + EPS) * g


def _layer_norm(x, g, b=None):
    mu = jnp.mean(x, axis=-1, keepdims=True)
    xc = x - mu
    var = jnp.mean(xc * xc, axis=-1, keepdims=True)
    y = xc * lax.rsqrt(var + EPS) * g
    return y if b is None else y + b


def _store_rows(ref, x, row0=0):
    r = x.shape[0]
    for k in range(ROW_SUB):
        ref[pl.ds(row0 * ROW_SUB + k, r, stride=ROW_SUB), :] = x[:, k * LANES:(k + 1) * LANES]


def _load_rows(ref, r):
    return jnp.concatenate([ref[pl.ds(k, r, stride=ROW_SUB), :] for k in range(ROW_SUB)], axis=1)


def _route(logits):
    lane = lax.broadcasted_iota(I32, logits.shape, 1).astype(F32)
    neg = jnp.float32(-jnp.inf)
    big = jnp.float32(1e9)
    is_g = lane < N_GROUPS
    gl = jnp.where(is_g, logits, neg)
    gmax = jnp.max(gl, axis=1, keepdims=True)
    gsel = jnp.min(jnp.where(gl == gmax, lane, big), axis=1, keepdims=True)
    pg = 1.0 / jnp.sum(jnp.where(is_g, jnp.exp(gl - gmax), 0.0), axis=1, keepdims=True)
    lo = N_GROUPS + EXP_PER_GROUP * gsel
    emask = (lane >= lo) & (lane < lo + EXP_PER_GROUP)
    el = jnp.where(emask, logits, neg)
    v1 = jnp.max(el, axis=1, keepdims=True)
    i1 = jnp.min(jnp.where(el == v1, lane, big), axis=1, keepdims=True)
    el2 = jnp.where(lane == i1, neg, el)
    v2 = jnp.max(el2, axis=1, keepdims=True)
    i2 = jnp.min(jnp.where(el2 == v2, lane, big), axis=1, keepdims=True)
    d = jnp.exp(v2 - v1)
    w1 = pg / (1.0 + d)
    w2 = pg * d / (1.0 + d)
    return jnp.where(lane == 0, i1 - N_GROUPS,
                     jnp.where(lane == 1, i2 - N_GROUPS,
                               jnp.where(lane == 2, w1, jnp.where(lane == 3, w2, 0.0))))


def _ada_kernel(c_ref, w_ref, b_ref, o_ref):
    w_hi, w_lo = _split2(w_ref[...])
    o_ref[...] = _dot3(_silu(c_ref[...]), w_hi, w_lo) + b_ref[...]


def _ada(c_all, w_ada, b_ada):
    rows, d = c_all.shape
    n_out = w_ada.shape[1]
    blk = 1024
    return pl.pallas_call(
        _ada_kernel,
        grid=(n_out // blk,),
        in_specs=[pl.BlockSpec((rows, d), lambda j: (0, 0)),
                  pl.BlockSpec((d, blk), lambda j: (0, j)),
                  pl.BlockSpec((1, blk), lambda j: (0, j))],
        out_specs=pl.BlockSpec((rows, blk), lambda j: (0, j)),
        out_shape=jax.ShapeDtypeStruct((rows, n_out), F32),
        compiler_params=pltpu.CompilerParams(dimension_semantics=("arbitrary",),
                                             vmem_limit_bytes=VMEM_LIMIT),
        name="ada",
    )(c_all, w_ada, b_ada.reshape(1, n_out))


def _post(x, attn_cat, mod, g2, wout, wr_hi, wr_lo, rbias, d, wout_lo=None):
    gate1 = mod[:, 2 * d:3 * d]
    sh2 = mod[:, 3 * d:4 * d]
    sc2 = mod[:, 4 * d:5 * d]
    if wout_lo is None:
        proj = jnp.dot(attn_cat.astype(BF16), wout, preferred_element_type=F32)
    else:
        proj = _dot3(attn_cat, wout, wout_lo)
    x1 = x + gate1 * proj
    h2 = _rms(x1, g2) * (1.0 + sc2) + sh2
    logits = _dot3(h2, wr_hi, wr_lo) + rbias
    return x1, h2, _route(logits)


def _mix_prompt_kernel(x_ref, mod_ref, g1_ref, g2_ref, wmain_ref, wgh_ref, wgl_ref, gbias_ref,
                       mhg_ref, wdw_ref, bdw_ref, clg_ref, clb_ref, wout_ref, wrh_ref, wrl_ref,
                       rbias_ref,
                       x1_ref, h2_ref, route_ref, c_ref, n_ref, m_ref, cv_ref,
                       ubuf, yc_s, q_s, k_s, v_s, so_s, hm_s, p_s, u_s, cm_s, nb_s, m_s,
                       *, tt, d, dm, dk):
    t = pl.program_id(1)
    dc = d - dm
    n_lt = dc // LANES
    sub = min(MIX_SUB, tt)

    @pl.when(t == 0)
    def _():
        c_ref[...] = jnp.zeros_like(c_ref)
        nb_s[...] = jnp.zeros_like(nb_s)
        m_s[...] = jnp.zeros_like(m_s)
        ubuf[0:CONV_PAD, :] = jnp.zeros((CONV_PAD, dc), F32)

    mod = mod_ref[0]
    row8 = lax.broadcasted_iota(I32, (8, LANES), 0)
    row = lax.broadcasted_iota(I32, (CHUNK, CHUNK), 0)
    col = lax.broadcasted_iota(I32, (CHUNK, CHUNK), 1)
    causal = col <= row
    triu = (row <= col).astype(BF16)
    neg = jnp.float32(-jnp.inf)
    ones_b = jnp.ones((CHUNK, dk), BF16)
    pad_rows = jnp.zeros((CHUNK - 8, LANES), F32)

    lanes_of = [slice(lt * LANES, (lt + 1) * LANES) for lt in range(n_lt)]
    wrows = [[jnp.broadcast_to(wdw_ref[j:j + 1, ls], (8, LANES)) for j in range(CONV_W)]
             for ls in lanes_of]
    bias = [jnp.broadcast_to(bdw_ref[:, ls], (8, LANES)) for ls in lanes_of]

    def partial_sums(r0, lt):
        blocks = [ubuf[r0 + 8 * a:r0 + 8 * a + 8, lanes_of[lt]] for a in range(CONV_PAD // 8)]
        sums = []
        for s in range(8):
            acc = None
            for a in range(CONV_PAD // 8):
                j = 8 * a + s - CONV_OFF
                if 0 <= j < CONV_W:
                    term = blocks[a] * wrows[lt][j]
                    acc = term if acc is None else acc + term
            sums.append(acc)
        return tuple(sums)

    q_prev = [None] * n_lt
    ca = [jnp.concatenate([c_ref[0, 0, hd], nb_s[hd]], axis=1) for hd in range(N_HEADS)]
    m_prev = [m_s[hd:hd + 1, :] for hd in range(N_HEADS)]

    n_sub = tt // sub
    gates_of = {}
    hc_of = {}

    def rows_of(sb):
        return slice(sb * sub, (sb + 1) * sub)

    def proj_items(sb):
        rs = rows_of(sb)
        st = {}

        def head():
            x = x_ref[0, rs, :]
            h = _rms(x, g1_ref[...]) * (1.0 + mod[:, d:2 * d]) + mod[:, 0:d]
            st["hb"] = h.astype(BF16)
            gates_of[sb] = _dot3(h, wgh_ref[...], wgl_ref[...]) + gbias_ref[...]

        def proj(lo, hi):
            return jnp.dot(st["hb"], wmain_ref[:, lo:hi], preferred_element_type=F32)

        def glu():
            ubuf[CONV_PAD + sb * sub:CONV_PAD + (sb + 1) * sub, :] = (
                proj(4 * dm, 4 * dm + dc) * _sigmoid(proj(4 * dm + dc, 4 * dm + 2 * dc)))

        def q():
            q_s[rs, :] = proj(0, dm).astype(BF16)

        def k():
            k_s[rs, :] = proj(dm, 2 * dm) * (dk ** -0.5)

        def v():
            v_s[rs, :] = proj(2 * dm, 3 * dm).astype(BF16)

        def o():
            so_s[rs, :] = _sigmoid(proj(3 * dm, 4 * dm))

        return [head, glu, q, k, v, o]

    def conv_items(sb):
        items = []
        for lt in range(n_lt):
            for i in range(sb * sub // 8 + 1, (sb + 1) * sub // 8 + 1):
                def block(lt=lt, i=i):
                    if i == 1:
                        q_prev[lt] = partial_sums(0, lt)
                    q_cur = partial_sums(i * 8, lt)
                    cur = ubuf[(i - 1) * 8 + CONV_PAD:i * 8 + CONV_PAD, lanes_of[lt]]
                    y = bias[lt] + q_prev[lt][0] + cur * wrows[lt][CONV_W - 1]
                    for s in range(1, 8):
                        merged = jnp.where(row8 < s, q_cur[s], q_prev[lt][s])
                        y = y + pltpu.roll(merged, 8 - s, 0)
                    yc_s[(i - 1) * 8:i * 8, lanes_of[lt]] = y
                    q_prev[lt] = q_cur
                items.append(block)
        return items

    def post_items(sb):
        def post():
            rs = rows_of(sb)
            cat = jnp.concatenate([hm_s[rs, :], hc_of[sb]], axis=1)
            x1, h2, route = _post(x_ref[0, rs, :], cat, mod, g2_ref[...], wout_ref[...],
                                  wrh_ref[...], wrl_ref[...], rbias_ref[...], d)
            x1_ref[0, rs, :] = x1
            _store_rows(h2_ref, h2, sb * sub)
            route_ref[rs, :] = route
        return [post]

    def interleave(main, side):
        gap = len(main) / (len(side) + 1)
        due, done = gap, 0
        for n, item in enumerate(main):
            item()
            while done < len(side) and n + 1 >= due:
                side[done]()
                done += 1
                due += gap
        for item in side[done:]:
            item()

    for item in proj_items(0):
        item()
    for sb in range(n_sub):
        r_lo = sb * sub
        rs = rows_of(sb)
        side = post_items(sb - 1) if sb > 0 else []
        if sb + 1 < n_sub:
            side = side + proj_items(sb + 1)
        interleave(conv_items(sb), side)
        hc_of[sb] = _silu(_layer_norm(yc_s[rs, :], clg_ref[...], clb_ref[...]))
        gates = gates_of[sb]

        chunks = range(r_lo // CHUNK, (r_lo + sub) // CHUNK)
        b_cols = {}
        for c in chunks:
            r0 = c * CHUNK
            g8 = gates[r0 - r_lo:r0 - r_lo + CHUNK, :].T[0:8, :]
            b8 = _cumsum_lanes(triu, _log_sigmoid(g8))
            pk8 = jnp.where(row8 < N_HEADS, g8 - pltpu.roll(b8, N_HEADS, 0), b8)
            pk = jnp.concatenate([pk8, pad_rows], axis=0).T
            for hd in range(N_HEADS):
                cs = slice(hd * dk, (hd + 1) * dk)
                idx = c * N_HEADS + hd
                kf = k_s[r0:r0 + CHUNK, cs]
                va = jnp.concatenate([v_s[r0:r0 + CHUNK, cs], ones_b], axis=1)
                gm = jnp.where(causal, pk8[hd:hd + 1, :], neg)
                cm = jnp.max(gm, axis=1, keepdims=True)
                s = _bdot_nt(q_s[r0:r0 + CHUNK, cs], kf) * jnp.exp(gm - cm)
                p_s[idx] = jnp.dot(s.astype(BF16), va, preferred_element_type=F32)
                cm_s[idx] = jnp.broadcast_to(cm, (CHUNK, LANES))
                kw = kf * jnp.exp(pk[:, hd:hd + 1] - cm[CHUNK - 1:CHUNK, :])
                u_s[idx] = jnp.dot(kw.T.astype(BF16), va, preferred_element_type=F32)
                b_cols[idx] = pk[:, N_HEADS + hd:N_HEADS + hd + 1]

        for hd in range(N_HEADS):
            cs = slice(hd * dk, (hd + 1) * dk)
            for c in chunks:
                r0 = c * CHUNK
                idx = c * N_HEADS + hd
                cm = cm_s[idx]
                b_col = b_cols[idx]
                mt = jnp.maximum(m_prev[hd], cm)
                f_loc = jnp.exp(cm - mt)
                a_int = jnp.exp(m_prev[hd] - mt)
                qc = jnp.dot(q_s[r0:r0 + CHUNK, cs], ca[hd].astype(BF16),
                             preferred_element_type=F32)
                p = p_s[idx]
                num = f_loc * p[:, :dk] + a_int * qc[:, :dk]
                den = f_loc * p[:, dk:] + a_int * qc[:, dk:]
                hh = num / jnp.maximum(jnp.abs(den), jnp.exp(-(b_col + mt)))
                hm_s[r0:r0 + CHUNK, cs] = (_layer_norm(hh, mhg_ref[hd:hd + 1, :])
                                           * so_s[r0:r0 + CHUNK, cs])
                mt_l = mt[CHUNK - 1:CHUNK, :]
                u = u_s[idx]
                f_l = f_loc[CHUNK - 1:CHUNK, :]
                a_l = a_int[CHUNK - 1:CHUNK, :]
                ca[hd] = jnp.concatenate([a_l * ca[hd][:, :dk] + f_l * u[:, :dk],
                                          a_l * ca[hd][:, dk:] + f_l * u[:, dk:]], axis=1)
                m_prev[hd] = b_col[CHUNK - 1:CHUNK, :] + mt_l

    for item in post_items(n_sub - 1):
        item()

    for hd in range(N_HEADS):
        c_ref[0, 0, hd] = ca[hd][:, :dk]
        nb_s[hd] = ca[hd][:, dk:]
        m_s[hd:hd + 1, :] = m_prev[hd]
    cv_ref[0, 0] = ubuf[tt + CONV_PAD - (CONV_W - 1):tt + CONV_PAD, :]
    ubuf[0:CONV_PAD, :] = ubuf[tt:tt + CONV_PAD, :]

    @pl.when(t == pl.num_programs(1) - 1)
    def _():
        for hd in range(N_HEADS):
            n_ref[0, 0, hd:hd + 1, :] = nb_s[hd].T[0:1, :]

    lane1 = lax.broadcasted_iota(I32, (1, LANES), 1)
    m_row = jnp.zeros((1, LANES), F32)
    for hd in range(N_HEADS):
        m_row = jnp.where(lane1 == hd, m_s[hd:hd + 1, :], m_row)
    m_ref[0] = m_row


def _const_spec(shape):
    nd = len(shape)
    return pl.BlockSpec(shape, lambda *_: (0,) * nd)


def _mix_prompt(x, mod, wts):
    b, t, d = x.shape
    dm = wts["dm"]
    dk = dm // N_HEADS
    dc = d - dm
    tt = min(MIX_TILE, t)
    assert t % tt == 0 and tt % CHUNK == 0 and tt >= CONV_PAD
    nt = t // tt
    kern = functools.partial(_mix_prompt_kernel, tt=tt, d=d, dm=dm, dk=dk)
    const_names = ["g1", "g2", "wmain", "wg_hi", "wg_lo", "gbias", "mhg", "wdw", "bdw", "clg",
                   "clb", "wout", "wr_hi", "wr_lo", "rbias"]
    consts = [wts[k] for k in const_names]
    in_specs = ([pl.BlockSpec((1, tt, d), lambda i, j: (i, j, 0)),
                 pl.BlockSpec((1, 1, mod.shape[-1]), lambda i, j: (i, 0, 0))]
                + [_const_spec(c.shape) for c in consts])
    out_shape = [
        jax.ShapeDtypeStruct((b, t, d), F32),
        jax.ShapeDtypeStruct((b * t * ROW_SUB, LANES), F32),
        jax.ShapeDtypeStruct((b * t, LANES), F32),
        jax.ShapeDtypeStruct((1, b, N_HEADS, dk, dk), F32),
        jax.ShapeDtypeStruct((1, b, N_HEADS, dk), F32),
        jax.ShapeDtypeStruct((b, 1, LANES), F32),
        jax.ShapeDtypeStruct((1, b, CONV_W - 1, dc), F32),
    ]
    out_specs = [
        pl.BlockSpec((1, tt, d), lambda i, j: (i, j, 0)),
        pl.BlockSpec((tt * ROW_SUB, LANES), lambda i, j: (i * nt + j, 0)),
        pl.BlockSpec((tt, LANES), lambda i, j: (i * nt + j, 0)),
        pl.BlockSpec((1, 1, N_HEADS, dk, dk), lambda i, j: (0, i, 0, 0, 0)),
        pl.BlockSpec((1, 1, N_HEADS, dk), lambda i, j: (0, i, 0, 0)),
        pl.BlockSpec((1, 1, LANES), lambda i, j: (i, 0, 0)),
        pl.BlockSpec((1, 1, CONV_W - 1, dc), lambda i, j: (0, i, 0, 0)),
    ]
    n_hc = (tt // CHUNK) * N_HEADS
    scratch = [pltpu.VMEM((tt + CONV_PAD, dc), F32),
               pltpu.VMEM((tt, dc), F32),
               pltpu.VMEM((tt, dm), BF16),
               pltpu.VMEM((tt, dm), F32),
               pltpu.VMEM((tt, dm), BF16),
               pltpu.VMEM((tt, dm), F32),
               pltpu.VMEM((tt, dm), F32),
               pltpu.VMEM((n_hc, CHUNK, 2 * dk), F32),
               pltpu.VMEM((n_hc, CHUNK, 2 * dk), F32),
               pltpu.VMEM((n_hc, CHUNK, LANES), F32),
               pltpu.VMEM((N_HEADS, dk, LANES), F32),
               pltpu.VMEM((8, LANES), F32)]
    return pl.pallas_call(
        kern, grid=(b, nt), in_specs=in_specs, out_specs=out_specs, out_shape=out_shape,
        scratch_shapes=scratch,
        compiler_params=pltpu.CompilerParams(dimension_semantics=("arbitrary", "arbitrary"),
                                             vmem_limit_bytes=VMEM_LIMIT),
        name="mix_p",
    )(x, mod.reshape(b, 1, -1), *consts)


def _s_pre_kernel(x_ref, mod_ref, g1_ref, wmain_ref, wmainlo_ref, wgh_ref, wgl_ref, gbias_ref,
                  wkt_ref, wktlo_ref, wdw_ref, bdw_ref, clg_ref, clb_ref, cache_ref, n0_ref, m0_ref,
                  q_ref, kt_ref, vs_ref, ab_ref, sv_ref, den_ref, eb_ref, o_ref, hc_ref, u_ref,
                  n_ref, m_ref, *, d, dm, dk):
    dc = d - dm
    x = x_ref[...]
    mod = mod_ref[...]
    sh1 = mod[:, 0:d]
    sc1 = mod[:, d:2 * d]
    h = _rms(x, g1_ref[...]) * (1.0 + sc1) + sh1
    z = _dot3(h, wmain_ref[...], wmainlo_ref[...])
    gates = _dot3(h, wgh_ref[...], wgl_ref[...]) + gbias_ref[...]
    scale = dk ** -0.5
    h_hi, h_lo = _split2(h)
    kt = _bdot_nt(wkt_ref[...], h_hi) + _bdot_nt(wktlo_ref[...], h_hi) + _bdot_nt(wkt_ref[...], h_lo)
    kt_ref[...] = (kt * scale).astype(BF16)
    k_all = z[:, dm:2 * dm] * scale
    ga = z[:, 4 * dm:4 * dm + dc]
    gb = z[:, 4 * dm + dc:4 * dm + 2 * dc]
    u = ga * _sigmoid(gb)
    u_ref[...] = u
    acc = jnp.broadcast_to(bdw_ref[...], u.shape) + u * wdw_ref[CONV_W - 1:CONV_W, :]
    for j in range(CONV_W - 1):
        acc = acc + cache_ref[j] * wdw_ref[j:j + 1, :]
    hc_ref[...] = _silu(_layer_norm(acc, clg_ref[...], clb_ref[...]))
    o_ref[...] = z[:, 3 * dm:4 * dm]
    q_ref[...] = z[:, 0:dm]
    m0 = m0_ref[...]
    n0 = n0_ref[...]
    lane1 = lax.broadcasted_iota(I32, (1, LANES), 1)
    m_new = jnp.zeros(m0.shape, F32)
    for hd in range(N_HEADS):
        cs = slice(hd * dk, (hd + 1) * dk)
        ig = gates[:, hd:hd + 1]
        lf = _log_sigmoid(gates[:, N_HEADS + hd:N_HEADS + hd + 1])
        mp = m0[:, hd:hd + 1]
        inter = lf + mp
        mt = jnp.maximum(inter, ig)
        w = jnp.exp(ig - mt)
        a_int = jnp.exp(inter - mt)
        qf = z[:, cs]
        kf = k_all[:, cs]
        vf = z[:, 2 * dm + hd * dk:2 * dm + (hd + 1) * dk]
        s = jnp.sum(qf * kf, axis=1, keepdims=True) * w
        sv_ref[:, cs] = s * vf
        den_ref[:, cs] = jnp.broadcast_to(
            s + a_int * jnp.sum(qf * n0[:, cs], axis=1, keepdims=True), (x.shape[0], dk))
        eb_ref[:, cs] = jnp.broadcast_to(jnp.exp(-mt), (x.shape[0], dk))
        ab_ref[:, cs] = jnp.broadcast_to(a_int, (x.shape[0], dk))
        vs_ref[:, cs] = (vf * w).astype(BF16)
        n_ref[:, cs] = a_int * n0[:, cs] + w * kf
        m_new = jnp.where(lane1 == hd, mt, m_new)
    m_ref[...] = m_new


def _s_state_kernel(q_ref, kt_ref, vs_ref, ab_ref, c0_ref, c_ref, r_ref, *, bb, dk):
    i = pl.program_id(0)
    nb = q_ref.shape[0]
    rows = lax.broadcasted_iota(I32, (nb, dk), 0)

    @pl.when(i == 0)
    def _():
        r_ref[...] = jnp.zeros_like(r_ref)

    a_blk = ab_ref[pl.ds(pl.multiple_of(i * bb, bb), bb), :]
    for j in range(bb):
        sel = rows == i * bb + j
        for hd in range(N_HEADS):
            cs = slice(hd * dk, (hd + 1) * dk)
            c0 = c0_ref[j, hd]
            vmask = jnp.where(sel, vs_ref[:, cs], jnp.zeros((), BF16))
            c_ref[j, hd] = (a_blk[j:j + 1, cs] * c0
                            + jnp.dot(kt_ref[cs, :], vmask, preferred_element_type=F32))
            c_hi, c_lo = _split2(c0)
            q_hi, q_lo = _split2(q_ref[:, cs])
            r = (jnp.dot(q_hi, c_hi, preferred_element_type=F32)
                 + jnp.dot(q_lo, c_hi, preferred_element_type=F32)
                 + jnp.dot(q_hi, c_lo, preferred_element_type=F32))
            r_ref[:, cs] = r_ref[:, cs] + jnp.where(sel, r, 0.0)


def _s_post_kernel(x_ref, mod_ref, g2_ref, mhg_ref, r_ref, ab_ref, sv_ref, den_ref, eb_ref, o_ref,
                   hc_ref, wout_ref, woutlo_ref, wrh_ref, wrl_ref, rbias_ref,
                   x1_ref, h2_ref, route_ref, *, d, dm, dk):
    hm = []
    for hd in range(N_HEADS):
        cs = slice(hd * dk, (hd + 1) * dk)
        num = sv_ref[:, cs] + ab_ref[:, cs] * r_ref[:, cs]
        hh = num / jnp.maximum(jnp.abs(den_ref[:, cs]), eb_ref[:, cs])
        hm.append(_layer_norm(hh, mhg_ref[hd:hd + 1, :]) * _sigmoid(o_ref[:, cs]))
    cat = jnp.concatenate(hm + [hc_ref[...]], axis=1)
    x1, h2, route = _post(x_ref[...], cat, mod_ref[...], g2_ref[...], wout_ref[...], wrh_ref[...],
                          wrl_ref[...], rbias_ref[...], d, wout_lo=woutlo_ref[...])
    x1_ref[...] = x1
    _store_rows(h2_ref, h2)
    route_ref[...] = route


def _mix_sample(x, mod, c0, n0, m0, cache, wts):
    nb, d = x.shape
    dm = wts["dm"]
    dk = dm // N_HEADS
    dc = d - dm
    cp = pltpu.CompilerParams(dimension_semantics=("arbitrary",), vmem_limit_bytes=VMEM_LIMIT)
    cache_t = jnp.transpose(cache, (1, 0, 2))
    m0p = jnp.pad(m0, ((0, 0), (0, LANES - N_HEADS)))
    pre_in = [x, mod, wts["g1"], wts["wmain"], wts["wmain_lo"], wts["wg_hi"], wts["wg_lo"],
              wts["gbias"], wts["wk_t"], wts["wk_t_lo"], wts["wdw"], wts["bdw"], wts["clg"],
              wts["clb"], cache_t, n0.reshape(nb, dm), m0p]
    pre_out = [jax.ShapeDtypeStruct((nb, dm), F32),
               jax.ShapeDtypeStruct((dm, nb), BF16),
               jax.ShapeDtypeStruct((nb, dm), BF16),
               jax.ShapeDtypeStruct((nb, dm), F32),
               jax.ShapeDtypeStruct((nb, dm), F32),
               jax.ShapeDtypeStruct((nb, dm), F32),
               jax.ShapeDtypeStruct((nb, dm), F32),
               jax.ShapeDtypeStruct((nb, dm), F32),
               jax.ShapeDtypeStruct((nb, dc), F32),
               jax.ShapeDtypeStruct((nb, dc), F32),
               jax.ShapeDtypeStruct((nb, dm), F32),
               jax.ShapeDtypeStruct((nb, LANES), F32)]
    (q, kt, vs, ab, sv, den, eb, o, hc, u, n1, m1) = pl.pallas_call(
        functools.partial(_s_pre_kernel, d=d, dm=dm, dk=dk),
        grid=(1,),
        in_specs=[_const_spec(a.shape) for a in pre_in],
        out_specs=[_const_spec(s.shape) for s in pre_out],
        out_shape=pre_out, compiler_params=cp, name="s_pre")(*pre_in)

    bb = 8
    assert nb % bb == 0
    c1, r = pl.pallas_call(
        functools.partial(_s_state_kernel, bb=bb, dk=dk),
        grid=(nb // bb,),
        in_specs=[_const_spec(q.shape), _const_spec(kt.shape), _const_spec(vs.shape),
                  _const_spec(ab.shape),
                  pl.BlockSpec((bb, N_HEADS, dk, dk), lambda i: (i, 0, 0, 0))],
        out_specs=[pl.BlockSpec((bb, N_HEADS, dk, dk), lambda i: (i, 0, 0, 0)),
                   _const_spec((nb, dm))],
        out_shape=[jax.ShapeDtypeStruct((nb, N_HEADS, dk, dk), F32),
                   jax.ShapeDtypeStruct((nb, dm), F32)],
        compiler_params=cp, name="s_state")(q, kt, vs, ab, c0)

    post_in = [x, mod, wts["g2"], wts["mhg"], r, ab, sv, den, eb, o, hc, wts["wout"],
               wts["wout_lo"], wts["wr_hi"], wts["wr_lo"], wts["rbias"]]
    post_out = [jax.ShapeDtypeStruct((nb, d), F32),
                jax.ShapeDtypeStruct((nb * ROW_SUB, LANES), F32),
                jax.ShapeDtypeStruct((nb, LANES), F32)]
    x1, h2, route = pl.pallas_call(
        functools.partial(_s_post_kernel, d=d, dm=dm, dk=dk),
        grid=(1,),
        in_specs=[_const_spec(a.shape) for a in post_in],
        out_specs=[_const_spec(s.shape) for s in post_out],
        out_shape=post_out, compiler_params=cp, name="s_post")(*post_in)
    return x1, h2, route, c1, n1, m1, u


def _scatter_kernel(pos_ref, zoff_ref, src_ref, *rest, tp, tm, create):
    xs_ref, zbuf, sem, zsem = rest[-4:]
    i = pl.program_id(0)

    if create:
        @pl.when(i == 0)
        def _():
            zbuf[...] = jnp.zeros_like(zbuf)

            def zero_copy(e):
                start = pl.multiple_of(zoff_ref[e] * ROW_SUB, tm * ROW_SUB)
                return pltpu.make_async_copy(zbuf, xs_ref.at[pl.ds(start, tm * ROW_SUB)], zsem)

            for e in range(zoff_ref.shape[0]):
                @pl.when(zoff_ref[e] >= 0)
                def _():
                    zero_copy(e).start()
            for e in range(zoff_ref.shape[0]):
                @pl.when(zoff_ref[e] >= 0)
                def _():
                    zero_copy(e).wait()

    def row_start(r, carry):
        src = src_ref.at[pl.ds(pl.multiple_of(r * ROW_SUB, ROW_SUB), ROW_SUB)]
        for slot in range(2):
            dst = pl.multiple_of(pos_ref[0, slot, r] * ROW_SUB, ROW_SUB)
            pltpu.make_async_copy(src, xs_ref.at[pl.ds(dst, ROW_SUB)], sem).start(priority=slot)
        return carry

    lax.fori_loop(0, tp, row_start, 0, unroll=8)
    for slot in range(2):
        pltpu.make_async_copy(src_ref, xs_ref.at[pl.ds(0, tp * ROW_SUB)], sem).wait()


def _scatter_rows(h2, pos, zoff, xs_or_rows, tm):
    n, c = h2.shape[0] // ROW_SUB, LANES
    tp = min(PERM_TILE, n)
    assert n % tp == 0
    pos3 = pos.reshape(n // tp, tp, 2).transpose(0, 2, 1)
    create = isinstance(xs_or_rows, int)
    n_sorted = xs_or_rows * ROW_SUB if create else xs_or_rows.shape[0]
    in_specs = [pl.BlockSpec((1, 2, tp), lambda i: (i, 0, 0), memory_space=pltpu.SMEM),
                pl.BlockSpec(memory_space=pltpu.SMEM),
                pl.BlockSpec((tp * ROW_SUB, c), lambda i: (i, 0))]
    args = [pos3, zoff, h2]
    if not create:
        in_specs.append(pl.BlockSpec(memory_space=pl.ANY))
        args.append(xs_or_rows)
    return pl.pallas_call(
        functools.partial(_scatter_kernel, tp=tp, tm=tm, create=create),
        grid_spec=pltpu.PrefetchScalarGridSpec(
            num_scalar_prefetch=0,
            grid=(n // tp,),
            in_specs=in_specs,
            out_specs=pl.BlockSpec(memory_space=pl.ANY),
            scratch_shapes=[pltpu.VMEM((tm * ROW_SUB, c), F32), pltpu.SemaphoreType.DMA(()),
                            pltpu.SemaphoreType.DMA(())]),
        out_shape=jax.ShapeDtypeStruct((n_sorted, c), F32),
        input_output_aliases={} if create else {3: 0},
        compiler_params=pltpu.CompilerParams(dimension_semantics=("arbitrary",),
                                             vmem_limit_bytes=VMEM_LIMIT),
        name="scatter",
    )(*args)


def _moe_kernel(te_ref, nu_ref, xs_ref, wg_ref, wu_ref, wd_ref, ys_ref, wg_b, wu_b, wd_b):
    i = pl.program_id(0)
    used = i < nu_ref[0]

    @pl.when(used & ((i == 0) | (te_ref[i] != te_ref[jnp.maximum(i - 1, 0)])))
    def _():
        wg_b[...] = wg_ref[0].astype(BF16)
        wu_b[...] = wu_ref[0].astype(BF16)
        wd_b[...] = wd_ref[0].astype(BF16)

    @pl.when(used)
    def _():
        xb = _load_rows(xs_ref, xs_ref.shape[0] // ROW_SUB).astype(BF16)
        g = jnp.dot(xb, wg_b[...], preferred_element_type=F32)
        u = jnp.dot(xb, wu_b[...], preferred_element_type=F32)
        hid = (_silu(g) * u).astype(BF16)
        _store_rows(ys_ref, jnp.dot(hid, wd_b[...], preferred_element_type=F32))

    @pl.when(jnp.logical_not(used))
    def _():
        ys_ref[...] = jnp.zeros_like(ys_ref)


def _moe(xs, tile_expert, n_used, w_gate, w_up, w_down, tm):
    p, c = xs.shape[0] // ROW_SUB, LANES
    ne, d, de = w_gate.shape
    n_tiles = p // tm

    def x_map(i, te, nu):
        return (jnp.minimum(i, jnp.maximum(nu[0] - 1, 0)), 0)

    def w_map(i, te, nu):
        return (te[i], 0, 0)

    return pl.pallas_call(
        _moe_kernel,
        grid_spec=pltpu.PrefetchScalarGridSpec(
            num_scalar_prefetch=2,
            grid=(n_tiles,),
            in_specs=[pl.BlockSpec((tm * ROW_SUB, c), x_map),
                      pl.BlockSpec((1, d, de), w_map),
                      pl.BlockSpec((1, d, de), w_map),
                      pl.BlockSpec((1, de, d), w_map)],
            out_specs=pl.BlockSpec((tm * ROW_SUB, c), lambda i, te, nu: (i, 0)),
            scratch_shapes=[pltpu.VMEM((d, de), BF16), pltpu.VMEM((d, de), BF16),
                            pltpu.VMEM((de, d), BF16)]),
        out_shape=jax.ShapeDtypeStruct((p * ROW_SUB, c), F32),
        compiler_params=pltpu.CompilerParams(dimension_semantics=("arbitrary",),
                                             vmem_limit_bytes=VMEM_LIMIT),
        name="moe",
    )(tile_expert, n_used, xs, w_gate, w_up, w_down)


def _fin_kernel(pos_ref, posn_ref, x1_ref, route_ref, mod_ref, fg_ref, ys_ref, y_ref, ybuf, sem,
                *, tp, d, n_steps):
    i = pl.program_id(0)
    cur = i % 2

    def issue(p_ref, buf):
        def row_start(r, carry):
            dst = pl.ds(pl.multiple_of(r * ROW_SUB, ROW_SUB), ROW_SUB)
            for slot in range(2):
                src = pl.multiple_of(p_ref[0, slot, r] * ROW_SUB, ROW_SUB)
                pltpu.make_async_copy(ys_ref.at[pl.ds(src, ROW_SUB)],
                                      ybuf.at[buf, slot, dst], sem.at[buf]).start(priority=slot)
            return carry
        lax.fori_loop(0, tp, row_start, 0, unroll=8)

    @pl.when(i == 0)
    def _():
        issue(pos_ref, 0)

    @pl.when(i + 1 < n_steps)
    def _():
        issue(posn_ref, 1 - cur)

    for slot in range(2):
        pltpu.make_async_copy(ys_ref.at[pl.ds(0, tp * ROW_SUB)], ybuf.at[cur, slot],
                              sem.at[cur]).wait()
    route = route_ref[...]
    moe = (route[:, 2:3] * _load_rows(ybuf.at[cur, 0], tp)
           + route[:, 3:4] * _load_rows(ybuf.at[cur, 1], tp))
    gate2 = mod_ref[0][:, 5 * d:6 * d]
    y_ref[0] = _rms(x1_ref[0] + gate2 * moe, fg_ref[...])


def _finish(x1, route, mod, final_g, ys, pos):
    b, t, d = x1.shape
    tp = min(PERM_TILE, t)
    assert t % tp == 0
    nt = t // tp
    n_steps = b * nt
    pos3 = pos.reshape(n_steps, tp, 2).transpose(0, 2, 1)
    blk0 = 0
    if mod.shape[1] == 1:
        mod_spec = pl.BlockSpec((1, 1, 6 * d), lambda i: (i // nt, 0, 0))
    else:
        mod_spec = pl.BlockSpec((1, tp, 6 * d), lambda i: (i // nt, i % nt, 0))
    return pl.pallas_call(
        functools.partial(_fin_kernel, tp=tp, d=d, n_steps=n_steps),
        grid_spec=pltpu.PrefetchScalarGridSpec(
            num_scalar_prefetch=0,
            grid=(n_steps,),
            in_specs=[pl.BlockSpec((1, 2, tp), lambda i: (blk0 + i, 0, 0), memory_space=pltpu.SMEM),
                      pl.BlockSpec((1, 2, tp), lambda i: (blk0 + jnp.minimum(i + 1, n_steps - 1), 0, 0),
                                   memory_space=pltpu.SMEM),
                      pl.BlockSpec((1, tp, d), lambda i: (i // nt, i % nt, 0)),
                      pl.BlockSpec((tp, LANES), lambda i: (i, 0)),
                      mod_spec,
                      _const_spec((1, d)),
                      pl.BlockSpec(memory_space=pl.ANY)],
            out_specs=pl.BlockSpec((1, tp, d), lambda i: (i // nt, i % nt, 0)),
            scratch_shapes=[pltpu.VMEM((2, 2, tp * ROW_SUB, LANES), F32),
                            pltpu.SemaphoreType.DMA((2,))]),
        out_shape=jax.ShapeDtypeStruct((b, t, d), F32),
        compiler_params=pltpu.CompilerParams(dimension_semantics=("arbitrary",),
                                             vmem_limit_bytes=VMEM_LIMIT),
        name="fin",
    )(pos3, pos3, x1, route, mod, final_g.reshape(1, d), ys)


def _plan(expert_ids, tm, n_tiles):
    onehot = (expert_ids[:, None] == jnp.arange(N_EXPERTS, dtype=I32)[None, :]).astype(I32)
    csum = jnp.cumsum(onehot, axis=0)
    counts = csum[-1]
    rank = jnp.sum(csum * onehot, axis=1) - 1
    tiles_per = (counts + tm - 1) // tm
    tile_end = jnp.cumsum(tiles_per)
    seg_start = (tile_end - tiles_per) * tm
    pos = jnp.sum(onehot * seg_start[None, :], axis=1) + rank
    n_used = tile_end[-1]
    tile_ids = jnp.arange(n_tiles, dtype=I32)
    tile_expert = jnp.sum((tile_ids[:, None] >= tile_end[None, :]).astype(I32), axis=1)
    last_used = jnp.sum((n_used - 1 >= tile_end).astype(I32))
    tile_expert = jnp.where(tile_ids < n_used, tile_expert, last_used).astype(I32)
    z_expert = jnp.where((counts % tm) != 0, (tile_end - 1) * tm, -1)
    spare = n_used + jnp.arange(N_EXPERTS, dtype=I32)
    z_spare = jnp.where(spare < n_tiles, spare * tm, -1)
    zoff = jnp.concatenate([z_expert, z_spare]).astype(I32)
    return pos.astype(I32), tile_expert, n_used.reshape(1).astype(I32), zoff


def kernel(x_prompt, x_sample, c_prompt, c_sample, state_mlstm_C, state_mlstm_n, state_mlstm_m,
           cache_conv, w_ada, b_ada, norm1_g, w_in, b_igate, b_fgate, mh_norm_g, w_dw, b_dw,
           conv_ln_g, conv_ln_b, w_out, norm2_g, w_grp_router, b_grp_router, w_exp_router,
           b_exp_router, w_gate, w_up, w_down, final_g):
    depth = w_ada.shape[0]
    assert depth == 1, "one layer per step"
    bp, tp_, d = x_prompt.shape
    bs = x_sample.shape[0]
    assert x_sample.shape[1] == 1 and d == ROW_SUB * LANES
    dk = state_mlstm_C.shape[-1]
    dm = N_HEADS * dk
    dc = d - dm
    l = 0

    win = w_in[l]
    w_qkvo = win[:, 0:4 * dm]
    w_gates = win[:, 4 * dm:4 * dm + 2 * N_HEADS]
    w_glu = win[:, 4 * dm + 2 * N_HEADS:]
    wmain, wmain_lo = _split_weights(jnp.concatenate([w_qkvo, w_glu], axis=1))
    wout_hi, wout_lo = _split_weights(w_out[l])
    wkt_hi, wkt_lo = _split_weights(win[:, dm:2 * dm].T)
    wg_pad = jnp.pad(w_gates, ((0, 0), (0, LANES - 2 * N_HEADS)))
    wg_hi, wg_lo = _split_weights(wg_pad)
    gbias = jnp.pad(jnp.concatenate([b_igate[l], b_fgate[l]]), (0, LANES - 2 * N_HEADS)).reshape(1, LANES)
    w_r = jnp.concatenate([w_grp_router[l], w_exp_router[l]], axis=1)
    n_r = N_GROUPS + N_EXPERTS
    wr_hi, wr_lo = _split_weights(jnp.pad(w_r, ((0, 0), (0, LANES - n_r))))
    rbias = jnp.pad(jnp.concatenate([b_grp_router[l], b_exp_router[l]]), (0, LANES - n_r)).reshape(1, LANES)
    wts = dict(dm=dm, g1=norm1_g[l].reshape(1, d), g2=norm2_g[l].reshape(1, d), wmain=wmain,
               wg_hi=wg_hi, wg_lo=wg_lo, gbias=gbias, mhg=mh_norm_g[l],
               wdw=w_dw[l].reshape(CONV_W, dc), bdw=b_dw[l].reshape(1, dc),
               clg=conv_ln_g[l].reshape(1, dc), clb=conv_ln_b[l].reshape(1, dc),
               wout=wout_hi, wr_hi=wr_hi, wr_lo=wr_lo, rbias=rbias, wk_t=wkt_hi,
               wmain_lo=wmain_lo, wout_lo=wout_lo, wk_t_lo=wkt_lo)

    mod = _ada(jnp.concatenate([c_prompt, c_sample], axis=0), w_ada[l], b_ada[l])
    mod_p, mod_s = mod[:bp], mod[bp:]

    n_p = bp * tp_
    n_all = n_p + bs
    x1_p, h2_p, route_p, c_p, n_pr, m_p, cv_p = _mix_prompt(x_prompt, mod_p, wts)
    x1_s, h2_s, route_s, c_s, n_s, m_s, u_s = _mix_sample(
        x_sample.reshape(bs, d), mod_s, state_mlstm_C[l], state_mlstm_n[l], state_mlstm_m[l],
        cache_conv[l], wts)

    route = jnp.concatenate([route_p, route_s], axis=0)
    expert_ids = route[:, 0:2].astype(I32).reshape(-1)
    tm = MOE_TILE
    n_tiles = (2 * n_all) // tm + N_EXPERTS
    pos, tile_expert, n_used, zoff = _plan(expert_ids, tm, n_tiles)
    pos = pos.reshape(n_all, 2)

    pos_p, pos_s = pos[:n_p], pos[n_p:]

    xs = _scatter_rows(h2_p, pos_p, zoff, n_tiles * tm, tm)
    xs = _scatter_rows(h2_s, pos_s, zoff, xs, tm)
    ys = _moe(xs, tile_expert, n_used, w_gate[l], w_up[l], w_down[l], tm)

    y_p = _finish(x1_p, route_p, mod_p.reshape(bp, 1, -1), final_g, ys, pos_p)
    y_s = _finish(x1_s.reshape(1, bs, d), route_s, mod_s.reshape(1, bs, -1), final_g, ys,
                  pos_s).reshape(bs, 1, d)

    conv_s = jnp.concatenate([cache_conv[l][:, 1:, :], u_s[:, None, :]], axis=1)
    return (y_p, y_s,
            c_p, n_pr, m_p[:, 0, :N_HEADS].reshape(1, bp, N_HEADS), cv_p,
            c_s[None], n_s.reshape(1, bs, N_HEADS, dk), m_s[:, :N_HEADS].reshape(1, bs, N_HEADS),
            conv_s[None])
```

```python
import functools

import jax
import jax.numpy as jnp
from jax import lax
from jax.experimental import pallas as pl
from jax.experimental.pallas import tpu as pltpu

F32 = jnp.float32
BF16 = jnp.bfloat16
I32 = jnp.int32

EPS = 1e-6
LANES = 128
ROW_SUB = 8
CHUNK = 128
N_HEADS = 4
N_GROUPS = 4
EXP_PER_GROUP = 8
N_EXPERTS = N_GROUPS * EXP_PER_GROUP
CONV_W = 31
CONV_PAD = 32
CONV_OFF = CONV_PAD - (CONV_W - 1)
MIX_TILE = 512
MIX_SUB = 256
MOE_TILE = 256
PERM_TILE = 512
FIN_CHUNK = 64
VMEM_LIMIT = 56 * 1024 * 1024


def _sigmoid(x):
    return 1.0 / (1.0 + jnp.exp(-x))


def _silu(x):
    return x * _sigmoid(x)


def _log_sigmoid(x):
    return jnp.minimum(x, 0.0) - jnp.log(1.0 + jnp.exp(-jnp.abs(x)))


def _bdot(a, b):
    return jnp.dot(a.astype(BF16), b.astype(BF16), preferred_element_type=F32)


def _bdot_nt(a, b):
    return lax.dot_general(a.astype(BF16), b.astype(BF16), (((1,), (1,)), ((), ())),
                           preferred_element_type=F32)


def _split2(x):
    hi = x.astype(BF16)
    lo = (x - hi.astype(F32)).astype(BF16)
    return hi, lo


def _split_kernel(w_ref, hi_ref, lo_ref):
    hi, lo = _split2(w_ref[...])
    hi_ref[...] = hi
    lo_ref[...] = lo


def _split_weights(w):
    rows, cols = w.shape
    blk = min(cols, 512)
    assert cols % blk == 0
    spec = pl.BlockSpec((rows, blk), lambda j: (0, j))
    return pl.pallas_call(
        _split_kernel, grid=(cols // blk,), in_specs=[spec], out_specs=[spec, spec],
        out_shape=[jax.ShapeDtypeStruct(w.shape, BF16)] * 2,
        compiler_params=pltpu.CompilerParams(dimension_semantics=("arbitrary",),
                                             vmem_limit_bytes=VMEM_LIMIT),
        name="split",
    )(w)


def _dot3(a, w_hi, w_lo):
    a_hi, a_lo = _split2(a)
    return (jnp.dot(a_hi, w_hi, preferred_element_type=F32)
            + jnp.dot(a_lo, w_hi, preferred_element_type=F32)
            + jnp.dot(a_hi, w_lo, preferred_element_type=F32))


def _cumsum_lanes(triu_bf16, x):
    hi = x.astype(BF16)
    r1 = x - hi.astype(F32)
    mid = r1.astype(BF16)
    lo = (r1 - mid.astype(F32)).astype(BF16)
    return (jnp.dot(hi, triu_bf16, preferred_element_type=F32)
            + jnp.dot(mid, triu_bf16, preferred_element_type=F32)
            + jnp.dot(lo, triu_bf16, preferred_element_type=F32))


def _rms(x, g):
    return x * lax.rsqrt(jnp.mean(x * x, axis=-1, keepdims=True) + EPS) * g


def _layer_norm(x, g, b=None):
    mu = jnp.mean(x, axis=-1, keepdims=True)
    xc = x - mu
    var = jnp.mean(xc * xc, axis=-1, keepdims=True)
    y = xc * lax.rsqrt(var + EPS) * g
    return y if b is None else y + b


def _store_rows(ref, x, row0=0):
    r = x.shape[0]
    for k in range(ROW_SUB):
        ref[pl.ds(row0 * ROW_SUB + k, r, stride=ROW_SUB), :] = x[:, k * LANES:(k + 1) * LANES]


def _load_rows(ref, r, row0=0):
    return jnp.concatenate([ref[pl.ds(row0 * ROW_SUB + k, r, stride=ROW_SUB), :]
                            for k in range(ROW_SUB)], axis=1)


def _route(logits):
    lane = lax.broadcasted_iota(I32, logits.shape, 1).astype(F32)
    neg = jnp.float32(-jnp.inf)
    big = jnp.float32(1e9)
    is_g = lane < N_GROUPS
    gl = jnp.where(is_g, logits, neg)
    gmax = jnp.max(gl, axis=1, keepdims=True)
    gsel = jnp.min(jnp.where(gl == gmax, lane, big), axis=1, keepdims=True)
    pg = 1.0 / jnp.sum(jnp.where(is_g, jnp.exp(gl - gmax), 0.0), axis=1, keepdims=True)
    lo = N_GROUPS + EXP_PER_GROUP * gsel
    emask = (lane >= lo) & (lane < lo + EXP_PER_GROUP)
    el = jnp.where(emask, logits, neg)
    v1 = jnp.max(el, axis=1, keepdims=True)
    i1 = jnp.min(jnp.where(el == v1, lane, big), axis=1, keepdims=True)
    el2 = jnp.where(lane == i1, neg, el)
    v2 = jnp.max(el2, axis=1, keepdims=True)
    i2 = jnp.min(jnp.where(el2 == v2, lane, big), axis=1, keepdims=True)
    d = jnp.exp(v2 - v1)
    w1 = pg / (1.0 + d)
    w2 = pg * d / (1.0 + d)
    return jnp.where(lane == 0, i1 - N_GROUPS,
                     jnp.where(lane == 1, i2 - N_GROUPS,
                               jnp.where(lane == 2, w1, jnp.where(lane == 3, w2, 0.0))))


def _ada_kernel(c_ref, w_ref, b_ref, o_ref):
    w_hi, w_lo = _split2(w_ref[...])
    o_ref[...] = _dot3(_silu(c_ref[...]), w_hi, w_lo) + b_ref[...]


def _ada(c_all, w_ada, b_ada):
    rows, d = c_all.shape
    n_out = w_ada.shape[1]
    blk = 1024
    return pl.pallas_call(
        _ada_kernel,
        grid=(n_out // blk,),
        in_specs=[pl.BlockSpec((rows, d), lambda j: (0, 0)),
                  pl.BlockSpec((d, blk), lambda j: (0, j)),
                  pl.BlockSpec((1, blk), lambda j: (0, j))],
        out_specs=pl.BlockSpec((rows, blk), lambda j: (0, j)),
        out_shape=jax.ShapeDtypeStruct((rows, n_out), F32),
        compiler_params=pltpu.CompilerParams(dimension_semantics=("arbitrary",),
                                             vmem_limit_bytes=VMEM_LIMIT),
        name="ada",
    )(c_all, w_ada, b_ada.reshape(1, n_out))


def _post(x, attn_cat, mod, g2, wout, wr_hi, wr_lo, rbias, d, wout_lo=None):
    gate1 = mod[:, 2 * d:3 * d]
    sh2 = mod[:, 3 * d:4 * d]
    sc2 = mod[:, 4 * d:5 * d]
    if wout_lo is None:
        proj = jnp.dot(attn_cat.astype(BF16), wout, preferred_element_type=F32)
    else:
        proj = _dot3(attn_cat, wout, wout_lo)
    x1 = x + gate1 * proj
    h2 = _rms(x1, g2) * (1.0 + sc2) + sh2
    logits = _dot3(h2, wr_hi, wr_lo) + rbias
    return x1, h2, _route(logits)


def _mix_prompt_kernel(x_ref, mod_ref, g1_ref, g2_ref, wmain_ref, wgh_ref, wgl_ref, gbias_ref,
                       mhg_ref, wdw_ref, bdw_ref, clg_ref, clb_ref, wout_ref, wrh_ref, wrl_ref,
                       rbias_ref,
                       x1_ref, h2_ref, route_ref, c_ref, n_ref, m_ref, cv_ref,
                       ubuf, yc_s, q_s, k_s, v_s, so_s, hm_s, p_s, u_s, cm_s, nb_s, m_s,
                       *, tt, d, dm, dk):
    t = pl.program_id(1)
    dc = d - dm
    n_lt = dc // LANES
    sub = min(MIX_SUB, tt)

    @pl.when(t == 0)
    def _():
        c_ref[...] = jnp.zeros_like(c_ref)
        nb_s[...] = jnp.zeros_like(nb_s)
        m_s[...] = jnp.zeros_like(m_s)
        ubuf[0:CONV_PAD, :] = jnp.zeros((CONV_PAD, dc), F32)

    mod = mod_ref[0]
    row8 = lax.broadcasted_iota(I32, (8, LANES), 0)
    row = lax.broadcasted_iota(I32, (CHUNK, CHUNK), 0)
    col = lax.broadcasted_iota(I32, (CHUNK, CHUNK), 1)
    causal = col <= row
    triu = (row <= col).astype(BF16)
    neg = jnp.float32(-jnp.inf)
    ones_b = jnp.ones((CHUNK, dk), BF16)
    pad_rows = jnp.zeros((CHUNK - 8, LANES), F32)

    lanes_of = [slice(lt * LANES, (lt + 1) * LANES) for lt in range(n_lt)]
    wrows = [[jnp.broadcast_to(wdw_ref[j:j + 1, ls], (8, LANES)) for j in range(CONV_W)]
             for ls in lanes_of]
    bias = [jnp.broadcast_to(bdw_ref[:, ls], (8, LANES)) for ls in lanes_of]

    def partial_sums(r0, lt):
        blocks = [ubuf[r0 + 8 * a:r0 + 8 * a + 8, lanes_of[lt]] for a in range(CONV_PAD // 8)]
        sums = []
        for s in range(8):
            acc = None
            for a in range(CONV_PAD // 8):
                j = 8 * a + s - CONV_OFF
                if 0 <= j < CONV_W:
                    term = blocks[a] * wrows[lt][j]
                    acc = term if acc is None else acc + term
            sums.append(acc)
        return tuple(sums)

    q_prev = [None] * n_lt
    ca = [jnp.concatenate([c_ref[0, 0, hd], nb_s[hd]], axis=1) for hd in range(N_HEADS)]
    m_prev = [m_s[hd:hd + 1, :] for hd in range(N_HEADS)]

    n_sub = tt // sub
    gates_of = {}
    hc_of = {}

    def rows_of(sb):
        return slice(sb * sub, (sb + 1) * sub)

    def proj_items(sb):
        rs = rows_of(sb)
        st = {}

        def head():
            x = x_ref[0, rs, :]
            h = _rms(x, g1_ref[...]) * (1.0 + mod[:, d:2 * d]) + mod[:, 0:d]
            st["hb"] = h.astype(BF16)
            gates_of[sb] = _dot3(h, wgh_ref[...], wgl_ref[...]) + gbias_ref[...]

        def proj(lo, hi):
            return jnp.dot(st["hb"], wmain_ref[:, lo:hi], preferred_element_type=F32)

        def glu():
            ubuf[CONV_PAD + sb * sub:CONV_PAD + (sb + 1) * sub, :] = (
                proj(4 * dm, 4 * dm + dc) * _sigmoid(proj(4 * dm + dc, 4 * dm + 2 * dc)))

        def q():
            q_s[rs, :] = proj(0, dm).astype(BF16)

        def k():
            k_s[rs, :] = proj(dm, 2 * dm) * (dk ** -0.5)

        def v():
            v_s[rs, :] = proj(2 * dm, 3 * dm).astype(BF16)

        def o():
            so_s[rs, :] = _sigmoid(proj(3 * dm, 4 * dm))

        return [head, glu, q, k, v, o]

    def conv_items(sb):
        items = []
        for lt in range(n_lt):
            for i in range(sb * sub // 8 + 1, (sb + 1) * sub // 8 + 1):
                def block(lt=lt, i=i):
                    if i == 1:
                        q_prev[lt] = partial_sums(0, lt)
                    q_cur = partial_sums(i * 8, lt)
                    cur = ubuf[(i - 1) * 8 + CONV_PAD:i * 8 + CONV_PAD, lanes_of[lt]]
                    y = bias[lt] + q_prev[lt][0] + cur * wrows[lt][CONV_W - 1]
                    for s in range(1, 8):
                        merged = jnp.where(row8 < s, q_cur[s], q_prev[lt][s])
                        y = y + pltpu.roll(merged, 8 - s, 0)
                    yc_s[(i - 1) * 8:i * 8, lanes_of[lt]] = y
                    q_prev[lt] = q_cur
                items.append(block)
        return items

    def post_items(sb):
        def post():
            rs = rows_of(sb)
            cat = jnp.concatenate([hm_s[rs, :], hc_of[sb]], axis=1)
            x1, h2, route = _post(x_ref[0, rs, :], cat, mod, g2_ref[...], wout_ref[...],
                                  wrh_ref[...], wrl_ref[...], rbias_ref[...], d)
            x1_ref[0, rs, :] = x1
            _store_rows(h2_ref, h2, sb * sub)
            route_ref[rs, :] = route
        return [post]

    def interleave(main, side):
        gap = len(main) / (len(side) + 1)
        due, done = gap, 0
        for n, item in enumerate(main):
            item()
            while done < len(side) and n + 1 >= due:
                side[done]()
                done += 1
                due += gap
        for item in side[done:]:
            item()

    for item in proj_items(0):
        item()
    for sb in range(n_sub):
        r_lo = sb * sub
        rs = rows_of(sb)
        side = post_items(sb - 1) if sb > 0 else []
        if sb + 1 < n_sub:
            side = side + proj_items(sb + 1)
        interleave(conv_items(sb), side)
        hc_of[sb] = _silu(_layer_norm(yc_s[rs, :], clg_ref[...], clb_ref[...]))
        gates = gates_of[sb]

        chunks = range(r_lo // CHUNK, (r_lo + sub) // CHUNK)
        b_cols = {}
        for c in chunks:
            r0 = c * CHUNK
            g8 = gates[r0 - r_lo:r0 - r_lo + CHUNK, :].T[0:8, :]
            b8 = _cumsum_lanes(triu, _log_sigmoid(g8))
            pk8 = jnp.where(row8 < N_HEADS, g8 - pltpu.roll(b8, N_HEADS, 0), b8)
            pk = jnp.concatenate([pk8, pad_rows], axis=0).T
            for hd in range(N_HEADS):
                cs = slice(hd * dk, (hd + 1) * dk)
                idx = c * N_HEADS + hd
                kf = k_s[r0:r0 + CHUNK, cs]
                va = jnp.concatenate([v_s[r0:r0 + CHUNK, cs], ones_b], axis=1)
                gm = jnp.where(causal, pk8[hd:hd + 1, :], neg)
                cm = jnp.max(gm, axis=1, keepdims=True)
                s = _bdot_nt(q_s[r0:r0 + CHUNK, cs], kf) * jnp.exp(gm - cm)
                p_s[idx] = jnp.dot(s.astype(BF16), va, preferred_element_type=F32)
                cm_s[idx] = jnp.broadcast_to(cm, (CHUNK, LANES))
                kw = kf * jnp.exp(pk[:, hd:hd + 1] - cm[CHUNK - 1:CHUNK, :])
                u_s[idx] = jnp.dot(kw.T.astype(BF16), va, preferred_element_type=F32)
                b_cols[idx] = pk[:, N_HEADS + hd:N_HEADS + hd + 1]

        for hd in range(N_HEADS):
            cs = slice(hd * dk, (hd + 1) * dk)
            for c in chunks:
                r0 = c * CHUNK
                idx = c * N_HEADS + hd
                cm = cm_s[idx]
                b_col = b_cols[idx]
                mt = jnp.maximum(m_prev[hd], cm)
                f_loc = jnp.exp(cm - mt)
                a_int = jnp.exp(m_prev[hd] - mt)
                qc = jnp.dot(q_s[r0:r0 + CHUNK, cs], ca[hd].astype(BF16),
                             preferred_element_type=F32)
                p = p_s[idx]
                num = f_loc * p[:, :dk] + a_int * qc[:, :dk]
                den = f_loc * p[:, dk:] + a_int * qc[:, dk:]
                hh = num / jnp.maximum(jnp.abs(den), jnp.exp(-(b_col + mt)))
                hm_s[r0:r0 + CHUNK, cs] = (_layer_norm(hh, mhg_ref[hd:hd + 1, :])
                                           * so_s[r0:r0 + CHUNK, cs])
                mt_l = mt[CHUNK - 1:CHUNK, :]
                u = u_s[idx]
                f_l = f_loc[CHUNK - 1:CHUNK, :]
                a_l = a_int[CHUNK - 1:CHUNK, :]
                ca[hd] = jnp.concatenate([a_l * ca[hd][:, :dk] + f_l * u[:, :dk],
                                          a_l * ca[hd][:, dk:] + f_l * u[:, dk:]], axis=1)
                m_prev[hd] = b_col[CHUNK - 1:CHUNK, :] + mt_l

    for item in post_items(n_sub - 1):
        item()

    for hd in range(N_HEADS):
        c_ref[0, 0, hd] = ca[hd][:, :dk]
        nb_s[hd] = ca[hd][:, dk:]
        m_s[hd:hd + 1, :] = m_prev[hd]
    cv_ref[0, 0] = ubuf[tt + CONV_PAD - (CONV_W - 1):tt + CONV_PAD, :]
    ubuf[0:CONV_PAD, :] = ubuf[tt:tt + CONV_PAD, :]

    @pl.when(t == pl.num_programs(1) - 1)
    def _():
        for hd in range(N_HEADS):
            n_ref[0, 0, hd:hd + 1, :] = nb_s[hd].T[0:1, :]

    lane1 = lax.broadcasted_iota(I32, (1, LANES), 1)
    m_row = jnp.zeros((1, LANES), F32)
    for hd in range(N_HEADS):
        m_row = jnp.where(lane1 == hd, m_s[hd:hd + 1, :], m_row)
    m_ref[0] = m_row


def _const_spec(shape):
    nd = len(shape)
    return pl.BlockSpec(shape, lambda *_: (0,) * nd)


def _mix_prompt(x, mod, wts):
    b, t, d = x.shape
    dm = wts["dm"]
    dk = dm // N_HEADS
    dc = d - dm
    tt = min(MIX_TILE, t)
    assert t % tt == 0 and tt % CHUNK == 0 and tt >= CONV_PAD
    nt = t // tt
    kern = functools.partial(_mix_prompt_kernel, tt=tt, d=d, dm=dm, dk=dk)
    const_names = ["g1", "g2", "wmain", "wg_hi", "wg_lo", "gbias", "mhg", "wdw", "bdw", "clg",
                   "clb", "wout", "wr_hi", "wr_lo", "rbias"]
    consts = [wts[k] for k in const_names]
    in_specs = ([pl.BlockSpec((1, tt, d), lambda i, j: (i, j, 0)),
                 pl.BlockSpec((1, 1, mod.shape[-1]), lambda i, j: (i, 0, 0))]
                + [_const_spec(c.shape) for c in consts])
    out_shape = [
        jax.ShapeDtypeStruct((b, t, d), F32),
        jax.ShapeDtypeStruct((b * t * ROW_SUB, LANES), F32),
        jax.ShapeDtypeStruct((b * t, LANES), F32),
        jax.ShapeDtypeStruct((1, b, N_HEADS, dk, dk), F32),
        jax.ShapeDtypeStruct((1, b, N_HEADS, dk), F32),
        jax.ShapeDtypeStruct((b, 1, LANES), F32),
        jax.ShapeDtypeStruct((1, b, CONV_W - 1, dc), F32),
    ]
    out_specs = [
        pl.BlockSpec((1, tt, d), lambda i, j: (i, j, 0)),
        pl.BlockSpec((tt * ROW_SUB, LANES), lambda i, j: (i * nt + j, 0)),
        pl.BlockSpec((tt, LANES), lambda i, j: (i * nt + j, 0)),
        pl.BlockSpec((1, 1, N_HEADS, dk, dk), lambda i, j: (0, i, 0, 0, 0)),
        pl.BlockSpec((1, 1, N_HEADS, dk), lambda i, j: (0, i, 0, 0)),
        pl.BlockSpec((1, 1, LANES), lambda i, j: (i, 0, 0)),
        pl.BlockSpec((1, 1, CONV_W - 1, dc), lambda i, j: (0, i, 0, 0)),
    ]
    n_hc = (tt // CHUNK) * N_HEADS
    scratch = [pltpu.VMEM((tt + CONV_PAD, dc), F32),
               pltpu.VMEM((tt, dc), F32),
               pltpu.VMEM((tt, dm), BF16),
               pltpu.VMEM((tt, dm), F32),
               pltpu.VMEM((tt, dm), BF16),
               pltpu.VMEM((tt, dm), F32),
               pltpu.VMEM((tt, dm), F32),
               pltpu.VMEM((n_hc, CHUNK, 2 * dk), F32),
               pltpu.VMEM((n_hc, CHUNK, 2 * dk), F32),
               pltpu.VMEM((n_hc, CHUNK, LANES), F32),
               pltpu.VMEM((N_HEADS, dk, LANES), F32),
               pltpu.VMEM((8, LANES), F32)]
    return pl.pallas_call(
        kern, grid=(b, nt), in_specs=in_specs, out_specs=out_specs, out_shape=out_shape,
        scratch_shapes=scratch,
        compiler_params=pltpu.CompilerParams(dimension_semantics=("arbitrary", "arbitrary"),
                                             vmem_limit_bytes=VMEM_LIMIT),
        name="mix_p",
    )(x, mod.reshape(b, 1, -1), *consts)


def _s_pre_kernel(x_ref, mod_ref, g1_ref, wmain_ref, wmainlo_ref, wgh_ref, wgl_ref, gbias_ref,
                  wkt_ref, wktlo_ref, wdw_ref, bdw_ref, clg_ref, clb_ref, cache_ref, n0_ref, m0_ref,
                  q_ref, kt_ref, vs_ref, ab_ref, sv_ref, den_ref, eb_ref, o_ref, hc_ref, u_ref,
                  n_ref, m_ref, *, d, dm, dk):
    dc = d - dm
    x = x_ref[...]
    mod = mod_ref[...]
    sh1 = mod[:, 0:d]
    sc1 = mod[:, d:2 * d]
    h = _rms(x, g1_ref[...]) * (1.0 + sc1) + sh1
    z = _dot3(h, wmain_ref[...], wmainlo_ref[...])
    gates = _dot3(h, wgh_ref[...], wgl_ref[...]) + gbias_ref[...]
    scale = dk ** -0.5
    h_hi, h_lo = _split2(h)
    kt = _bdot_nt(wkt_ref[...], h_hi) + _bdot_nt(wktlo_ref[...], h_hi) + _bdot_nt(wkt_ref[...], h_lo)
    kt_ref[...] = (kt * scale).astype(BF16)
    k_all = z[:, dm:2 * dm] * scale
    ga = z[:, 4 * dm:4 * dm + dc]
    gb = z[:, 4 * dm + dc:4 * dm + 2 * dc]
    u = ga * _sigmoid(gb)
    u_ref[...] = u
    acc = jnp.broadcast_to(bdw_ref[...], u.shape) + u * wdw_ref[CONV_W - 1:CONV_W, :]
    for j in range(CONV_W - 1):
        acc = acc + cache_ref[j] * wdw_ref[j:j + 1, :]
    hc_ref[...] = _silu(_layer_norm(acc, clg_ref[...], clb_ref[...]))
    o_ref[...] = z[:, 3 * dm:4 * dm]
    q_ref[...] = z[:, 0:dm]
    m0 = m0_ref[...]
    n0 = n0_ref[...]
    lane1 = lax.broadcasted_iota(I32, (1, LANES), 1)
    m_new = jnp.zeros(m0.shape, F32)
    for hd in range(N_HEADS):
        cs = slice(hd * dk, (hd + 1) * dk)
        ig = gates[:, hd:hd + 1]
        lf = _log_sigmoid(gates[:, N_HEADS + hd:N_HEADS + hd + 1])
        mp = m0[:, hd:hd + 1]
        inter = lf + mp
        mt = jnp.maximum(inter, ig)
        w = jnp.exp(ig - mt)
        a_int = jnp.exp(inter - mt)
        qf = z[:, cs]
        kf = k_all[:, cs]
        vf = z[:, 2 * dm + hd * dk:2 * dm + (hd + 1) * dk]
        s = jnp.sum(qf * kf, axis=1, keepdims=True) * w
        sv_ref[:, cs] = s * vf
        den_ref[:, cs] = jnp.broadcast_to(
            s + a_int * jnp.sum(qf * n0[:, cs], axis=1, keepdims=True), (x.shape[0], dk))
        eb_ref[:, cs] = jnp.broadcast_to(jnp.exp(-mt), (x.shape[0], dk))
        ab_ref[:, cs] = jnp.broadcast_to(a_int, (x.shape[0], dk))
        vs_ref[:, cs] = (vf * w).astype(BF16)
        n_ref[:, cs] = a_int * n0[:, cs] + w * kf
        m_new = jnp.where(lane1 == hd, mt, m_new)
    m_ref[...] = m_new


def _s_state_kernel(q_ref, kt_ref, vs_ref, ab_ref, c0_ref, c_ref, r_ref, *, bb, dk):
    i = pl.program_id(0)
    nb = q_ref.shape[0]
    rows = lax.broadcasted_iota(I32, (nb, dk), 0)

    @pl.when(i == 0)
    def _():
        r_ref[...] = jnp.zeros_like(r_ref)

    a_blk = ab_ref[pl.ds(pl.multiple_of(i * bb, bb), bb), :]
    for j in range(bb):
        sel = rows == i * bb + j
        for hd in range(N_HEADS):
            cs = slice(hd * dk, (hd + 1) * dk)
            c0 = c0_ref[j, hd]
            vmask = jnp.where(sel, vs_ref[:, cs], jnp.zeros((), BF16))
            c_ref[j, hd] = (a_blk[j:j + 1, cs] * c0
                            + jnp.dot(kt_ref[cs, :], vmask, preferred_element_type=F32))
            c_hi, c_lo = _split2(c0)
            q_hi, q_lo = _split2(q_ref[:, cs])
            r = (jnp.dot(q_hi, c_hi, preferred_element_type=F32)
                 + jnp.dot(q_lo, c_hi, preferred_element_type=F32)
                 + jnp.dot(q_hi, c_lo, preferred_element_type=F32))
            r_ref[:, cs] = r_ref[:, cs] + jnp.where(sel, r, 0.0)


def _s_post_kernel(x_ref, mod_ref, g2_ref, mhg_ref, r_ref, ab_ref, sv_ref, den_ref, eb_ref, o_ref,
                   hc_ref, wout_ref, woutlo_ref, wrh_ref, wrl_ref, rbias_ref,
                   x1_ref, h2_ref, route_ref, *, d, dm, dk):
    hm = []
    for hd in range(N_HEADS):
        cs = slice(hd * dk, (hd + 1) * dk)
        num = sv_ref[:, cs] + ab_ref[:, cs] * r_ref[:, cs]
        hh = num / jnp.maximum(jnp.abs(den_ref[:, cs]), eb_ref[:, cs])
        hm.append(_layer_norm(hh, mhg_ref[hd:hd + 1, :]) * _sigmoid(o_ref[:, cs]))
    cat = jnp.concatenate(hm + [hc_ref[...]], axis=1)
    x1, h2, route = _post(x_ref[...], cat, mod_ref[...], g2_ref[...], wout_ref[...], wrh_ref[...],
                          wrl_ref[...], rbias_ref[...], d, wout_lo=woutlo_ref[...])
    x1_ref[...] = x1
    _store_rows(h2_ref, h2)
    route_ref[...] = route


def _mix_sample(x, mod, c0, n0, m0, cache, wts):
    nb, d = x.shape
    dm = wts["dm"]
    dk = dm // N_HEADS
    dc = d - dm
    cp = pltpu.CompilerParams(dimension_semantics=("arbitrary",), vmem_limit_bytes=VMEM_LIMIT)
    cache_t = jnp.transpose(cache, (1, 0, 2))
    m0p = jnp.pad(m0, ((0, 0), (0, LANES - N_HEADS)))
    pre_in = [x, mod, wts["g1"], wts["wmain"], wts["wmain_lo"], wts["wg_hi"], wts["wg_lo"],
              wts["gbias"], wts["wk_t"], wts["wk_t_lo"], wts["wdw"], wts["bdw"], wts["clg"],
              wts["clb"], cache_t, n0.reshape(nb, dm), m0p]
    pre_out = [jax.ShapeDtypeStruct((nb, dm), F32),
               jax.ShapeDtypeStruct((dm, nb), BF16),
               jax.ShapeDtypeStruct((nb, dm), BF16),
               jax.ShapeDtypeStruct((nb, dm), F32),
               jax.ShapeDtypeStruct((nb, dm), F32),
               jax.ShapeDtypeStruct((nb, dm), F32),
               jax.ShapeDtypeStruct((nb, dm), F32),
               jax.ShapeDtypeStruct((nb, dm), F32),
               jax.ShapeDtypeStruct((nb, dc), F32),
               jax.ShapeDtypeStruct((nb, dc), F32),
               jax.ShapeDtypeStruct((nb, dm), F32),
               jax.ShapeDtypeStruct((nb, LANES), F32)]
    (q, kt, vs, ab, sv, den, eb, o, hc, u, n1, m1) = pl.pallas_call(
        functools.partial(_s_pre_kernel, d=d, dm=dm, dk=dk),
        grid=(1,),
        in_specs=[_const_spec(a.shape) for a in pre_in],
        out_specs=[_const_spec(s.shape) for s in pre_out],
        out_shape=pre_out, compiler_params=cp, name="s_pre")(*pre_in)

    bb = 8
    assert nb % bb == 0
    c1, r = pl.pallas_call(
        functools.partial(_s_state_kernel, bb=bb, dk=dk),
        grid=(nb // bb,),
        in_specs=[_const_spec(q.shape), _const_spec(kt.shape), _const_spec(vs.shape),
                  _const_spec(ab.shape),
                  pl.BlockSpec((bb, N_HEADS, dk, dk), lambda i: (i, 0, 0, 0))],
        out_specs=[pl.BlockSpec((bb, N_HEADS, dk, dk), lambda i: (i, 0, 0, 0)),
                   _const_spec((nb, dm))],
        out_shape=[jax.ShapeDtypeStruct((nb, N_HEADS, dk, dk), F32),
                   jax.ShapeDtypeStruct((nb, dm), F32)],
        compiler_params=cp, name="s_state")(q, kt, vs, ab, c0)

    post_in = [x, mod, wts["g2"], wts["mhg"], r, ab, sv, den, eb, o, hc, wts["wout"],
               wts["wout_lo"], wts["wr_hi"], wts["wr_lo"], wts["rbias"]]
    post_out = [jax.ShapeDtypeStruct((nb, d), F32),
                jax.ShapeDtypeStruct((nb * ROW_SUB, LANES), F32),
                jax.ShapeDtypeStruct((nb, LANES), F32)]
    x1, h2, route = pl.pallas_call(
        functools.partial(_s_post_kernel, d=d, dm=dm, dk=dk),
        grid=(1,),
        in_specs=[_const_spec(a.shape) for a in post_in],
        out_specs=[_const_spec(s.shape) for s in post_out],
        out_shape=post_out, compiler_params=cp, name="s_post")(*post_in)
    return x1, h2, route, c1, n1, m1, u


def _scatter_kernel(pos_ref, zoff_ref, src_ref, *rest, tp, tm, create):
    xs_ref, zbuf, sem, zsem = rest[-4:]
    i = pl.program_id(0)

    if create:
        @pl.when(i == 0)
        def _():
            zbuf[...] = jnp.zeros_like(zbuf)

            def zero_copy(e):
                start = pl.multiple_of(zoff_ref[e] * ROW_SUB, tm * ROW_SUB)
                return pltpu.make_async_copy(zbuf, xs_ref.at[pl.ds(start, tm * ROW_SUB)], zsem)

            for e in range(zoff_ref.shape[0]):
                @pl.when(zoff_ref[e] >= 0)
                def _():
                    zero_copy(e).start()
            for e in range(zoff_ref.shape[0]):
                @pl.when(zoff_ref[e] >= 0)
                def _():
                    zero_copy(e).wait()

    def row_start(r, carry):
        src = src_ref.at[pl.ds(pl.multiple_of(r * ROW_SUB, ROW_SUB), ROW_SUB)]
        for slot in range(2):
            dst = pl.multiple_of(pos_ref[0, slot, r] * ROW_SUB, ROW_SUB)
            pltpu.make_async_copy(src, xs_ref.at[pl.ds(dst, ROW_SUB)], sem).start(priority=slot)
        return carry

    lax.fori_loop(0, tp, row_start, 0, unroll=8)
    for slot in range(2):
        pltpu.make_async_copy(src_ref, xs_ref.at[pl.ds(0, tp * ROW_SUB)], sem).wait()


def _scatter_rows(h2, pos3, zoff, xs_or_rows, tm):
    n, c = h2.shape[0] // ROW_SUB, LANES
    tp = pos3.shape[2]
    assert pos3.shape == (n // tp, 2, tp)
    create = isinstance(xs_or_rows, int)
    n_sorted = xs_or_rows * ROW_SUB if create else xs_or_rows.shape[0]
    in_specs = [pl.BlockSpec((1, 2, tp), lambda i: (i, 0, 0), memory_space=pltpu.SMEM),
                pl.BlockSpec(memory_space=pltpu.SMEM),
                pl.BlockSpec((tp * ROW_SUB, c), lambda i: (i, 0))]
    args = [pos3, zoff, h2]
    if not create:
        in_specs.append(pl.BlockSpec(memory_space=pl.ANY))
        args.append(xs_or_rows)
    return pl.pallas_call(
        functools.partial(_scatter_kernel, tp=tp, tm=tm, create=create),
        grid_spec=pltpu.PrefetchScalarGridSpec(
            num_scalar_prefetch=0,
            grid=(n // tp,),
            in_specs=in_specs,
            out_specs=pl.BlockSpec(memory_space=pl.ANY),
            scratch_shapes=[pltpu.VMEM((tm * ROW_SUB, c), F32), pltpu.SemaphoreType.DMA(()),
                            pltpu.SemaphoreType.DMA(())]),
        out_shape=jax.ShapeDtypeStruct((n_sorted, c), F32),
        input_output_aliases={} if create else {3: 0},
        compiler_params=pltpu.CompilerParams(dimension_semantics=("arbitrary",),
                                             vmem_limit_bytes=VMEM_LIMIT),
        name="scatter",
    )(*args)


def _moe_kernel(te_ref, nu_ref, xs_ref, wg_ref, wu_ref, wd_ref, ys_ref, wg_b, wu_b, wd_b):
    i = pl.program_id(0)
    used = i < nu_ref[0]

    @pl.when(used & ((i == 0) | (te_ref[i] != te_ref[jnp.maximum(i - 1, 0)])))
    def _():
        wg_b[...] = wg_ref[0].astype(BF16)
        wu_b[...] = wu_ref[0].astype(BF16)
        wd_b[...] = wd_ref[0].astype(BF16)

    @pl.when(used)
    def _():
        xb = _load_rows(xs_ref, xs_ref.shape[0] // ROW_SUB).astype(BF16)
        g = jnp.dot(xb, wg_b[...], preferred_element_type=F32)
        u = jnp.dot(xb, wu_b[...], preferred_element_type=F32)
        hid = (_silu(g) * u).astype(BF16)
        _store_rows(ys_ref, jnp.dot(hid, wd_b[...], preferred_element_type=F32))

    @pl.when(jnp.logical_not(used))
    def _():
        ys_ref[...] = jnp.zeros_like(ys_ref)


def _moe(xs, tile_expert, n_used, w_gate, w_up, w_down, tm):
    p, c = xs.shape[0] // ROW_SUB, LANES
    ne, d, de = w_gate.shape
    n_tiles = p // tm

    def x_map(i, te, nu):
        return (jnp.minimum(i, jnp.maximum(nu[0] - 1, 0)), 0)

    def w_map(i, te, nu):
        return (te[i], 0, 0)

    return pl.pallas_call(
        _moe_kernel,
        grid_spec=pltpu.PrefetchScalarGridSpec(
            num_scalar_prefetch=2,
            grid=(n_tiles,),
            in_specs=[pl.BlockSpec((tm * ROW_SUB, c), x_map),
                      pl.BlockSpec((1, d, de), w_map),
                      pl.BlockSpec((1, d, de), w_map),
                      pl.BlockSpec((1, de, d), w_map)],
            out_specs=pl.BlockSpec((tm * ROW_SUB, c), lambda i, te, nu: (i, 0)),
            scratch_shapes=[pltpu.VMEM((d, de), BF16), pltpu.VMEM((d, de), BF16),
                            pltpu.VMEM((de, d), BF16)]),
        out_shape=jax.ShapeDtypeStruct((p * ROW_SUB, c), F32),
        compiler_params=pltpu.CompilerParams(dimension_semantics=("arbitrary",),
                                             vmem_limit_bytes=VMEM_LIMIT),
        name="moe",
    )(tile_expert, n_used, xs, w_gate, w_up, w_down)


def _fin_kernel(pos_ref, posn_ref, x1_ref, route_ref, mod_ref, fg_ref, ys_ref, y_ref, ybuf, sem,
                *, tp, d, n_steps):
    i = pl.program_id(0)
    cur = i % 2

    def issue(p_ref, buf):
        def row_start(r, carry):
            dst = pl.ds(pl.multiple_of(r * ROW_SUB, ROW_SUB), ROW_SUB)
            for slot in range(2):
                src = pl.multiple_of(p_ref[0, slot, r] * ROW_SUB, ROW_SUB)
                pltpu.make_async_copy(ys_ref.at[pl.ds(src, ROW_SUB)],
                                      ybuf.at[buf, slot, dst], sem.at[buf]).start(priority=slot)
            return carry
        lax.fori_loop(0, tp, row_start, 0, unroll=8)

    def wait_buf(buf):
        for slot in range(2):
            pltpu.make_async_copy(ys_ref.at[pl.ds(0, tp * ROW_SUB)], ybuf.at[buf, slot],
                                  sem.at[buf]).wait()

    @pl.when(i == 0)
    def _():
        issue(pos_ref, 0)

    wait_buf(cur)
    nxt = 1 - cur
    per_token_mod = mod_ref.shape[1] != 1
    for lo in range(0, tp, FIN_CHUNK):
        hi = min(lo + FIN_CHUNK, tp)
        for r in range(lo, hi):
            for slot in range(2):
                src = pl.multiple_of(posn_ref[0, slot, r] * ROW_SUB, ROW_SUB)
                pltpu.make_async_copy(
                    ys_ref.at[pl.ds(src, ROW_SUB)],
                    ybuf.at[nxt, slot, pl.ds(r * ROW_SUB, ROW_SUB)], sem.at[nxt]).start(priority=slot)
        route = route_ref[lo:hi, :]
        moe = (route[:, 2:3] * _load_rows(ybuf.at[cur, 0], hi - lo, lo)
               + route[:, 3:4] * _load_rows(ybuf.at[cur, 1], hi - lo, lo))
        mod = mod_ref[0, lo:hi, :] if per_token_mod else mod_ref[0]
        y_ref[0, lo:hi, :] = _rms(x1_ref[0, lo:hi, :] + mod[:, 5 * d:6 * d] * moe, fg_ref[...])

    @pl.when(i == n_steps - 1)
    def _():
        wait_buf(nxt)


def _finish(x1, route, mod, final_g, ys, pos3):
    b, t, d = x1.shape
    tp = pos3.shape[2]
    assert t % tp == 0
    nt = t // tp
    n_steps = b * nt
    assert pos3.shape[0] == n_steps
    blk0 = 0
    if mod.shape[1] == 1:
        mod_spec = pl.BlockSpec((1, 1, 6 * d), lambda i: (i // nt, 0, 0))
    else:
        mod_spec = pl.BlockSpec((1, tp, 6 * d), lambda i: (i // nt, i % nt, 0))
    return pl.pallas_call(
        functools.partial(_fin_kernel, tp=tp, d=d, n_steps=n_steps),
        grid_spec=pltpu.PrefetchScalarGridSpec(
            num_scalar_prefetch=0,
            grid=(n_steps,),
            in_specs=[pl.BlockSpec((1, 2, tp), lambda i: (blk0 + i, 0, 0), memory_space=pltpu.SMEM),
                      pl.BlockSpec((1, 2, tp), lambda i: (blk0 + jnp.minimum(i + 1, n_steps - 1), 0, 0),
                                   memory_space=pltpu.SMEM),
                      pl.BlockSpec((1, tp, d), lambda i: (i // nt, i % nt, 0)),
                      pl.BlockSpec((tp, LANES), lambda i: (i, 0)),
                      mod_spec,
                      _const_spec((1, d)),
                      pl.BlockSpec(memory_space=pl.ANY)],
            out_specs=pl.BlockSpec((1, tp, d), lambda i: (i // nt, i % nt, 0)),
            scratch_shapes=[pltpu.VMEM((2, 2, tp * ROW_SUB, LANES), F32),
                            pltpu.SemaphoreType.DMA((2,))]),
        out_shape=jax.ShapeDtypeStruct((b, t, d), F32),
        compiler_params=pltpu.CompilerParams(dimension_semantics=("arbitrary",),
                                             vmem_limit_bytes=VMEM_LIMIT),
        name="fin",
    )(pos3, pos3, x1, route, mod, final_g.reshape(1, d), ys)


def _expert_onehots(route):
    lane = lax.broadcasted_iota(I32, route.shape, 1).astype(F32)
    return ((lane == route[:, 0:1]).astype(F32), (lane == route[:, 1:2]).astype(F32))


def _count_kernel(route_ref, base_ref, tot_ref, acc):
    @pl.when(pl.program_id(0) == 0)
    def _():
        acc[...] = jnp.zeros_like(acc)

    oh0, oh1 = _expert_onehots(route_ref[...])
    base_ref[0] = acc[...]
    acc[...] = acc[...] + jnp.sum(oh0 + oh1, axis=0, keepdims=True)
    tot_ref[...] = acc[...]


def _pos_kernel(route_ref, base_ref, seg_ref, pos_ref, *, tp):
    oh0, oh1 = _expert_onehots(route_ref[...])
    r_i = lax.broadcasted_iota(I32, (tp, tp), 0)
    c_i = lax.broadcasted_iota(I32, (tp, tp), 1)
    before = jnp.dot((c_i < r_i).astype(BF16), (oh0 + oh1).astype(BF16),
                     preferred_element_type=F32)
    offs = seg_ref[0:1, :] + base_ref[0][0:1, :] + before
    ones8 = jnp.ones((8, LANES), BF16)

    def lane_sum_as_row(v):
        hi = v.astype(BF16)
        r1 = v - hi.astype(F32)
        mid = r1.astype(BF16)
        lo = (r1 - mid.astype(F32)).astype(BF16)
        return _bdot_nt(ones8, hi) + _bdot_nt(ones8, mid) + _bdot_nt(ones8, lo)

    p0 = lane_sum_as_row(oh0 * offs)
    p1 = lane_sum_as_row(oh1 * offs)
    row = lax.broadcasted_iota(I32, (8, tp), 0)
    pos_ref[0] = jnp.where(row == 0, p0, jnp.where(row == 1, p1, 0.0)).astype(I32)


def _count(route, tp):
    n = route.shape[0]
    assert n % tp == 0
    return pl.pallas_call(
        _count_kernel, grid=(n // tp,),
        in_specs=[pl.BlockSpec((tp, LANES), lambda i: (i, 0))],
        out_specs=[pl.BlockSpec((1, 8, LANES), lambda i: (i, 0, 0)), _const_spec((8, LANES))],
        out_shape=[jax.ShapeDtypeStruct((n // tp, 8, LANES), F32),
                   jax.ShapeDtypeStruct((8, LANES), F32)],
        scratch_shapes=[pltpu.VMEM((8, LANES), F32)],
        compiler_params=pltpu.CompilerParams(dimension_semantics=("arbitrary",),
                                             vmem_limit_bytes=VMEM_LIMIT),
        name="count",
    )(route)


def _positions(route, base, seg_start, tp):
    n = route.shape[0]
    seg = jnp.broadcast_to(jnp.pad(seg_start.astype(F32), (0, LANES - N_EXPERTS))[None, :], (8, LANES))
    pos = pl.pallas_call(
        functools.partial(_pos_kernel, tp=tp), grid=(n // tp,),
        in_specs=[pl.BlockSpec((tp, LANES), lambda i: (i, 0)),
                  pl.BlockSpec((1, 8, LANES), lambda i: (i, 0, 0)),
                  _const_spec((8, LANES))],
        out_specs=pl.BlockSpec((1, 8, tp), lambda i: (i, 0, 0)),
        out_shape=jax.ShapeDtypeStruct((n // tp, 8, tp), I32),
        compiler_params=pltpu.CompilerParams(dimension_semantics=("arbitrary",),
                                             vmem_limit_bytes=VMEM_LIMIT),
        name="positions",
    )(route, base, seg)
    return pos[:, 0:2, :]


def _plan(counts, tm, n_tiles):
    tiles_per = (counts + tm - 1) // tm
    tile_end = jnp.cumsum(tiles_per)
    seg_start = (tile_end - tiles_per) * tm
    n_used = tile_end[-1]
    tile_ids = jnp.arange(n_tiles, dtype=I32)
    tile_expert = jnp.sum((tile_ids[:, None] >= tile_end[None, :]).astype(I32), axis=1)
    last_used = jnp.sum((n_used - 1 >= tile_end).astype(I32))
    tile_expert = jnp.where(tile_ids < n_used, tile_expert, last_used).astype(I32)
    z_expert = jnp.where((counts % tm) != 0, (tile_end - 1) * tm, -1)
    spare = n_used + jnp.arange(N_EXPERTS, dtype=I32)
    z_spare = jnp.where(spare < n_tiles, spare * tm, -1)
    zoff = jnp.concatenate([z_expert, z_spare]).astype(I32)
    return seg_start, tile_expert, n_used.reshape(1).astype(I32), zoff


def kernel(x_prompt, x_sample, c_prompt, c_sample, state_mlstm_C, state_mlstm_n, state_mlstm_m,
           cache_conv, w_ada, b_ada, norm1_g, w_in, b_igate, b_fgate, mh_norm_g, w_dw, b_dw,
           conv_ln_g, conv_ln_b, w_out, norm2_g, w_grp_router, b_grp_router, w_exp_router,
           b_exp_router, w_gate, w_up, w_down, final_g):
    depth = w_ada.shape[0]
    assert depth == 1, "one layer per step"
    bp, tp_, d = x_prompt.shape
    bs = x_sample.shape[0]
    assert x_sample.shape[1] == 1 and d == ROW_SUB * LANES
    dk = state_mlstm_C.shape[-1]
    dm = N_HEADS * dk
    dc = d - dm
    l = 0

    win = w_in[l]
    w_qkvo = win[:, 0:4 * dm]
    w_gates = win[:, 4 * dm:4 * dm + 2 * N_HEADS]
    w_glu = win[:, 4 * dm + 2 * N_HEADS:]
    wmain, wmain_lo = _split_weights(jnp.concatenate([w_qkvo, w_glu], axis=1))
    wout_hi, wout_lo = _split_weights(w_out[l])
    wkt_hi, wkt_lo = _split_weights(win[:, dm:2 * dm].T)
    wg_pad = jnp.pad(w_gates, ((0, 0), (0, LANES - 2 * N_HEADS)))
    wg_hi, wg_lo = _split_weights(wg_pad)
    gbias = jnp.pad(jnp.concatenate([b_igate[l], b_fgate[l]]), (0, LANES - 2 * N_HEADS)).reshape(1, LANES)
    w_r = jnp.concatenate([w_grp_router[l], w_exp_router[l]], axis=1)
    n_r = N_GROUPS + N_EXPERTS
    wr_hi, wr_lo = _split_weights(jnp.pad(w_r, ((0, 0), (0, LANES - n_r))))
    rbias = jnp.pad(jnp.concatenate([b_grp_router[l], b_exp_router[l]]), (0, LANES - n_r)).reshape(1, LANES)
    wts = dict(dm=dm, g1=norm1_g[l].reshape(1, d), g2=norm2_g[l].reshape(1, d), wmain=wmain,
               wg_hi=wg_hi, wg_lo=wg_lo, gbias=gbias, mhg=mh_norm_g[l],
               wdw=w_dw[l].reshape(CONV_W, dc), bdw=b_dw[l].reshape(1, dc),
               clg=conv_ln_g[l].reshape(1, dc), clb=conv_ln_b[l].reshape(1, dc),
               wout=wout_hi, wr_hi=wr_hi, wr_lo=wr_lo, rbias=rbias, wk_t=wkt_hi,
               wmain_lo=wmain_lo, wout_lo=wout_lo, wk_t_lo=wkt_lo)

    mod = _ada(jnp.concatenate([c_prompt, c_sample], axis=0), w_ada[l], b_ada[l])
    mod_p, mod_s = mod[:bp], mod[bp:]

    n_p = bp * tp_
    n_all = n_p + bs
    x1_p, h2_p, route_p, c_p, n_pr, m_p, cv_p = _mix_prompt(x_prompt, mod_p, wts)
    x1_s, h2_s, route_s, c_s, n_s, m_s, u_s = _mix_sample(
        x_sample.reshape(bs, d), mod_s, state_mlstm_C[l], state_mlstm_n[l], state_mlstm_m[l],
        cache_conv[l], wts)

    tm = MOE_TILE
    n_tiles = (2 * n_all) // tm + N_EXPERTS
    tp_p, tp_s = min(PERM_TILE, tp_), min(PERM_TILE, bs)
    base_p, tot_p = _count(route_p, tp_p)
    base_s, tot_s = _count(route_s, tp_s)
    counts = (tot_p[0, :N_EXPERTS] + tot_s[0, :N_EXPERTS]).astype(I32)
    seg_start, tile_expert, n_used, zoff = _plan(counts, tm, n_tiles)
    pos_p = _positions(route_p, base_p, seg_start, tp_p)
    pos_s = _positions(route_s, base_s + tot_p[None], seg_start, tp_s)

    xs = _scatter_rows(h2_p, pos_p, zoff, n_tiles * tm, tm)
    xs = _scatter_rows(h2_s, pos_s, zoff, xs, tm)
    ys = _moe(xs, tile_expert, n_used, w_gate[l], w_up[l], w_down[l], tm)

    y_p = _finish(x1_p, route_p, mod_p.reshape(bp, 1, -1), final_g, ys, pos_p)
    y_s = _finish(x1_s.reshape(1, bs, d), route_s, mod_s.reshape(1, bs, -1), final_g, ys,
                  pos_s).reshape(bs, 1, d)

    conv_s = jnp.concatenate([cache_conv[l][:, 1:, :], u_s[:, None, :]], axis=1)
    return (y_p, y_s,
            c_p, n_pr, m_p[:, 0, :N_HEADS].reshape(1, bp, N_HEADS), cv_p,
            c_s[None], n_s.reshape(1, bs, N_HEADS, dk), m_s[:, :N_HEADS].reshape(1, bs, N_HEADS),
            conv_s[None])
```

```python
import functools

import jax
import jax.numpy as jnp
from jax import lax
from jax.experimental import pallas as pl
from jax.experimental.pallas import tpu as pltpu

F32 = jnp.float32
BF16 = jnp.bfloat16
I32 = jnp.int32

EPS = 1e-6
LANES = 128
ROW_SUB = 8
CHUNK = 128
N_HEADS = 4
N_GROUPS = 4
EXP_PER_GROUP = 8
N_EXPERTS = N_GROUPS * EXP_PER_GROUP
CONV_W = 31
CONV_PAD = 32
CONV_OFF = CONV_PAD - (CONV_W - 1)
MIX_TILE = 512
MIX_SUB = 256
MOE_TILE = 512
PERM_TILE = 512
FIN_CHUNK = 64
VMEM_LIMIT = 56 * 1024 * 1024


def _sigmoid(x):
    return 1.0 / (1.0 + jnp.exp(-x))


def _silu(x):
    return x * _sigmoid(x)


def _log_sigmoid(x):
    return jnp.minimum(x, 0.0) - jnp.log(1.0 + jnp.exp(-jnp.abs(x)))


def _bdot(a, b):
    return jnp.dot(a.astype(BF16), b.astype(BF16), preferred_element_type=F32)


def _bdot_nt(a, b):
    return lax.dot_general(a.astype(BF16), b.astype(BF16), (((1,), (1,)), ((), ())),
                           preferred_element_type=F32)


def _split2(x):
    hi = x.astype(BF16)
    lo = (x - hi.astype(F32)).astype(BF16)
    return hi, lo


def _split_kernel(w_ref, hi_ref, lo_ref):
    hi, lo = _split2(w_ref[...])
    hi_ref[...] = hi
    lo_ref[...] = lo


def _split_weights(w):
    rows, cols = w.shape
    blk = min(cols, 512)
    assert cols % blk == 0
    spec = pl.BlockSpec((rows, blk), lambda j: (0, j))
    return pl.pallas_call(
        _split_kernel, grid=(cols // blk,), in_specs=[spec], out_specs=[spec, spec],
        out_shape=[jax.ShapeDtypeStruct(w.shape, BF16)] * 2,
        compiler_params=pltpu.CompilerParams(dimension_semantics=("arbitrary",),
                                             vmem_limit_bytes=VMEM_LIMIT),
        name="split",
    )(w)


def _dot3(a, w_hi, w_lo):
    a_hi, a_lo = _split2(a)
    return (jnp.dot(a_hi, w_hi, preferred_element_type=F32)
            + jnp.dot(a_lo, w_hi, preferred_element_type=F32)
            + jnp.dot(a_hi, w_lo, preferred_element_type=F32))


def _cumsum_lanes(triu_bf16, x):
    hi = x.astype(BF16)
    r1 = x - hi.astype(F32)
    mid = r1.astype(BF16)
    lo = (r1 - mid.astype(F32)).astype(BF16)
    return (jnp.dot(hi, triu_bf16, preferred_element_type=F32)
            + jnp.dot(mid, triu_bf16, preferred_element_type=F32)
            + jnp.dot(lo, triu_bf16, preferred_element_type=F32))


def _rms(x, g):
    return x * lax.rsqrt(jnp.mean(x * x, axis=-1, keepdims=True) + EPS) * g


def _layer_norm(x, g, b=None):
    mu = jnp.mean(x, axis=-1, keepdims=True)
    xc = x - mu
    var = jnp.mean(xc * xc, axis=-1, keepdims=True)
    y = xc * lax.rsqrt(var + EPS) * g
    return y if b is None else y + b


def _store_rows(ref, x, row0=0):
    r = x.shape[0]
    for k in range(ROW_SUB):
        ref[pl.ds(row0 * ROW_SUB + k, r, stride=ROW_SUB), :] = x[:, k * LANES:(k + 1) * LANES]


def _load_rows(ref, r, row0=0):
    return jnp.concatenate([ref[pl.ds(row0 * ROW_SUB + k, r, stride=ROW_SUB), :]
                            for k in range(ROW_SUB)], axis=1)


def _route(logits):
    lane = lax.broadcasted_iota(I32, logits.shape, 1).astype(F32)
    neg = jnp.float32(-jnp.inf)
    big = jnp.float32(1e9)
    is_g = lane < N_GROUPS
    gl = jnp.where(is_g, logits, neg)
    gmax = jnp.max(gl, axis=1, keepdims=True)
    gsel = jnp.min(jnp.where(gl == gmax, lane, big), axis=1, keepdims=True)
    pg = 1.0 / jnp.sum(jnp.where(is_g, jnp.exp(gl - gmax), 0.0), axis=1, keepdims=True)
    lo = N_GROUPS + EXP_PER_GROUP * gsel
    emask = (lane >= lo) & (lane < lo + EXP_PER_GROUP)
    el = jnp.where(emask, logits, neg)
    v1 = jnp.max(el, axis=1, keepdims=True)
    i1 = jnp.min(jnp.where(el == v1, lane, big), axis=1, keepdims=True)
    el2 = jnp.where(lane == i1, neg, el)
    v2 = jnp.max(el2, axis=1, keepdims=True)
    i2 = jnp.min(jnp.where(el2 == v2, lane, big), axis=1, keepdims=True)
    d = jnp.exp(v2 - v1)
    w1 = pg / (1.0 + d)
    w2 = pg * d / (1.0 + d)
    return jnp.where(lane == 0, i1 - N_GROUPS,
                     jnp.where(lane == 1, i2 - N_GROUPS,
                               jnp.where(lane == 2, w1, jnp.where(lane == 3, w2, 0.0))))


def _ada_kernel(c_ref, w_ref, b_ref, o_ref):
    w_hi, w_lo = _split2(w_ref[...])
    o_ref[...] = _dot3(_silu(c_ref[...]), w_hi, w_lo) + b_ref[...]


def _ada(c_all, w_ada, b_ada):
    rows, d = c_all.shape
    n_out = w_ada.shape[1]
    blk = 1024
    return pl.pallas_call(
        _ada_kernel,
        grid=(n_out // blk,),
        in_specs=[pl.BlockSpec((rows, d), lambda j: (0, 0)),
                  pl.BlockSpec((d, blk), lambda j: (0, j)),
                  pl.BlockSpec((1, blk), lambda j: (0, j))],
        out_specs=pl.BlockSpec((rows, blk), lambda j: (0, j)),
        out_shape=jax.ShapeDtypeStruct((rows, n_out), F32),
        compiler_params=pltpu.CompilerParams(dimension_semantics=("arbitrary",),
                                             vmem_limit_bytes=VMEM_LIMIT),
        name="ada",
    )(c_all, w_ada, b_ada.reshape(1, n_out))


def _post(x, attn_cat, mod, g2, wout, wr_hi, wr_lo, rbias, d, wout_lo=None):
    gate1 = mod[:, 2 * d:3 * d]
    sh2 = mod[:, 3 * d:4 * d]
    sc2 = mod[:, 4 * d:5 * d]
    if wout_lo is None:
        proj = jnp.dot(attn_cat.astype(BF16), wout, preferred_element_type=F32)
    else:
        proj = _dot3(attn_cat, wout, wout_lo)
    x1 = x + gate1 * proj
    h2 = _rms(x1, g2) * (1.0 + sc2) + sh2
    logits = _dot3(h2, wr_hi, wr_lo) + rbias
    return x1, h2, _route(logits)


def _mix_prompt_kernel(x_ref, mod_ref, g1_ref, g2_ref, wmain_ref, wgh_ref, wgl_ref, gbias_ref,
                       mhg_ref, wdw_ref, bdw_ref, clg_ref, clb_ref, wout_ref, wrh_ref, wrl_ref,
                       rbias_ref,
                       x1_ref, h2_ref, route_ref, c_ref, n_ref, m_ref, cv_ref,
                       ubuf, yc_s, q_s, k_s, v_s, so_s, hm_s, p_s, u_s, cm_s, nb_s, m_s,
                       *, tt, d, dm, dk):
    t = pl.program_id(1)
    dc = d - dm
    n_lt = dc // LANES
    sub = min(MIX_SUB, tt)

    @pl.when(t == 0)
    def _():
        c_ref[...] = jnp.zeros_like(c_ref)
        nb_s[...] = jnp.zeros_like(nb_s)
        m_s[...] = jnp.zeros_like(m_s)
        ubuf[0:CONV_PAD, :] = jnp.zeros((CONV_PAD, dc), F32)

    mod = mod_ref[0]
    row8 = lax.broadcasted_iota(I32, (8, LANES), 0)
    row = lax.broadcasted_iota(I32, (CHUNK, CHUNK), 0)
    col = lax.broadcasted_iota(I32, (CHUNK, CHUNK), 1)
    causal = col <= row
    triu = (row <= col).astype(BF16)
    neg = jnp.float32(-jnp.inf)
    ones_b = jnp.ones((CHUNK, dk), BF16)
    pad_rows = jnp.zeros((CHUNK - 8, LANES), F32)

    lanes_of = [slice(lt * LANES, (lt + 1) * LANES) for lt in range(n_lt)]
    wrows = [[jnp.broadcast_to(wdw_ref[j:j + 1, ls], (8, LANES)) for j in range(CONV_W)]
             for ls in lanes_of]
    bias = [jnp.broadcast_to(bdw_ref[:, ls], (8, LANES)) for ls in lanes_of]

    def partial_sums(r0, lt):
        blocks = [ubuf[r0 + 8 * a:r0 + 8 * a + 8, lanes_of[lt]] for a in range(CONV_PAD // 8)]
        sums = []
        for s in range(8):
            acc = None
            for a in range(CONV_PAD // 8):
                j = 8 * a + s - CONV_OFF
                if 0 <= j < CONV_W:
                    term = blocks[a] * wrows[lt][j]
                    acc = term if acc is None else acc + term
            sums.append(acc)
        return tuple(sums)

    q_prev = [None] * n_lt
    ca = [jnp.concatenate([c_ref[0, 0, hd], nb_s[hd]], axis=1) for hd in range(N_HEADS)]
    m_prev = [m_s[hd:hd + 1, :] for hd in range(N_HEADS)]

    n_sub = tt // sub
    gates_of = {}
    hc_of = {}

    def rows_of(sb):
        return slice(sb * sub, (sb + 1) * sub)

    def proj_items(sb):
        rs = rows_of(sb)
        st = {}

        def head():
            x = x_ref[0, rs, :]
            h = _rms(x, g1_ref[...]) * (1.0 + mod[:, d:2 * d]) + mod[:, 0:d]
            st["hb"] = h.astype(BF16)
            gates_of[sb] = _dot3(h, wgh_ref[...], wgl_ref[...]) + gbias_ref[...]

        def proj(lo, hi):
            return jnp.dot(st["hb"], wmain_ref[:, lo:hi], preferred_element_type=F32)

        def glu():
            ubuf[CONV_PAD + sb * sub:CONV_PAD + (sb + 1) * sub, :] = (
                proj(4 * dm, 4 * dm + dc) * _sigmoid(proj(4 * dm + dc, 4 * dm + 2 * dc)))

        def q():
            q_s[rs, :] = proj(0, dm).astype(BF16)

        def k():
            k_s[rs, :] = proj(dm, 2 * dm) * (dk ** -0.5)

        def v():
            v_s[rs, :] = proj(2 * dm, 3 * dm).astype(BF16)

        def o():
            so_s[rs, :] = _sigmoid(proj(3 * dm, 4 * dm))

        return [head, glu, q, k, v, o]

    def conv_items(sb):
        items = []
        for lt in range(n_lt):
            for i in range(sb * sub // 8 + 1, (sb + 1) * sub // 8 + 1):
                def block(lt=lt, i=i):
                    if i == 1:
                        q_prev[lt] = partial_sums(0, lt)
                    q_cur = partial_sums(i * 8, lt)
                    cur = ubuf[(i - 1) * 8 + CONV_PAD:i * 8 + CONV_PAD, lanes_of[lt]]
                    y = bias[lt] + q_prev[lt][0] + cur * wrows[lt][CONV_W - 1]
                    for s in range(1, 8):
                        merged = jnp.where(row8 < s, q_cur[s], q_prev[lt][s])
                        y = y + pltpu.roll(merged, 8 - s, 0)
                    yc_s[(i - 1) * 8:i * 8, lanes_of[lt]] = y
                    q_prev[lt] = q_cur
                items.append(block)
        return items

    def post_items(sb):
        def post():
            rs = rows_of(sb)
            cat = jnp.concatenate([hm_s[rs, :], hc_of[sb]], axis=1)
            x1, h2, route = _post(x_ref[0, rs, :], cat, mod, g2_ref[...], wout_ref[...],
                                  wrh_ref[...], wrl_ref[...], rbias_ref[...], d)
            x1_ref[0, rs, :] = x1
            _store_rows(h2_ref, h2, sb * sub)
            route_ref[rs, :] = route
        return [post]

    def interleave(main, side):
        gap = len(main) / (len(side) + 1)
        due, done = gap, 0
        for n, item in enumerate(main):
            item()
            while done < len(side) and n + 1 >= due:
                side[done]()
                done += 1
                due += gap
        for item in side[done:]:
            item()

    for item in proj_items(0):
        item()
    for sb in range(n_sub):
        r_lo = sb * sub
        rs = rows_of(sb)
        side = post_items(sb - 1) if sb > 0 else []
        if sb + 1 < n_sub:
            side = side + proj_items(sb + 1)
        interleave(conv_items(sb), side)
        hc_of[sb] = _silu(_layer_norm(yc_s[rs, :], clg_ref[...], clb_ref[...]))
        gates = gates_of[sb]

        chunks = range(r_lo // CHUNK, (r_lo + sub) // CHUNK)
        b_cols = {}
        for c in chunks:
            r0 = c * CHUNK
            g8 = gates[r0 - r_lo:r0 - r_lo + CHUNK, :].T[0:8, :]
            b8 = _cumsum_lanes(triu, _log_sigmoid(g8))
            pk8 = jnp.where(row8 < N_HEADS, g8 - pltpu.roll(b8, N_HEADS, 0), b8)
            pk = jnp.concatenate([pk8, pad_rows], axis=0).T
            for hd in range(N_HEADS):
                cs = slice(hd * dk, (hd + 1) * dk)
                idx = c * N_HEADS + hd
                kf = k_s[r0:r0 + CHUNK, cs]
                va = jnp.concatenate([v_s[r0:r0 + CHUNK, cs], ones_b], axis=1)
                gm = jnp.where(causal, pk8[hd:hd + 1, :], neg)
                cm = jnp.max(gm, axis=1, keepdims=True)
                s = _bdot_nt(q_s[r0:r0 + CHUNK, cs], kf) * jnp.exp(gm - cm)
                p_s[idx] = jnp.dot(s.astype(BF16), va, preferred_element_type=F32)
                cm_s[idx] = jnp.broadcast_to(cm, (CHUNK, LANES))
                kw = kf * jnp.exp(pk[:, hd:hd + 1] - cm[CHUNK - 1:CHUNK, :])
                u_s[idx] = jnp.dot(kw.T.astype(BF16), va, preferred_element_type=F32)
                b_cols[idx] = pk[:, N_HEADS + hd:N_HEADS + hd + 1]

        for hd in range(N_HEADS):
            cs = slice(hd * dk, (hd + 1) * dk)
            for c in chunks:
                r0 = c * CHUNK
                idx = c * N_HEADS + hd
                cm = cm_s[idx]
                b_col = b_cols[idx]
                mt = jnp.maximum(m_prev[hd], cm)
                f_loc = jnp.exp(cm - mt)
                a_int = jnp.exp(m_prev[hd] - mt)
                qc = jnp.dot(q_s[r0:r0 + CHUNK, cs], ca[hd].astype(BF16),
                             preferred_element_type=F32)
                p = p_s[idx]
                num = f_loc * p[:, :dk] + a_int * qc[:, :dk]
                den = f_loc * p[:, dk:] + a_int * qc[:, dk:]
                hh = num / jnp.maximum(jnp.abs(den), jnp.exp(-(b_col + mt)))
                hm_s[r0:r0 + CHUNK, cs] = (_layer_norm(hh, mhg_ref[hd:hd + 1, :])
                                           * so_s[r0:r0 + CHUNK, cs])
                mt_l = mt[CHUNK - 1:CHUNK, :]
                u = u_s[idx]
                f_l = f_loc[CHUNK - 1:CHUNK, :]
                a_l = a_int[CHUNK - 1:CHUNK, :]
                ca[hd] = jnp.concatenate([a_l * ca[hd][:, :dk] + f_l * u[:, :dk],
                                          a_l * ca[hd][:, dk:] + f_l * u[:, dk:]], axis=1)
                m_prev[hd] = b_col[CHUNK - 1:CHUNK, :] + mt_l

    for item in post_items(n_sub - 1):
        item()

    for hd in range(N_HEADS):
        c_ref[0, 0, hd] = ca[hd][:, :dk]
        nb_s[hd] = ca[hd][:, dk:]
        m_s[hd:hd + 1, :] = m_prev[hd]
    cv_ref[0, 0] = ubuf[tt + CONV_PAD - (CONV_W - 1):tt + CONV_PAD, :]
    ubuf[0:CONV_PAD, :] = ubuf[tt:tt + CONV_PAD, :]

    @pl.when(t == pl.num_programs(1) - 1)
    def _():
        for hd in range(N_HEADS):
            n_ref[0, 0, hd:hd + 1, :] = nb_s[hd].T[0:1, :]

    lane1 = lax.broadcasted_iota(I32, (1, LANES), 1)
    m_row = jnp.zeros((1, LANES), F32)
    for hd in range(N_HEADS):
        m_row = jnp.where(lane1 == hd, m_s[hd:hd + 1, :], m_row)
    m_ref[0] = m_row


def _const_spec(shape):
    nd = len(shape)
    return pl.BlockSpec(shape, lambda *_: (0,) * nd)


def _mix_prompt(x, mod, wts):
    b, t, d = x.shape
    dm = wts["dm"]
    dk = dm // N_HEADS
    dc = d - dm
    tt = min(MIX_TILE, t)
    assert t % tt == 0 and tt % CHUNK == 0 and tt >= CONV_PAD
    nt = t // tt
    kern = functools.partial(_mix_prompt_kernel, tt=tt, d=d, dm=dm, dk=dk)
    const_names = ["g1", "g2", "wmain", "wg_hi", "wg_lo", "gbias", "mhg", "wdw", "bdw", "clg",
                   "clb", "wout", "wr_hi", "wr_lo", "rbias"]
    consts = [wts[k] for k in const_names]
    in_specs = ([pl.BlockSpec((1, tt, d), lambda i, j: (i, j, 0)),
                 pl.BlockSpec((1, 1, mod.shape[-1]), lambda i, j: (i, 0, 0))]
                + [_const_spec(c.shape) for c in consts])
    out_shape = [
        jax.ShapeDtypeStruct((b, t, d), F32),
        jax.ShapeDtypeStruct((b * t * ROW_SUB, LANES), F32),
        jax.ShapeDtypeStruct((b * t, LANES), F32),
        jax.ShapeDtypeStruct((1, b, N_HEADS, dk, dk), F32),
        jax.ShapeDtypeStruct((1, b, N_HEADS, dk), F32),
        jax.ShapeDtypeStruct((b, 1, LANES), F32),
        jax.ShapeDtypeStruct((1, b, CONV_W - 1, dc), F32),
    ]
    out_specs = [
        pl.BlockSpec((1, tt, d), lambda i, j: (i, j, 0)),
        pl.BlockSpec((tt * ROW_SUB, LANES), lambda i, j: (i * nt + j, 0)),
        pl.BlockSpec((tt, LANES), lambda i, j: (i * nt + j, 0)),
        pl.BlockSpec((1, 1, N_HEADS, dk, dk), lambda i, j: (0, i, 0, 0, 0)),
        pl.BlockSpec((1, 1, N_HEADS, dk), lambda i, j: (0, i, 0, 0)),
        pl.BlockSpec((1, 1, LANES), lambda i, j: (i, 0, 0)),
        pl.BlockSpec((1, 1, CONV_W - 1, dc), lambda i, j: (0, i, 0, 0)),
    ]
    n_hc = (tt // CHUNK) * N_HEADS
    scratch = [pltpu.VMEM((tt + CONV_PAD, dc), F32),
               pltpu.VMEM((tt, dc), F32),
               pltpu.VMEM((tt, dm), BF16),
               pltpu.VMEM((tt, dm), F32),
               pltpu.VMEM((tt, dm), BF16),
               pltpu.VMEM((tt, dm), F32),
               pltpu.VMEM((tt, dm), F32),
               pltpu.VMEM((n_hc, CHUNK, 2 * dk), F32),
               pltpu.VMEM((n_hc, CHUNK, 2 * dk), F32),
               pltpu.VMEM((n_hc, CHUNK, LANES), F32),
               pltpu.VMEM((N_HEADS, dk, LANES), F32),
               pltpu.VMEM((8, LANES), F32)]
    return pl.pallas_call(
        kern, grid=(b, nt), in_specs=in_specs, out_specs=out_specs, out_shape=out_shape,
        scratch_shapes=scratch,
        compiler_params=pltpu.CompilerParams(dimension_semantics=("arbitrary", "arbitrary"),
                                             vmem_limit_bytes=VMEM_LIMIT),
        name="mix_p",
    )(x, mod.reshape(b, 1, -1), *consts)


def _s_pre_kernel(x_ref, mod_ref, g1_ref, wmain_ref, wmainlo_ref, wgh_ref, wgl_ref, gbias_ref,
                  wkt_ref, wktlo_ref, wdw_ref, bdw_ref, clg_ref, clb_ref, cache_ref, n0_ref, m0_ref,
                  q_ref, kt_ref, vs_ref, ab_ref, sv_ref, den_ref, eb_ref, o_ref, hc_ref, u_ref,
                  n_ref, m_ref, *, d, dm, dk):
    dc = d - dm
    x = x_ref[...]
    mod = mod_ref[...]
    sh1 = mod[:, 0:d]
    sc1 = mod[:, d:2 * d]
    h = _rms(x, g1_ref[...]) * (1.0 + sc1) + sh1
    z = _dot3(h, wmain_ref[...], wmainlo_ref[...])
    gates = _dot3(h, wgh_ref[...], wgl_ref[...]) + gbias_ref[...]
    scale = dk ** -0.5
    h_hi, h_lo = _split2(h)
    kt = _bdot_nt(wkt_ref[...], h_hi) + _bdot_nt(wktlo_ref[...], h_hi) + _bdot_nt(wkt_ref[...], h_lo)
    kt_ref[...] = (kt * scale).astype(BF16)
    k_all = z[:, dm:2 * dm] * scale
    ga = z[:, 4 * dm:4 * dm + dc]
    gb = z[:, 4 * dm + dc:4 * dm + 2 * dc]
    u = ga * _sigmoid(gb)
    u_ref[...] = u
    acc = jnp.broadcast_to(bdw_ref[...], u.shape) + u * wdw_ref[CONV_W - 1:CONV_W, :]
    for j in range(CONV_W - 1):
        acc = acc + cache_ref[j] * wdw_ref[j:j + 1, :]
    hc_ref[...] = _silu(_layer_norm(acc, clg_ref[...], clb_ref[...]))
    o_ref[...] = z[:, 3 * dm:4 * dm]
    q_ref[...] = z[:, 0:dm]
    m0 = m0_ref[...]
    n0 = n0_ref[...]
    lane1 = lax.broadcasted_iota(I32, (1, LANES), 1)
    m_new = jnp.zeros(m0.shape, F32)
    for hd in range(N_HEADS):
        cs = slice(hd * dk, (hd + 1) * dk)
        ig = gates[:, hd:hd + 1]
        lf = _log_sigmoid(gates[:, N_HEADS + hd:N_HEADS + hd + 1])
        mp = m0[:, hd:hd + 1]
        inter = lf + mp
        mt = jnp.maximum(inter, ig)
        w = jnp.exp(ig - mt)
        a_int = jnp.exp(inter - mt)
        qf = z[:, cs]
        kf = k_all[:, cs]
        vf = z[:, 2 * dm + hd * dk:2 * dm + (hd + 1) * dk]
        s = jnp.sum(qf * kf, axis=1, keepdims=True) * w
        sv_ref[:, cs] = s * vf
        den_ref[:, cs] = jnp.broadcast_to(
            s + a_int * jnp.sum(qf * n0[:, cs], axis=1, keepdims=True), (x.shape[0], dk))
        eb_ref[:, cs] = jnp.broadcast_to(jnp.exp(-mt), (x.shape[0], dk))
        ab_ref[:, cs] = jnp.broadcast_to(a_int, (x.shape[0], dk))
        vs_ref[:, cs] = (vf * w).astype(BF16)
        n_ref[:, cs] = a_int * n0[:, cs] + w * kf
        m_new = jnp.where(lane1 == hd, mt, m_new)
    m_ref[...] = m_new


def _s_state_kernel(q_ref, kt_ref, vs_ref, ab_ref, c0_ref, c_ref, r_ref, *, bb, dk):
    i = pl.program_id(0)
    nb = q_ref.shape[0]
    rows = lax.broadcasted_iota(I32, (nb, dk), 0)

    @pl.when(i == 0)
    def _():
        r_ref[...] = jnp.zeros_like(r_ref)

    a_blk = ab_ref[pl.ds(pl.multiple_of(i * bb, bb), bb), :]
    for j in range(bb):
        sel = rows == i * bb + j
        for hd in range(N_HEADS):
            cs = slice(hd * dk, (hd + 1) * dk)
            c0 = c0_ref[j, hd]
            vmask = jnp.where(sel, vs_ref[:, cs], jnp.zeros((), BF16))
            c_ref[j, hd] = (a_blk[j:j + 1, cs] * c0
                            + jnp.dot(kt_ref[cs, :], vmask, preferred_element_type=F32))
            c_hi, c_lo = _split2(c0)
            q_hi, q_lo = _split2(q_ref[:, cs])
            r = (jnp.dot(q_hi, c_hi, preferred_element_type=F32)
                 + jnp.dot(q_lo, c_hi, preferred_element_type=F32)
                 + jnp.dot(q_hi, c_lo, preferred_element_type=F32))
            r_ref[:, cs] = r_ref[:, cs] + jnp.where(sel, r, 0.0)


def _s_post_kernel(x_ref, mod_ref, g2_ref, mhg_ref, r_ref, ab_ref, sv_ref, den_ref, eb_ref, o_ref,
                   hc_ref, wout_ref, woutlo_ref, wrh_ref, wrl_ref, rbias_ref,
                   x1_ref, h2_ref, route_ref, *, d, dm, dk):
    hm = []
    for hd in range(N_HEADS):
        cs = slice(hd * dk, (hd + 1) * dk)
        num = sv_ref[:, cs] + ab_ref[:, cs] * r_ref[:, cs]
        hh = num / jnp.maximum(jnp.abs(den_ref[:, cs]), eb_ref[:, cs])
        hm.append(_layer_norm(hh, mhg_ref[hd:hd + 1, :]) * _sigmoid(o_ref[:, cs]))
    cat = jnp.concatenate(hm + [hc_ref[...]], axis=1)
    x1, h2, route = _post(x_ref[...], cat, mod_ref[...], g2_ref[...], wout_ref[...], wrh_ref[...],
                          wrl_ref[...], rbias_ref[...], d, wout_lo=woutlo_ref[...])
    x1_ref[...] = x1
    _store_rows(h2_ref, h2)
    route_ref[...] = route


def _mix_sample(x, mod, c0, n0, m0, cache, wts):
    nb, d = x.shape
    dm = wts["dm"]
    dk = dm // N_HEADS
    dc = d - dm
    cp = pltpu.CompilerParams(dimension_semantics=("arbitrary",), vmem_limit_bytes=VMEM_LIMIT)
    cache_t = jnp.transpose(cache, (1, 0, 2))
    m0p = jnp.pad(m0, ((0, 0), (0, LANES - N_HEADS)))
    pre_in = [x, mod, wts["g1"], wts["wmain"], wts["wmain_lo"], wts["wg_hi"], wts["wg_lo"],
              wts["gbias"], wts["wk_t"], wts["wk_t_lo"], wts["wdw"], wts["bdw"], wts["clg"],
              wts["clb"], cache_t, n0.reshape(nb, dm), m0p]
    pre_out = [jax.ShapeDtypeStruct((nb, dm), F32),
               jax.ShapeDtypeStruct((dm, nb), BF16),
               jax.ShapeDtypeStruct((nb, dm), BF16),
               jax.ShapeDtypeStruct((nb, dm), F32),
               jax.ShapeDtypeStruct((nb, dm), F32),
               jax.ShapeDtypeStruct((nb, dm), F32),
               jax.ShapeDtypeStruct((nb, dm), F32),
               jax.ShapeDtypeStruct((nb, dm), F32),
               jax.ShapeDtypeStruct((nb, dc), F32),
               jax.ShapeDtypeStruct((nb, dc), F32),
               jax.ShapeDtypeStruct((nb, dm), F32),
               jax.ShapeDtypeStruct((nb, LANES), F32)]
    (q, kt, vs, ab, sv, den, eb, o, hc, u, n1, m1) = pl.pallas_call(
        functools.partial(_s_pre_kernel, d=d, dm=dm, dk=dk),
        grid=(1,),
        in_specs=[_const_spec(a.shape) for a in pre_in],
        out_specs=[_const_spec(s.shape) for s in pre_out],
        out_shape=pre_out, compiler_params=cp, name="s_pre")(*pre_in)

    bb = 8
    assert nb % bb == 0
    c1, r = pl.pallas_call(
        functools.partial(_s_state_kernel, bb=bb, dk=dk),
        grid=(nb // bb,),
        in_specs=[_const_spec(q.shape), _const_spec(kt.shape), _const_spec(vs.shape),
                  _const_spec(ab.shape),
                  pl.BlockSpec((bb, N_HEADS, dk, dk), lambda i: (i, 0, 0, 0))],
        out_specs=[pl.BlockSpec((bb, N_HEADS, dk, dk), lambda i: (i, 0, 0, 0)),
                   _const_spec((nb, dm))],
        out_shape=[jax.ShapeDtypeStruct((nb, N_HEADS, dk, dk), F32),
                   jax.ShapeDtypeStruct((nb, dm), F32)],
        compiler_params=cp, name="s_state")(q, kt, vs, ab, c0)

    post_in = [x, mod, wts["g2"], wts["mhg"], r, ab, sv, den, eb, o, hc, wts["wout"],
               wts["wout_lo"], wts["wr_hi"], wts["wr_lo"], wts["rbias"]]
    post_out = [jax.ShapeDtypeStruct((nb, d), F32),
                jax.ShapeDtypeStruct((nb * ROW_SUB, LANES), F32),
                jax.ShapeDtypeStruct((nb, LANES), F32)]
    x1, h2, route = pl.pallas_call(
        functools.partial(_s_post_kernel, d=d, dm=dm, dk=dk),
        grid=(1,),
        in_specs=[_const_spec(a.shape) for a in post_in],
        out_specs=[_const_spec(s.shape) for s in post_out],
        out_shape=post_out, compiler_params=cp, name="s_post")(*post_in)
    return x1, h2, route, c1, n1, m1, u


def _scatter_kernel(pos_ref, zoff_ref, src_ref, *rest, tp, tm, create):
    xs_ref, zbuf, sem, zsem = rest[-4:]
    i = pl.program_id(0)

    if create:
        @pl.when(i == 0)
        def _():
            zbuf[...] = jnp.zeros_like(zbuf)

            def zero_copy(e):
                start = pl.multiple_of(zoff_ref[e] * ROW_SUB, tm * ROW_SUB)
                return pltpu.make_async_copy(zbuf, xs_ref.at[pl.ds(start, tm * ROW_SUB)], zsem)

            for e in range(zoff_ref.shape[0]):
                @pl.when(zoff_ref[e] >= 0)
                def _():
                    zero_copy(e).start()
            for e in range(zoff_ref.shape[0]):
                @pl.when(zoff_ref[e] >= 0)
                def _():
                    zero_copy(e).wait()

    for r in range(tp):
        src = src_ref.at[pl.ds(r * ROW_SUB, ROW_SUB)]
        for slot in range(2):
            dst = pl.multiple_of(pos_ref[0, slot, r] * ROW_SUB, ROW_SUB)
            pltpu.make_async_copy(src, xs_ref.at[pl.ds(dst, ROW_SUB)], sem).start(priority=slot)
    for slot in range(2):
        pltpu.make_async_copy(src_ref, xs_ref.at[pl.ds(0, tp * ROW_SUB)], sem).wait()


def _scatter_rows(h2, pos3, zoff, xs_or_rows, tm):
    n, c = h2.shape[0] // ROW_SUB, LANES
    tp = pos3.shape[2]
    assert pos3.shape == (n // tp, 2, tp)
    create = isinstance(xs_or_rows, int)
    n_sorted = xs_or_rows * ROW_SUB if create else xs_or_rows.shape[0]
    in_specs = [pl.BlockSpec((1, 2, tp), lambda i: (i, 0, 0), memory_space=pltpu.SMEM),
                pl.BlockSpec(memory_space=pltpu.SMEM),
                pl.BlockSpec((tp * ROW_SUB, c), lambda i: (i, 0))]
    args = [pos3, zoff, h2]
    if not create:
        in_specs.append(pl.BlockSpec(memory_space=pl.ANY))
        args.append(xs_or_rows)
    return pl.pallas_call(
        functools.partial(_scatter_kernel, tp=tp, tm=tm, create=create),
        grid_spec=pltpu.PrefetchScalarGridSpec(
            num_scalar_prefetch=0,
            grid=(n // tp,),
            in_specs=in_specs,
            out_specs=pl.BlockSpec(memory_space=pl.ANY),
            scratch_shapes=[pltpu.VMEM((tm * ROW_SUB, c), F32), pltpu.SemaphoreType.DMA(()),
                            pltpu.SemaphoreType.DMA(())]),
        out_shape=jax.ShapeDtypeStruct((n_sorted, c), F32),
        input_output_aliases={} if create else {3: 0},
        compiler_params=pltpu.CompilerParams(dimension_semantics=("arbitrary",),
                                             vmem_limit_bytes=VMEM_LIMIT),
        name="scatter",
    )(*args)


def _moe_kernel(te_ref, nu_ref, xs_ref, wg_ref, wu_ref, wd_ref, ys_ref, wg_b, wu_b, wd_b):
    i = pl.program_id(0)
    used = i < nu_ref[0]

    @pl.when(used & ((i == 0) | (te_ref[i] != te_ref[jnp.maximum(i - 1, 0)])))
    def _():
        wg_b[...] = wg_ref[0].astype(BF16)
        wu_b[...] = wu_ref[0].astype(BF16)
        wd_b[...] = wd_ref[0].astype(BF16)

    @pl.when(used)
    def _():
        xb = _load_rows(xs_ref, xs_ref.shape[0] // ROW_SUB).astype(BF16)
        g = jnp.dot(xb, wg_b[...], preferred_element_type=F32)
        u = jnp.dot(xb, wu_b[...], preferred_element_type=F32)
        hid = (_silu(g) * u).astype(BF16)
        _store_rows(ys_ref, jnp.dot(hid, wd_b[...], preferred_element_type=F32))

    @pl.when(jnp.logical_not(used))
    def _():
        ys_ref[...] = jnp.zeros_like(ys_ref)


def _moe(xs, tile_expert, n_used, w_gate, w_up, w_down, tm):
    p, c = xs.shape[0] // ROW_SUB, LANES
    ne, d, de = w_gate.shape
    n_tiles = p // tm

    def x_map(i, te, nu):
        return (jnp.minimum(i, jnp.maximum(nu[0] - 1, 0)), 0)

    def w_map(i, te, nu):
        return (te[i], 0, 0)

    return pl.pallas_call(
        _moe_kernel,
        grid_spec=pltpu.PrefetchScalarGridSpec(
            num_scalar_prefetch=2,
            grid=(n_tiles,),
            in_specs=[pl.BlockSpec((tm * ROW_SUB, c), x_map),
                      pl.BlockSpec((1, d, de), w_map),
                      pl.BlockSpec((1, d, de), w_map),
                      pl.BlockSpec((1, de, d), w_map)],
            out_specs=pl.BlockSpec((tm * ROW_SUB, c), lambda i, te, nu: (i, 0)),
            scratch_shapes=[pltpu.VMEM((d, de), BF16), pltpu.VMEM((d, de), BF16),
                            pltpu.VMEM((de, d), BF16)]),
        out_shape=jax.ShapeDtypeStruct((p * ROW_SUB, c), F32),
        compiler_params=pltpu.CompilerParams(dimension_semantics=("arbitrary",),
                                             vmem_limit_bytes=VMEM_LIMIT),
        name="moe",
    )(tile_expert, n_used, xs, w_gate, w_up, w_down)


def _fin_kernel(pos_ref, posn_ref, x1_ref, route_ref, mod_ref, fg_ref, ys_ref, y_ref, ybuf, sem,
                *, tp, d, n_steps):
    i = pl.program_id(0)
    cur = i % 2

    def issue(p_ref, buf):
        def row_start(r, carry):
            dst = pl.ds(pl.multiple_of(r * ROW_SUB, ROW_SUB), ROW_SUB)
            for slot in range(2):
                src = pl.multiple_of(p_ref[0, slot, r] * ROW_SUB, ROW_SUB)
                pltpu.make_async_copy(ys_ref.at[pl.ds(src, ROW_SUB)],
                                      ybuf.at[buf, slot, dst], sem.at[buf]).start(priority=slot)
            return carry
        lax.fori_loop(0, tp, row_start, 0, unroll=8)

    def wait_buf(buf):
        for slot in range(2):
            pltpu.make_async_copy(ys_ref.at[pl.ds(0, tp * ROW_SUB)], ybuf.at[buf, slot],
                                  sem.at[buf]).wait()

    @pl.when(i == 0)
    def _():
        issue(pos_ref, 0)

    wait_buf(cur)
    nxt = 1 - cur
    per_token_mod = mod_ref.shape[1] != 1
    for lo in range(0, tp, FIN_CHUNK):
        hi = min(lo + FIN_CHUNK, tp)
        for r in range(lo, hi):
            for slot in range(2):
                src = pl.multiple_of(posn_ref[0, slot, r] * ROW_SUB, ROW_SUB)
                pltpu.make_async_copy(
                    ys_ref.at[pl.ds(src, ROW_SUB)],
                    ybuf.at[nxt, slot, pl.ds(r * ROW_SUB, ROW_SUB)], sem.at[nxt]).start(priority=slot)
        route = route_ref[lo:hi, :]
        moe = (route[:, 2:3] * _load_rows(ybuf.at[cur, 0], hi - lo, lo)
               + route[:, 3:4] * _load_rows(ybuf.at[cur, 1], hi - lo, lo))
        mod = mod_ref[0, lo:hi, :] if per_token_mod else mod_ref[0]
        y_ref[0, lo:hi, :] = _rms(x1_ref[0, lo:hi, :] + mod[:, 5 * d:6 * d] * moe, fg_ref[...])

    @pl.when(i == n_steps - 1)
    def _():
        wait_buf(nxt)


def _finish(x1, route, mod, final_g, ys, pos3):
    b, t, d = x1.shape
    tp = pos3.shape[2]
    assert t % tp == 0
    nt = t // tp
    n_steps = b * nt
    assert pos3.shape[0] == n_steps
    blk0 = 0
    if mod.shape[1] == 1:
        mod_spec = pl.BlockSpec((1, 1, 6 * d), lambda i: (i // nt, 0, 0))
    else:
        mod_spec = pl.BlockSpec((1, tp, 6 * d), lambda i: (i // nt, i % nt, 0))
    return pl.pallas_call(
        functools.partial(_fin_kernel, tp=tp, d=d, n_steps=n_steps),
        grid_spec=pltpu.PrefetchScalarGridSpec(
            num_scalar_prefetch=0,
            grid=(n_steps,),
            in_specs=[pl.BlockSpec((1, 2, tp), lambda i: (blk0 + i, 0, 0), memory_space=pltpu.SMEM),
                      pl.BlockSpec((1, 2, tp), lambda i: (blk0 + jnp.minimum(i + 1, n_steps - 1), 0, 0),
                                   memory_space=pltpu.SMEM),
                      pl.BlockSpec((1, tp, d), lambda i: (i // nt, i % nt, 0)),
                      pl.BlockSpec((tp, LANES), lambda i: (i, 0)),
                      mod_spec,
                      _const_spec((1, d)),
                      pl.BlockSpec(memory_space=pl.ANY)],
            out_specs=pl.BlockSpec((1, tp, d), lambda i: (i // nt, i % nt, 0)),
            scratch_shapes=[pltpu.VMEM((2, 2, tp * ROW_SUB, LANES), F32),
                            pltpu.SemaphoreType.DMA((2,))]),
        out_shape=jax.ShapeDtypeStruct((b, t, d), F32),
        compiler_params=pltpu.CompilerParams(dimension_semantics=("arbitrary",),
                                             vmem_limit_bytes=VMEM_LIMIT),
        name="fin",
    )(pos3, pos3, x1, route, mod, final_g.reshape(1, d), ys)


def _expert_onehots(route):
    lane = lax.broadcasted_iota(I32, route.shape, 1).astype(F32)
    return ((lane == route[:, 0:1]).astype(F32), (lane == route[:, 1:2]).astype(F32))


def _count_kernel(route_ref, base_ref, tot_ref, acc):
    @pl.when(pl.program_id(0) == 0)
    def _():
        acc[...] = jnp.zeros_like(acc)

    oh0, oh1 = _expert_onehots(route_ref[...])
    base_ref[0] = acc[...]
    acc[...] = acc[...] + jnp.sum(oh0 + oh1, axis=0, keepdims=True)
    tot_ref[...] = acc[...]


def _pos_kernel(route_ref, base_ref, seg_ref, pos_ref, *, tp):
    oh0, oh1 = _expert_onehots(route_ref[...])
    r_i = lax.broadcasted_iota(I32, (tp, tp), 0)
    c_i = lax.broadcasted_iota(I32, (tp, tp), 1)
    before = jnp.dot((c_i < r_i).astype(BF16), (oh0 + oh1).astype(BF16),
                     preferred_element_type=F32)
    offs = seg_ref[0:1, :] + base_ref[0][0:1, :] + before
    ones8 = jnp.ones((8, LANES), BF16)

    def lane_sum_as_row(v):
        hi = v.astype(BF16)
        r1 = v - hi.astype(F32)
        mid = r1.astype(BF16)
        lo = (r1 - mid.astype(F32)).astype(BF16)
        return _bdot_nt(ones8, hi) + _bdot_nt(ones8, mid) + _bdot_nt(ones8, lo)

    p0 = lane_sum_as_row(oh0 * offs)
    p1 = lane_sum_as_row(oh1 * offs)
    row = lax.broadcasted_iota(I32, (8, tp), 0)
    pos_ref[0] = jnp.where(row == 0, p0, jnp.where(row == 1, p1, 0.0)).astype(I32)


def _count(route, tp):
    n = route.shape[0]
    assert n % tp == 0
    return pl.pallas_call(
        _count_kernel, grid=(n // tp,),
        in_specs=[pl.BlockSpec((tp, LANES), lambda i: (i, 0))],
        out_specs=[pl.BlockSpec((1, 8, LANES), lambda i: (i, 0, 0)), _const_spec((8, LANES))],
        out_shape=[jax.ShapeDtypeStruct((n // tp, 8, LANES), F32),
                   jax.ShapeDtypeStruct((8, LANES), F32)],
        scratch_shapes=[pltpu.VMEM((8, LANES), F32)],
        compiler_params=pltpu.CompilerParams(dimension_semantics=("arbitrary",),
                                             vmem_limit_bytes=VMEM_LIMIT),
        name="count",
    )(route)


def _positions(route, base, seg_start, tp):
    n = route.shape[0]
    seg = jnp.broadcast_to(jnp.pad(seg_start.astype(F32), (0, LANES - N_EXPERTS))[None, :], (8, LANES))
    pos = pl.pallas_call(
        functools.partial(_pos_kernel, tp=tp), grid=(n // tp,),
        in_specs=[pl.BlockSpec((tp, LANES), lambda i: (i, 0)),
                  pl.BlockSpec((1, 8, LANES), lambda i: (i, 0, 0)),
                  _const_spec((8, LANES))],
        out_specs=pl.BlockSpec((1, 8, tp), lambda i: (i, 0, 0)),
        out_shape=jax.ShapeDtypeStruct((n // tp, 8, tp), I32),
        compiler_params=pltpu.CompilerParams(dimension_semantics=("arbitrary",),
                                             vmem_limit_bytes=VMEM_LIMIT),
        name="positions",
    )(route, base, seg)
    return pos[:, 0:2, :]


def _plan(counts, tm, n_tiles):
    tiles_per = (counts + tm - 1) // tm
    tile_end = jnp.cumsum(tiles_per)
    seg_start = (tile_end - tiles_per) * tm
    n_used = tile_end[-1]
    tile_ids = jnp.arange(n_tiles, dtype=I32)
    tile_expert = jnp.sum((tile_ids[:, None] >= tile_end[None, :]).astype(I32), axis=1)
    last_used = jnp.sum((n_used - 1 >= tile_end).astype(I32))
    tile_expert = jnp.where(tile_ids < n_used, tile_expert, last_used).astype(I32)
    z_expert = jnp.where((counts % tm) != 0, (tile_end - 1) * tm, -1)
    spare = n_used + jnp.arange(N_EXPERTS, dtype=I32)
    z_spare = jnp.where(spare < n_tiles, spare * tm, -1)
    zoff = jnp.concatenate([z_expert, z_spare]).astype(I32)
    return seg_start, tile_expert, n_used.reshape(1).astype(I32), zoff


def kernel(x_prompt, x_sample, c_prompt, c_sample, state_mlstm_C, state_mlstm_n, state_mlstm_m,
           cache_conv, w_ada, b_ada, norm1_g, w_in, b_igate, b_fgate, mh_norm_g, w_dw, b_dw,
           conv_ln_g, conv_ln_b, w_out, norm2_g, w_grp_router, b_grp_router, w_exp_router,
           b_exp_router, w_gate, w_up, w_down, final_g):
    depth = w_ada.shape[0]
    assert depth == 1, "one layer per step"
    bp, tp_, d = x_prompt.shape
    bs = x_sample.shape[0]
    assert x_sample.shape[1] == 1 and d == ROW_SUB * LANES
    dk = state_mlstm_C.shape[-1]
    dm = N_HEADS * dk
    dc = d - dm
    l = 0

    win = w_in[l]
    w_qkvo = win[:, 0:4 * dm]
    w_gates = win[:, 4 * dm:4 * dm + 2 * N_HEADS]
    w_glu = win[:, 4 * dm + 2 * N_HEADS:]
    wmain, wmain_lo = _split_weights(jnp.concatenate([w_qkvo, w_glu], axis=1))
    wout_hi, wout_lo = _split_weights(w_out[l])
    wkt_hi, wkt_lo = _split_weights(win[:, dm:2 * dm].T)
    wg_pad = jnp.pad(w_gates, ((0, 0), (0, LANES - 2 * N_HEADS)))
    wg_hi, wg_lo = _split_weights(wg_pad)
    gbias = jnp.pad(jnp.concatenate([b_igate[l], b_fgate[l]]), (0, LANES - 2 * N_HEADS)).reshape(1, LANES)
    w_r = jnp.concatenate([w_grp_router[l], w_exp_router[l]], axis=1)
    n_r = N_GROUPS + N_EXPERTS
    wr_hi, wr_lo = _split_weights(jnp.pad(w_r, ((0, 0), (0, LANES - n_r))))
    rbias = jnp.pad(jnp.concatenate([b_grp_router[l], b_exp_router[l]]), (0, LANES - n_r)).reshape(1, LANES)
    wts = dict(dm=dm, g1=norm1_g[l].reshape(1, d), g2=norm2_g[l].reshape(1, d), wmain=wmain,
               wg_hi=wg_hi, wg_lo=wg_lo, gbias=gbias, mhg=mh_norm_g[l],
               wdw=w_dw[l].reshape(CONV_W, dc), bdw=b_dw[l].reshape(1, dc),
               clg=conv_ln_g[l].reshape(1, dc), clb=conv_ln_b[l].reshape(1, dc),
               wout=wout_hi, wr_hi=wr_hi, wr_lo=wr_lo, rbias=rbias, wk_t=wkt_hi,
               wmain_lo=wmain_lo, wout_lo=wout_lo, wk_t_lo=wkt_lo)

    mod = _ada(jnp.concatenate([c_prompt, c_sample], axis=0), w_ada[l], b_ada[l])
    mod_p, mod_s = mod[:bp], mod[bp:]

    n_p = bp * tp_
    n_all = n_p + bs
    x1_p, h2_p, route_p, c_p, n_pr, m_p, cv_p = _mix_prompt(x_prompt, mod_p, wts)
    x1_s, h2_s, route_s, c_s, n_s, m_s, u_s = _mix_sample(
        x_sample.reshape(bs, d), mod_s, state_mlstm_C[l], state_mlstm_n[l], state_mlstm_m[l],
        cache_conv[l], wts)

    tm = MOE_TILE
    n_tiles = (2 * n_all) // tm + N_EXPERTS
    tp_p, tp_s = min(PERM_TILE, tp_), min(PERM_TILE, bs)
    base_p, tot_p = _count(route_p, tp_p)
    base_s, tot_s = _count(route_s, tp_s)
    counts = (tot_p[0, :N_EXPERTS] + tot_s[0, :N_EXPERTS]).astype(I32)
    seg_start, tile_expert, n_used, zoff = _plan(counts, tm, n_tiles)
    pos_p = _positions(route_p, base_p, seg_start, tp_p)
    pos_s = _positions(route_s, base_s + tot_p[None], seg_start, tp_s)

    xs = _scatter_rows(h2_p, pos_p, zoff, n_tiles * tm, tm)
    xs = _scatter_rows(h2_s, pos_s, zoff, xs, tm)
    ys = _moe(xs, tile_expert, n_used, w_gate[l], w_up[l], w_down[l], tm)

    y_p = _finish(x1_p, route_p, mod_p.reshape(bp, 1, -1), final_g, ys, pos_p)
    y_s = _finish(x1_s.reshape(1, bs, d), route_s, mod_s.reshape(1, bs, -1), final_g, ys,
                  pos_s).reshape(bs, 1, d)

    conv_s = jnp.concatenate([cache_conv[l][:, 1:, :], u_s[:, None, :]], axis=1)
    return (y_p, y_s,
            c_p, n_pr, m_p[:, 0, :N_HEADS].reshape(1, bp, N_HEADS), cv_p,
            c_s[None], n_s.reshape(1, bs, N_HEADS, dk), m_s[:, :N_HEADS].reshape(1, bs, N_HEADS),
            conv_s[None])
```

```python
import functools

import jax
import jax.numpy as jnp
from jax import lax
from jax.experimental import pallas as pl
from jax.experimental.pallas import tpu as pltpu

F32 = jnp.float32
BF16 = jnp.bfloat16
I32 = jnp.int32

EPS = 1e-6
LANES = 128
ROW_SUB = 8
CHUNK = 128
N_HEADS = 4
N_GROUPS = 4
EXP_PER_GROUP = 8
N_EXPERTS = N_GROUPS * EXP_PER_GROUP
CONV_W = 31
CONV_PAD = 32
CONV_OFF = CONV_PAD - (CONV_W - 1)
MIX_TILE = 512
MIX_SUB = 256
MOE_TILE = 512
PERM_TILE = 512
FIN_CHUNK = 64
VMEM_LIMIT = 56 * 1024 * 1024


def _sigmoid(x):
    return 1.0 / (1.0 + jnp.exp(-x))


def _silu(x):
    return x * _sigmoid(x)


def _log_sigmoid(x):
    return jnp.minimum(x, 0.0) - jnp.log(1.0 + jnp.exp(-jnp.abs(x)))


def _bdot(a, b):
    return jnp.dot(a.astype(BF16), b.astype(BF16), preferred_element_type=F32)


def _bdot_nt(a, b):
    return lax.dot_general(a.astype(BF16), b.astype(BF16), (((1,), (1,)), ((), ())),
                           preferred_element_type=F32)


def _split2(x):
    hi = x.astype(BF16)
    lo = (x - hi.astype(F32)).astype(BF16)
    return hi, lo


def _split_kernel(w_ref, hi_ref, lo_ref):
    hi, lo = _split2(w_ref[...])
    hi_ref[...] = hi
    lo_ref[...] = lo


def _split_weights(w):
    rows, cols = w.shape
    blk = min(cols, 512)
    assert cols % blk == 0
    spec = pl.BlockSpec((rows, blk), lambda j: (0, j))
    return pl.pallas_call(
        _split_kernel, grid=(cols // blk,), in_specs=[spec], out_specs=[spec, spec],
        out_shape=[jax.ShapeDtypeStruct(w.shape, BF16)] * 2,
        compiler_params=pltpu.CompilerParams(dimension_semantics=("arbitrary",),
                                             vmem_limit_bytes=VMEM_LIMIT),
        name="split",
    )(w)


def _dot3(a, w_hi, w_lo):
    a_hi, a_lo = _split2(a)
    return (jnp.dot(a_hi, w_hi, preferred_element_type=F32)
            + jnp.dot(a_lo, w_hi, preferred_element_type=F32)
            + jnp.dot(a_hi, w_lo, preferred_element_type=F32))


def _cumsum_lanes(triu_bf16, x):
    hi = x.astype(BF16)
    r1 = x - hi.astype(F32)
    mid = r1.astype(BF16)
    lo = (r1 - mid.astype(F32)).astype(BF16)
    return (jnp.dot(hi, triu_bf16, preferred_element_type=F32)
            + jnp.dot(mid, triu_bf16, preferred_element_type=F32)
            + jnp.dot(lo, triu_bf16, preferred_element_type=F32))


def _rms(x, g):
    return x * lax.rsqrt(jnp.mean(x * x, axis=-1, keepdims=True) + EPS) * g


def _layer_norm(x, g, b=None):
    mu = jnp.mean(x, axis=-1, keepdims=True)
    xc = x - mu
    var = jnp.mean(xc * xc, axis=-1, keepdims=True)
    y = xc * lax.rsqrt(var + EPS) * g
    return y if b is None else y + b


def _store_rows(ref, x, row0=0):
    r = x.shape[0]
    for k in range(ROW_SUB):
        ref[pl.ds(row0 * ROW_SUB + k, r, stride=ROW_SUB), :] = x[:, k * LANES:(k + 1) * LANES]


def _load_rows(ref, r, row0=0):
    return jnp.concatenate([ref[pl.ds(row0 * ROW_SUB + k, r, stride=ROW_SUB), :]
                            for k in range(ROW_SUB)], axis=1)


def _route(logits):
    lane = lax.broadcasted_iota(I32, logits.shape, 1).astype(F32)
    neg = jnp.float32(-jnp.inf)
    big = jnp.float32(1e9)
    is_g = lane < N_GROUPS
    gl = jnp.where(is_g, logits, neg)
    gmax = jnp.max(gl, axis=1, keepdims=True)
    gsel = jnp.min(jnp.where(gl == gmax, lane, big), axis=1, keepdims=True)
    pg = 1.0 / jnp.sum(jnp.where(is_g, jnp.exp(gl - gmax), 0.0), axis=1, keepdims=True)
    lo = N_GROUPS + EXP_PER_GROUP * gsel
    emask = (lane >= lo) & (lane < lo + EXP_PER_GROUP)
    el = jnp.where(emask, logits, neg)
    v1 = jnp.max(el, axis=1, keepdims=True)
    i1 = jnp.min(jnp.where(el == v1, lane, big), axis=1, keepdims=True)
    el2 = jnp.where(lane == i1, neg, el)
    v2 = jnp.max(el2, axis=1, keepdims=True)
    i2 = jnp.min(jnp.where(el2 == v2, lane, big), axis=1, keepdims=True)
    d = jnp.exp(v2 - v1)
    w1 = pg / (1.0 + d)
    w2 = pg * d / (1.0 + d)
    return jnp.where(lane == 0, i1 - N_GROUPS,
                     jnp.where(lane == 1, i2 - N_GROUPS,
                               jnp.where(lane == 2, w1, jnp.where(lane == 3, w2, 0.0))))


def _ada_kernel(c_ref, w_ref, b_ref, o_ref):
    w_hi, w_lo = _split2(w_ref[...])
    o_ref[...] = _dot3(_silu(c_ref[...]), w_hi, w_lo) + b_ref[...]


def _ada(c_all, w_ada, b_ada):
    rows, d = c_all.shape
    n_out = w_ada.shape[1]
    blk = 1024
    return pl.pallas_call(
        _ada_kernel,
        grid=(n_out // blk,),
        in_specs=[pl.BlockSpec((rows, d), lambda j: (0, 0)),
                  pl.BlockSpec((d, blk), lambda j: (0, j)),
                  pl.BlockSpec((1, blk), lambda j: (0, j))],
        out_specs=pl.BlockSpec((rows, blk), lambda j: (0, j)),
        out_shape=jax.ShapeDtypeStruct((rows, n_out), F32),
        compiler_params=pltpu.CompilerParams(dimension_semantics=("arbitrary",),
                                             vmem_limit_bytes=VMEM_LIMIT),
        name="ada",
    )(c_all, w_ada, b_ada.reshape(1, n_out))


def _post(x, attn_cat, mod, g2, wout, wr_hi, wr_lo, rbias, d, wout_lo=None):
    gate1 = mod[:, 2 * d:3 * d]
    sh2 = mod[:, 3 * d:4 * d]
    sc2 = mod[:, 4 * d:5 * d]
    if wout_lo is None:
        proj = jnp.dot(attn_cat.astype(BF16), wout, preferred_element_type=F32)
    else:
        proj = _dot3(attn_cat, wout, wout_lo)
    x1 = x + gate1 * proj
    h2 = _rms(x1, g2) * (1.0 + sc2) + sh2
    logits = _dot3(h2, wr_hi, wr_lo) + rbias
    return x1, h2, _route(logits)


def _mix_prompt_kernel(x_ref, mod_ref, g1_ref, g2_ref, wmain_ref, wgh_ref, wgl_ref, gbias_ref,
                       mhg_ref, wdw_ref, bdw_ref, clg_ref, clb_ref, wout_ref, wrh_ref, wrl_ref,
                       rbias_ref,
                       x1_ref, h2_ref, route_ref, c_ref, n_ref, m_ref, cv_ref, base_ref, tot_ref,
                       ubuf, yc_s, q_s, k_s, v_s, so_s, hm_s, p_s, u_s, cm_s, nb_s, m_s, cnt_s,
                       *, tt, d, dm, dk):
    t = pl.program_id(1)
    dc = d - dm
    n_lt = dc // LANES
    sub = min(MIX_SUB, tt)

    @pl.when(t == 0)
    def _():
        c_ref[...] = jnp.zeros_like(c_ref)
        nb_s[...] = jnp.zeros_like(nb_s)
        m_s[...] = jnp.zeros_like(m_s)
        ubuf[0:CONV_PAD, :] = jnp.zeros((CONV_PAD, dc), F32)

    mod = mod_ref[0]
    row8 = lax.broadcasted_iota(I32, (8, LANES), 0)
    row = lax.broadcasted_iota(I32, (CHUNK, CHUNK), 0)
    col = lax.broadcasted_iota(I32, (CHUNK, CHUNK), 1)
    causal = col <= row
    triu = (row <= col).astype(BF16)
    neg = jnp.float32(-jnp.inf)
    ones_b = jnp.ones((CHUNK, dk), BF16)
    pad_rows = jnp.zeros((CHUNK - 8, LANES), F32)

    lanes_of = [slice(lt * LANES, (lt + 1) * LANES) for lt in range(n_lt)]
    wrows = [[jnp.broadcast_to(wdw_ref[j:j + 1, ls], (8, LANES)) for j in range(CONV_W)]
             for ls in lanes_of]
    bias = [jnp.broadcast_to(bdw_ref[:, ls], (8, LANES)) for ls in lanes_of]

    def partial_sums(r0, lt):
        blocks = [ubuf[r0 + 8 * a:r0 + 8 * a + 8, lanes_of[lt]] for a in range(CONV_PAD // 8)]
        sums = []
        for s in range(8):
            acc = None
            for a in range(CONV_PAD // 8):
                j = 8 * a + s - CONV_OFF
                if 0 <= j < CONV_W:
                    term = blocks[a] * wrows[lt][j]
                    acc = term if acc is None else acc + term
            sums.append(acc)
        return tuple(sums)

    q_prev = [None] * n_lt
    ca = [jnp.concatenate([c_ref[0, 0, hd], nb_s[hd]], axis=1) for hd in range(N_HEADS)]
    m_prev = [m_s[hd:hd + 1, :] for hd in range(N_HEADS)]

    n_sub = tt // sub
    gates_of = {}
    hc_of = {}
    tile_counts = []

    def rows_of(sb):
        return slice(sb * sub, (sb + 1) * sub)

    def proj_items(sb):
        rs = rows_of(sb)
        st = {}

        def head():
            x = x_ref[0, rs, :]
            h = _rms(x, g1_ref[...]) * (1.0 + mod[:, d:2 * d]) + mod[:, 0:d]
            st["hb"] = h.astype(BF16)
            gates_of[sb] = _dot3(h, wgh_ref[...], wgl_ref[...]) + gbias_ref[...]

        def proj(lo, hi):
            return jnp.dot(st["hb"], wmain_ref[:, lo:hi], preferred_element_type=F32)

        def glu():
            ubuf[CONV_PAD + sb * sub:CONV_PAD + (sb + 1) * sub, :] = (
                proj(4 * dm, 4 * dm + dc) * _sigmoid(proj(4 * dm + dc, 4 * dm + 2 * dc)))

        def q():
            q_s[rs, :] = proj(0, dm).astype(BF16)

        def k():
            k_s[rs, :] = proj(dm, 2 * dm) * (dk ** -0.5)

        def v():
            v_s[rs, :] = proj(2 * dm, 3 * dm).astype(BF16)

        def o():
            so_s[rs, :] = _sigmoid(proj(3 * dm, 4 * dm))

        return [head, glu, q, k, v, o]

    def conv_items(sb):
        items = []
        for lt in range(n_lt):
            for i in range(sb * sub // 8 + 1, (sb + 1) * sub // 8 + 1):
                def block(lt=lt, i=i):
                    if i == 1:
                        q_prev[lt] = partial_sums(0, lt)
                    q_cur = partial_sums(i * 8, lt)
                    cur = ubuf[(i - 1) * 8 + CONV_PAD:i * 8 + CONV_PAD, lanes_of[lt]]
                    y = bias[lt] + q_prev[lt][0] + cur * wrows[lt][CONV_W - 1]
                    for s in range(1, 8):
                        merged = jnp.where(row8 < s, q_cur[s], q_prev[lt][s])
                        y = y + pltpu.roll(merged, 8 - s, 0)
                    yc_s[(i - 1) * 8:i * 8, lanes_of[lt]] = y
                    q_prev[lt] = q_cur
                items.append(block)
        return items

    def post_items(sb):
        def post():
            rs = rows_of(sb)
            cat = jnp.concatenate([hm_s[rs, :], hc_of[sb]], axis=1)
            x1, h2, route = _post(x_ref[0, rs, :], cat, mod, g2_ref[...], wout_ref[...],
                                  wrh_ref[...], wrl_ref[...], rbias_ref[...], d)
            x1_ref[0, rs, :] = x1
            _store_rows(h2_ref, h2, sb * sub)
            route_ref[rs, :] = route
            oh0, oh1 = _expert_onehots(route)
            tile_counts.append(jnp.sum(oh0 + oh1, axis=0, keepdims=True))
        return [post]

    def interleave(main, side):
        gap = len(main) / (len(side) + 1)
        due, done = gap, 0
        for n, item in enumerate(main):
            item()
            while done < len(side) and n + 1 >= due:
                side[done]()
                done += 1
                due += gap
        for item in side[done:]:
            item()

    for item in proj_items(0):
        item()
    for sb in range(n_sub):
        r_lo = sb * sub
        rs = rows_of(sb)
        side = post_items(sb - 1) if sb > 0 else []
        if sb + 1 < n_sub:
            side = side + proj_items(sb + 1)
        interleave(conv_items(sb), side)
        hc_of[sb] = _silu(_layer_norm(yc_s[rs, :], clg_ref[...], clb_ref[...]))
        gates = gates_of[sb]

        chunks = range(r_lo // CHUNK, (r_lo + sub) // CHUNK)
        b_cols = {}
        for c in chunks:
            r0 = c * CHUNK
            g8 = gates[r0 - r_lo:r0 - r_lo + CHUNK, :].T[0:8, :]
            b8 = _cumsum_lanes(triu, _log_sigmoid(g8))
            pk8 = jnp.where(row8 < N_HEADS, g8 - pltpu.roll(b8, N_HEADS, 0), b8)
            pk = jnp.concatenate([pk8, pad_rows], axis=0).T
            for hd in range(N_HEADS):
                cs = slice(hd * dk, (hd + 1) * dk)
                idx = c * N_HEADS + hd
                kf = k_s[r0:r0 + CHUNK, cs]
                va = jnp.concatenate([v_s[r0:r0 + CHUNK, cs], ones_b], axis=1)
                gm = jnp.where(causal, pk8[hd:hd + 1, :], neg)
                cm = jnp.max(gm, axis=1, keepdims=True)
                s = _bdot_nt(q_s[r0:r0 + CHUNK, cs], kf) * jnp.exp(gm - cm)
                p_s[idx] = jnp.dot(s.astype(BF16), va, preferred_element_type=F32)
                cm_s[idx] = jnp.broadcast_to(cm, (CHUNK, LANES))
                kw = kf * jnp.exp(pk[:, hd:hd + 1] - cm[CHUNK - 1:CHUNK, :])
                u_s[idx] = jnp.dot(kw.T.astype(BF16), va, preferred_element_type=F32)
                b_cols[idx] = pk[:, N_HEADS + hd:N_HEADS + hd + 1]

        for hd in range(N_HEADS):
            cs = slice(hd * dk, (hd + 1) * dk)
            for c in chunks:
                r0 = c * CHUNK
                idx = c * N_HEADS + hd
                cm = cm_s[idx]
                b_col = b_cols[idx]
                mt = jnp.maximum(m_prev[hd], cm)
                f_loc = jnp.exp(cm - mt)
                a_int = jnp.exp(m_prev[hd] - mt)
                qc = jnp.dot(q_s[r0:r0 + CHUNK, cs], ca[hd].astype(BF16),
                             preferred_element_type=F32)
                p = p_s[idx]
                num = f_loc * p[:, :dk] + a_int * qc[:, :dk]
                den = f_loc * p[:, dk:] + a_int * qc[:, dk:]
                hh = num / jnp.maximum(jnp.abs(den), jnp.exp(-(b_col + mt)))
                hm_s[r0:r0 + CHUNK, cs] = (_layer_norm(hh, mhg_ref[hd:hd + 1, :])
                                           * so_s[r0:r0 + CHUNK, cs])
                mt_l = mt[CHUNK - 1:CHUNK, :]
                u = u_s[idx]
                f_l = f_loc[CHUNK - 1:CHUNK, :]
                a_l = a_int[CHUNK - 1:CHUNK, :]
                ca[hd] = jnp.concatenate([a_l * ca[hd][:, :dk] + f_l * u[:, :dk],
                                          a_l * ca[hd][:, dk:] + f_l * u[:, dk:]], axis=1)
                m_prev[hd] = b_col[CHUNK - 1:CHUNK, :] + mt_l

    for item in post_items(n_sub - 1):
        item()

    for hd in range(N_HEADS):
        c_ref[0, 0, hd] = ca[hd][:, :dk]
        nb_s[hd] = ca[hd][:, dk:]
        m_s[hd:hd + 1, :] = m_prev[hd]
    cv_ref[0, 0] = ubuf[tt + CONV_PAD - (CONV_W - 1):tt + CONV_PAD, :]
    ubuf[0:CONV_PAD, :] = ubuf[tt:tt + CONV_PAD, :]

    @pl.when(t == pl.num_programs(1) - 1)
    def _():
        for hd in range(N_HEADS):
            n_ref[0, 0, hd:hd + 1, :] = nb_s[hd].T[0:1, :]

    lane1 = lax.broadcasted_iota(I32, (1, LANES), 1)
    m_row = jnp.zeros((1, LANES), F32)
    for hd in range(N_HEADS):
        m_row = jnp.where(lane1 == hd, m_s[hd:hd + 1, :], m_row)
    m_ref[0] = m_row

    @pl.when((pl.program_id(0) == 0) & (t == 0))
    def _():
        cnt_s[...] = jnp.zeros_like(cnt_s)

    base_ref[0] = cnt_s[...]
    cnt_s[...] = cnt_s[...] + sum(tile_counts)
    tot_ref[...] = cnt_s[...]


def _const_spec(shape):
    nd = len(shape)
    return pl.BlockSpec(shape, lambda *_: (0,) * nd)


def _mix_prompt(x, mod, wts):
    b, t, d = x.shape
    dm = wts["dm"]
    dk = dm // N_HEADS
    dc = d - dm
    tt = min(MIX_TILE, t)
    assert t % tt == 0 and tt % CHUNK == 0 and tt >= CONV_PAD
    nt = t // tt
    kern = functools.partial(_mix_prompt_kernel, tt=tt, d=d, dm=dm, dk=dk)
    const_names = ["g1", "g2", "wmain", "wg_hi", "wg_lo", "gbias", "mhg", "wdw", "bdw", "clg",
                   "clb", "wout", "wr_hi", "wr_lo", "rbias"]
    consts = [wts[k] for k in const_names]
    in_specs = ([pl.BlockSpec((1, tt, d), lambda i, j: (i, j, 0)),
                 pl.BlockSpec((1, 1, mod.shape[-1]), lambda i, j: (i, 0, 0))]
                + [_const_spec(c.shape) for c in consts])
    out_shape = [
        jax.ShapeDtypeStruct((b, t, d), F32),
        jax.ShapeDtypeStruct((b * t * ROW_SUB, LANES), F32),
        jax.ShapeDtypeStruct((b * t, LANES), F32),
        jax.ShapeDtypeStruct((1, b, N_HEADS, dk, dk), F32),
        jax.ShapeDtypeStruct((1, b, N_HEADS, dk), F32),
        jax.ShapeDtypeStruct((b, 1, LANES), F32),
        jax.ShapeDtypeStruct((1, b, CONV_W - 1, dc), F32),
        jax.ShapeDtypeStruct((b * nt, 8, LANES), F32),
        jax.ShapeDtypeStruct((8, LANES), F32),
    ]
    out_specs = [
        pl.BlockSpec((1, tt, d), lambda i, j: (i, j, 0)),
        pl.BlockSpec((tt * ROW_SUB, LANES), lambda i, j: (i * nt + j, 0)),
        pl.BlockSpec((tt, LANES), lambda i, j: (i * nt + j, 0)),
        pl.BlockSpec((1, 1, N_HEADS, dk, dk), lambda i, j: (0, i, 0, 0, 0)),
        pl.BlockSpec((1, 1, N_HEADS, dk), lambda i, j: (0, i, 0, 0)),
        pl.BlockSpec((1, 1, LANES), lambda i, j: (i, 0, 0)),
        pl.BlockSpec((1, 1, CONV_W - 1, dc), lambda i, j: (0, i, 0, 0)),
        pl.BlockSpec((1, 8, LANES), lambda i, j: (i * nt + j, 0, 0)),
        pl.BlockSpec((8, LANES), lambda i, j: (0, 0)),
    ]
    n_hc = (tt // CHUNK) * N_HEADS
    scratch = [pltpu.VMEM((tt + CONV_PAD, dc), F32),
               pltpu.VMEM((tt, dc), F32),
               pltpu.VMEM((tt, dm), BF16),
               pltpu.VMEM((tt, dm), F32),
               pltpu.VMEM((tt, dm), BF16),
               pltpu.VMEM((tt, dm), F32),
               pltpu.VMEM((tt, dm), F32),
               pltpu.VMEM((n_hc, CHUNK, 2 * dk), F32),
               pltpu.VMEM((n_hc, CHUNK, 2 * dk), F32),
               pltpu.VMEM((n_hc, CHUNK, LANES), F32),
               pltpu.VMEM((N_HEADS, dk, LANES), F32),
               pltpu.VMEM((8, LANES), F32),
               pltpu.VMEM((8, LANES), F32)]
    return pl.pallas_call(
        kern, grid=(b, nt), in_specs=in_specs, out_specs=out_specs, out_shape=out_shape,
        scratch_shapes=scratch,
        compiler_params=pltpu.CompilerParams(dimension_semantics=("arbitrary", "arbitrary"),
                                             vmem_limit_bytes=VMEM_LIMIT),
        name="mix_p",
    )(x, mod.reshape(b, 1, -1), *consts)


def _s_pre_kernel(x_ref, mod_ref, g1_ref, wmain_ref, wmainlo_ref, wgh_ref, wgl_ref, gbias_ref,
                  wkt_ref, wktlo_ref, wdw_ref, bdw_ref, clg_ref, clb_ref, cache_ref, n0_ref, m0_ref,
                  q_ref, kt_ref, vs_ref, ab_ref, sv_ref, den_ref, eb_ref, o_ref, hc_ref, u_ref,
                  n_ref, m_ref, *, d, dm, dk):
    dc = d - dm
    x = x_ref[...]
    mod = mod_ref[...]
    sh1 = mod[:, 0:d]
    sc1 = mod[:, d:2 * d]
    h = _rms(x, g1_ref[...]) * (1.0 + sc1) + sh1
    z = _dot3(h, wmain_ref[...], wmainlo_ref[...])
    gates = _dot3(h, wgh_ref[...], wgl_ref[...]) + gbias_ref[...]
    scale = dk ** -0.5
    h_hi, h_lo = _split2(h)
    kt = _bdot_nt(wkt_ref[...], h_hi) + _bdot_nt(wktlo_ref[...], h_hi) + _bdot_nt(wkt_ref[...], h_lo)
    kt_ref[...] = (kt * scale).astype(BF16)
    k_all = z[:, dm:2 * dm] * scale
    ga = z[:, 4 * dm:4 * dm + dc]
    gb = z[:, 4 * dm + dc:4 * dm + 2 * dc]
    u = ga * _sigmoid(gb)
    u_ref[...] = u
    acc = jnp.broadcast_to(bdw_ref[...], u.shape) + u * wdw_ref[CONV_W - 1:CONV_W, :]
    for j in range(CONV_W - 1):
        acc = acc + cache_ref[j] * wdw_ref[j:j + 1, :]
    hc_ref[...] = _silu(_layer_norm(acc, clg_ref[...], clb_ref[...]))
    o_ref[...] = z[:, 3 * dm:4 * dm]
    q_ref[...] = z[:, 0:dm]
    m0 = m0_ref[...]
    n0 = n0_ref[...]
    lane1 = lax.broadcasted_iota(I32, (1, LANES), 1)
    m_new = jnp.zeros(m0.shape, F32)
    for hd in range(N_HEADS):
        cs = slice(hd * dk, (hd + 1) * dk)
        ig = gates[:, hd:hd + 1]
        lf = _log_sigmoid(gates[:, N_HEADS + hd:N_HEADS + hd + 1])
        mp = m0[:, hd:hd + 1]
        inter = lf + mp
        mt = jnp.maximum(inter, ig)
        w = jnp.exp(ig - mt)
        a_int = jnp.exp(inter - mt)
        qf = z[:, cs]
        kf = k_all[:, cs]
        vf = z[:, 2 * dm + hd * dk:2 * dm + (hd + 1) * dk]
        s = jnp.sum(qf * kf, axis=1, keepdims=True) * w
        sv_ref[:, cs] = s * vf
        den_ref[:, cs] = jnp.broadcast_to(
            s + a_int * jnp.sum(qf * n0[:, cs], axis=1, keepdims=True), (x.shape[0], dk))
        eb_ref[:, cs] = jnp.broadcast_to(jnp.exp(-mt), (x.shape[0], dk))
        ab_ref[:, cs] = jnp.broadcast_to(a_int, (x.shape[0], dk))
        vs_ref[:, cs] = (vf * w).astype(BF16)
        n_ref[:, cs] = a_int * n0[:, cs] + w * kf
        m_new = jnp.where(lane1 == hd, mt, m_new)
    m_ref[...] = m_new


def _s_state_kernel(q_ref, kt_ref, vs_ref, ab_ref, c0_ref, c_ref, r_ref, *, bb, dk):
    i = pl.program_id(0)
    nb = q_ref.shape[0]
    rows = lax.broadcasted_iota(I32, (nb, dk), 0)

    @pl.when(i == 0)
    def _():
        r_ref[...] = jnp.zeros_like(r_ref)

    a_blk = ab_ref[pl.ds(pl.multiple_of(i * bb, bb), bb), :]
    for j in range(bb):
        sel = rows == i * bb + j
        for hd in range(N_HEADS):
            cs = slice(hd * dk, (hd + 1) * dk)
            c0 = c0_ref[j, hd]
            vmask = jnp.where(sel, vs_ref[:, cs], jnp.zeros((), BF16))
            c_ref[j, hd] = (a_blk[j:j + 1, cs] * c0
                            + jnp.dot(kt_ref[cs, :], vmask, preferred_element_type=F32))
            c_hi, c_lo = _split2(c0)
            q_hi, q_lo = _split2(q_ref[:, cs])
            r = (jnp.dot(q_hi, c_hi, preferred_element_type=F32)
                 + jnp.dot(q_lo, c_hi, preferred_element_type=F32)
                 + jnp.dot(q_hi, c_lo, preferred_element_type=F32))
            r_ref[:, cs] = r_ref[:, cs] + jnp.where(sel, r, 0.0)


def _s_post_kernel(x_ref, mod_ref, g2_ref, mhg_ref, r_ref, ab_ref, sv_ref, den_ref, eb_ref, o_ref,
                   hc_ref, wout_ref, woutlo_ref, wrh_ref, wrl_ref, rbias_ref,
                   x1_ref, h2_ref, route_ref, *, d, dm, dk):
    hm = []
    for hd in range(N_HEADS):
        cs = slice(hd * dk, (hd + 1) * dk)
        num = sv_ref[:, cs] + ab_ref[:, cs] * r_ref[:, cs]
        hh = num / jnp.maximum(jnp.abs(den_ref[:, cs]), eb_ref[:, cs])
        hm.append(_layer_norm(hh, mhg_ref[hd:hd + 1, :]) * _sigmoid(o_ref[:, cs]))
    cat = jnp.concatenate(hm + [hc_ref[...]], axis=1)
    x1, h2, route = _post(x_ref[...], cat, mod_ref[...], g2_ref[...], wout_ref[...], wrh_ref[...],
                          wrl_ref[...], rbias_ref[...], d, wout_lo=woutlo_ref[...])
    x1_ref[...] = x1
    _store_rows(h2_ref, h2)
    route_ref[...] = route


def _mix_sample(x, mod, c0, n0, m0, cache, wts):
    nb, d = x.shape
    dm = wts["dm"]
    dk = dm // N_HEADS
    dc = d - dm
    cp = pltpu.CompilerParams(dimension_semantics=("arbitrary",), vmem_limit_bytes=VMEM_LIMIT)
    cache_t = jnp.transpose(cache, (1, 0, 2))
    m0p = jnp.pad(m0, ((0, 0), (0, LANES - N_HEADS)))
    pre_in = [x, mod, wts["g1"], wts["wmain"], wts["wmain_lo"], wts["wg_hi"], wts["wg_lo"],
              wts["gbias"], wts["wk_t"], wts["wk_t_lo"], wts["wdw"], wts["bdw"], wts["clg"],
              wts["clb"], cache_t, n0.reshape(nb, dm), m0p]
    pre_out = [jax.ShapeDtypeStruct((nb, dm), F32),
               jax.ShapeDtypeStruct((dm, nb), BF16),
               jax.ShapeDtypeStruct((nb, dm), BF16),
               jax.ShapeDtypeStruct((nb, dm), F32),
               jax.ShapeDtypeStruct((nb, dm), F32),
               jax.ShapeDtypeStruct((nb, dm), F32),
               jax.ShapeDtypeStruct((nb, dm), F32),
               jax.ShapeDtypeStruct((nb, dm), F32),
               jax.ShapeDtypeStruct((nb, dc), F32),
               jax.ShapeDtypeStruct((nb, dc), F32),
               jax.ShapeDtypeStruct((nb, dm), F32),
               jax.ShapeDtypeStruct((nb, LANES), F32)]
    (q, kt, vs, ab, sv, den, eb, o, hc, u, n1, m1) = pl.pallas_call(
        functools.partial(_s_pre_kernel, d=d, dm=dm, dk=dk),
        grid=(1,),
        in_specs=[_const_spec(a.shape) for a in pre_in],
        out_specs=[_const_spec(s.shape) for s in pre_out],
        out_shape=pre_out, compiler_params=cp, name="s_pre")(*pre_in)

    bb = 8
    assert nb % bb == 0
    c1, r = pl.pallas_call(
        functools.partial(_s_state_kernel, bb=bb, dk=dk),
        grid=(nb // bb,),
        in_specs=[_const_spec(q.shape), _const_spec(kt.shape), _const_spec(vs.shape),
                  _const_spec(ab.shape),
                  pl.BlockSpec((bb, N_HEADS, dk, dk), lambda i: (i, 0, 0, 0))],
        out_specs=[pl.BlockSpec((bb, N_HEADS, dk, dk), lambda i: (i, 0, 0, 0)),
                   _const_spec((nb, dm))],
        out_shape=[jax.ShapeDtypeStruct((nb, N_HEADS, dk, dk), F32),
                   jax.ShapeDtypeStruct((nb, dm), F32)],
        compiler_params=cp, name="s_state")(q, kt, vs, ab, c0)

    post_in = [x, mod, wts["g2"], wts["mhg"], r, ab, sv, den, eb, o, hc, wts["wout"],
               wts["wout_lo"], wts["wr_hi"], wts["wr_lo"], wts["rbias"]]
    post_out = [jax.ShapeDtypeStruct((nb, d), F32),
                jax.ShapeDtypeStruct((nb * ROW_SUB, LANES), F32),
                jax.ShapeDtypeStruct((nb, LANES), F32)]
    x1, h2, route = pl.pallas_call(
        functools.partial(_s_post_kernel, d=d, dm=dm, dk=dk),
        grid=(1,),
        in_specs=[_const_spec(a.shape) for a in post_in],
        out_specs=[_const_spec(s.shape) for s in post_out],
        out_shape=post_out, compiler_params=cp, name="s_post")(*post_in)
    return x1, h2, route, c1, n1, m1, u


def _scatter_kernel(pos_ref, zoff_ref, src_ref, *rest, tp, tm, create):
    xs_ref, zbuf, sem, zsem = rest[-4:]
    i = pl.program_id(0)

    if create:
        @pl.when(i == 0)
        def _():
            zbuf[...] = jnp.zeros_like(zbuf)

            def zero_copy(e):
                start = pl.multiple_of(zoff_ref[e] * ROW_SUB, tm * ROW_SUB)
                return pltpu.make_async_copy(zbuf, xs_ref.at[pl.ds(start, tm * ROW_SUB)], zsem)

            for e in range(zoff_ref.shape[0]):
                @pl.when(zoff_ref[e] >= 0)
                def _():
                    zero_copy(e).start()
            for e in range(zoff_ref.shape[0]):
                @pl.when(zoff_ref[e] >= 0)
                def _():
                    zero_copy(e).wait()

    for r in range(tp):
        src = src_ref.at[pl.ds(r * ROW_SUB, ROW_SUB)]
        for slot in range(2):
            dst = pl.multiple_of(pos_ref[0, slot, r] * ROW_SUB, ROW_SUB)
            pltpu.make_async_copy(src, xs_ref.at[pl.ds(dst, ROW_SUB)], sem).start(priority=slot)
    for slot in range(2):
        pltpu.make_async_copy(src_ref, xs_ref.at[pl.ds(0, tp * ROW_SUB)], sem).wait()


def _scatter_rows(h2, pos3, zoff, xs_or_rows, tm):
    n, c = h2.shape[0] // ROW_SUB, LANES
    tp = pos3.shape[2]
    assert pos3.shape == (n // tp, 2, tp)
    create = isinstance(xs_or_rows, int)
    n_sorted = xs_or_rows * ROW_SUB if create else xs_or_rows.shape[0]
    in_specs = [pl.BlockSpec((1, 2, tp), lambda i: (i, 0, 0), memory_space=pltpu.SMEM),
                pl.BlockSpec(memory_space=pltpu.SMEM),
                pl.BlockSpec((tp * ROW_SUB, c), lambda i: (i, 0))]
    args = [pos3, zoff, h2]
    if not create:
        in_specs.append(pl.BlockSpec(memory_space=pl.ANY))
        args.append(xs_or_rows)
    return pl.pallas_call(
        functools.partial(_scatter_kernel, tp=tp, tm=tm, create=create),
        grid_spec=pltpu.PrefetchScalarGridSpec(
            num_scalar_prefetch=0,
            grid=(n // tp,),
            in_specs=in_specs,
            out_specs=pl.BlockSpec(memory_space=pl.ANY),
            scratch_shapes=[pltpu.VMEM((tm * ROW_SUB, c), F32), pltpu.SemaphoreType.DMA(()),
                            pltpu.SemaphoreType.DMA(())]),
        out_shape=jax.ShapeDtypeStruct((n_sorted, c), F32),
        input_output_aliases={} if create else {3: 0},
        compiler_params=pltpu.CompilerParams(dimension_semantics=("arbitrary",),
                                             vmem_limit_bytes=VMEM_LIMIT),
        name="scatter",
    )(*args)


def _moe_kernel(te_ref, nu_ref, xs_ref, wg_ref, wu_ref, wd_ref, ys_ref, wg_b, wu_b, wd_b):
    i = pl.program_id(0)
    used = i < nu_ref[0]

    @pl.when(used & ((i == 0) | (te_ref[i] != te_ref[jnp.maximum(i - 1, 0)])))
    def _():
        wg_b[...] = wg_ref[0].astype(BF16)
        wu_b[...] = wu_ref[0].astype(BF16)
        wd_b[...] = wd_ref[0].astype(BF16)

    @pl.when(used)
    def _():
        xb = _load_rows(xs_ref, xs_ref.shape[0] // ROW_SUB).astype(BF16)
        g = jnp.dot(xb, wg_b[...], preferred_element_type=F32)
        u = jnp.dot(xb, wu_b[...], preferred_element_type=F32)
        hid = (_silu(g) * u).astype(BF16)
        _store_rows(ys_ref, jnp.dot(hid, wd_b[...], preferred_element_type=F32))

    @pl.when(jnp.logical_not(used))
    def _():
        ys_ref[...] = jnp.zeros_like(ys_ref)


def _moe(xs, tile_expert, n_used, w_gate, w_up, w_down, tm):
    p, c = xs.shape[0] // ROW_SUB, LANES
    ne, d, de = w_gate.shape
    n_tiles = p // tm

    def x_map(i, te, nu):
        return (jnp.minimum(i, jnp.maximum(nu[0] - 1, 0)), 0)

    def w_map(i, te, nu):
        return (te[i], 0, 0)

    return pl.pallas_call(
        _moe_kernel,
        grid_spec=pltpu.PrefetchScalarGridSpec(
            num_scalar_prefetch=2,
            grid=(n_tiles,),
            in_specs=[pl.BlockSpec((tm * ROW_SUB, c), x_map),
                      pl.BlockSpec((1, d, de), w_map),
                      pl.BlockSpec((1, d, de), w_map),
                      pl.BlockSpec((1, de, d), w_map)],
            out_specs=pl.BlockSpec((tm * ROW_SUB, c), lambda i, te, nu: (i, 0)),
            scratch_shapes=[pltpu.VMEM((d, de), BF16), pltpu.VMEM((d, de), BF16),
                            pltpu.VMEM((de, d), BF16)]),
        out_shape=jax.ShapeDtypeStruct((p * ROW_SUB, c), F32),
        compiler_params=pltpu.CompilerParams(dimension_semantics=("arbitrary",),
                                             vmem_limit_bytes=VMEM_LIMIT),
        name="moe",
    )(tile_expert, n_used, xs, w_gate, w_up, w_down)


def _fin_kernel(pos_ref, posn_ref, x1_ref, route_ref, mod_ref, fg_ref, ys_ref, y_ref, ybuf, sem,
                *, tp, d, n_steps):
    i = pl.program_id(0)
    cur = i % 2

    def issue(p_ref, buf):
        def row_start(r, carry):
            dst = pl.ds(pl.multiple_of(r * ROW_SUB, ROW_SUB), ROW_SUB)
            for slot in range(2):
                src = pl.multiple_of(p_ref[0, slot, r] * ROW_SUB, ROW_SUB)
                pltpu.make_async_copy(ys_ref.at[pl.ds(src, ROW_SUB)],
                                      ybuf.at[buf, slot, dst], sem.at[buf]).start(priority=slot)
            return carry
        lax.fori_loop(0, tp, row_start, 0, unroll=8)

    def wait_buf(buf):
        for slot in range(2):
            pltpu.make_async_copy(ys_ref.at[pl.ds(0, tp * ROW_SUB)], ybuf.at[buf, slot],
                                  sem.at[buf]).wait()

    @pl.when(i == 0)
    def _():
        issue(pos_ref, 0)

    wait_buf(cur)
    nxt = 1 - cur
    per_token_mod = mod_ref.shape[1] != 1
    for lo in range(0, tp, FIN_CHUNK):
        hi = min(lo + FIN_CHUNK, tp)
        for r in range(lo, hi):
            for slot in range(2):
                src = pl.multiple_of(posn_ref[0, slot, r] * ROW_SUB, ROW_SUB)
                pltpu.make_async_copy(
                    ys_ref.at[pl.ds(src, ROW_SUB)],
                    ybuf.at[nxt, slot, pl.ds(r * ROW_SUB, ROW_SUB)], sem.at[nxt]).start(priority=slot)
        route = route_ref[lo:hi, :]
        moe = (route[:, 2:3] * _load_rows(ybuf.at[cur, 0], hi - lo, lo)
               + route[:, 3:4] * _load_rows(ybuf.at[cur, 1], hi - lo, lo))
        mod = mod_ref[0, lo:hi, :] if per_token_mod else mod_ref[0]
        y_ref[0, lo:hi, :] = _rms(x1_ref[0, lo:hi, :] + mod[:, 5 * d:6 * d] * moe, fg_ref[...])

    @pl.when(i == n_steps - 1)
    def _():
        wait_buf(nxt)


def _finish(x1, route, mod, final_g, ys, pos3):
    b, t, d = x1.shape
    tp = pos3.shape[2]
    assert t % tp == 0
    nt = t // tp
    n_steps = b * nt
    assert pos3.shape[0] == n_steps
    blk0 = 0
    if mod.shape[1] == 1:
        mod_spec = pl.BlockSpec((1, 1, 6 * d), lambda i: (i // nt, 0, 0))
    else:
        mod_spec = pl.BlockSpec((1, tp, 6 * d), lambda i: (i // nt, i % nt, 0))
    return pl.pallas_call(
        functools.partial(_fin_kernel, tp=tp, d=d, n_steps=n_steps),
        grid_spec=pltpu.PrefetchScalarGridSpec(
            num_scalar_prefetch=0,
            grid=(n_steps,),
            in_specs=[pl.BlockSpec((1, 2, tp), lambda i: (blk0 + i, 0, 0), memory_space=pltpu.SMEM),
                      pl.BlockSpec((1, 2, tp), lambda i: (blk0 + jnp.minimum(i + 1, n_steps - 1), 0, 0),
                                   memory_space=pltpu.SMEM),
                      pl.BlockSpec((1, tp, d), lambda i: (i // nt, i % nt, 0)),
                      pl.BlockSpec((tp, LANES), lambda i: (i, 0)),
                      mod_spec,
                      _const_spec((1, d)),
                      pl.BlockSpec(memory_space=pl.ANY)],
            out_specs=pl.BlockSpec((1, tp, d), lambda i: (i // nt, i % nt, 0)),
            scratch_shapes=[pltpu.VMEM((2, 2, tp * ROW_SUB, LANES), F32),
                            pltpu.SemaphoreType.DMA((2,))]),
        out_shape=jax.ShapeDtypeStruct((b, t, d), F32),
        compiler_params=pltpu.CompilerParams(dimension_semantics=("arbitrary",),
                                             vmem_limit_bytes=VMEM_LIMIT),
        name="fin",
    )(pos3, pos3, x1, route, mod, final_g.reshape(1, d), ys)


def _expert_onehots(route):
    lane = lax.broadcasted_iota(I32, route.shape, 1).astype(F32)
    return ((lane == route[:, 0:1]).astype(F32), (lane == route[:, 1:2]).astype(F32))


def _count_kernel(route_ref, base_ref, tot_ref, acc):
    @pl.when(pl.program_id(0) == 0)
    def _():
        acc[...] = jnp.zeros_like(acc)

    oh0, oh1 = _expert_onehots(route_ref[...])
    base_ref[0] = acc[...]
    acc[...] = acc[...] + jnp.sum(oh0 + oh1, axis=0, keepdims=True)
    tot_ref[...] = acc[...]


def _pos_kernel(route_ref, base_ref, seg_ref, pos_ref, tril_s, *, tp):
    @pl.when(pl.program_id(0) == 0)
    def _():
        r_i = lax.broadcasted_iota(I32, (tp, tp), 0)
        c_i = lax.broadcasted_iota(I32, (tp, tp), 1)
        tril_s[...] = (c_i < r_i).astype(BF16)

    oh0, oh1 = _expert_onehots(route_ref[...])
    before = jnp.dot(tril_s[...], (oh0 + oh1).astype(BF16), preferred_element_type=F32)
    offs = seg_ref[0:1, :] + base_ref[0][0:1, :] + before
    ones8 = jnp.ones((8, LANES), BF16)

    def lane_sum_as_row(v):
        hi = v.astype(BF16)
        r1 = v - hi.astype(F32)
        mid = r1.astype(BF16)
        lo = (r1 - mid.astype(F32)).astype(BF16)
        return _bdot_nt(ones8, hi) + _bdot_nt(ones8, mid) + _bdot_nt(ones8, lo)

    p0 = lane_sum_as_row(oh0 * offs)
    p1 = lane_sum_as_row(oh1 * offs)
    row = lax.broadcasted_iota(I32, (8, tp), 0)
    pos_ref[0] = jnp.where(row == 0, p0, jnp.where(row == 1, p1, 0.0)).astype(I32)


def _count(route, tp):
    n = route.shape[0]
    assert n % tp == 0
    return pl.pallas_call(
        _count_kernel, grid=(n // tp,),
        in_specs=[pl.BlockSpec((tp, LANES), lambda i: (i, 0))],
        out_specs=[pl.BlockSpec((1, 8, LANES), lambda i: (i, 0, 0)), _const_spec((8, LANES))],
        out_shape=[jax.ShapeDtypeStruct((n // tp, 8, LANES), F32),
                   jax.ShapeDtypeStruct((8, LANES), F32)],
        scratch_shapes=[pltpu.VMEM((8, LANES), F32)],
        compiler_params=pltpu.CompilerParams(dimension_semantics=("arbitrary",),
                                             vmem_limit_bytes=VMEM_LIMIT),
        name="count",
    )(route)


def _positions(route, base, seg_start, tp):
    n = route.shape[0]
    seg = jnp.broadcast_to(jnp.pad(seg_start.astype(F32), (0, LANES - N_EXPERTS))[None, :], (8, LANES))
    pos = pl.pallas_call(
        functools.partial(_pos_kernel, tp=tp), grid=(n // tp,),
        in_specs=[pl.BlockSpec((tp, LANES), lambda i: (i, 0)),
                  pl.BlockSpec((1, 8, LANES), lambda i: (i, 0, 0)),
                  _const_spec((8, LANES))],
        out_specs=pl.BlockSpec((1, 8, tp), lambda i: (i, 0, 0)),
        out_shape=jax.ShapeDtypeStruct((n // tp, 8, tp), I32),
        scratch_shapes=[pltpu.VMEM((tp, tp), BF16)],
        compiler_params=pltpu.CompilerParams(dimension_semantics=("arbitrary",),
                                             vmem_limit_bytes=VMEM_LIMIT),
        name="positions",
    )(route, base, seg)
    return pos[:, 0:2, :]


def _plan(counts, tm, n_tiles):
    tiles_per = (counts + tm - 1) // tm
    tile_end = jnp.cumsum(tiles_per)
    seg_start = (tile_end - tiles_per) * tm
    n_used = tile_end[-1]
    tile_ids = jnp.arange(n_tiles, dtype=I32)
    tile_expert = jnp.sum((tile_ids[:, None] >= tile_end[None, :]).astype(I32), axis=1)
    last_used = jnp.sum((n_used - 1 >= tile_end).astype(I32))
    tile_expert = jnp.where(tile_ids < n_used, tile_expert, last_used).astype(I32)
    z_expert = jnp.where((counts % tm) != 0, (tile_end - 1) * tm, -1)
    spare = n_used + jnp.arange(N_EXPERTS, dtype=I32)
    z_spare = jnp.where(spare < n_tiles, spare * tm, -1)
    zoff = jnp.concatenate([z_expert, z_spare]).astype(I32)
    return seg_start, tile_expert, n_used.reshape(1).astype(I32), zoff


def kernel(x_prompt, x_sample, c_prompt, c_sample, state_mlstm_C, state_mlstm_n, state_mlstm_m,
           cache_conv, w_ada, b_ada, norm1_g, w_in, b_igate, b_fgate, mh_norm_g, w_dw, b_dw,
           conv_ln_g, conv_ln_b, w_out, norm2_g, w_grp_router, b_grp_router, w_exp_router,
           b_exp_router, w_gate, w_up, w_down, final_g):
    depth = w_ada.shape[0]
    assert depth == 1, "one layer per step"
    bp, tp_, d = x_prompt.shape
    bs = x_sample.shape[0]
    assert x_sample.shape[1] == 1 and d == ROW_SUB * LANES
    dk = state_mlstm_C.shape[-1]
    dm = N_HEADS * dk
    dc = d - dm
    l = 0

    win = w_in[l]
    w_qkvo = win[:, 0:4 * dm]
    w_gates = win[:, 4 * dm:4 * dm + 2 * N_HEADS]
    w_glu = win[:, 4 * dm + 2 * N_HEADS:]
    wmain, wmain_lo = _split_weights(jnp.concatenate([w_qkvo, w_glu], axis=1))
    wout_hi, wout_lo = _split_weights(w_out[l])
    wkt_hi, wkt_lo = _split_weights(win[:, dm:2 * dm].T)
    wg_pad = jnp.pad(w_gates, ((0, 0), (0, LANES - 2 * N_HEADS)))
    wg_hi, wg_lo = _split_weights(wg_pad)
    gbias = jnp.pad(jnp.concatenate([b_igate[l], b_fgate[l]]), (0, LANES - 2 * N_HEADS)).reshape(1, LANES)
    w_r = jnp.concatenate([w_grp_router[l], w_exp_router[l]], axis=1)
    n_r = N_GROUPS + N_EXPERTS
    wr_hi, wr_lo = _split_weights(jnp.pad(w_r, ((0, 0), (0, LANES - n_r))))
    rbias = jnp.pad(jnp.concatenate([b_grp_router[l], b_exp_router[l]]), (0, LANES - n_r)).reshape(1, LANES)
    wts = dict(dm=dm, g1=norm1_g[l].reshape(1, d), g2=norm2_g[l].reshape(1, d), wmain=wmain,
               wg_hi=wg_hi, wg_lo=wg_lo, gbias=gbias, mhg=mh_norm_g[l],
               wdw=w_dw[l].reshape(CONV_W, dc), bdw=b_dw[l].reshape(1, dc),
               clg=conv_ln_g[l].reshape(1, dc), clb=conv_ln_b[l].reshape(1, dc),
               wout=wout_hi, wr_hi=wr_hi, wr_lo=wr_lo, rbias=rbias, wk_t=wkt_hi,
               wmain_lo=wmain_lo, wout_lo=wout_lo, wk_t_lo=wkt_lo)

    mod = _ada(jnp.concatenate([c_prompt, c_sample], axis=0), w_ada[l], b_ada[l])
    mod_p, mod_s = mod[:bp], mod[bp:]

    n_p = bp * tp_
    n_all = n_p + bs
    x1_p, h2_p, route_p, c_p, n_pr, m_p, cv_p, base_p, tot_p = _mix_prompt(x_prompt, mod_p, wts)
    x1_s, h2_s, route_s, c_s, n_s, m_s, u_s = _mix_sample(
        x_sample.reshape(bs, d), mod_s, state_mlstm_C[l], state_mlstm_n[l], state_mlstm_m[l],
        cache_conv[l], wts)

    tm = MOE_TILE
    n_tiles = (2 * n_all) // tm + N_EXPERTS
    tp_p, tp_s = min(PERM_TILE, tp_), min(PERM_TILE, bs)
    assert tp_p == min(MIX_TILE, tp_), "the prompt mixer counts experts per PERM_TILE tokens"
    base_s, tot_s = _count(route_s, tp_s)
    counts = (tot_p[0, :N_EXPERTS] + tot_s[0, :N_EXPERTS]).astype(I32)
    seg_start, tile_expert, n_used, zoff = _plan(counts, tm, n_tiles)
    pos_p = _positions(route_p, base_p, seg_start, tp_p)
    pos_s = _positions(route_s, base_s + tot_p[None], seg_start, tp_s)

    xs = _scatter_rows(h2_p, pos_p, zoff, n_tiles * tm, tm)
    xs = _scatter_rows(h2_s, pos_s, zoff, xs, tm)
    ys = _moe(xs, tile_expert, n_used, w_gate[l], w_up[l], w_down[l], tm)

    y_p = _finish(x1_p, route_p, mod_p.reshape(bp, 1, -1), final_g, ys, pos_p)
    y_s = _finish(x1_s.reshape(1, bs, d), route_s, mod_s.reshape(1, bs, -1), final_g, ys,
                  pos_s).reshape(bs, 1, d)

    conv_s = jnp.concatenate([cache_conv[l][:, 1:, :], u_s[:, None, :]], axis=1)
    return (y_p, y_s,
            c_p, n_pr, m_p[:, 0, :N_HEADS].reshape(1, bp, N_HEADS), cv_p,
            c_s[None], n_s.reshape(1, bs, N_HEADS, dk), m_s[:, :N_HEADS].reshape(1, bs, N_HEADS),
            conv_s[None])
```

```python
import functools

import jax
import jax.numpy as jnp
from jax import lax
from jax.experimental import pallas as pl
from jax.experimental.pallas import tpu as pltpu

F32 = jnp.float32
BF16 = jnp.bfloat16
I32 = jnp.int32

EPS = 1e-6
LANES = 128
ROW_SUB = 8
CHUNK = 128
N_HEADS = 4
N_GROUPS = 4
EXP_PER_GROUP = 8
N_EXPERTS = N_GROUPS * EXP_PER_GROUP
CONV_W = 31
CONV_PAD = 32
CONV_OFF = CONV_PAD - (CONV_W - 1)
MIX_TILE = 512
MIX_SUB = 256
MOE_TILE = 512
PERM_TILE = 512
FIN_CHUNK = 64
VMEM_LIMIT = 56 * 1024 * 1024


def _sigmoid(x):
    return 1.0 / (1.0 + jnp.exp(-x))


def _silu(x):
    return x * _sigmoid(x)


def _log_sigmoid(x):
    return jnp.minimum(x, 0.0) - jnp.log(1.0 + jnp.exp(-jnp.abs(x)))


def _bdot(a, b):
    return jnp.dot(a.astype(BF16), b.astype(BF16), preferred_element_type=F32)


def _bdot_nt(a, b):
    return lax.dot_general(a.astype(BF16), b.astype(BF16), (((1,), (1,)), ((), ())),
                           preferred_element_type=F32)


def _split2(x):
    hi = x.astype(BF16)
    lo = (x - hi.astype(F32)).astype(BF16)
    return hi, lo


def _split_kernel(w_ref, hi_ref, lo_ref):
    hi, lo = _split2(w_ref[...])
    hi_ref[...] = hi
    lo_ref[...] = lo


def _split_weights(w):
    rows, cols = w.shape
    blk = min(cols, 512)
    assert cols % blk == 0
    spec = pl.BlockSpec((rows, blk), lambda j: (0, j))
    return pl.pallas_call(
        _split_kernel, grid=(cols // blk,), in_specs=[spec], out_specs=[spec, spec],
        out_shape=[jax.ShapeDtypeStruct(w.shape, BF16)] * 2,
        compiler_params=pltpu.CompilerParams(dimension_semantics=("arbitrary",),
                                             vmem_limit_bytes=VMEM_LIMIT),
        name="split",
    )(w)


def _dot3(a, w_hi, w_lo):
    a_hi, a_lo = _split2(a)
    return (jnp.dot(a_hi, w_hi, preferred_element_type=F32)
            + jnp.dot(a_lo, w_hi, preferred_element_type=F32)
            + jnp.dot(a_hi, w_lo, preferred_element_type=F32))


def _cumsum_lanes(triu_bf16, x):
    hi = x.astype(BF16)
    r1 = x - hi.astype(F32)
    mid = r1.astype(BF16)
    lo = (r1 - mid.astype(F32)).astype(BF16)
    return (jnp.dot(hi, triu_bf16, preferred_element_type=F32)
            + jnp.dot(mid, triu_bf16, preferred_element_type=F32)
            + jnp.dot(lo, triu_bf16, preferred_element_type=F32))


def _rms(x, g):
    return x * lax.rsqrt(jnp.mean(x * x, axis=-1, keepdims=True) + EPS) * g


def _layer_norm(x, g, b=None):
    mu = jnp.mean(x, axis=-1, keepdims=True)
    xc = x - mu
    var = jnp.mean(xc * xc, axis=-1, keepdims=True)
    y = xc * lax.rsqrt(var + EPS) * g
    return y if b is None else y + b


def _store_rows(ref, x, row0=0):
    r = x.shape[0]
    for k in range(ROW_SUB):
        ref[pl.ds(row0 * ROW_SUB + k, r, stride=ROW_SUB), :] = x[:, k * LANES:(k + 1) * LANES]


def _load_rows(ref, r, row0=0):
    return jnp.concatenate([ref[pl.ds(row0 * ROW_SUB + k, r, stride=ROW_SUB), :]
                            for k in range(ROW_SUB)], axis=1)


def _route(logits):
    lane = lax.broadcasted_iota(I32, logits.shape, 1).astype(F32)
    neg = jnp.float32(-jnp.inf)
    big = jnp.float32(1e9)
    is_g = lane < N_GROUPS
    gl = jnp.where(is_g, logits, neg)
    gmax = jnp.max(gl, axis=1, keepdims=True)
    gsel = jnp.min(jnp.where(gl == gmax, lane, big), axis=1, keepdims=True)
    pg = 1.0 / jnp.sum(jnp.where(is_g, jnp.exp(gl - gmax), 0.0), axis=1, keepdims=True)
    lo = N_GROUPS + EXP_PER_GROUP * gsel
    emask = (lane >= lo) & (lane < lo + EXP_PER_GROUP)
    el = jnp.where(emask, logits, neg)
    v1 = jnp.max(el, axis=1, keepdims=True)
    i1 = jnp.min(jnp.where(el == v1, lane, big), axis=1, keepdims=True)
    el2 = jnp.where(lane == i1, neg, el)
    v2 = jnp.max(el2, axis=1, keepdims=True)
    i2 = jnp.min(jnp.where(el2 == v2, lane, big), axis=1, keepdims=True)
    d = jnp.exp(v2 - v1)
    w1 = pg / (1.0 + d)
    w2 = pg * d / (1.0 + d)
    return jnp.where(lane == 0, i1 - N_GROUPS,
                     jnp.where(lane == 1, i2 - N_GROUPS,
                               jnp.where(lane == 2, w1, jnp.where(lane == 3, w2, 0.0))))


def _ada_kernel(c_ref, w_ref, b_ref, o_ref):
    w_hi, w_lo = _split2(w_ref[...])
    o_ref[...] = _dot3(_silu(c_ref[...]), w_hi, w_lo) + b_ref[...]


def _ada(c_all, w_ada, b_ada):
    rows, d = c_all.shape
    n_out = w_ada.shape[1]
    blk = 1024
    return pl.pallas_call(
        _ada_kernel,
        grid=(n_out // blk,),
        in_specs=[pl.BlockSpec((rows, d), lambda j: (0, 0)),
                  pl.BlockSpec((d, blk), lambda j: (0, j)),
                  pl.BlockSpec((1, blk), lambda j: (0, j))],
        out_specs=pl.BlockSpec((rows, blk), lambda j: (0, j)),
        out_shape=jax.ShapeDtypeStruct((rows, n_out), F32),
        compiler_params=pltpu.CompilerParams(dimension_semantics=("arbitrary",),
                                             vmem_limit_bytes=VMEM_LIMIT),
        name="ada",
    )(c_all, w_ada, b_ada.reshape(1, n_out))


def _post(x, attn_cat, mod, g2, wout, wr_hi, wr_lo, rbias, d, wout_lo=None):
    gate1 = mod[:, 2 * d:3 * d]
    sh2 = mod[:, 3 * d:4 * d]
    sc2 = mod[:, 4 * d:5 * d]
    if wout_lo is None:
        proj = jnp.dot(attn_cat.astype(BF16), wout, preferred_element_type=F32)
    else:
        proj = _dot3(attn_cat, wout, wout_lo)
    x1 = x + gate1 * proj
    h2 = _rms(x1, g2) * (1.0 + sc2) + sh2
    logits = _dot3(h2, wr_hi, wr_lo) + rbias
    return x1, h2, _route(logits)


def _mix_prompt_kernel(x_ref, mod_ref, g1_ref, g2_ref, wmain_ref, wgh_ref, wgl_ref, gbias_ref,
                       mhg_ref, wdw_ref, bdw_ref, clg_ref, clb_ref, wout_ref, wrh_ref, wrl_ref,
                       rbias_ref,
                       x1_ref, h2_ref, route_ref, c_ref, n_ref, m_ref, cv_ref, base_ref, tot_ref,
                       ubuf, yc_s, q_s, k_s, v_s, so_s, hm_s, p_s, u_s, cm_s, nb_s, m_s, cnt_s,
                       *, tt, d, dm, dk):
    t = pl.program_id(1)
    dc = d - dm
    n_lt = dc // LANES
    sub = min(MIX_SUB, tt)

    @pl.when(t == 0)
    def _():
        c_ref[...] = jnp.zeros_like(c_ref)
        nb_s[...] = jnp.zeros_like(nb_s)
        m_s[...] = jnp.zeros_like(m_s)
        ubuf[0:CONV_PAD, :] = jnp.zeros((CONV_PAD, dc), F32)

    mod = mod_ref[0]
    row8 = lax.broadcasted_iota(I32, (8, LANES), 0)
    row = lax.broadcasted_iota(I32, (CHUNK, CHUNK), 0)
    col = lax.broadcasted_iota(I32, (CHUNK, CHUNK), 1)
    causal = col <= row
    triu = (row <= col).astype(BF16)
    neg = jnp.float32(-jnp.inf)
    ones_b = jnp.ones((CHUNK, dk), BF16)
    pad_rows = jnp.zeros((CHUNK - 8, LANES), F32)

    lanes_of = [slice(lt * LANES, (lt + 1) * LANES) for lt in range(n_lt)]
    wrows = [[jnp.broadcast_to(wdw_ref[j:j + 1, ls], (8, LANES)) for j in range(CONV_W)]
             for ls in lanes_of]
    bias = [jnp.broadcast_to(bdw_ref[:, ls], (8, LANES)) for ls in lanes_of]

    def partial_sums(r0, lt):
        blocks = [ubuf[r0 + 8 * a:r0 + 8 * a + 8, lanes_of[lt]] for a in range(CONV_PAD // 8)]
        sums = []
        for s in range(8):
            acc = None
            for a in range(CONV_PAD // 8):
                j = 8 * a + s - CONV_OFF
                if 0 <= j < CONV_W:
                    term = blocks[a] * wrows[lt][j]
                    acc = term if acc is None else acc + term
            sums.append(acc)
        return tuple(sums)

    q_prev = [None] * n_lt
    ca = [jnp.concatenate([c_ref[0, 0, hd], nb_s[hd]], axis=1) for hd in range(N_HEADS)]
    m_prev = [m_s[hd:hd + 1, :] for hd in range(N_HEADS)]

    n_sub = tt // sub
    gates_of = {}
    hc_of = {}
    tile_counts = []

    def rows_of(sb):
        return slice(sb * sub, (sb + 1) * sub)

    def proj_items(sb):
        rs = rows_of(sb)
        st = {}

        def head():
            x = x_ref[0, rs, :]
            h = _rms(x, g1_ref[...]) * (1.0 + mod[:, d:2 * d]) + mod[:, 0:d]
            st["hb"] = h.astype(BF16)
            gates_of[sb] = _dot3(h, wgh_ref[...], wgl_ref[...]) + gbias_ref[...]

        def proj(lo, hi):
            return jnp.dot(st["hb"], wmain_ref[:, lo:hi], preferred_element_type=F32)

        def glu():
            ubuf[CONV_PAD + sb * sub:CONV_PAD + (sb + 1) * sub, :] = (
                proj(4 * dm, 4 * dm + dc) * _sigmoid(proj(4 * dm + dc, 4 * dm + 2 * dc)))

        def q():
            q_s[rs, :] = proj(0, dm).astype(BF16)

        def k():
            k_s[rs, :] = proj(dm, 2 * dm) * (dk ** -0.5)

        def v():
            v_s[rs, :] = proj(2 * dm, 3 * dm).astype(BF16)

        def o():
            so_s[rs, :] = _sigmoid(proj(3 * dm, 4 * dm))

        return [head, glu, q, k, v, o]

    def conv_items(sb):
        items = []
        for lt in range(n_lt):
            for i in range(sb * sub // 8 + 1, (sb + 1) * sub // 8 + 1):
                def block(lt=lt, i=i):
                    if i == 1:
                        q_prev[lt] = partial_sums(0, lt)
                    q_cur = partial_sums(i * 8, lt)
                    cur = ubuf[(i - 1) * 8 + CONV_PAD:i * 8 + CONV_PAD, lanes_of[lt]]
                    y = bias[lt] + q_prev[lt][0] + cur * wrows[lt][CONV_W - 1]
                    for s in range(1, 8):
                        merged = jnp.where(row8 < s, q_cur[s], q_prev[lt][s])
                        y = y + pltpu.roll(merged, 8 - s, 0)
                    yc_s[(i - 1) * 8:i * 8, lanes_of[lt]] = y
                    q_prev[lt] = q_cur
                items.append(block)
        return items

    def post_items(sb):
        def post():
            rs = rows_of(sb)
            cat = jnp.concatenate([hm_s[rs, :], hc_of[sb]], axis=1)
            x1, h2, route = _post(x_ref[0, rs, :], cat, mod, g2_ref[...], wout_ref[...],
                                  wrh_ref[...], wrl_ref[...], rbias_ref[...], d)
            x1_ref[0, rs, :] = x1
            _store_rows(h2_ref, h2, sb * sub)
            route_ref[rs, :] = route
            oh0, oh1 = _expert_onehots(route)
            tile_counts.append(jnp.sum(oh0 + oh1, axis=0, keepdims=True))
        return [post]

    def interleave(main, side):
        gap = len(main) / (len(side) + 1)
        due, done = gap, 0
        for n, item in enumerate(main):
            item()
            while done < len(side) and n + 1 >= due:
                side[done]()
                done += 1
                due += gap
        for item in side[done:]:
            item()

    for item in proj_items(0):
        item()
    for sb in range(n_sub):
        r_lo = sb * sub
        rs = rows_of(sb)
        side = post_items(sb - 1) if sb > 0 else []
        if sb + 1 < n_sub:
            side = side + proj_items(sb + 1)
        interleave(conv_items(sb), side)
        hc_of[sb] = _silu(_layer_norm(yc_s[rs, :], clg_ref[...], clb_ref[...]))
        gates = gates_of[sb]

        chunks = range(r_lo // CHUNK, (r_lo + sub) // CHUNK)
        b_cols = {}
        for c in chunks:
            r0 = c * CHUNK
            g8 = gates[r0 - r_lo:r0 - r_lo + CHUNK, :].T[0:8, :]
            b8 = _cumsum_lanes(triu, _log_sigmoid(g8))
            pk8 = jnp.where(row8 < N_HEADS, g8 - pltpu.roll(b8, N_HEADS, 0), b8)
            pk = jnp.concatenate([pk8, pad_rows], axis=0).T
            for hd in range(N_HEADS):
                cs = slice(hd * dk, (hd + 1) * dk)
                idx = c * N_HEADS + hd
                kf = k_s[r0:r0 + CHUNK, cs]
                va = jnp.concatenate([v_s[r0:r0 + CHUNK, cs], ones_b], axis=1)
                gm = jnp.where(causal, pk8[hd:hd + 1, :], neg)
                cm = jnp.max(gm, axis=1, keepdims=True)
                s = _bdot_nt(q_s[r0:r0 + CHUNK, cs], kf) * jnp.exp(gm - cm)
                p_s[idx] = jnp.dot(s.astype(BF16), va, preferred_element_type=F32)
                cm_s[idx] = jnp.broadcast_to(cm, (CHUNK, LANES))
                kw = kf * jnp.exp(pk[:, hd:hd + 1] - cm[CHUNK - 1:CHUNK, :])
                u_s[idx] = jnp.dot(kw.T.astype(BF16), va, preferred_element_type=F32)
                b_cols[idx] = pk[:, N_HEADS + hd:N_HEADS + hd + 1]

        for hd in range(N_HEADS):
            cs = slice(hd * dk, (hd + 1) * dk)
            for c in chunks:
                r0 = c * CHUNK
                idx = c * N_HEADS + hd
                cm = cm_s[idx]
                b_col = b_cols[idx]
                mt = jnp.maximum(m_prev[hd], cm)
                f_loc = jnp.exp(cm - mt)
                a_int = jnp.exp(m_prev[hd] - mt)
                qc = jnp.dot(q_s[r0:r0 + CHUNK, cs], ca[hd].astype(BF16),
                             preferred_element_type=F32)
                p = p_s[idx]
                num = f_loc * p[:, :dk] + a_int * qc[:, :dk]
                den = f_loc * p[:, dk:] + a_int * qc[:, dk:]
                hh = num / jnp.maximum(jnp.abs(den), jnp.exp(-(b_col + mt)))
                hm_s[r0:r0 + CHUNK, cs] = (_layer_norm(hh, mhg_ref[hd:hd + 1, :])
                                           * so_s[r0:r0 + CHUNK, cs])
                mt_l = mt[CHUNK - 1:CHUNK, :]
                u = u_s[idx]
                f_l = f_loc[CHUNK - 1:CHUNK, :]
                a_l = a_int[CHUNK - 1:CHUNK, :]
                ca[hd] = jnp.concatenate([a_l * ca[hd][:, :dk] + f_l * u[:, :dk],
                                          a_l * ca[hd][:, dk:] + f_l * u[:, dk:]], axis=1)
                m_prev[hd] = b_col[CHUNK - 1:CHUNK, :] + mt_l

    for item in post_items(n_sub - 1):
        item()

    for hd in range(N_HEADS):
        c_ref[0, 0, hd] = ca[hd][:, :dk]
        nb_s[hd] = ca[hd][:, dk:]
        m_s[hd:hd + 1, :] = m_prev[hd]
    cv_ref[0, 0] = ubuf[tt + CONV_PAD - (CONV_W - 1):tt + CONV_PAD, :]
    ubuf[0:CONV_PAD, :] = ubuf[tt:tt + CONV_PAD, :]

    @pl.when(t == pl.num_programs(1) - 1)
    def _():
        for hd in range(N_HEADS):
            n_ref[0, 0, hd:hd + 1, :] = nb_s[hd].T[0:1, :]

    lane1 = lax.broadcasted_iota(I32, (1, LANES), 1)
    m_row = jnp.zeros((1, LANES), F32)
    for hd in range(N_HEADS):
        m_row = jnp.where(lane1 == hd, m_s[hd:hd + 1, :], m_row)
    m_ref[0] = m_row

    @pl.when((pl.program_id(0) == 0) & (t == 0))
    def _():
        cnt_s[...] = jnp.zeros_like(cnt_s)

    base_ref[0] = cnt_s[...]
    cnt_s[...] = cnt_s[...] + sum(tile_counts)
    tot_ref[...] = cnt_s[...]


def _const_spec(shape):
    nd = len(shape)
    return pl.BlockSpec(shape, lambda *_: (0,) * nd)


def _mix_prompt(x, mod, wts):
    b, t, d = x.shape
    dm = wts["dm"]
    dk = dm // N_HEADS
    dc = d - dm
    tt = min(MIX_TILE, t)
    assert t % tt == 0 and tt % CHUNK == 0 and tt >= CONV_PAD
    nt = t // tt
    kern = functools.partial(_mix_prompt_kernel, tt=tt, d=d, dm=dm, dk=dk)
    const_names = ["g1", "g2", "wmain", "wg_hi", "wg_lo", "gbias", "mhg", "wdw", "bdw", "clg",
                   "clb", "wout", "wr_hi", "wr_lo", "rbias"]
    consts = [wts[k] for k in const_names]
    in_specs = ([pl.BlockSpec((1, tt, d), lambda i, j: (i, j, 0)),
                 pl.BlockSpec((1, 1, mod.shape[-1]), lambda i, j: (i, 0, 0))]
                + [_const_spec(c.shape) for c in consts])
    out_shape = [
        jax.ShapeDtypeStruct((b, t, d), F32),
        jax.ShapeDtypeStruct((b * t * ROW_SUB, LANES), F32),
        jax.ShapeDtypeStruct((b * t, LANES), F32),
        jax.ShapeDtypeStruct((1, b, N_HEADS, dk, dk), F32),
        jax.ShapeDtypeStruct((1, b, N_HEADS, dk), F32),
        jax.ShapeDtypeStruct((b, 1, LANES), F32),
        jax.ShapeDtypeStruct((1, b, CONV_W - 1, dc), F32),
        jax.ShapeDtypeStruct((b * nt, 8, LANES), F32),
        jax.ShapeDtypeStruct((8, LANES), F32),
    ]
    out_specs = [
        pl.BlockSpec((1, tt, d), lambda i, j: (i, j, 0)),
        pl.BlockSpec((tt * ROW_SUB, LANES), lambda i, j: (i * nt + j, 0)),
        pl.BlockSpec((tt, LANES), lambda i, j: (i * nt + j, 0)),
        pl.BlockSpec((1, 1, N_HEADS, dk, dk), lambda i, j: (0, i, 0, 0, 0)),
        pl.BlockSpec((1, 1, N_HEADS, dk), lambda i, j: (0, i, 0, 0)),
        pl.BlockSpec((1, 1, LANES), lambda i, j: (i, 0, 0)),
        pl.BlockSpec((1, 1, CONV_W - 1, dc), lambda i, j: (0, i, 0, 0)),
        pl.BlockSpec((1, 8, LANES), lambda i, j: (i * nt + j, 0, 0)),
        pl.BlockSpec((8, LANES), lambda i, j: (0, 0)),
    ]
    n_hc = (tt // CHUNK) * N_HEADS
    scratch = [pltpu.VMEM((tt + CONV_PAD, dc), F32),
               pltpu.VMEM((tt, dc), F32),
               pltpu.VMEM((tt, dm), BF16),
               pltpu.VMEM((tt, dm), F32),
               pltpu.VMEM((tt, dm), BF16),
               pltpu.VMEM((tt, dm), F32),
               pltpu.VMEM((tt, dm), F32),
               pltpu.VMEM((n_hc, CHUNK, 2 * dk), F32),
               pltpu.VMEM((n_hc, CHUNK, 2 * dk), F32),
               pltpu.VMEM((n_hc, CHUNK, LANES), F32),
               pltpu.VMEM((N_HEADS, dk, LANES), F32),
               pltpu.VMEM((8, LANES), F32),
               pltpu.VMEM((8, LANES), F32)]
    return pl.pallas_call(
        kern, grid=(b, nt), in_specs=in_specs, out_specs=out_specs, out_shape=out_shape,
        scratch_shapes=scratch,
        compiler_params=pltpu.CompilerParams(dimension_semantics=("arbitrary", "arbitrary"),
                                             vmem_limit_bytes=VMEM_LIMIT),
        name="mix_p",
    )(x, mod.reshape(b, 1, -1), *consts)


def _s_pre_kernel(x_ref, mod_ref, g1_ref, wmain_ref, wmainlo_ref, wgh_ref, wgl_ref, gbias_ref,
                  wkt_ref, wktlo_ref, wdw_ref, bdw_ref, clg_ref, clb_ref, cache_ref, n0_ref, m0_ref,
                  q_ref, kt_ref, vs_ref, ab_ref, sv_ref, den_ref, eb_ref, o_ref, hc_ref, u_ref,
                  n_ref, m_ref, *, d, dm, dk):
    dc = d - dm
    x = x_ref[...]
    mod = mod_ref[...]
    sh1 = mod[:, 0:d]
    sc1 = mod[:, d:2 * d]
    h = _rms(x, g1_ref[...]) * (1.0 + sc1) + sh1
    z = _dot3(h, wmain_ref[...], wmainlo_ref[...])
    gates = _dot3(h, wgh_ref[...], wgl_ref[...]) + gbias_ref[...]
    scale = dk ** -0.5
    h_hi, h_lo = _split2(h)
    kt = _bdot_nt(wkt_ref[...], h_hi) + _bdot_nt(wktlo_ref[...], h_hi) + _bdot_nt(wkt_ref[...], h_lo)
    kt_ref[...] = (kt * scale).astype(BF16)
    k_all = z[:, dm:2 * dm] * scale
    ga = z[:, 4 * dm:4 * dm + dc]
    gb = z[:, 4 * dm + dc:4 * dm + 2 * dc]
    u = ga * _sigmoid(gb)
    u_ref[...] = u
    acc = jnp.broadcast_to(bdw_ref[...], u.shape) + u * wdw_ref[CONV_W - 1:CONV_W, :]
    for j in range(CONV_W - 1):
        acc = acc + cache_ref[j] * wdw_ref[j:j + 1, :]
    hc_ref[...] = _silu(_layer_norm(acc, clg_ref[...], clb_ref[...]))
    o_ref[...] = z[:, 3 * dm:4 * dm]
    q_ref[...] = z[:, 0:dm]
    m0 = m0_ref[...]
    n0 = n0_ref[...]
    lane1 = lax.broadcasted_iota(I32, (1, LANES), 1)
    m_new = jnp.zeros(m0.shape, F32)
    for hd in range(N_HEADS):
        cs = slice(hd * dk, (hd + 1) * dk)
        ig = gates[:, hd:hd + 1]
        lf = _log_sigmoid(gates[:, N_HEADS + hd:N_HEADS + hd + 1])
        mp = m0[:, hd:hd + 1]
        inter = lf + mp
        mt = jnp.maximum(inter, ig)
        w = jnp.exp(ig - mt)
        a_int = jnp.exp(inter - mt)
        qf = z[:, cs]
        kf = k_all[:, cs]
        vf = z[:, 2 * dm + hd * dk:2 * dm + (hd + 1) * dk]
        s = jnp.sum(qf * kf, axis=1, keepdims=True) * w
        sv_ref[:, cs] = s * vf
        den_ref[:, cs] = jnp.broadcast_to(
            s + a_int * jnp.sum(qf * n0[:, cs], axis=1, keepdims=True), (x.shape[0], dk))
        eb_ref[:, cs] = jnp.broadcast_to(jnp.exp(-mt), (x.shape[0], dk))
        ab_ref[:, cs] = jnp.broadcast_to(a_int, (x.shape[0], dk))
        vs_ref[:, cs] = (vf * w).astype(BF16)
        n_ref[:, cs] = a_int * n0[:, cs] + w * kf
        m_new = jnp.where(lane1 == hd, mt, m_new)
    m_ref[...] = m_new


def _s_state_kernel(q_ref, kt_ref, vs_ref, ab_ref, c0_ref, c_ref, r_ref, *, bb, dk):
    i = pl.program_id(0)
    nb = q_ref.shape[0]
    rows = lax.broadcasted_iota(I32, (nb, dk), 0)

    @pl.when(i == 0)
    def _():
        r_ref[...] = jnp.zeros_like(r_ref)

    a_blk = ab_ref[pl.ds(pl.multiple_of(i * bb, bb), bb), :]
    for j in range(bb):
        sel = rows == i * bb + j
        for hd in range(N_HEADS):
            cs = slice(hd * dk, (hd + 1) * dk)
            c0 = c0_ref[j, hd]
            vmask = jnp.where(sel, vs_ref[:, cs], jnp.zeros((), BF16))
            c_ref[j, hd] = (a_blk[j:j + 1, cs] * c0
                            + jnp.dot(kt_ref[cs, :], vmask, preferred_element_type=F32))
            c_hi, c_lo = _split2(c0)
            q_hi, q_lo = _split2(q_ref[:, cs])
            r = (jnp.dot(q_hi, c_hi, preferred_element_type=F32)
                 + jnp.dot(q_lo, c_hi, preferred_element_type=F32)
                 + jnp.dot(q_hi, c_lo, preferred_element_type=F32))
            r_ref[:, cs] = r_ref[:, cs] + jnp.where(sel, r, 0.0)


def _s_post_kernel(x_ref, mod_ref, g2_ref, mhg_ref, r_ref, ab_ref, sv_ref, den_ref, eb_ref, o_ref,
                   hc_ref, wout_ref, woutlo_ref, wrh_ref, wrl_ref, rbias_ref,
                   x1_ref, h2_ref, route_ref, *, d, dm, dk):
    hm = []
    for hd in range(N_HEADS):
        cs = slice(hd * dk, (hd + 1) * dk)
        num = sv_ref[:, cs] + ab_ref[:, cs] * r_ref[:, cs]
        hh = num / jnp.maximum(jnp.abs(den_ref[:, cs]), eb_ref[:, cs])
        hm.append(_layer_norm(hh, mhg_ref[hd:hd + 1, :]) * _sigmoid(o_ref[:, cs]))
    cat = jnp.concatenate(hm + [hc_ref[...]], axis=1)
    x1, h2, route = _post(x_ref[...], cat, mod_ref[...], g2_ref[...], wout_ref[...], wrh_ref[...],
                          wrl_ref[...], rbias_ref[...], d, wout_lo=woutlo_ref[...])
    x1_ref[...] = x1
    _store_rows(h2_ref, h2)
    route_ref[...] = route


def _mix_sample(x, mod, c0, n0, m0, cache, wts):
    nb, d = x.shape
    dm = wts["dm"]
    dk = dm // N_HEADS
    dc = d - dm
    cp = pltpu.CompilerParams(dimension_semantics=("arbitrary",), vmem_limit_bytes=VMEM_LIMIT)
    cache_t = jnp.transpose(cache, (1, 0, 2))
    m0p = jnp.pad(m0, ((0, 0), (0, LANES - N_HEADS)))
    pre_in = [x, mod, wts["g1"], wts["wmain"], wts["wmain_lo"], wts["wg_hi"], wts["wg_lo"],
              wts["gbias"], wts["wk_t"], wts["wk_t_lo"], wts["wdw"], wts["bdw"], wts["clg"],
              wts["clb"], cache_t, n0.reshape(nb, dm), m0p]
    pre_out = [jax.ShapeDtypeStruct((nb, dm), F32),
               jax.ShapeDtypeStruct((dm, nb), BF16),
               jax.ShapeDtypeStruct((nb, dm), BF16),
               jax.ShapeDtypeStruct((nb, dm), F32),
               jax.ShapeDtypeStruct((nb, dm), F32),
               jax.ShapeDtypeStruct((nb, dm), F32),
               jax.ShapeDtypeStruct((nb, dm), F32),
               jax.ShapeDtypeStruct((nb, dm), F32),
               jax.ShapeDtypeStruct((nb, dc), F32),
               jax.ShapeDtypeStruct((nb, dc), F32),
               jax.ShapeDtypeStruct((nb, dm), F32),
               jax.ShapeDtypeStruct((nb, LANES), F32)]
    (q, kt, vs, ab, sv, den, eb, o, hc, u, n1, m1) = pl.pallas_call(
        functools.partial(_s_pre_kernel, d=d, dm=dm, dk=dk),
        grid=(1,),
        in_specs=[_const_spec(a.shape) for a in pre_in],
        out_specs=[_const_spec(s.shape) for s in pre_out],
        out_shape=pre_out, compiler_params=cp, name="s_pre")(*pre_in)

    bb = 8
    assert nb % bb == 0
    c1, r = pl.pallas_call(
        functools.partial(_s_state_kernel, bb=bb, dk=dk),
        grid=(nb // bb,),
        in_specs=[_const_spec(q.shape), _const_spec(kt.shape), _const_spec(vs.shape),
                  _const_spec(ab.shape),
                  pl.BlockSpec((bb, N_HEADS, dk, dk), lambda i: (i, 0, 0, 0))],
        out_specs=[pl.BlockSpec((bb, N_HEADS, dk, dk), lambda i: (i, 0, 0, 0)),
                   _const_spec((nb, dm))],
        out_shape=[jax.ShapeDtypeStruct((nb, N_HEADS, dk, dk), F32),
                   jax.ShapeDtypeStruct((nb, dm), F32)],
        compiler_params=cp, name="s_state")(q, kt, vs, ab, c0)

    post_in = [x, mod, wts["g2"], wts["mhg"], r, ab, sv, den, eb, o, hc, wts["wout"],
               wts["wout_lo"], wts["wr_hi"], wts["wr_lo"], wts["rbias"]]
    post_out = [jax.ShapeDtypeStruct((nb, d), F32),
                jax.ShapeDtypeStruct((nb * ROW_SUB, LANES), F32),
                jax.ShapeDtypeStruct((nb, LANES), F32)]
    x1, h2, route = pl.pallas_call(
        functools.partial(_s_post_kernel, d=d, dm=dm, dk=dk),
        grid=(1,),
        in_specs=[_const_spec(a.shape) for a in post_in],
        out_specs=[_const_spec(s.shape) for s in post_out],
        out_shape=post_out, compiler_params=cp, name="s_post")(*post_in)
    return x1, h2, route, c1, n1, m1, u


def _scatter_kernel(pos_ref, zoff_ref, src_ref, h2_hbm_ref, *rest, tp, tm, create):
    xs_ref, zbuf, sem, sem2, zsem = rest[-5:]
    i = pl.program_id(0)

    if create:
        @pl.when(i == 0)
        def _():
            zbuf[...] = jnp.zeros_like(zbuf)

            def zero_copy(e):
                start = pl.multiple_of(zoff_ref[e] * ROW_SUB, tm * ROW_SUB)
                return pltpu.make_async_copy(zbuf, xs_ref.at[pl.ds(start, tm * ROW_SUB)], zsem)

            for e in range(zoff_ref.shape[0]):
                @pl.when(zoff_ref[e] >= 0)
                def _():
                    zero_copy(e).start()
            for e in range(zoff_ref.shape[0]):
                @pl.when(zoff_ref[e] >= 0)
                def _():
                    zero_copy(e).wait()

    row0 = pl.multiple_of(i * (tp * ROW_SUB), tp * ROW_SUB)
    for r in range(tp):
        dst0 = pl.multiple_of(pos_ref[0, 0, r] * ROW_SUB, ROW_SUB)
        pltpu.make_async_copy(src_ref.at[pl.ds(r * ROW_SUB, ROW_SUB)],
                              xs_ref.at[pl.ds(dst0, ROW_SUB)], sem).start()
        dst1 = pl.multiple_of(pos_ref[0, 1, r] * ROW_SUB, ROW_SUB)
        pltpu.make_async_copy(h2_hbm_ref.at[pl.ds(row0 + r * ROW_SUB, ROW_SUB)],
                              xs_ref.at[pl.ds(dst1, ROW_SUB)], sem2).start()
    pltpu.make_async_copy(src_ref, xs_ref.at[pl.ds(0, tp * ROW_SUB)], sem).wait()
    pltpu.make_async_copy(h2_hbm_ref.at[pl.ds(0, tp * ROW_SUB)], xs_ref.at[pl.ds(0, tp * ROW_SUB)],
                          sem2).wait()


def _scatter_rows(h2, pos3, zoff, xs_or_rows, tm):
    n, c = h2.shape[0] // ROW_SUB, LANES
    tp = pos3.shape[2]
    assert pos3.shape == (n // tp, 2, tp)
    create = isinstance(xs_or_rows, int)
    n_sorted = xs_or_rows * ROW_SUB if create else xs_or_rows.shape[0]
    in_specs = [pl.BlockSpec((1, 2, tp), lambda i: (i, 0, 0), memory_space=pltpu.SMEM),
                pl.BlockSpec(memory_space=pltpu.SMEM),
                pl.BlockSpec((tp * ROW_SUB, c), lambda i: (i, 0)),
                pl.BlockSpec(memory_space=pl.ANY)]
    args = [pos3, zoff, h2, h2]
    if not create:
        in_specs.append(pl.BlockSpec(memory_space=pl.ANY))
        args.append(xs_or_rows)
    return pl.pallas_call(
        functools.partial(_scatter_kernel, tp=tp, tm=tm, create=create),
        grid_spec=pltpu.PrefetchScalarGridSpec(
            num_scalar_prefetch=0,
            grid=(n // tp,),
            in_specs=in_specs,
            out_specs=pl.BlockSpec(memory_space=pl.ANY),
            scratch_shapes=[pltpu.VMEM((tm * ROW_SUB, c), F32), pltpu.SemaphoreType.DMA(()),
                            pltpu.SemaphoreType.DMA(()), pltpu.SemaphoreType.DMA(())]),
        out_shape=jax.ShapeDtypeStruct((n_sorted, c), F32),
        input_output_aliases={} if create else {4: 0},
        compiler_params=pltpu.CompilerParams(dimension_semantics=("arbitrary",),
                                             vmem_limit_bytes=VMEM_LIMIT),
        name="scatter",
    )(*args)


def _moe_kernel(te_ref, nu_ref, xs_ref, wg_ref, wu_ref, wd_ref, ys_ref, wg_b, wu_b, wd_b):
    i = pl.program_id(0)
    used = i < nu_ref[0]

    @pl.when(used & ((i == 0) | (te_ref[i] != te_ref[jnp.maximum(i - 1, 0)])))
    def _():
        wg_b[...] = wg_ref[0].astype(BF16)
        wu_b[...] = wu_ref[0].astype(BF16)
        wd_b[...] = wd_ref[0].astype(BF16)

    @pl.when(used)
    def _():
        xb = _load_rows(xs_ref, xs_ref.shape[0] // ROW_SUB).astype(BF16)
        g = jnp.dot(xb, wg_b[...], preferred_element_type=F32)
        u = jnp.dot(xb, wu_b[...], preferred_element_type=F32)
        hid = (_silu(g) * u).astype(BF16)
        _store_rows(ys_ref, jnp.dot(hid, wd_b[...], preferred_element_type=F32))

    @pl.when(jnp.logical_not(used))
    def _():
        ys_ref[...] = jnp.zeros_like(ys_ref)


def _moe(xs, tile_expert, n_used, w_gate, w_up, w_down, tm):
    p, c = xs.shape[0] // ROW_SUB, LANES
    ne, d, de = w_gate.shape
    n_tiles = p // tm

    def x_map(i, te, nu):
        return (jnp.minimum(i, jnp.maximum(nu[0] - 1, 0)), 0)

    def w_map(i, te, nu):
        return (te[i], 0, 0)

    return pl.pallas_call(
        _moe_kernel,
        grid_spec=pltpu.PrefetchScalarGridSpec(
            num_scalar_prefetch=2,
            grid=(n_tiles,),
            in_specs=[pl.BlockSpec((tm * ROW_SUB, c), x_map),
                      pl.BlockSpec((1, d, de), w_map),
                      pl.BlockSpec((1, d, de), w_map),
                      pl.BlockSpec((1, de, d), w_map)],
            out_specs=pl.BlockSpec((tm * ROW_SUB, c), lambda i, te, nu: (i, 0)),
            scratch_shapes=[pltpu.VMEM((d, de), BF16), pltpu.VMEM((d, de), BF16),
                            pltpu.VMEM((de, d), BF16)]),
        out_shape=jax.ShapeDtypeStruct((p * ROW_SUB, c), F32),
        compiler_params=pltpu.CompilerParams(dimension_semantics=("arbitrary",),
                                             vmem_limit_bytes=VMEM_LIMIT),
        name="moe",
    )(tile_expert, n_used, xs, w_gate, w_up, w_down)


def _fin_kernel(pos_ref, posn_ref, x1_ref, route_ref, mod_ref, fg_ref, ys_ref, y_ref, ybuf, sem,
                *, tp, d, n_steps):
    i = pl.program_id(0)
    cur = i % 2

    def issue(p_ref, buf):
        def row_start(r, carry):
            dst = pl.ds(pl.multiple_of(r * ROW_SUB, ROW_SUB), ROW_SUB)
            for slot in range(2):
                src = pl.multiple_of(p_ref[0, slot, r] * ROW_SUB, ROW_SUB)
                pltpu.make_async_copy(ys_ref.at[pl.ds(src, ROW_SUB)],
                                      ybuf.at[buf, slot, dst], sem.at[buf]).start(priority=slot)
            return carry
        lax.fori_loop(0, tp, row_start, 0, unroll=8)

    def wait_buf(buf):
        for slot in range(2):
            pltpu.make_async_copy(ys_ref.at[pl.ds(0, tp * ROW_SUB)], ybuf.at[buf, slot],
                                  sem.at[buf]).wait()

    @pl.when(i == 0)
    def _():
        issue(pos_ref, 0)

    wait_buf(cur)
    nxt = 1 - cur
    per_token_mod = mod_ref.shape[1] != 1
    for lo in range(0, tp, FIN_CHUNK):
        hi = min(lo + FIN_CHUNK, tp)
        for r in range(lo, hi):
            for slot in range(2):
                src = pl.multiple_of(posn_ref[0, slot, r] * ROW_SUB, ROW_SUB)
                pltpu.make_async_copy(
                    ys_ref.at[pl.ds(src, ROW_SUB)],
                    ybuf.at[nxt, slot, pl.ds(r * ROW_SUB, ROW_SUB)], sem.at[nxt]).start(priority=slot)
        route = route_ref[lo:hi, :]
        moe = (route[:, 2:3] * _load_rows(ybuf.at[cur, 0], hi - lo, lo)
               + route[:, 3:4] * _load_rows(ybuf.at[cur, 1], hi - lo, lo))
        mod = mod_ref[0, lo:hi, :] if per_token_mod else mod_ref[0]
        y_ref[0, lo:hi, :] = _rms(x1_ref[0, lo:hi, :] + mod[:, 5 * d:6 * d] * moe, fg_ref[...])

    @pl.when(i == n_steps - 1)
    def _():
        wait_buf(nxt)


def _finish(x1, route, mod, final_g, ys, pos3):
    b, t, d = x1.shape
    tp = pos3.shape[2]
    assert t % tp == 0
    nt = t // tp
    n_steps = b * nt
    assert pos3.shape[0] == n_steps
    blk0 = 0
    if mod.shape[1] == 1:
        mod_spec = pl.BlockSpec((1, 1, 6 * d), lambda i: (i // nt, 0, 0))
    else:
        mod_spec = pl.BlockSpec((1, tp, 6 * d), lambda i: (i // nt, i % nt, 0))
    return pl.pallas_call(
        functools.partial(_fin_kernel, tp=tp, d=d, n_steps=n_steps),
        grid_spec=pltpu.PrefetchScalarGridSpec(
            num_scalar_prefetch=0,
            grid=(n_steps,),
            in_specs=[pl.BlockSpec((1, 2, tp), lambda i: (blk0 + i, 0, 0), memory_space=pltpu.SMEM),
                      pl.BlockSpec((1, 2, tp), lambda i: (blk0 + jnp.minimum(i + 1, n_steps - 1), 0, 0),
                                   memory_space=pltpu.SMEM),
                      pl.BlockSpec((1, tp, d), lambda i: (i // nt, i % nt, 0)),
                      pl.BlockSpec((tp, LANES), lambda i: (i, 0)),
                      mod_spec,
                      _const_spec((1, d)),
                      pl.BlockSpec(memory_space=pl.ANY)],
            out_specs=pl.BlockSpec((1, tp, d), lambda i: (i // nt, i % nt, 0)),
            scratch_shapes=[pltpu.VMEM((2, 2, tp * ROW_SUB, LANES), F32),
                            pltpu.SemaphoreType.DMA((2,))]),
        out_shape=jax.ShapeDtypeStruct((b, t, d), F32),
        compiler_params=pltpu.CompilerParams(dimension_semantics=("arbitrary",),
                                             vmem_limit_bytes=VMEM_LIMIT),
        name="fin",
    )(pos3, pos3, x1, route, mod, final_g.reshape(1, d), ys)


def _expert_onehots(route):
    lane = lax.broadcasted_iota(I32, route.shape, 1).astype(F32)
    return ((lane == route[:, 0:1]).astype(F32), (lane == route[:, 1:2]).astype(F32))


def _count_kernel(route_ref, base_ref, tot_ref, acc):
    @pl.when(pl.program_id(0) == 0)
    def _():
        acc[...] = jnp.zeros_like(acc)

    oh0, oh1 = _expert_onehots(route_ref[...])
    base_ref[0] = acc[...]
    acc[...] = acc[...] + jnp.sum(oh0 + oh1, axis=0, keepdims=True)
    tot_ref[...] = acc[...]


def _pos_kernel(route_ref, base_ref, seg_ref, pos_ref, tril_s, *, tp):
    @pl.when(pl.program_id(0) == 0)
    def _():
        r_i = lax.broadcasted_iota(I32, (tp, tp), 0)
        c_i = lax.broadcasted_iota(I32, (tp, tp), 1)
        tril_s[...] = (c_i < r_i).astype(BF16)

    oh0, oh1 = _expert_onehots(route_ref[...])
    before = jnp.dot(tril_s[...], (oh0 + oh1).astype(BF16), preferred_element_type=F32)
    offs = seg_ref[0:1, :] + base_ref[0][0:1, :] + before
    ones8 = jnp.ones((8, LANES), BF16)

    def lane_sum_as_row(v):
        hi = v.astype(BF16)
        r1 = v - hi.astype(F32)
        mid = r1.astype(BF16)
        lo = (r1 - mid.astype(F32)).astype(BF16)
        return _bdot_nt(ones8, hi) + _bdot_nt(ones8, mid) + _bdot_nt(ones8, lo)

    p0 = lane_sum_as_row(oh0 * offs)
    p1 = lane_sum_as_row(oh1 * offs)
    row = lax.broadcasted_iota(I32, (8, tp), 0)
    pos_ref[0] = jnp.where(row == 0, p0, jnp.where(row == 1, p1, 0.0)).astype(I32)


def _count(route, tp):
    n = route.shape[0]
    assert n % tp == 0
    return pl.pallas_call(
        _count_kernel, grid=(n // tp,),
        in_specs=[pl.BlockSpec((tp, LANES), lambda i: (i, 0))],
        out_specs=[pl.BlockSpec((1, 8, LANES), lambda i: (i, 0, 0)), _const_spec((8, LANES))],
        out_shape=[jax.ShapeDtypeStruct((n // tp, 8, LANES), F32),
                   jax.ShapeDtypeStruct((8, LANES), F32)],
        scratch_shapes=[pltpu.VMEM((8, LANES), F32)],
        compiler_params=pltpu.CompilerParams(dimension_semantics=("arbitrary",),
                                             vmem_limit_bytes=VMEM_LIMIT),
        name="count",
    )(route)


def _positions(route, base, seg_start, tp):
    n = route.shape[0]
    seg = jnp.broadcast_to(jnp.pad(seg_start.astype(F32), (0, LANES - N_EXPERTS))[None, :], (8, LANES))
    pos = pl.pallas_call(
        functools.partial(_pos_kernel, tp=tp), grid=(n // tp,),
        in_specs=[pl.BlockSpec((tp, LANES), lambda i: (i, 0)),
                  pl.BlockSpec((1, 8, LANES), lambda i: (i, 0, 0)),
                  _const_spec((8, LANES))],
        out_specs=pl.BlockSpec((1, 8, tp), lambda i: (i, 0, 0)),
        out_shape=jax.ShapeDtypeStruct((n // tp, 8, tp), I32),
        scratch_shapes=[pltpu.VMEM((tp, tp), BF16)],
        compiler_params=pltpu.CompilerParams(dimension_semantics=("arbitrary",),
                                             vmem_limit_bytes=VMEM_LIMIT),
        name="positions",
    )(route, base, seg)
    return pos[:, 0:2, :]


def _plan(counts, tm, n_tiles):
    tiles_per = (counts + tm - 1) // tm
    tile_end = jnp.cumsum(tiles_per)
    seg_start = (tile_end - tiles_per) * tm
    n_used = tile_end[-1]
    tile_ids = jnp.arange(n_tiles, dtype=I32)
    tile_expert = jnp.sum((tile_ids[:, None] >= tile_end[None, :]).astype(I32), axis=1)
    last_used = jnp.sum((n_used - 1 >= tile_end).astype(I32))
    tile_expert = jnp.where(tile_ids < n_used, tile_expert, last_used).astype(I32)
    z_expert = jnp.where((counts % tm) != 0, (tile_end - 1) * tm, -1)
    spare = n_used + jnp.arange(N_EXPERTS, dtype=I32)
    z_spare = jnp.where(spare < n_tiles, spare * tm, -1)
    zoff = jnp.concatenate([z_expert, z_spare]).astype(I32)
    return seg_start, tile_expert, n_used.reshape(1).astype(I32), zoff


def kernel(x_prompt, x_sample, c_prompt, c_sample, state_mlstm_C, state_mlstm_n, state_mlstm_m,
           cache_conv, w_ada, b_ada, norm1_g, w_in, b_igate, b_fgate, mh_norm_g, w_dw, b_dw,
           conv_ln_g, conv_ln_b, w_out, norm2_g, w_grp_router, b_grp_router, w_exp_router,
           b_exp_router, w_gate, w_up, w_down, final_g):
    depth = w_ada.shape[0]
    assert depth == 1, "one layer per step"
    bp, tp_, d = x_prompt.shape
    bs = x_sample.shape[0]
    assert x_sample.shape[1] == 1 and d == ROW_SUB * LANES
    dk = state_mlstm_C.shape[-1]
    dm = N_HEADS * dk
    dc = d - dm
    l = 0

    win = w_in[l]
    w_qkvo = win[:, 0:4 * dm]
    w_gates = win[:, 4 * dm:4 * dm + 2 * N_HEADS]
    w_glu = win[:, 4 * dm + 2 * N_HEADS:]
    wmain, wmain_lo = _split_weights(jnp.concatenate([w_qkvo, w_glu], axis=1))
    wout_hi, wout_lo = _split_weights(w_out[l])
    wkt_hi, wkt_lo = _split_weights(win[:, dm:2 * dm].T)
    wg_pad = jnp.pad(w_gates, ((0, 0), (0, LANES - 2 * N_HEADS)))
    wg_hi, wg_lo = _split_weights(wg_pad)
    gbias = jnp.pad(jnp.concatenate([b_igate[l], b_fgate[l]]), (0, LANES - 2 * N_HEADS)).reshape(1, LANES)
    w_r = jnp.concatenate([w_grp_router[l], w_exp_router[l]], axis=1)
    n_r = N_GROUPS + N_EXPERTS
    wr_hi, wr_lo = _split_weights(jnp.pad(w_r, ((0, 0), (0, LANES - n_r))))
    rbias = jnp.pad(jnp.concatenate([b_grp_router[l], b_exp_router[l]]), (0, LANES - n_r)).reshape(1, LANES)
    wts = dict(dm=dm, g1=norm1_g[l].reshape(1, d), g2=norm2_g[l].reshape(1, d), wmain=wmain,
               wg_hi=wg_hi, wg_lo=wg_lo, gbias=gbias, mhg=mh_norm_g[l],
               wdw=w_dw[l].reshape(CONV_W, dc), bdw=b_dw[l].reshape(1, dc),
               clg=conv_ln_g[l].reshape(1, dc), clb=conv_ln_b[l].reshape(1, dc),
               wout=wout_hi, wr_hi=wr_hi, wr_lo=wr_lo, rbias=rbias, wk_t=wkt_hi,
               wmain_lo=wmain_lo, wout_lo=wout_lo, wk_t_lo=wkt_lo)

    mod = _ada(jnp.concatenate([c_prompt, c_sample], axis=0), w_ada[l], b_ada[l])
    mod_p, mod_s = mod[:bp], mod[bp:]

    n_p = bp * tp_
    n_all = n_p + bs
    x1_p, h2_p, route_p, c_p, n_pr, m_p, cv_p, base_p, tot_p = _mix_prompt(x_prompt, mod_p, wts)
    x1_s, h2_s, route_s, c_s, n_s, m_s, u_s = _mix_sample(
        x_sample.reshape(bs, d), mod_s, state_mlstm_C[l], state_mlstm_n[l], state_mlstm_m[l],
        cache_conv[l], wts)

    tm = MOE_TILE
    n_tiles = (2 * n_all) // tm + N_EXPERTS
    tp_p, tp_s = min(PERM_TILE, tp_), min(PERM_TILE, bs)
    assert tp_p == min(MIX_TILE, tp_), "the prompt mixer counts experts per PERM_TILE tokens"
    base_s, tot_s = _count(route_s, tp_s)
    counts = (tot_p[0, :N_EXPERTS] + tot_s[0, :N_EXPERTS]).astype(I32)
    seg_start, tile_expert, n_used, zoff = _plan(counts, tm, n_tiles)
    pos_p = _positions(route_p, base_p, seg_start, tp_p)
    pos_s = _positions(route_s, base_s + tot_p[None], seg_start, tp_s)

    xs = _scatter_rows(h2_p, pos_p, zoff, n_tiles * tm, tm)
    xs = _scatter_rows(h2_s, pos_s, zoff, xs, tm)
    ys = _moe(xs, tile_expert, n_used, w_gate[l], w_up[l], w_down[l], tm)

    y_p = _finish(x1_p, route_p, mod_p.reshape(bp, 1, -1), final_g, ys, pos_p)
    y_s = _finish(x1_s.reshape(1, bs, d), route_s, mod_s.reshape(1, bs, -1), final_g, ys,
                  pos_s).reshape(bs, 1, d)

    conv_s = jnp.concatenate([cache_conv[l][:, 1:, :], u_s[:, None, :]], axis=1)
    return (y_p, y_s,
            c_p, n_pr, m_p[:, 0, :N_HEADS].reshape(1, bp, N_HEADS), cv_p,
            c_s[None], n_s.reshape(1, bs, N_HEADS, dk), m_s[:, :N_HEADS].reshape(1, bs, N_HEADS),
            conv_s[None])
```

```python
import functools

import jax
import jax.numpy as jnp
from jax import lax
from jax.experimental import pallas as pl
from jax.experimental.pallas import tpu as pltpu

F32 = jnp.float32
BF16 = jnp.bfloat16
I32 = jnp.int32

EPS = 1e-6
LANES = 128
ROW_SUB = 8
CHUNK = 128
N_HEADS = 4
N_GROUPS = 4
EXP_PER_GROUP = 8
N_EXPERTS = N_GROUPS * EXP_PER_GROUP
CONV_W = 31
CONV_PAD = 32
CONV_OFF = CONV_PAD - (CONV_W - 1)
MIX_TILE = 512
MIX_SUB = 256
MOE_TILE = 512
PERM_TILE = 512
FIN_CHUNK = 64
VMEM_LIMIT = 56 * 1024 * 1024


def _sigmoid(x):
    return 1.0 / (1.0 + jnp.exp(-x))


def _silu(x):
    return x * _sigmoid(x)


def _log_sigmoid(x):
    return jnp.minimum(x, 0.0) - jnp.log(1.0 + jnp.exp(-jnp.abs(x)))


def _bdot(a, b):
    return jnp.dot(a.astype(BF16), b.astype(BF16), preferred_element_type=F32)


def _bdot_nt(a, b):
    return lax.dot_general(a.astype(BF16), b.astype(BF16), (((1,), (1,)), ((), ())),
                           preferred_element_type=F32)


def _split2(x):
    hi = x.astype(BF16)
    lo = (x - hi.astype(F32)).astype(BF16)
    return hi, lo


def _split_kernel(w_ref, hi_ref, lo_ref):
    hi, lo = _split2(w_ref[...])
    hi_ref[...] = hi
    lo_ref[...] = lo


def _split_weights(w):
    rows, cols = w.shape
    blk = min(cols, 512)
    assert cols % blk == 0
    spec = pl.BlockSpec((rows, blk), lambda j: (0, j))
    return pl.pallas_call(
        _split_kernel, grid=(cols // blk,), in_specs=[spec], out_specs=[spec, spec],
        out_shape=[jax.ShapeDtypeStruct(w.shape, BF16)] * 2,
        compiler_params=pltpu.CompilerParams(dimension_semantics=("arbitrary",),
                                             vmem_limit_bytes=VMEM_LIMIT),
        name="split",
    )(w)


def _dot3(a, w_hi, w_lo):
    a_hi, a_lo = _split2(a)
    return (jnp.dot(a_hi, w_hi, preferred_element_type=F32)
            + jnp.dot(a_lo, w_hi, preferred_element_type=F32)
            + jnp.dot(a_hi, w_lo, preferred_element_type=F32))


def _cumsum_lanes(triu_bf16, x):
    hi = x.astype(BF16)
    r1 = x - hi.astype(F32)
    mid = r1.astype(BF16)
    lo = (r1 - mid.astype(F32)).astype(BF16)
    return (jnp.dot(hi, triu_bf16, preferred_element_type=F32)
            + jnp.dot(mid, triu_bf16, preferred_element_type=F32)
            + jnp.dot(lo, triu_bf16, preferred_element_type=F32))


def _rms(x, g):
    return x * lax.rsqrt(jnp.mean(x * x, axis=-1, keepdims=True) + EPS) * g


def _layer_norm(x, g, b=None):
    mu = jnp.mean(x, axis=-1, keepdims=True)
    xc = x - mu
    var = jnp.mean(xc * xc, axis=-1, keepdims=True)
    y = xc * lax.rsqrt(var + EPS) * g
    return y if b is None else y + b


def _store_rows(ref, x, row0=0):
    r = x.shape[0]
    for k in range(ROW_SUB):
        ref[pl.ds(row0 * ROW_SUB + k, r, stride=ROW_SUB), :] = x[:, k * LANES:(k + 1) * LANES]


def _load_rows(ref, r, row0=0):
    return jnp.concatenate([ref[pl.ds(row0 * ROW_SUB + k, r, stride=ROW_SUB), :]
                            for k in range(ROW_SUB)], axis=1)


def _route(logits):
    lane = lax.broadcasted_iota(I32, logits.shape, 1).astype(F32)
    neg = jnp.float32(-jnp.inf)
    big = jnp.float32(1e9)
    is_g = lane < N_GROUPS
    gl = jnp.where(is_g, logits, neg)
    gmax = jnp.max(gl, axis=1, keepdims=True)
    gsel = jnp.min(jnp.where(gl == gmax, lane, big), axis=1, keepdims=True)
    pg = 1.0 / jnp.sum(jnp.where(is_g, jnp.exp(gl - gmax), 0.0), axis=1, keepdims=True)
    lo = N_GROUPS + EXP_PER_GROUP * gsel
    emask = (lane >= lo) & (lane < lo + EXP_PER_GROUP)
    el = jnp.where(emask, logits, neg)
    v1 = jnp.max(el, axis=1, keepdims=True)
    i1 = jnp.min(jnp.where(el == v1, lane, big), axis=1, keepdims=True)
    el2 = jnp.where(lane == i1, neg, el)
    v2 = jnp.max(el2, axis=1, keepdims=True)
    i2 = jnp.min(jnp.where(el2 == v2, lane, big), axis=1, keepdims=True)
    d = jnp.exp(v2 - v1)
    w1 = pg / (1.0 + d)
    w2 = pg * d / (1.0 + d)
    return jnp.where(lane == 0, i1 - N_GROUPS,
                     jnp.where(lane == 1, i2 - N_GROUPS,
                               jnp.where(lane == 2, w1, jnp.where(lane == 3, w2, 0.0))))


def _ada_kernel(cp_ref, cs_ref, w_ref, b_ref, op_ref, os_ref):
    w_hi, w_lo = _split2(w_ref[...])
    op_ref[...] = _dot3(_silu(cp_ref[...]), w_hi, w_lo) + b_ref[...]
    os_ref[...] = _dot3(_silu(cs_ref[...]), w_hi, w_lo) + b_ref[...]


def _ada(c_p, c_s, w_ada, b_ada):
    (rp, d), rs = c_p.shape, c_s.shape[0]
    n_out = w_ada.shape[1]
    blk = 1024
    return pl.pallas_call(
        _ada_kernel,
        grid=(n_out // blk,),
        in_specs=[pl.BlockSpec((rp, d), lambda j: (0, 0)),
                  pl.BlockSpec((rs, d), lambda j: (0, 0)),
                  pl.BlockSpec((d, blk), lambda j: (0, j)),
                  pl.BlockSpec((1, blk), lambda j: (0, j))],
        out_specs=[pl.BlockSpec((rp, blk), lambda j: (0, j)),
                   pl.BlockSpec((rs, blk), lambda j: (0, j))],
        out_shape=[jax.ShapeDtypeStruct((rp, n_out), F32), jax.ShapeDtypeStruct((rs, n_out), F32)],
        compiler_params=pltpu.CompilerParams(dimension_semantics=("arbitrary",),
                                             vmem_limit_bytes=VMEM_LIMIT),
        name="ada",
    )(c_p, c_s, w_ada, b_ada.reshape(1, n_out))


def _post(x, attn_cat, mod, g2, wout, wr_hi, wr_lo, rbias, d, wout_lo=None):
    gate1 = mod[:, 2 * d:3 * d]
    sh2 = mod[:, 3 * d:4 * d]
    sc2 = mod[:, 4 * d:5 * d]
    if wout_lo is None:
        proj = jnp.dot(attn_cat.astype(BF16), wout, preferred_element_type=F32)
    else:
        proj = _dot3(attn_cat, wout, wout_lo)
    x1 = x + gate1 * proj
    h2 = _rms(x1, g2) * (1.0 + sc2) + sh2
    logits = _dot3(h2, wr_hi, wr_lo) + rbias
    return x1, h2, _route(logits)


def _mix_prompt_kernel(x_ref, mod_ref, g1_ref, g2_ref, wmain_ref, wgh_ref, wgl_ref, gbias_ref,
                       mhg_ref, wdw_ref, bdw_ref, clg_ref, clb_ref, wout_ref, wrh_ref, wrl_ref,
                       rbias_ref,
                       x1_ref, h2_ref, route_ref, c_ref, n_ref, m_ref, cv_ref, base_ref, tot_ref,
                       ubuf, yc_s, q_s, k_s, v_s, so_s, hm_s, p_s, u_s, cm_s, nb_s, m_s, cnt_s,
                       *, tt, d, dm, dk):
    t = pl.program_id(1)
    dc = d - dm
    n_lt = dc // LANES
    sub = min(MIX_SUB, tt)

    @pl.when(t == 0)
    def _():
        c_ref[...] = jnp.zeros_like(c_ref)
        nb_s[...] = jnp.zeros_like(nb_s)
        m_s[...] = jnp.zeros_like(m_s)
        ubuf[0:CONV_PAD, :] = jnp.zeros((CONV_PAD, dc), F32)

    mod = mod_ref[0]
    row8 = lax.broadcasted_iota(I32, (8, LANES), 0)
    row = lax.broadcasted_iota(I32, (CHUNK, CHUNK), 0)
    col = lax.broadcasted_iota(I32, (CHUNK, CHUNK), 1)
    causal = col <= row
    triu = (row <= col).astype(BF16)
    neg = jnp.float32(-jnp.inf)
    ones_b = jnp.ones((CHUNK, dk), BF16)
    pad_rows = jnp.zeros((CHUNK - 8, LANES), F32)

    lanes_of = [slice(lt * LANES, (lt + 1) * LANES) for lt in range(n_lt)]
    wrows = [[jnp.broadcast_to(wdw_ref[j:j + 1, ls], (8, LANES)) for j in range(CONV_W)]
             for ls in lanes_of]
    bias = [jnp.broadcast_to(bdw_ref[:, ls], (8, LANES)) for ls in lanes_of]

    def partial_sums(r0, lt):
        blocks = [ubuf[r0 + 8 * a:r0 + 8 * a + 8, lanes_of[lt]] for a in range(CONV_PAD // 8)]
        sums = []
        for s in range(8):
            acc = None
            for a in range(CONV_PAD // 8):
                j = 8 * a + s - CONV_OFF
                if 0 <= j < CONV_W:
                    term = blocks[a] * wrows[lt][j]
                    acc = term if acc is None else acc + term
            sums.append(acc)
        return tuple(sums)

    q_prev = [None] * n_lt
    ca = [jnp.concatenate([c_ref[0, 0, hd], nb_s[hd]], axis=1) for hd in range(N_HEADS)]
    m_prev = [m_s[hd:hd + 1, :] for hd in range(N_HEADS)]

    n_sub = tt // sub
    gates_of = {}
    hc_of = {}
    tile_counts = []

    def rows_of(sb):
        return slice(sb * sub, (sb + 1) * sub)

    def proj_items(sb):
        rs = rows_of(sb)
        st = {}

        def head():
            x = x_ref[0, rs, :]
            h = _rms(x, g1_ref[...]) * (1.0 + mod[:, d:2 * d]) + mod[:, 0:d]
            st["hb"] = h.astype(BF16)
            gates_of[sb] = _dot3(h, wgh_ref[...], wgl_ref[...]) + gbias_ref[...]

        def proj(lo, hi):
            return jnp.dot(st["hb"], wmain_ref[:, lo:hi], preferred_element_type=F32)

        def glu():
            ubuf[CONV_PAD + sb * sub:CONV_PAD + (sb + 1) * sub, :] = (
                proj(4 * dm, 4 * dm + dc) * _sigmoid(proj(4 * dm + dc, 4 * dm + 2 * dc)))

        def q():
            q_s[rs, :] = proj(0, dm).astype(BF16)

        def k():
            k_s[rs, :] = proj(dm, 2 * dm) * (dk ** -0.5)

        def v():
            v_s[rs, :] = proj(2 * dm, 3 * dm).astype(BF16)

        def o():
            so_s[rs, :] = _sigmoid(proj(3 * dm, 4 * dm))

        return [head, glu, q, k, v, o]

    def conv_items(sb):
        items = []
        for lt in range(n_lt):
            for i in range(sb * sub // 8 + 1, (sb + 1) * sub // 8 + 1):
                def block(lt=lt, i=i):
                    if i == 1:
                        q_prev[lt] = partial_sums(0, lt)
                    q_cur = partial_sums(i * 8, lt)
                    cur = ubuf[(i - 1) * 8 + CONV_PAD:i * 8 + CONV_PAD, lanes_of[lt]]
                    y = bias[lt] + q_prev[lt][0] + cur * wrows[lt][CONV_W - 1]
                    for s in range(1, 8):
                        merged = jnp.where(row8 < s, q_cur[s], q_prev[lt][s])
                        y = y + pltpu.roll(merged, 8 - s, 0)
                    yc_s[(i - 1) * 8:i * 8, lanes_of[lt]] = y
                    q_prev[lt] = q_cur
                items.append(block)
        return items

    def post_items(sb):
        def post():
            rs = rows_of(sb)
            cat = jnp.concatenate([hm_s[rs, :], hc_of[sb]], axis=1)
            x1, h2, route = _post(x_ref[0, rs, :], cat, mod, g2_ref[...], wout_ref[...],
                                  wrh_ref[...], wrl_ref[...], rbias_ref[...], d)
            x1_ref[0, rs, :] = x1
            _store_rows(h2_ref, h2, sb * sub)
            route_ref[rs, :] = route
            oh0, oh1 = _expert_onehots(route)
            tile_counts.append(jnp.sum(oh0 + oh1, axis=0, keepdims=True))
        return [post]

    def interleave(main, side):
        gap = len(main) / (len(side) + 1)
        due, done = gap, 0
        for n, item in enumerate(main):
            item()
            while done < len(side) and n + 1 >= due:
                side[done]()
                done += 1
                due += gap
        for item in side[done:]:
            item()

    for item in proj_items(0):
        item()
    for sb in range(n_sub):
        r_lo = sb * sub
        rs = rows_of(sb)
        side = post_items(sb - 1) if sb > 0 else []
        if sb + 1 < n_sub:
            side = side + proj_items(sb + 1)
        interleave(conv_items(sb), side)
        hc_of[sb] = _silu(_layer_norm(yc_s[rs, :], clg_ref[...], clb_ref[...]))
        gates = gates_of[sb]

        chunks = range(r_lo // CHUNK, (r_lo + sub) // CHUNK)
        b_cols = {}
        for c in chunks:
            r0 = c * CHUNK
            g8 = gates[r0 - r_lo:r0 - r_lo + CHUNK, :].T[0:8, :]
            b8 = _cumsum_lanes(triu, _log_sigmoid(g8))
            pk8 = jnp.where(row8 < N_HEADS, g8 - pltpu.roll(b8, N_HEADS, 0), b8)
            pk = jnp.concatenate([pk8, pad_rows], axis=0).T
            for hd in range(N_HEADS):
                cs = slice(hd * dk, (hd + 1) * dk)
                idx = c * N_HEADS + hd
                kf = k_s[r0:r0 + CHUNK, cs]
                va = jnp.concatenate([v_s[r0:r0 + CHUNK, cs], ones_b], axis=1)
                gm = jnp.where(causal, pk8[hd:hd + 1, :], neg)
                cm = jnp.max(gm, axis=1, keepdims=True)
                s = _bdot_nt(q_s[r0:r0 + CHUNK, cs], kf) * jnp.exp(gm - cm)
                p_s[idx] = jnp.dot(s.astype(BF16), va, preferred_element_type=F32)
                cm_s[idx] = jnp.broadcast_to(cm, (CHUNK, LANES))
                kw = kf * jnp.exp(pk[:, hd:hd + 1] - cm[CHUNK - 1:CHUNK, :])
                u_s[idx] = jnp.dot(kw.T.astype(BF16), va, preferred_element_type=F32)
                b_cols[idx] = pk[:, N_HEADS + hd:N_HEADS + hd + 1]

        for hd in range(N_HEADS):
            cs = slice(hd * dk, (hd + 1) * dk)
            for c in chunks:
                r0 = c * CHUNK
                idx = c * N_HEADS + hd
                cm = cm_s[idx]
                b_col = b_cols[idx]
                mt = jnp.maximum(m_prev[hd], cm)
                f_loc = jnp.exp(cm - mt)
                a_int = jnp.exp(m_prev[hd] - mt)
                qc = jnp.dot(q_s[r0:r0 + CHUNK, cs], ca[hd].astype(BF16),
                             preferred_element_type=F32)
                p = p_s[idx]
                num = f_loc * p[:, :dk] + a_int * qc[:, :dk]
                den = f_loc * p[:, dk:] + a_int * qc[:, dk:]
                hh = num / jnp.maximum(jnp.abs(den), jnp.exp(-(b_col + mt)))
                hm_s[r0:r0 + CHUNK, cs] = (_layer_norm(hh, mhg_ref[hd:hd + 1, :])
                                           * so_s[r0:r0 + CHUNK, cs])
                mt_l = mt[CHUNK - 1:CHUNK, :]
                u = u_s[idx]
                f_l = f_loc[CHUNK - 1:CHUNK, :]
                a_l = a_int[CHUNK - 1:CHUNK, :]
                ca[hd] = jnp.concatenate([a_l * ca[hd][:, :dk] + f_l * u[:, :dk],
                                          a_l * ca[hd][:, dk:] + f_l * u[:, dk:]], axis=1)
                m_prev[hd] = b_col[CHUNK - 1:CHUNK, :] + mt_l

    for item in post_items(n_sub - 1):
        item()

    for hd in range(N_HEADS):
        c_ref[0, 0, hd] = ca[hd][:, :dk]
        nb_s[hd] = ca[hd][:, dk:]
        m_s[hd:hd + 1, :] = m_prev[hd]
    cv_ref[0, 0] = ubuf[tt + CONV_PAD - (CONV_W - 1):tt + CONV_PAD, :]
    ubuf[0:CONV_PAD, :] = ubuf[tt:tt + CONV_PAD, :]

    @pl.when(t == pl.num_programs(1) - 1)
    def _():
        for hd in range(N_HEADS):
            n_ref[0, 0, hd:hd + 1, :] = nb_s[hd].T[0:1, :]

    lane1 = lax.broadcasted_iota(I32, (1, LANES), 1)
    m_row = jnp.zeros((1, LANES), F32)
    for hd in range(N_HEADS):
        m_row = jnp.where(lane1 == hd, m_s[hd:hd + 1, :], m_row)
    m_ref[0] = m_row

    @pl.when((pl.program_id(0) == 0) & (t == 0))
    def _():
        cnt_s[...] = jnp.zeros_like(cnt_s)

    base_ref[0] = cnt_s[...]
    cnt_s[...] = cnt_s[...] + sum(tile_counts)
    tot_ref[...] = cnt_s[...]


def _const_spec(shape):
    nd = len(shape)
    return pl.BlockSpec(shape, lambda *_: (0,) * nd)


def _mix_prompt(x, mod, wts):
    b, t, d = x.shape
    dm = wts["dm"]
    dk = dm // N_HEADS
    dc = d - dm
    tt = min(MIX_TILE, t)
    assert t % tt == 0 and tt % CHUNK == 0 and tt >= CONV_PAD
    nt = t // tt
    kern = functools.partial(_mix_prompt_kernel, tt=tt, d=d, dm=dm, dk=dk)
    const_names = ["g1", "g2", "wmain", "wg_hi", "wg_lo", "gbias", "mhg", "wdw", "bdw", "clg",
                   "clb", "wout", "wr_hi", "wr_lo", "rbias"]
    consts = [wts[k] for k in const_names]
    in_specs = ([pl.BlockSpec((1, tt, d), lambda i, j: (i, j, 0)),
                 pl.BlockSpec((1, 1, mod.shape[-1]), lambda i, j: (i, 0, 0))]
                + [_const_spec(c.shape) for c in consts])
    out_shape = [
        jax.ShapeDtypeStruct((b, t, d), F32),
        jax.ShapeDtypeStruct((b * t * ROW_SUB, LANES), F32),
        jax.ShapeDtypeStruct((b * t, LANES), F32),
        jax.ShapeDtypeStruct((1, b, N_HEADS, dk, dk), F32),
        jax.ShapeDtypeStruct((1, b, N_HEADS, dk), F32),
        jax.ShapeDtypeStruct((b, 1, LANES), F32),
        jax.ShapeDtypeStruct((1, b, CONV_W - 1, dc), F32),
        jax.ShapeDtypeStruct((b * nt, 8, LANES), F32),
        jax.ShapeDtypeStruct((8, LANES), F32),
    ]
    out_specs = [
        pl.BlockSpec((1, tt, d), lambda i, j: (i, j, 0)),
        pl.BlockSpec((tt * ROW_SUB, LANES), lambda i, j: (i * nt + j, 0)),
        pl.BlockSpec((tt, LANES), lambda i, j: (i * nt + j, 0)),
        pl.BlockSpec((1, 1, N_HEADS, dk, dk), lambda i, j: (0, i, 0, 0, 0)),
        pl.BlockSpec((1, 1, N_HEADS, dk), lambda i, j: (0, i, 0, 0)),
        pl.BlockSpec((1, 1, LANES), lambda i, j: (i, 0, 0)),
        pl.BlockSpec((1, 1, CONV_W - 1, dc), lambda i, j: (0, i, 0, 0)),
        pl.BlockSpec((1, 8, LANES), lambda i, j: (i * nt + j, 0, 0)),
        pl.BlockSpec((8, LANES), lambda i, j: (0, 0)),
    ]
    n_hc = (tt // CHUNK) * N_HEADS
    scratch = [pltpu.VMEM((tt + CONV_PAD, dc), F32),
               pltpu.VMEM((tt, dc), F32),
               pltpu.VMEM((tt, dm), BF16),
               pltpu.VMEM((tt, dm), F32),
               pltpu.VMEM((tt, dm), BF16),
               pltpu.VMEM((tt, dm), F32),
               pltpu.VMEM((tt, dm), F32),
               pltpu.VMEM((n_hc, CHUNK, 2 * dk), F32),
               pltpu.VMEM((n_hc, CHUNK, 2 * dk), F32),
               pltpu.VMEM((n_hc, CHUNK, LANES), F32),
               pltpu.VMEM((N_HEADS, dk, LANES), F32),
               pltpu.VMEM((8, LANES), F32),
               pltpu.VMEM((8, LANES), F32)]
    return pl.pallas_call(
        kern, grid=(b, nt), in_specs=in_specs, out_specs=out_specs, out_shape=out_shape,
        scratch_shapes=scratch,
        compiler_params=pltpu.CompilerParams(dimension_semantics=("arbitrary", "arbitrary"),
                                             vmem_limit_bytes=VMEM_LIMIT),
        name="mix_p",
    )(x, mod.reshape(b, 1, -1), *consts)


def _s_pre_kernel(x_ref, mod_ref, g1_ref, wmain_ref, wmainlo_ref, wgh_ref, wgl_ref, gbias_ref,
                  wkt_ref, wktlo_ref, wdw_ref, bdw_ref, clg_ref, clb_ref, cache_ref, n0_ref, m0_ref,
                  q_ref, kt_ref, vs_ref, ab_ref, sv_ref, den_ref, eb_ref, o_ref, hc_ref, u_ref,
                  n_ref, m_ref, *, d, dm, dk):
    dc = d - dm
    x = x_ref[...]
    mod = mod_ref[...]
    sh1 = mod[:, 0:d]
    sc1 = mod[:, d:2 * d]
    h = _rms(x, g1_ref[...]) * (1.0 + sc1) + sh1
    z = _dot3(h, wmain_ref[...], wmainlo_ref[...])
    gates = _dot3(h, wgh_ref[...], wgl_ref[...]) + gbias_ref[...]
    scale = dk ** -0.5
    h_hi, h_lo = _split2(h)
    kt = _bdot_nt(wkt_ref[...], h_hi) + _bdot_nt(wktlo_ref[...], h_hi) + _bdot_nt(wkt_ref[...], h_lo)
    kt_ref[...] = (kt * scale).astype(BF16)
    k_all = z[:, dm:2 * dm] * scale
    ga = z[:, 4 * dm:4 * dm + dc]
    gb = z[:, 4 * dm + dc:4 * dm + 2 * dc]
    u = ga * _sigmoid(gb)
    u_ref[...] = u
    acc = jnp.broadcast_to(bdw_ref[...], u.shape) + u * wdw_ref[CONV_W - 1:CONV_W, :]
    for j in range(CONV_W - 1):
        acc = acc + cache_ref[j] * wdw_ref[j:j + 1, :]
    hc_ref[...] = _silu(_layer_norm(acc, clg_ref[...], clb_ref[...]))
    o_ref[...] = z[:, 3 * dm:4 * dm]
    q_ref[...] = z[:, 0:dm]
    m0 = m0_ref[...]
    n0 = n0_ref[...]
    lane1 = lax.broadcasted_iota(I32, (1, LANES), 1)
    m_new = jnp.zeros(m0.shape, F32)
    for hd in range(N_HEADS):
        cs = slice(hd * dk, (hd + 1) * dk)
        ig = gates[:, hd:hd + 1]
        lf = _log_sigmoid(gates[:, N_HEADS + hd:N_HEADS + hd + 1])
        mp = m0[:, hd:hd + 1]
        inter = lf + mp
        mt = jnp.maximum(inter, ig)
        w = jnp.exp(ig - mt)
        a_int = jnp.exp(inter - mt)
        qf = z[:, cs]
        kf = k_all[:, cs]
        vf = z[:, 2 * dm + hd * dk:2 * dm + (hd + 1) * dk]
        s = jnp.sum(qf * kf, axis=1, keepdims=True) * w
        sv_ref[:, cs] = s * vf
        den_ref[:, cs] = jnp.broadcast_to(
            s + a_int * jnp.sum(qf * n0[:, cs], axis=1, keepdims=True), (x.shape[0], dk))
        eb_ref[:, cs] = jnp.broadcast_to(jnp.exp(-mt), (x.shape[0], dk))
        ab_ref[:, cs] = jnp.broadcast_to(a_int, (x.shape[0], dk))
        vs_ref[:, cs] = (vf * w).astype(BF16)
        n_ref[:, cs] = a_int * n0[:, cs] + w * kf
        m_new = jnp.where(lane1 == hd, mt, m_new)
    m_ref[...] = m_new


def _s_state_kernel(q_ref, kt_ref, vs_ref, ab_ref, c0_ref, c_ref, r_ref, *, bb, dk):
    i = pl.program_id(0)
    nb = q_ref.shape[0]
    rows = lax.broadcasted_iota(I32, (nb, dk), 0)

    @pl.when(i == 0)
    def _():
        r_ref[...] = jnp.zeros_like(r_ref)

    a_blk = ab_ref[pl.ds(pl.multiple_of(i * bb, bb), bb), :]
    for j in range(bb):
        sel = rows == i * bb + j
        for hd in range(N_HEADS):
            cs = slice(hd * dk, (hd + 1) * dk)
            c0 = c0_ref[j, hd]
            vmask = jnp.where(sel, vs_ref[:, cs], jnp.zeros((), BF16))
            c_ref[j, hd] = (a_blk[j:j + 1, cs] * c0
                            + jnp.dot(kt_ref[cs, :], vmask, preferred_element_type=F32))
            c_hi, c_lo = _split2(c0)
            q_hi, q_lo = _split2(q_ref[:, cs])
            r = (jnp.dot(q_hi, c_hi, preferred_element_type=F32)
                 + jnp.dot(q_lo, c_hi, preferred_element_type=F32)
                 + jnp.dot(q_hi, c_lo, preferred_element_type=F32))
            r_ref[:, cs] = r_ref[:, cs] + jnp.where(sel, r, 0.0)


def _s_post_kernel(x_ref, mod_ref, g2_ref, mhg_ref, r_ref, ab_ref, sv_ref, den_ref, eb_ref, o_ref,
                   hc_ref, wout_ref, woutlo_ref, wrh_ref, wrl_ref, rbias_ref,
                   x1_ref, h2_ref, route_ref, *, d, dm, dk):
    hm = []
    for hd in range(N_HEADS):
        cs = slice(hd * dk, (hd + 1) * dk)
        num = sv_ref[:, cs] + ab_ref[:, cs] * r_ref[:, cs]
        hh = num / jnp.maximum(jnp.abs(den_ref[:, cs]), eb_ref[:, cs])
        hm.append(_layer_norm(hh, mhg_ref[hd:hd + 1, :]) * _sigmoid(o_ref[:, cs]))
    cat = jnp.concatenate(hm + [hc_ref[...]], axis=1)
    x1, h2, route = _post(x_ref[...], cat, mod_ref[...], g2_ref[...], wout_ref[...], wrh_ref[...],
                          wrl_ref[...], rbias_ref[...], d, wout_lo=woutlo_ref[...])
    x1_ref[...] = x1
    _store_rows(h2_ref, h2)
    route_ref[...] = route


def _mix_sample(x, mod, c0, n0, m0, cache, wts):
    nb, d = x.shape
    dm = wts["dm"]
    dk = dm // N_HEADS
    dc = d - dm
    cp = pltpu.CompilerParams(dimension_semantics=("arbitrary",), vmem_limit_bytes=VMEM_LIMIT)
    cache_t = jnp.transpose(cache, (1, 0, 2))
    m0p = jnp.pad(m0, ((0, 0), (0, LANES - N_HEADS)))
    pre_in = [x, mod, wts["g1"], wts["wmain"], wts["wmain_lo"], wts["wg_hi"], wts["wg_lo"],
              wts["gbias"], wts["wk_t"], wts["wk_t_lo"], wts["wdw"], wts["bdw"], wts["clg"],
              wts["clb"], cache_t, n0.reshape(nb, dm), m0p]
    pre_out = [jax.ShapeDtypeStruct((nb, dm), F32),
               jax.ShapeDtypeStruct((dm, nb), BF16),
               jax.ShapeDtypeStruct((nb, dm), BF16),
               jax.ShapeDtypeStruct((nb, dm), F32),
               jax.ShapeDtypeStruct((nb, dm), F32),
               jax.ShapeDtypeStruct((nb, dm), F32),
               jax.ShapeDtypeStruct((nb, dm), F32),
               jax.ShapeDtypeStruct((nb, dm), F32),
               jax.ShapeDtypeStruct((nb, dc), F32),
               jax.ShapeDtypeStruct((nb, dc), F32),
               jax.ShapeDtypeStruct((nb, dm), F32),
               jax.ShapeDtypeStruct((nb, LANES), F32)]
    (q, kt, vs, ab, sv, den, eb, o, hc, u, n1, m1) = pl.pallas_call(
        functools.partial(_s_pre_kernel, d=d, dm=dm, dk=dk),
        grid=(1,),
        in_specs=[_const_spec(a.shape) for a in pre_in],
        out_specs=[_const_spec(s.shape) for s in pre_out],
        out_shape=pre_out, compiler_params=cp, name="s_pre")(*pre_in)

    bb = 8
    assert nb % bb == 0
    c1, r = pl.pallas_call(
        functools.partial(_s_state_kernel, bb=bb, dk=dk),
        grid=(nb // bb,),
        in_specs=[_const_spec(q.shape), _const_spec(kt.shape), _const_spec(vs.shape),
                  _const_spec(ab.shape),
                  pl.BlockSpec((bb, N_HEADS, dk, dk), lambda i: (i, 0, 0, 0))],
        out_specs=[pl.BlockSpec((bb, N_HEADS, dk, dk), lambda i: (i, 0, 0, 0)),
                   _const_spec((nb, dm))],
        out_shape=[jax.ShapeDtypeStruct((nb, N_HEADS, dk, dk), F32),
                   jax.ShapeDtypeStruct((nb, dm), F32)],
        compiler_params=cp, name="s_state")(q, kt, vs, ab, c0)

    post_in = [x, mod, wts["g2"], wts["mhg"], r, ab, sv, den, eb, o, hc, wts["wout"],
               wts["wout_lo"], wts["wr_hi"], wts["wr_lo"], wts["rbias"]]
    post_out = [jax.ShapeDtypeStruct((nb, d), F32),
                jax.ShapeDtypeStruct((nb * ROW_SUB, LANES), F32),
                jax.ShapeDtypeStruct((nb, LANES), F32)]
    x1, h2, route = pl.pallas_call(
        functools.partial(_s_post_kernel, d=d, dm=dm, dk=dk),
        grid=(1,),
        in_specs=[_const_spec(a.shape) for a in post_in],
        out_specs=[_const_spec(s.shape) for s in post_out],
        out_shape=post_out, compiler_params=cp, name="s_post")(*post_in)
    return x1, h2, route, c1, n1, m1, u


def _scatter_kernel(pos_ref, zoff_ref, src_ref, *rest, tp, tm, create):
    xs_ref, zbuf, sem, zsem = rest[-4:]
    i = pl.program_id(0)

    if create:
        @pl.when(i == 0)
        def _():
            zbuf[...] = jnp.zeros_like(zbuf)

            def zero_copy(e):
                start = pl.multiple_of(zoff_ref[e] * ROW_SUB, tm * ROW_SUB)
                return pltpu.make_async_copy(zbuf, xs_ref.at[pl.ds(start, tm * ROW_SUB)], zsem)

            for e in range(zoff_ref.shape[0]):
                @pl.when(zoff_ref[e] >= 0)
                def _():
                    zero_copy(e).start()
            for e in range(zoff_ref.shape[0]):
                @pl.when(zoff_ref[e] >= 0)
                def _():
                    zero_copy(e).wait()

    for r in range(tp):
        src = src_ref.at[pl.ds(r * ROW_SUB, ROW_SUB)]
        for slot in range(2):
            dst = pl.multiple_of(pos_ref[0, slot, r], ROW_SUB)
            pltpu.make_async_copy(src, xs_ref.at[pl.ds(dst, ROW_SUB)], sem).start(priority=slot)
    for slot in range(2):
        pltpu.make_async_copy(src_ref, xs_ref.at[pl.ds(0, tp * ROW_SUB)], sem).wait()


def _scatter_rows(h2, pos3, zoff, xs_or_rows, tm):
    n, c = h2.shape[0] // ROW_SUB, LANES
    tp = pos3.shape[2]
    assert pos3.shape == (n // tp, 2, tp)
    create = isinstance(xs_or_rows, int)
    n_sorted = xs_or_rows * ROW_SUB if create else xs_or_rows.shape[0]
    in_specs = [pl.BlockSpec((1, 2, tp), lambda i: (i, 0, 0), memory_space=pltpu.SMEM),
                pl.BlockSpec(memory_space=pltpu.SMEM),
                pl.BlockSpec((tp * ROW_SUB, c), lambda i: (i, 0))]
    args = [pos3, zoff, h2]
    if not create:
        in_specs.append(pl.BlockSpec(memory_space=pl.ANY))
        args.append(xs_or_rows)
    return pl.pallas_call(
        functools.partial(_scatter_kernel, tp=tp, tm=tm, create=create),
        grid_spec=pltpu.PrefetchScalarGridSpec(
            num_scalar_prefetch=0,
            grid=(n // tp,),
            in_specs=in_specs,
            out_specs=pl.BlockSpec(memory_space=pl.ANY),
            scratch_shapes=[pltpu.VMEM((tm * ROW_SUB, c), F32), pltpu.SemaphoreType.DMA(()),
                            pltpu.SemaphoreType.DMA(())]),
        out_shape=jax.ShapeDtypeStruct((n_sorted, c), F32),
        input_output_aliases={} if create else {3: 0},
        compiler_params=pltpu.CompilerParams(dimension_semantics=("arbitrary",),
                                             vmem_limit_bytes=VMEM_LIMIT),
        name="scatter",
    )(*args)


def _moe_kernel(te_ref, nu_ref, xs_ref, wg_ref, wu_ref, wd_ref, ys_ref, wg_b, wu_b, wd_b):
    i = pl.program_id(0)
    used = i < nu_ref[0]

    @pl.when(used & ((i == 0) | (te_ref[i] != te_ref[jnp.maximum(i - 1, 0)])))
    def _():
        wg_b[...] = wg_ref[0].astype(BF16)
        wu_b[...] = wu_ref[0].astype(BF16)
        wd_b[...] = wd_ref[0].astype(BF16)

    @pl.when(used)
    def _():
        xb = _load_rows(xs_ref, xs_ref.shape[0] // ROW_SUB).astype(BF16)
        g = jnp.dot(xb, wg_b[...], preferred_element_type=F32)
        u = jnp.dot(xb, wu_b[...], preferred_element_type=F32)
        hid = (_silu(g) * u).astype(BF16)
        _store_rows(ys_ref, jnp.dot(hid, wd_b[...], preferred_element_type=F32))

    @pl.when(jnp.logical_not(used))
    def _():
        ys_ref[...] = jnp.zeros_like(ys_ref)


def _moe(xs, tile_expert, n_used, w_gate, w_up, w_down, tm):
    p, c = xs.shape[0] // ROW_SUB, LANES
    ne, d, de = w_gate.shape
    n_tiles = p // tm

    def x_map(i, te, nu):
        return (jnp.minimum(i, jnp.maximum(nu[0] - 1, 0)), 0)

    def w_map(i, te, nu):
        return (te[i], 0, 0)

    return pl.pallas_call(
        _moe_kernel,
        grid_spec=pltpu.PrefetchScalarGridSpec(
            num_scalar_prefetch=2,
            grid=(n_tiles,),
            in_specs=[pl.BlockSpec((tm * ROW_SUB, c), x_map),
                      pl.BlockSpec((1, d, de), w_map),
                      pl.BlockSpec((1, d, de), w_map),
                      pl.BlockSpec((1, de, d), w_map)],
            out_specs=pl.BlockSpec((tm * ROW_SUB, c), lambda i, te, nu: (i, 0)),
            scratch_shapes=[pltpu.VMEM((d, de), BF16), pltpu.VMEM((d, de), BF16),
                            pltpu.VMEM((de, d), BF16)]),
        out_shape=jax.ShapeDtypeStruct((p * ROW_SUB, c), F32),
        compiler_params=pltpu.CompilerParams(dimension_semantics=("arbitrary",),
                                             vmem_limit_bytes=VMEM_LIMIT),
        name="moe",
    )(tile_expert, n_used, xs, w_gate, w_up, w_down)


def _fin_kernel(pos_ref, posn_ref, x1_ref, route_ref, mod_ref, fg_ref, ys_ref, y_ref, ybuf, sem,
                *, tp, d, n_steps):
    i = pl.program_id(0)
    cur = i % 2

    def issue(p_ref, buf):
        def row_start(r, carry):
            dst = pl.ds(pl.multiple_of(r * ROW_SUB, ROW_SUB), ROW_SUB)
            for slot in range(2):
                src = pl.multiple_of(p_ref[0, slot, r], ROW_SUB)
                pltpu.make_async_copy(ys_ref.at[pl.ds(src, ROW_SUB)],
                                      ybuf.at[buf, slot, dst], sem.at[buf]).start(priority=slot)
            return carry
        lax.fori_loop(0, tp, row_start, 0, unroll=8)

    def wait_buf(buf):
        for slot in range(2):
            pltpu.make_async_copy(ys_ref.at[pl.ds(0, tp * ROW_SUB)], ybuf.at[buf, slot],
                                  sem.at[buf]).wait()

    @pl.when(i == 0)
    def _():
        issue(pos_ref, 0)

    wait_buf(cur)
    nxt = 1 - cur
    per_token_mod = mod_ref.shape[1] != 1
    for lo in range(0, tp, FIN_CHUNK):
        hi = min(lo + FIN_CHUNK, tp)
        for r in range(lo, hi):
            for slot in range(2):
                src = pl.multiple_of(posn_ref[0, slot, r], ROW_SUB)
                pltpu.make_async_copy(
                    ys_ref.at[pl.ds(src, ROW_SUB)],
                    ybuf.at[nxt, slot, pl.ds(r * ROW_SUB, ROW_SUB)], sem.at[nxt]).start(priority=slot)
        route = route_ref[lo:hi, :]
        moe = (route[:, 2:3] * _load_rows(ybuf.at[cur, 0], hi - lo, lo)
               + route[:, 3:4] * _load_rows(ybuf.at[cur, 1], hi - lo, lo))
        mod = mod_ref[0, lo:hi, :] if per_token_mod else mod_ref[0]
        y_ref[0, lo:hi, :] = _rms(x1_ref[0, lo:hi, :] + mod[:, 5 * d:6 * d] * moe, fg_ref[...])

    @pl.when(i == n_steps - 1)
    def _():
        wait_buf(nxt)


def _finish(x1, route, mod, final_g, ys, pos3):
    b, t, d = x1.shape
    tp = pos3.shape[2]
    assert t % tp == 0
    nt = t // tp
    n_steps = b * nt
    assert pos3.shape[0] == n_steps
    blk0 = 0
    if mod.shape[1] == 1:
        mod_spec = pl.BlockSpec((1, 1, 6 * d), lambda i: (i // nt, 0, 0))
    else:
        mod_spec = pl.BlockSpec((1, tp, 6 * d), lambda i: (i // nt, i % nt, 0))
    return pl.pallas_call(
        functools.partial(_fin_kernel, tp=tp, d=d, n_steps=n_steps),
        grid_spec=pltpu.PrefetchScalarGridSpec(
            num_scalar_prefetch=0,
            grid=(n_steps,),
            in_specs=[pl.BlockSpec((1, 2, tp), lambda i: (blk0 + i, 0, 0), memory_space=pltpu.SMEM),
                      pl.BlockSpec((1, 2, tp), lambda i: (blk0 + jnp.minimum(i + 1, n_steps - 1), 0, 0),
                                   memory_space=pltpu.SMEM),
                      pl.BlockSpec((1, tp, d), lambda i: (i // nt, i % nt, 0)),
                      pl.BlockSpec((tp, LANES), lambda i: (i, 0)),
                      mod_spec,
                      _const_spec((1, d)),
                      pl.BlockSpec(memory_space=pl.ANY)],
            out_specs=pl.BlockSpec((1, tp, d), lambda i: (i // nt, i % nt, 0)),
            scratch_shapes=[pltpu.VMEM((2, 2, tp * ROW_SUB, LANES), F32),
                            pltpu.SemaphoreType.DMA((2,))]),
        out_shape=jax.ShapeDtypeStruct((b, t, d), F32),
        compiler_params=pltpu.CompilerParams(dimension_semantics=("arbitrary",),
                                             vmem_limit_bytes=VMEM_LIMIT),
        name="fin",
    )(pos3, pos3, x1, route, mod, final_g.reshape(1, d), ys)


def _expert_onehots(route):
    lane = lax.broadcasted_iota(I32, route.shape, 1).astype(F32)
    return ((lane == route[:, 0:1]).astype(F32), (lane == route[:, 1:2]).astype(F32))


def _count_kernel(route_ref, base_ref, tot_ref, acc):
    @pl.when(pl.program_id(0) == 0)
    def _():
        acc[...] = jnp.zeros_like(acc)

    oh0, oh1 = _expert_onehots(route_ref[...])
    base_ref[0] = acc[...]
    acc[...] = acc[...] + jnp.sum(oh0 + oh1, axis=0, keepdims=True)
    tot_ref[...] = acc[...]


def _pos_kernel(route_ref, base_ref, seg_ref, pos_ref, tril_s, *, tp):
    @pl.when(pl.program_id(0) == 0)
    def _():
        r_i = lax.broadcasted_iota(I32, (tp, tp), 0)
        c_i = lax.broadcasted_iota(I32, (tp, tp), 1)
        tril_s[...] = (c_i < r_i).astype(BF16)

    oh0, oh1 = _expert_onehots(route_ref[...])
    before = jnp.dot(tril_s[...], (oh0 + oh1).astype(BF16), preferred_element_type=F32)
    offs = seg_ref[0:1, :] + base_ref[0][0:1, :] + before
    ones8 = jnp.ones((8, LANES), BF16)

    def lane_sum_as_row(v):
        hi = v.astype(BF16)
        r1 = v - hi.astype(F32)
        mid = r1.astype(BF16)
        lo = (r1 - mid.astype(F32)).astype(BF16)
        return _bdot_nt(ones8, hi) + _bdot_nt(ones8, mid) + _bdot_nt(ones8, lo)

    p0 = lane_sum_as_row(oh0 * offs)
    p1 = lane_sum_as_row(oh1 * offs)
    row = lax.broadcasted_iota(I32, (8, tp), 0)
    pos_ref[0] = (jnp.where(row == 0, p0, jnp.where(row == 1, p1, 0.0)) * ROW_SUB).astype(I32)


def _count(route, tp):
    n = route.shape[0]
    assert n % tp == 0
    return pl.pallas_call(
        _count_kernel, grid=(n // tp,),
        in_specs=[pl.BlockSpec((tp, LANES), lambda i: (i, 0))],
        out_specs=[pl.BlockSpec((1, 8, LANES), lambda i: (i, 0, 0)), _const_spec((8, LANES))],
        out_shape=[jax.ShapeDtypeStruct((n // tp, 8, LANES), F32),
                   jax.ShapeDtypeStruct((8, LANES), F32)],
        scratch_shapes=[pltpu.VMEM((8, LANES), F32)],
        compiler_params=pltpu.CompilerParams(dimension_semantics=("arbitrary",),
                                             vmem_limit_bytes=VMEM_LIMIT),
        name="count",
    )(route)


def _positions(route, base, seg_start, tp):
    n = route.shape[0]
    seg = jnp.broadcast_to(jnp.pad(seg_start.astype(F32), (0, LANES - N_EXPERTS))[None, :], (8, LANES))
    pos = pl.pallas_call(
        functools.partial(_pos_kernel, tp=tp), grid=(n // tp,),
        in_specs=[pl.BlockSpec((tp, LANES), lambda i: (i, 0)),
                  pl.BlockSpec((1, 8, LANES), lambda i: (i, 0, 0)),
                  _const_spec((8, LANES))],
        out_specs=pl.BlockSpec((1, 8, tp), lambda i: (i, 0, 0)),
        out_shape=jax.ShapeDtypeStruct((n // tp, 8, tp), I32),
        scratch_shapes=[pltpu.VMEM((tp, tp), BF16)],
        compiler_params=pltpu.CompilerParams(dimension_semantics=("arbitrary",),
                                             vmem_limit_bytes=VMEM_LIMIT),
        name="positions",
    )(route, base, seg)
    return pos[:, 0:2, :]


def _plan(counts, tm, n_tiles):
    tiles_per = (counts + tm - 1) // tm
    tile_end = jnp.cumsum(tiles_per)
    seg_start = (tile_end - tiles_per) * tm
    n_used = tile_end[-1]
    tile_ids = jnp.arange(n_tiles, dtype=I32)
    tile_expert = jnp.sum((tile_ids[:, None] >= tile_end[None, :]).astype(I32), axis=1)
    last_used = jnp.sum((n_used - 1 >= tile_end).astype(I32))
    tile_expert = jnp.where(tile_ids < n_used, tile_expert, last_used).astype(I32)
    z_expert = jnp.where((counts % tm) != 0, (tile_end - 1) * tm, -1)
    spare = n_used + jnp.arange(N_EXPERTS, dtype=I32)
    z_spare = jnp.where(spare < n_tiles, spare * tm, -1)
    zoff = jnp.concatenate([z_expert, z_spare]).astype(I32)
    return seg_start, tile_expert, n_used.reshape(1).astype(I32), zoff


def kernel(x_prompt, x_sample, c_prompt, c_sample, state_mlstm_C, state_mlstm_n, state_mlstm_m,
           cache_conv, w_ada, b_ada, norm1_g, w_in, b_igate, b_fgate, mh_norm_g, w_dw, b_dw,
           conv_ln_g, conv_ln_b, w_out, norm2_g, w_grp_router, b_grp_router, w_exp_router,
           b_exp_router, w_gate, w_up, w_down, final_g):
    depth = w_ada.shape[0]
    assert depth == 1, "one layer per step"
    bp, tp_, d = x_prompt.shape
    bs = x_sample.shape[0]
    assert x_sample.shape[1] == 1 and d == ROW_SUB * LANES
    dk = state_mlstm_C.shape[-1]
    dm = N_HEADS * dk
    dc = d - dm
    l = 0

    win = w_in[l]
    w_qkvo = win[:, 0:4 * dm]
    w_gates = win[:, 4 * dm:4 * dm + 2 * N_HEADS]
    w_glu = win[:, 4 * dm + 2 * N_HEADS:]
    wmain, wmain_lo = _split_weights(jnp.concatenate([w_qkvo, w_glu], axis=1))
    wout_hi, wout_lo = _split_weights(w_out[l])
    wkt_hi, wkt_lo = _split_weights(win[:, dm:2 * dm].T)
    wg_pad = jnp.pad(w_gates, ((0, 0), (0, LANES - 2 * N_HEADS)))
    wg_hi, wg_lo = _split_weights(wg_pad)
    gbias = jnp.pad(jnp.concatenate([b_igate[l], b_fgate[l]]), (0, LANES - 2 * N_HEADS)).reshape(1, LANES)
    w_r = jnp.concatenate([w_grp_router[l], w_exp_router[l]], axis=1)
    n_r = N_GROUPS + N_EXPERTS
    wr_hi, wr_lo = _split_weights(jnp.pad(w_r, ((0, 0), (0, LANES - n_r))))
    rbias = jnp.pad(jnp.concatenate([b_grp_router[l], b_exp_router[l]]), (0, LANES - n_r)).reshape(1, LANES)
    wts = dict(dm=dm, g1=norm1_g[l].reshape(1, d), g2=norm2_g[l].reshape(1, d), wmain=wmain,
               wg_hi=wg_hi, wg_lo=wg_lo, gbias=gbias, mhg=mh_norm_g[l],
               wdw=w_dw[l].reshape(CONV_W, dc), bdw=b_dw[l].reshape(1, dc),
               clg=conv_ln_g[l].reshape(1, dc), clb=conv_ln_b[l].reshape(1, dc),
               wout=wout_hi, wr_hi=wr_hi, wr_lo=wr_lo, rbias=rbias, wk_t=wkt_hi,
               wmain_lo=wmain_lo, wout_lo=wout_lo, wk_t_lo=wkt_lo)

    mod_p, mod_s = _ada(c_prompt, c_sample, w_ada[l], b_ada[l])

    n_p = bp * tp_
    n_all = n_p + bs
    x1_p, h2_p, route_p, c_p, n_pr, m_p, cv_p, base_p, tot_p = _mix_prompt(x_prompt, mod_p, wts)
    x1_s, h2_s, route_s, c_s, n_s, m_s, u_s = _mix_sample(
        x_sample.reshape(bs, d), mod_s, state_mlstm_C[l], state_mlstm_n[l], state_mlstm_m[l],
        cache_conv[l], wts)

    tm = MOE_TILE
    n_tiles = (2 * n_all) // tm + N_EXPERTS
    tp_p, tp_s = min(PERM_TILE, tp_), min(PERM_TILE, bs)
    assert tp_p == min(MIX_TILE, tp_), "the prompt mixer counts experts per PERM_TILE tokens"
    base_s, tot_s = _count(route_s, tp_s)
    counts = (tot_p[0, :N_EXPERTS] + tot_s[0, :N_EXPERTS]).astype(I32)
    seg_start, tile_expert, n_used, zoff = _plan(counts, tm, n_tiles)
    pos_p = _positions(route_p, base_p, seg_start, tp_p)
    pos_s = _positions(route_s, base_s + tot_p[None], seg_start, tp_s)

    xs = _scatter_rows(h2_p, pos_p, zoff, n_tiles * tm, tm)
    xs = _scatter_rows(h2_s, pos_s, zoff, xs, tm)
    ys = _moe(xs, tile_expert, n_used, w_gate[l], w_up[l], w_down[l], tm)

    y_p = _finish(x1_p, route_p, mod_p.reshape(bp, 1, -1), final_g, ys, pos_p)
    y_s = _finish(x1_s.reshape(1, bs, d), route_s, mod_s.reshape(1, bs, -1), final_g, ys,
                  pos_s).reshape(bs, 1, d)

    conv_s = jnp.concatenate([cache_conv[l][:, 1:, :], u_s[:, None, :]], axis=1)
    return (y_p, y_s,
            c_p, n_pr, m_p[:, 0, :N_HEADS].reshape(1, bp, N_HEADS), cv_p,
            c_s[None], n_s.reshape(1, bs, N_HEADS, dk), m_s[:, :N_HEADS].reshape(1, bs, N_HEADS),
            conv_s[None])
```

```python
import functools

import jax
import jax.numpy as jnp
from jax import lax
from jax.experimental import pallas as pl
from jax.experimental.pallas import tpu as pltpu

F32 = jnp.float32
BF16 = jnp.bfloat16
I32 = jnp.int32

EPS = 1e-6
LANES = 128
ROW_SUB = 8
CHUNK = 128
N_HEADS = 4
N_GROUPS = 4
EXP_PER_GROUP = 8
N_EXPERTS = N_GROUPS * EXP_PER_GROUP
CONV_W = 31
CONV_PAD = 32
CONV_OFF = CONV_PAD - (CONV_W - 1)
MIX_TILE = 512
MIX_SUB = 256
MOE_TILE = 512
PERM_TILE = 512
FIN_CHUNK = 64
VMEM_LIMIT = 56 * 1024 * 1024


def _sigmoid(x):
    return 1.0 / (1.0 + jnp.exp(-x))


def _silu(x):
    return x * _sigmoid(x)


def _log_sigmoid(x):
    return jnp.minimum(x, 0.0) - jnp.log(1.0 + jnp.exp(-jnp.abs(x)))


def _bdot_nt(a, b):
    return lax.dot_general(a.astype(BF16), b.astype(BF16), (((1,), (1,)), ((), ())),
                           preferred_element_type=F32)


def _split2(x):
    hi = x.astype(BF16)
    lo = (x - hi.astype(F32)).astype(BF16)
    return hi, lo


def _split_kernel(w_ref, hi_ref, lo_ref):
    hi, lo = _split2(w_ref[...])
    hi_ref[...] = hi
    lo_ref[...] = lo


def _split_weights(w):
    rows, cols = w.shape
    blk = min(cols, 512)
    assert cols % blk == 0
    spec = pl.BlockSpec((rows, blk), lambda j: (0, j))
    return pl.pallas_call(
        _split_kernel, grid=(cols // blk,), in_specs=[spec], out_specs=[spec, spec],
        out_shape=[jax.ShapeDtypeStruct(w.shape, BF16)] * 2,
        compiler_params=pltpu.CompilerParams(dimension_semantics=("arbitrary",),
                                             vmem_limit_bytes=VMEM_LIMIT),
        name="split",
    )(w)


def _dot3(a, w_hi, w_lo):
    a_hi, a_lo = _split2(a)
    return (jnp.dot(a_hi, w_hi, preferred_element_type=F32)
            + jnp.dot(a_lo, w_hi, preferred_element_type=F32)
            + jnp.dot(a_hi, w_lo, preferred_element_type=F32))


def _cumsum_lanes(triu_bf16, x):
    hi = x.astype(BF16)
    r1 = x - hi.astype(F32)
    mid = r1.astype(BF16)
    lo = (r1 - mid.astype(F32)).astype(BF16)
    return (jnp.dot(hi, triu_bf16, preferred_element_type=F32)
            + jnp.dot(mid, triu_bf16, preferred_element_type=F32)
            + jnp.dot(lo, triu_bf16, preferred_element_type=F32))


def _rms(x, g):
    return x * lax.rsqrt(jnp.mean(x * x, axis=-1, keepdims=True) + EPS) * g


def _layer_norm(x, g, b=None):
    mu = jnp.mean(x, axis=-1, keepdims=True)
    xc = x - mu
    var = jnp.mean(xc * xc, axis=-1, keepdims=True)
    y = xc * lax.rsqrt(var + EPS) * g
    return y if b is None else y + b


def _store_rows(ref, x, row0=0):
    r = x.shape[0]
    for k in range(ROW_SUB):
        ref[pl.ds(row0 * ROW_SUB + k, r, stride=ROW_SUB), :] = x[:, k * LANES:(k + 1) * LANES]


def _load_rows(ref, r, row0=0):
    return jnp.concatenate([ref[pl.ds(row0 * ROW_SUB + k, r, stride=ROW_SUB), :]
                            for k in range(ROW_SUB)], axis=1)


def _route(logits):
    lane = lax.broadcasted_iota(I32, logits.shape, 1).astype(F32)
    neg = jnp.float32(-jnp.inf)
    big = jnp.float32(1e9)
    is_g = lane < N_GROUPS
    gl = jnp.where(is_g, logits, neg)
    gmax = jnp.max(gl, axis=1, keepdims=True)
    gsel = jnp.min(jnp.where(gl == gmax, lane, big), axis=1, keepdims=True)
    pg = 1.0 / jnp.sum(jnp.where(is_g, jnp.exp(gl - gmax), 0.0), axis=1, keepdims=True)
    lo = N_GROUPS + EXP_PER_GROUP * gsel
    emask = (lane >= lo) & (lane < lo + EXP_PER_GROUP)
    el = jnp.where(emask, logits, neg)
    v1 = jnp.max(el, axis=1, keepdims=True)
    i1 = jnp.min(jnp.where(el == v1, lane, big), axis=1, keepdims=True)
    el2 = jnp.where(lane == i1, neg, el)
    v2 = jnp.max(el2, axis=1, keepdims=True)
    i2 = jnp.min(jnp.where(el2 == v2, lane, big), axis=1, keepdims=True)
    d = jnp.exp(v2 - v1)
    w1 = pg / (1.0 + d)
    w2 = pg * d / (1.0 + d)
    return jnp.where(lane == 0, i1 - N_GROUPS,
                     jnp.where(lane == 1, i2 - N_GROUPS,
                               jnp.where(lane == 2, w1, jnp.where(lane == 3, w2, 0.0))))


def _ada_kernel(cp_ref, cs_ref, w_ref, b_ref, op_ref, os_ref):
    w_hi, w_lo = _split2(w_ref[...])
    op_ref[...] = _dot3(_silu(cp_ref[...]), w_hi, w_lo) + b_ref[...]
    os_ref[...] = _dot3(_silu(cs_ref[...]), w_hi, w_lo) + b_ref[...]


def _ada(c_p, c_s, w_ada, b_ada):
    (rp, d), rs = c_p.shape, c_s.shape[0]
    n_out = w_ada.shape[1]
    blk = 1024
    return pl.pallas_call(
        _ada_kernel,
        grid=(n_out // blk,),
        in_specs=[pl.BlockSpec((rp, d), lambda j: (0, 0)),
                  pl.BlockSpec((rs, d), lambda j: (0, 0)),
                  pl.BlockSpec((d, blk), lambda j: (0, j)),
                  pl.BlockSpec((1, blk), lambda j: (0, j))],
        out_specs=[pl.BlockSpec((rp, blk), lambda j: (0, j)),
                   pl.BlockSpec((rs, blk), lambda j: (0, j))],
        out_shape=[jax.ShapeDtypeStruct((rp, n_out), F32), jax.ShapeDtypeStruct((rs, n_out), F32)],
        compiler_params=pltpu.CompilerParams(dimension_semantics=("arbitrary",),
                                             vmem_limit_bytes=VMEM_LIMIT),
        name="ada",
    )(c_p, c_s, w_ada, b_ada.reshape(1, n_out))


def _post(x, attn_cat, mod, g2, wout, wr_hi, wr_lo, rbias, d, wout_lo=None):
    gate1 = mod[:, 2 * d:3 * d]
    sh2 = mod[:, 3 * d:4 * d]
    sc2 = mod[:, 4 * d:5 * d]
    if wout_lo is None:
        proj = jnp.dot(attn_cat.astype(BF16), wout, preferred_element_type=F32)
    else:
        proj = _dot3(attn_cat, wout, wout_lo)
    x1 = x + gate1 * proj
    h2 = _rms(x1, g2) * (1.0 + sc2) + sh2
    logits = _dot3(h2, wr_hi, wr_lo) + rbias
    return x1, h2, _route(logits)


def _mix_prompt_kernel(x_ref, mod_ref, g1_ref, g2_ref, wmain_ref, wgh_ref, wgl_ref, gbias_ref,
                       mhg_ref, wdw_ref, bdw_ref, clg_ref, clb_ref, wout_ref, wrh_ref, wrl_ref,
                       rbias_ref,
                       x1_ref, h2_ref, route_ref, c_ref, n_ref, m_ref, cv_ref, base_ref, tot_ref,
                       ubuf, yc_s, q_s, k_s, v_s, so_s, hm_s, p_s, u_s, cm_s, nb_s, m_s, cnt_s,
                       *, tt, d, dm, dk):
    t = pl.program_id(1)
    dc = d - dm
    n_lt = dc // LANES
    sub = min(MIX_SUB, tt)

    @pl.when(t == 0)
    def _():
        c_ref[...] = jnp.zeros_like(c_ref)
        nb_s[...] = jnp.zeros_like(nb_s)
        m_s[...] = jnp.zeros_like(m_s)
        ubuf[0:CONV_PAD, :] = jnp.zeros((CONV_PAD, dc), F32)

    mod = mod_ref[0]
    row8 = lax.broadcasted_iota(I32, (8, LANES), 0)
    row = lax.broadcasted_iota(I32, (CHUNK, CHUNK), 0)
    col = lax.broadcasted_iota(I32, (CHUNK, CHUNK), 1)
    causal = col <= row
    triu = (row <= col).astype(BF16)
    neg = jnp.float32(-jnp.inf)
    ones_b = jnp.ones((CHUNK, dk), BF16)
    pad_rows = jnp.zeros((CHUNK - 8, LANES), F32)

    lanes_of = [slice(lt * LANES, (lt + 1) * LANES) for lt in range(n_lt)]
    wrows = [[jnp.broadcast_to(wdw_ref[j:j + 1, ls], (8, LANES)) for j in range(CONV_W)]
             for ls in lanes_of]
    bias = [jnp.broadcast_to(bdw_ref[:, ls], (8, LANES)) for ls in lanes_of]

    def partial_sums(r0, lt):
        blocks = [ubuf[r0 + 8 * a:r0 + 8 * a + 8, lanes_of[lt]] for a in range(CONV_PAD // 8)]
        sums = []
        for s in range(8):
            acc = None
            for a in range(CONV_PAD // 8):
                j = 8 * a + s - CONV_OFF
                if 0 <= j < CONV_W:
                    term = blocks[a] * wrows[lt][j]
                    acc = term if acc is None else acc + term
            sums.append(acc)
        return tuple(sums)

    q_prev = [None] * n_lt
    ca = [jnp.concatenate([c_ref[0, 0, hd], nb_s[hd]], axis=1) for hd in range(N_HEADS)]
    m_prev = [m_s[hd:hd + 1, :] for hd in range(N_HEADS)]

    n_sub = tt // sub
    gates_of = {}
    hc_of = {}
    tile_counts = []

    def rows_of(sb):
        return slice(sb * sub, (sb + 1) * sub)

    def proj_items(sb):
        rs = rows_of(sb)
        st = {}

        def head():
            x = x_ref[0, rs, :]
            h = _rms(x, g1_ref[...]) * (1.0 + mod[:, d:2 * d]) + mod[:, 0:d]
            st["hb"] = h.astype(BF16)
            gates_of[sb] = _dot3(h, wgh_ref[...], wgl_ref[...]) + gbias_ref[...]

        def proj(lo, hi):
            return jnp.dot(st["hb"], wmain_ref[:, lo:hi], preferred_element_type=F32)

        def glu():
            ubuf[CONV_PAD + sb * sub:CONV_PAD + (sb + 1) * sub, :] = (
                proj(4 * dm, 4 * dm + dc) * _sigmoid(proj(4 * dm + dc, 4 * dm + 2 * dc)))

        def q():
            q_s[rs, :] = proj(0, dm).astype(BF16)

        def k():
            k_s[rs, :] = proj(dm, 2 * dm) * (dk ** -0.5)

        def v():
            v_s[rs, :] = proj(2 * dm, 3 * dm).astype(BF16)

        def o():
            so_s[rs, :] = _sigmoid(proj(3 * dm, 4 * dm))

        return [head, glu, q, k, v, o]

    def conv_items(sb):
        items = []
        for lt in range(n_lt):
            for i in range(sb * sub // 8 + 1, (sb + 1) * sub // 8 + 1):
                def block(lt=lt, i=i):
                    if i == 1:
                        q_prev[lt] = partial_sums(0, lt)
                    q_cur = partial_sums(i * 8, lt)
                    cur = ubuf[(i - 1) * 8 + CONV_PAD:i * 8 + CONV_PAD, lanes_of[lt]]
                    y = bias[lt] + q_prev[lt][0] + cur * wrows[lt][CONV_W - 1]
                    for s in range(1, 8):
                        merged = jnp.where(row8 < s, q_cur[s], q_prev[lt][s])
                        y = y + pltpu.roll(merged, 8 - s, 0)
                    yc_s[(i - 1) * 8:i * 8, lanes_of[lt]] = y
                    q_prev[lt] = q_cur
                items.append(block)
        return items

    def post_items(sb):
        def post():
            rs = rows_of(sb)
            cat = jnp.concatenate([hm_s[rs, :], hc_of[sb]], axis=1)
            x1, h2, route = _post(x_ref[0, rs, :], cat, mod, g2_ref[...], wout_ref[...],
                                  wrh_ref[...], wrl_ref[...], rbias_ref[...], d)
            x1_ref[0, rs, :] = x1
            _store_rows(h2_ref, h2, sb * sub)
            route_ref[rs, :] = route
            oh0, oh1 = _expert_onehots(route)
            tile_counts.append(jnp.sum(oh0 + oh1, axis=0, keepdims=True))
        return [post]

    def interleave(main, side):
        gap = len(main) / (len(side) + 1)
        due, done = gap, 0
        for n, item in enumerate(main):
            item()
            while done < len(side) and n + 1 >= due:
                side[done]()
                done += 1
                due += gap
        for item in side[done:]:
            item()

    for item in proj_items(0):
        item()
    for sb in range(n_sub):
        r_lo = sb * sub
        rs = rows_of(sb)
        side = post_items(sb - 1) if sb > 0 else []
        if sb + 1 < n_sub:
            side = side + proj_items(sb + 1)
        interleave(conv_items(sb), side)
        hc_of[sb] = _silu(_layer_norm(yc_s[rs, :], clg_ref[...], clb_ref[...]))
        gates = gates_of[sb]

        chunks = range(r_lo // CHUNK, (r_lo + sub) // CHUNK)
        b_cols = {}
        for c in chunks:
            r0 = c * CHUNK
            g8 = gates[r0 - r_lo:r0 - r_lo + CHUNK, :].T[0:8, :]
            b8 = _cumsum_lanes(triu, _log_sigmoid(g8))
            pk8 = jnp.where(row8 < N_HEADS, g8 - pltpu.roll(b8, N_HEADS, 0), b8)
            pk = jnp.concatenate([pk8, pad_rows], axis=0).T
            for hd in range(N_HEADS):
                cs = slice(hd * dk, (hd + 1) * dk)
                idx = c * N_HEADS + hd
                kf = k_s[r0:r0 + CHUNK, cs]
                va = jnp.concatenate([v_s[r0:r0 + CHUNK, cs], ones_b], axis=1)
                gm = jnp.where(causal, pk8[hd:hd + 1, :], neg)
                cm = jnp.max(gm, axis=1, keepdims=True)
                s = _bdot_nt(q_s[r0:r0 + CHUNK, cs], kf) * jnp.exp(gm - cm)
                p_s[idx] = jnp.dot(s.astype(BF16), va, preferred_element_type=F32)
                cm_s[idx] = jnp.broadcast_to(cm, (CHUNK, LANES))
                kw = kf * jnp.exp(pk[:, hd:hd + 1] - cm[CHUNK - 1:CHUNK, :])
                u_s[idx] = jnp.dot(kw.T.astype(BF16), va, preferred_element_type=F32)
                b_cols[idx] = pk[:, N_HEADS + hd:N_HEADS + hd + 1]

        for hd in range(N_HEADS):
            cs = slice(hd * dk, (hd + 1) * dk)
            for c in chunks:
                r0 = c * CHUNK
                idx = c * N_HEADS + hd
                cm = cm_s[idx]
                b_col = b_cols[idx]
                mt = jnp.maximum(m_prev[hd], cm)
                f_loc = jnp.exp(cm - mt)
                a_int = jnp.exp(m_prev[hd] - mt)
                qc = jnp.dot(q_s[r0:r0 + CHUNK, cs], ca[hd].astype(BF16),
                             preferred_element_type=F32)
                p = p_s[idx]
                num = f_loc * p[:, :dk] + a_int * qc[:, :dk]
                den = f_loc * p[:, dk:] + a_int * qc[:, dk:]
                hh = num / jnp.maximum(jnp.abs(den), jnp.exp(-(b_col + mt)))
                hm_s[r0:r0 + CHUNK, cs] = (_layer_norm(hh, mhg_ref[hd:hd + 1, :])
                                           * so_s[r0:r0 + CHUNK, cs])
                mt_l = mt[CHUNK - 1:CHUNK, :]
                u = u_s[idx]
                f_l = f_loc[CHUNK - 1:CHUNK, :]
                a_l = a_int[CHUNK - 1:CHUNK, :]
                ca[hd] = jnp.concatenate([a_l * ca[hd][:, :dk] + f_l * u[:, :dk],
                                          a_l * ca[hd][:, dk:] + f_l * u[:, dk:]], axis=1)
                m_prev[hd] = b_col[CHUNK - 1:CHUNK, :] + mt_l

    for item in post_items(n_sub - 1):
        item()

    for hd in range(N_HEADS):
        c_ref[0, 0, hd] = ca[hd][:, :dk]
        nb_s[hd] = ca[hd][:, dk:]
        m_s[hd:hd + 1, :] = m_prev[hd]
    cv_ref[0, 0] = ubuf[tt + CONV_PAD - (CONV_W - 1):tt + CONV_PAD, :]
    ubuf[0:CONV_PAD, :] = ubuf[tt:tt + CONV_PAD, :]

    @pl.when(t == pl.num_programs(1) - 1)
    def _():
        for hd in range(N_HEADS):
            n_ref[0, 0, hd:hd + 1, :] = nb_s[hd].T[0:1, :]

    lane1 = lax.broadcasted_iota(I32, (1, LANES), 1)
    m_row = jnp.zeros((1, LANES), F32)
    for hd in range(N_HEADS):
        m_row = jnp.where(lane1 == hd, m_s[hd:hd + 1, :], m_row)
    m_ref[0] = m_row

    @pl.when((pl.program_id(0) == 0) & (t == 0))
    def _():
        cnt_s[...] = jnp.zeros_like(cnt_s)

    base_ref[0] = cnt_s[...]
    cnt_s[...] = cnt_s[...] + sum(tile_counts)
    tot_ref[...] = cnt_s[...]


def _const_spec(shape):
    nd = len(shape)
    return pl.BlockSpec(shape, lambda *_: (0,) * nd)


def _mix_prompt(x, mod, wts):
    b, t, d = x.shape
    dm = wts["dm"]
    dk = dm // N_HEADS
    dc = d - dm
    tt = min(MIX_TILE, t)
    assert t % tt == 0 and tt % CHUNK == 0 and tt >= CONV_PAD
    nt = t // tt
    kern = functools.partial(_mix_prompt_kernel, tt=tt, d=d, dm=dm, dk=dk)
    const_names = ["g1", "g2", "wmain", "wg_hi", "wg_lo", "gbias", "mhg", "wdw", "bdw", "clg",
                   "clb", "wout", "wr_hi", "wr_lo", "rbias"]
    consts = [wts[k] for k in const_names]
    in_specs = ([pl.BlockSpec((1, tt, d), lambda i, j: (i, j, 0)),
                 pl.BlockSpec((1, 1, mod.shape[-1]), lambda i, j: (i, 0, 0))]
                + [_const_spec(c.shape) for c in consts])
    out_shape = [
        jax.ShapeDtypeStruct((b, t, d), F32),
        jax.ShapeDtypeStruct((b * t * ROW_SUB, LANES), F32),
        jax.ShapeDtypeStruct((b * t, LANES), F32),
        jax.ShapeDtypeStruct((1, b, N_HEADS, dk, dk), F32),
        jax.ShapeDtypeStruct((1, b, N_HEADS, dk), F32),
        jax.ShapeDtypeStruct((b, 1, LANES), F32),
        jax.ShapeDtypeStruct((1, b, CONV_W - 1, dc), F32),
        jax.ShapeDtypeStruct((b * nt, 8, LANES), F32),
        jax.ShapeDtypeStruct((8, LANES), F32),
    ]
    out_specs = [
        pl.BlockSpec((1, tt, d), lambda i, j: (i, j, 0)),
        pl.BlockSpec((tt * ROW_SUB, LANES), lambda i, j: (i * nt + j, 0)),
        pl.BlockSpec((tt, LANES), lambda i, j: (i * nt + j, 0)),
        pl.BlockSpec((1, 1, N_HEADS, dk, dk), lambda i, j: (0, i, 0, 0, 0)),
        pl.BlockSpec((1, 1, N_HEADS, dk), lambda i, j: (0, i, 0, 0)),
        pl.BlockSpec((1, 1, LANES), lambda i, j: (i, 0, 0)),
        pl.BlockSpec((1, 1, CONV_W - 1, dc), lambda i, j: (0, i, 0, 0)),
        pl.BlockSpec((1, 8, LANES), lambda i, j: (i * nt + j, 0, 0)),
        pl.BlockSpec((8, LANES), lambda i, j: (0, 0)),
    ]
    n_hc = (tt // CHUNK) * N_HEADS
    scratch = [pltpu.VMEM((tt + CONV_PAD, dc), F32),
               pltpu.VMEM((tt, dc), F32),
               pltpu.VMEM((tt, dm), BF16),
               pltpu.VMEM((tt, dm), F32),
               pltpu.VMEM((tt, dm), BF16),
               pltpu.VMEM((tt, dm), F32),
               pltpu.VMEM((tt, dm), F32),
               pltpu.VMEM((n_hc, CHUNK, 2 * dk), F32),
               pltpu.VMEM((n_hc, CHUNK, 2 * dk), F32),
               pltpu.VMEM((n_hc, CHUNK, LANES), F32),
               pltpu.VMEM((N_HEADS, dk, LANES), F32),
               pltpu.VMEM((8, LANES), F32),
               pltpu.VMEM((8, LANES), F32)]
    return pl.pallas_call(
        kern, grid=(b, nt), in_specs=in_specs, out_specs=out_specs, out_shape=out_shape,
        scratch_shapes=scratch,
        compiler_params=pltpu.CompilerParams(dimension_semantics=("arbitrary", "arbitrary"),
                                             vmem_limit_bytes=VMEM_LIMIT),
        name="mix_p",
    )(x, mod.reshape(b, 1, -1), *consts)


def _s_pre_kernel(x_ref, mod_ref, g1_ref, wmain_ref, wmainlo_ref, wgh_ref, wgl_ref, gbias_ref,
                  wkt_ref, wktlo_ref, wdw_ref, bdw_ref, clg_ref, clb_ref, cache_ref, n0_ref, m0_ref,
                  q_ref, kt_ref, vs_ref, ab_ref, sv_ref, den_ref, eb_ref, o_ref, hc_ref, u_ref,
                  n_ref, m_ref, *, d, dm, dk):
    dc = d - dm
    x = x_ref[...]
    mod = mod_ref[...]
    sh1 = mod[:, 0:d]
    sc1 = mod[:, d:2 * d]
    h = _rms(x, g1_ref[...]) * (1.0 + sc1) + sh1
    z = _dot3(h, wmain_ref[...], wmainlo_ref[...])
    gates = _dot3(h, wgh_ref[...], wgl_ref[...]) + gbias_ref[...]
    scale = dk ** -0.5
    h_hi, h_lo = _split2(h)
    kt = _bdot_nt(wkt_ref[...], h_hi) + _bdot_nt(wktlo_ref[...], h_hi) + _bdot_nt(wkt_ref[...], h_lo)
    kt_ref[...] = (kt * scale).astype(BF16)
    k_all = z[:, dm:2 * dm] * scale
    ga = z[:, 4 * dm:4 * dm + dc]
    gb = z[:, 4 * dm + dc:4 * dm + 2 * dc]
    u = ga * _sigmoid(gb)
    u_ref[...] = u
    acc = jnp.broadcast_to(bdw_ref[...], u.shape) + u * wdw_ref[CONV_W - 1:CONV_W, :]
    for j in range(CONV_W - 1):
        acc = acc + cache_ref[j] * wdw_ref[j:j + 1, :]
    hc_ref[...] = _silu(_layer_norm(acc, clg_ref[...], clb_ref[...]))
    o_ref[...] = z[:, 3 * dm:4 * dm]
    q_ref[...] = z[:, 0:dm]
    m0 = m0_ref[...]
    n0 = n0_ref[...]
    lane1 = lax.broadcasted_iota(I32, (1, LANES), 1)
    m_new = jnp.zeros(m0.shape, F32)
    for hd in range(N_HEADS):
        cs = slice(hd * dk, (hd + 1) * dk)
        ig = gates[:, hd:hd + 1]
        lf = _log_sigmoid(gates[:, N_HEADS + hd:N_HEADS + hd + 1])
        mp = m0[:, hd:hd + 1]
        inter = lf + mp
        mt = jnp.maximum(inter, ig)
        w = jnp.exp(ig - mt)
        a_int = jnp.exp(inter - mt)
        qf = z[:, cs]
        kf = k_all[:, cs]
        vf = z[:, 2 * dm + hd * dk:2 * dm + (hd + 1) * dk]
        s = jnp.sum(qf * kf, axis=1, keepdims=True) * w
        sv_ref[:, cs] = s * vf
        den_ref[:, cs] = jnp.broadcast_to(
            s + a_int * jnp.sum(qf * n0[:, cs], axis=1, keepdims=True), (x.shape[0], dk))
        eb_ref[:, cs] = jnp.broadcast_to(jnp.exp(-mt), (x.shape[0], dk))
        ab_ref[:, cs] = jnp.broadcast_to(a_int, (x.shape[0], dk))
        vs_ref[:, cs] = (vf * w).astype(BF16)
        n_ref[:, cs] = a_int * n0[:, cs] + w * kf
        m_new = jnp.where(lane1 == hd, mt, m_new)
    m_ref[...] = m_new


def _s_state_kernel(q_ref, kt_ref, vs_ref, ab_ref, c0_ref, c_ref, r_ref, *, bb, dk):
    i = pl.program_id(0)
    nb = q_ref.shape[0]
    rows = lax.broadcasted_iota(I32, (nb, dk), 0)

    @pl.when(i == 0)
    def _():
        r_ref[...] = jnp.zeros_like(r_ref)

    a_blk = ab_ref[pl.ds(pl.multiple_of(i * bb, bb), bb), :]
    for j in range(bb):
        sel = rows == i * bb + j
        for hd in range(N_HEADS):
            cs = slice(hd * dk, (hd + 1) * dk)
            c0 = c0_ref[j, hd]
            vmask = jnp.where(sel, vs_ref[:, cs], jnp.zeros((), BF16))
            c_ref[j, hd] = (a_blk[j:j + 1, cs] * c0
                            + jnp.dot(kt_ref[cs, :], vmask, preferred_element_type=F32))
            c_hi, c_lo = _split2(c0)
            q_hi, q_lo = _split2(q_ref[:, cs])
            r = (jnp.dot(q_hi, c_hi, preferred_element_type=F32)
                 + jnp.dot(q_lo, c_hi, preferred_element_type=F32)
                 + jnp.dot(q_hi, c_lo, preferred_element_type=F32))
            r_ref[:, cs] = r_ref[:, cs] + jnp.where(sel, r, 0.0)


def _s_post_kernel(x_ref, mod_ref, g2_ref, mhg_ref, r_ref, ab_ref, sv_ref, den_ref, eb_ref, o_ref,
                   hc_ref, wout_ref, woutlo_ref, wrh_ref, wrl_ref, rbias_ref,
                   x1_ref, h2_ref, route_ref, *, d, dm, dk):
    hm = []
    for hd in range(N_HEADS):
        cs = slice(hd * dk, (hd + 1) * dk)
        num = sv_ref[:, cs] + ab_ref[:, cs] * r_ref[:, cs]
        hh = num / jnp.maximum(jnp.abs(den_ref[:, cs]), eb_ref[:, cs])
        hm.append(_layer_norm(hh, mhg_ref[hd:hd + 1, :]) * _sigmoid(o_ref[:, cs]))
    cat = jnp.concatenate(hm + [hc_ref[...]], axis=1)
    x1, h2, route = _post(x_ref[...], cat, mod_ref[...], g2_ref[...], wout_ref[...], wrh_ref[...],
                          wrl_ref[...], rbias_ref[...], d, wout_lo=woutlo_ref[...])
    x1_ref[...] = x1
    _store_rows(h2_ref, h2)
    route_ref[...] = route


def _mix_sample(x, mod, c0, n0, m0, cache, wts):
    nb, d = x.shape
    dm = wts["dm"]
    dk = dm // N_HEADS
    dc = d - dm
    cp = pltpu.CompilerParams(dimension_semantics=("arbitrary",), vmem_limit_bytes=VMEM_LIMIT)
    cache_t = jnp.transpose(cache, (1, 0, 2))
    m0p = jnp.pad(m0, ((0, 0), (0, LANES - N_HEADS)))
    pre_in = [x, mod, wts["g1"], wts["wmain"], wts["wmain_lo"], wts["wg_hi"], wts["wg_lo"],
              wts["gbias"], wts["wk_t"], wts["wk_t_lo"], wts["wdw"], wts["bdw"], wts["clg"],
              wts["clb"], cache_t, n0.reshape(nb, dm), m0p]
    pre_out = [jax.ShapeDtypeStruct((nb, dm), F32),
               jax.ShapeDtypeStruct((dm, nb), BF16),
               jax.ShapeDtypeStruct((nb, dm), BF16),
               jax.ShapeDtypeStruct((nb, dm), F32),
               jax.ShapeDtypeStruct((nb, dm), F32),
               jax.ShapeDtypeStruct((nb, dm), F32),
               jax.ShapeDtypeStruct((nb, dm), F32),
               jax.ShapeDtypeStruct((nb, dm), F32),
               jax.ShapeDtypeStruct((nb, dc), F32),
               jax.ShapeDtypeStruct((nb, dc), F32),
               jax.ShapeDtypeStruct((nb, dm), F32),
               jax.ShapeDtypeStruct((nb, LANES), F32)]
    (q, kt, vs, ab, sv, den, eb, o, hc, u, n1, m1) = pl.pallas_call(
        functools.partial(_s_pre_kernel, d=d, dm=dm, dk=dk),
        grid=(1,),
        in_specs=[_const_spec(a.shape) for a in pre_in],
        out_specs=[_const_spec(s.shape) for s in pre_out],
        out_shape=pre_out, compiler_params=cp, name="s_pre")(*pre_in)

    bb = 8
    assert nb % bb == 0
    c1, r = pl.pallas_call(
        functools.partial(_s_state_kernel, bb=bb, dk=dk),
        grid=(nb // bb,),
        in_specs=[_const_spec(q.shape), _const_spec(kt.shape), _const_spec(vs.shape),
                  _const_spec(ab.shape),
                  pl.BlockSpec((bb, N_HEADS, dk, dk), lambda i: (i, 0, 0, 0))],
        out_specs=[pl.BlockSpec((bb, N_HEADS, dk, dk), lambda i: (i, 0, 0, 0)),
                   _const_spec((nb, dm))],
        out_shape=[jax.ShapeDtypeStruct((nb, N_HEADS, dk, dk), F32),
                   jax.ShapeDtypeStruct((nb, dm), F32)],
        compiler_params=cp, name="s_state")(q, kt, vs, ab, c0)

    post_in = [x, mod, wts["g2"], wts["mhg"], r, ab, sv, den, eb, o, hc, wts["wout"],
               wts["wout_lo"], wts["wr_hi"], wts["wr_lo"], wts["rbias"]]
    post_out = [jax.ShapeDtypeStruct((nb, d), F32),
                jax.ShapeDtypeStruct((nb * ROW_SUB, LANES), F32),
                jax.ShapeDtypeStruct((nb, LANES), F32)]
    x1, h2, route = pl.pallas_call(
        functools.partial(_s_post_kernel, d=d, dm=dm, dk=dk),
        grid=(1,),
        in_specs=[_const_spec(a.shape) for a in post_in],
        out_specs=[_const_spec(s.shape) for s in post_out],
        out_shape=post_out, compiler_params=cp, name="s_post")(*post_in)
    return x1, h2, route, c1, n1, m1, u


def _scatter_kernel(pos_ref, zoff_ref, src_ref, *rest, tp, tm, create):
    xs_ref, zbuf, sem, zsem = rest[-4:]
    i = pl.program_id(0)

    if create:
        @pl.when(i == 0)
        def _():
            zbuf[...] = jnp.zeros_like(zbuf)

            def zero_copy(e):
                start = pl.multiple_of(zoff_ref[e] * ROW_SUB, tm * ROW_SUB)
                return pltpu.make_async_copy(zbuf, xs_ref.at[pl.ds(start, tm * ROW_SUB)], zsem)

            for e in range(zoff_ref.shape[0]):
                @pl.when(zoff_ref[e] >= 0)
                def _():
                    zero_copy(e).start()
            for e in range(zoff_ref.shape[0]):
                @pl.when(zoff_ref[e] >= 0)
                def _():
                    zero_copy(e).wait()

    for r in range(tp):
        src = src_ref.at[pl.ds(r * ROW_SUB, ROW_SUB)]
        for slot in range(2):
            dst = pl.multiple_of(pos_ref[0, slot, r], ROW_SUB)
            pltpu.make_async_copy(src, xs_ref.at[pl.ds(dst, ROW_SUB)], sem).start(priority=slot)
    for slot in range(2):
        pltpu.make_async_copy(src_ref, xs_ref.at[pl.ds(0, tp * ROW_SUB)], sem).wait()


def _scatter_rows(h2, pos3, zoff, xs_or_rows, tm):
    n, c = h2.shape[0] // ROW_SUB, LANES
    tp = pos3.shape[2]
    assert pos3.shape == (n // tp, 2, tp)
    create = isinstance(xs_or_rows, int)
    n_sorted = xs_or_rows * ROW_SUB if create else xs_or_rows.shape[0]
    in_specs = [pl.BlockSpec((1, 2, tp), lambda i: (i, 0, 0), memory_space=pltpu.SMEM),
                pl.BlockSpec(memory_space=pltpu.SMEM),
                pl.BlockSpec((tp * ROW_SUB, c), lambda i: (i, 0))]
    args = [pos3, zoff, h2]
    if not create:
        in_specs.append(pl.BlockSpec(memory_space=pl.ANY))
        args.append(xs_or_rows)
    return pl.pallas_call(
        functools.partial(_scatter_kernel, tp=tp, tm=tm, create=create),
        grid_spec=pltpu.PrefetchScalarGridSpec(
            num_scalar_prefetch=0,
            grid=(n // tp,),
            in_specs=in_specs,
            out_specs=pl.BlockSpec(memory_space=pl.ANY),
            scratch_shapes=[pltpu.VMEM((tm * ROW_SUB, c), F32), pltpu.SemaphoreType.DMA(()),
                            pltpu.SemaphoreType.DMA(())]),
        out_shape=jax.ShapeDtypeStruct((n_sorted, c), F32),
        input_output_aliases={} if create else {3: 0},
        compiler_params=pltpu.CompilerParams(dimension_semantics=("arbitrary",),
                                             vmem_limit_bytes=VMEM_LIMIT),
        name="scatter",
    )(*args)


def _moe_kernel(te_ref, nu_ref, xs_ref, wg_ref, wu_ref, wd_ref, ys_ref, wg_b, wu_b, wd_b):
    i = pl.program_id(0)
    used = i < nu_ref[0]

    @pl.when(used & ((i == 0) | (te_ref[i] != te_ref[jnp.maximum(i - 1, 0)])))
    def _():
        wg_b[...] = wg_ref[0].astype(BF16)
        wu_b[...] = wu_ref[0].astype(BF16)
        wd_b[...] = wd_ref[0].astype(BF16)

    @pl.when(used)
    def _():
        xb = _load_rows(xs_ref, xs_ref.shape[0] // ROW_SUB).astype(BF16)
        g = jnp.dot(xb, wg_b[...], preferred_element_type=F32)
        u = jnp.dot(xb, wu_b[...], preferred_element_type=F32)
        hid = (_silu(g) * u).astype(BF16)
        _store_rows(ys_ref, jnp.dot(hid, wd_b[...], preferred_element_type=F32))

    @pl.when(jnp.logical_not(used))
    def _():
        ys_ref[...] = jnp.zeros_like(ys_ref)


def _moe(xs, tile_expert, n_used, w_gate, w_up, w_down, tm):
    p, c = xs.shape[0] // ROW_SUB, LANES
    ne, d, de = w_gate.shape
    n_tiles = p // tm

    def x_map(i, te, nu):
        return (jnp.minimum(i, jnp.maximum(nu[0] - 1, 0)), 0)

    def w_map(i, te, nu):
        return (te[i], 0, 0)

    return pl.pallas_call(
        _moe_kernel,
        grid_spec=pltpu.PrefetchScalarGridSpec(
            num_scalar_prefetch=2,
            grid=(n_tiles,),
            in_specs=[pl.BlockSpec((tm * ROW_SUB, c), x_map),
                      pl.BlockSpec((1, d, de), w_map),
                      pl.BlockSpec((1, d, de), w_map),
                      pl.BlockSpec((1, de, d), w_map)],
            out_specs=pl.BlockSpec((tm * ROW_SUB, c), lambda i, te, nu: (i, 0)),
            scratch_shapes=[pltpu.VMEM((d, de), BF16), pltpu.VMEM((d, de), BF16),
                            pltpu.VMEM((de, d), BF16)]),
        out_shape=jax.ShapeDtypeStruct((p * ROW_SUB, c), F32),
        compiler_params=pltpu.CompilerParams(dimension_semantics=("arbitrary",),
                                             vmem_limit_bytes=VMEM_LIMIT),
        name="moe",
    )(tile_expert, n_used, xs, w_gate, w_up, w_down)


def _fin_kernel(pos_ref, posn_ref, x1_ref, route_ref, mod_ref, fg_ref, ys_ref, y_ref, ybuf, sem,
                *, tp, d, n_steps):
    i = pl.program_id(0)
    cur = i % 2

    def issue(p_ref, buf):
        def row_start(r, carry):
            dst = pl.ds(pl.multiple_of(r * ROW_SUB, ROW_SUB), ROW_SUB)
            for slot in range(2):
                src = pl.multiple_of(p_ref[0, slot, r], ROW_SUB)
                pltpu.make_async_copy(ys_ref.at[pl.ds(src, ROW_SUB)],
                                      ybuf.at[buf, slot, dst], sem.at[buf]).start(priority=slot)
            return carry
        lax.fori_loop(0, tp, row_start, 0, unroll=8)

    def wait_buf(buf):
        for slot in range(2):
            pltpu.make_async_copy(ys_ref.at[pl.ds(0, tp * ROW_SUB)], ybuf.at[buf, slot],
                                  sem.at[buf]).wait()

    @pl.when(i == 0)
    def _():
        issue(pos_ref, 0)

    wait_buf(cur)
    nxt = 1 - cur
    per_token_mod = mod_ref.shape[1] != 1
    for lo in range(0, tp, FIN_CHUNK):
        hi = min(lo + FIN_CHUNK, tp)
        for r in range(lo, hi):
            for slot in range(2):
                src = pl.multiple_of(posn_ref[0, slot, r], ROW_SUB)
                pltpu.make_async_copy(
                    ys_ref.at[pl.ds(src, ROW_SUB)],
                    ybuf.at[nxt, slot, pl.ds(r * ROW_SUB, ROW_SUB)], sem.at[nxt]).start(priority=slot)
        route = route_ref[lo:hi, :]
        moe = (route[:, 2:3] * _load_rows(ybuf.at[cur, 0], hi - lo, lo)
               + route[:, 3:4] * _load_rows(ybuf.at[cur, 1], hi - lo, lo))
        mod = mod_ref[0, lo:hi, :] if per_token_mod else mod_ref[0]
        y_ref[0, lo:hi, :] = _rms(x1_ref[0, lo:hi, :] + mod[:, 5 * d:6 * d] * moe, fg_ref[...])

    @pl.when(i == n_steps - 1)
    def _():
        wait_buf(nxt)


def _finish(x1, route, mod, final_g, ys, pos3):
    b, t, d = x1.shape
    tp = pos3.shape[2]
    assert t % tp == 0
    nt = t // tp
    n_steps = b * nt
    assert pos3.shape[0] == n_steps
    blk0 = 0
    if mod.shape[1] == 1:
        mod_spec = pl.BlockSpec((1, 1, 6 * d), lambda i: (i // nt, 0, 0))
    else:
        mod_spec = pl.BlockSpec((1, tp, 6 * d), lambda i: (i // nt, i % nt, 0))
    return pl.pallas_call(
        functools.partial(_fin_kernel, tp=tp, d=d, n_steps=n_steps),
        grid_spec=pltpu.PrefetchScalarGridSpec(
            num_scalar_prefetch=0,
            grid=(n_steps,),
            in_specs=[pl.BlockSpec((1, 2, tp), lambda i: (blk0 + i, 0, 0), memory_space=pltpu.SMEM),
                      pl.BlockSpec((1, 2, tp), lambda i: (blk0 + jnp.minimum(i + 1, n_steps - 1), 0, 0),
                                   memory_space=pltpu.SMEM),
                      pl.BlockSpec((1, tp, d), lambda i: (i // nt, i % nt, 0)),
                      pl.BlockSpec((tp, LANES), lambda i: (i, 0)),
                      mod_spec,
                      _const_spec((1, d)),
                      pl.BlockSpec(memory_space=pl.ANY)],
            out_specs=pl.BlockSpec((1, tp, d), lambda i: (i // nt, i % nt, 0)),
            scratch_shapes=[pltpu.VMEM((2, 2, tp * ROW_SUB, LANES), F32),
                            pltpu.SemaphoreType.DMA((2,))]),
        out_shape=jax.ShapeDtypeStruct((b, t, d), F32),
        compiler_params=pltpu.CompilerParams(dimension_semantics=("arbitrary",),
                                             vmem_limit_bytes=VMEM_LIMIT),
        name="fin",
    )(pos3, pos3, x1, route, mod, final_g.reshape(1, d), ys)


def _expert_onehots(route):
    lane = lax.broadcasted_iota(I32, route.shape, 1).astype(F32)
    return ((lane == route[:, 0:1]).astype(F32), (lane == route[:, 1:2]).astype(F32))


def _count_kernel(route_ref, base_ref, tot_ref, acc):
    @pl.when(pl.program_id(0) == 0)
    def _():
        acc[...] = jnp.zeros_like(acc)

    oh0, oh1 = _expert_onehots(route_ref[...])
    base_ref[0] = acc[...]
    acc[...] = acc[...] + jnp.sum(oh0 + oh1, axis=0, keepdims=True)
    tot_ref[...] = acc[...]


def _pos_kernel(route_ref, base_ref, seg_ref, pos_ref, tril_s, *, tp):
    @pl.when(pl.program_id(0) == 0)
    def _():
        r_i = lax.broadcasted_iota(I32, (tp, tp), 0)
        c_i = lax.broadcasted_iota(I32, (tp, tp), 1)
        tril_s[...] = (c_i < r_i).astype(BF16)

    oh0, oh1 = _expert_onehots(route_ref[...])
    before = jnp.dot(tril_s[...], (oh0 + oh1).astype(BF16), preferred_element_type=F32)
    offs = seg_ref[0:1, :] + base_ref[0][0:1, :] + before
    ones8 = jnp.ones((8, LANES), BF16)

    def lane_sum_as_row(v):
        hi = v.astype(BF16)
        r1 = v - hi.astype(F32)
        mid = r1.astype(BF16)
        lo = (r1 - mid.astype(F32)).astype(BF16)
        return _bdot_nt(ones8, hi) + _bdot_nt(ones8, mid) + _bdot_nt(ones8, lo)

    p0 = lane_sum_as_row(oh0 * offs)
    p1 = lane_sum_as_row(oh1 * offs)
    row = lax.broadcasted_iota(I32, (8, tp), 0)
    pos_ref[0] = (jnp.where(row == 0, p0, jnp.where(row == 1, p1, 0.0)) * ROW_SUB).astype(I32)


def _count(route, tp):
    n = route.shape[0]
    assert n % tp == 0
    return pl.pallas_call(
        _count_kernel, grid=(n // tp,),
        in_specs=[pl.BlockSpec((tp, LANES), lambda i: (i, 0))],
        out_specs=[pl.BlockSpec((1, 8, LANES), lambda i: (i, 0, 0)), _const_spec((8, LANES))],
        out_shape=[jax.ShapeDtypeStruct((n // tp, 8, LANES), F32),
                   jax.ShapeDtypeStruct((8, LANES), F32)],
        scratch_shapes=[pltpu.VMEM((8, LANES), F32)],
        compiler_params=pltpu.CompilerParams(dimension_semantics=("arbitrary",),
                                             vmem_limit_bytes=VMEM_LIMIT),
        name="count",
    )(route)


def _positions(route, base, seg_start, tp):
    n = route.shape[0]
    seg = jnp.broadcast_to(jnp.pad(seg_start.astype(F32), (0, LANES - N_EXPERTS))[None, :], (8, LANES))
    pos = pl.pallas_call(
        functools.partial(_pos_kernel, tp=tp), grid=(n // tp,),
        in_specs=[pl.BlockSpec((tp, LANES), lambda i: (i, 0)),
                  pl.BlockSpec((1, 8, LANES), lambda i: (i, 0, 0)),
                  _const_spec((8, LANES))],
        out_specs=pl.BlockSpec((1, 8, tp), lambda i: (i, 0, 0)),
        out_shape=jax.ShapeDtypeStruct((n // tp, 8, tp), I32),
        scratch_shapes=[pltpu.VMEM((tp, tp), BF16)],
        compiler_params=pltpu.CompilerParams(dimension_semantics=("arbitrary",),
                                             vmem_limit_bytes=VMEM_LIMIT),
        name="positions",
    )(route, base, seg)
    return pos[:, 0:2, :]


def _plan(counts, tm, n_tiles):
    tiles_per = (counts + tm - 1) // tm
    tile_end = jnp.cumsum(tiles_per)
    seg_start = (tile_end - tiles_per) * tm
    n_used = tile_end[-1]
    tile_ids = jnp.arange(n_tiles, dtype=I32)
    tile_expert = jnp.sum((tile_ids[:, None] >= tile_end[None, :]).astype(I32), axis=1)
    last_used = jnp.sum((n_used - 1 >= tile_end).astype(I32))
    tile_expert = jnp.where(tile_ids < n_used, tile_expert, last_used).astype(I32)
    z_expert = jnp.where((counts % tm) != 0, (tile_end - 1) * tm, -1)
    spare = n_used + jnp.arange(N_EXPERTS, dtype=I32)
    z_spare = jnp.where(spare < n_tiles, spare * tm, -1)
    zoff = jnp.concatenate([z_expert, z_spare]).astype(I32)
    return seg_start, tile_expert, n_used.reshape(1).astype(I32), zoff


def kernel(x_prompt, x_sample, c_prompt, c_sample, state_mlstm_C, state_mlstm_n, state_mlstm_m,
           cache_conv, w_ada, b_ada, norm1_g, w_in, b_igate, b_fgate, mh_norm_g, w_dw, b_dw,
           conv_ln_g, conv_ln_b, w_out, norm2_g, w_grp_router, b_grp_router, w_exp_router,
           b_exp_router, w_gate, w_up, w_down, final_g):
    depth = w_ada.shape[0]
    assert depth == 1, "one layer per step"
    bp, tp_, d = x_prompt.shape
    bs = x_sample.shape[0]
    assert x_sample.shape[1] == 1 and d == ROW_SUB * LANES
    dk = state_mlstm_C.shape[-1]
    dm = N_HEADS * dk
    dc = d - dm
    l = 0

    win = w_in[l]
    w_qkvo = win[:, 0:4 * dm]
    w_gates = win[:, 4 * dm:4 * dm + 2 * N_HEADS]
    w_glu = win[:, 4 * dm + 2 * N_HEADS:]
    wmain, wmain_lo = _split_weights(jnp.concatenate([w_qkvo, w_glu], axis=1))
    wout_hi, wout_lo = _split_weights(w_out[l])
    wkt_hi, wkt_lo = _split_weights(win[:, dm:2 * dm].T)
    wg_pad = jnp.pad(w_gates, ((0, 0), (0, LANES - 2 * N_HEADS)))
    wg_hi, wg_lo = _split_weights(wg_pad)
    gbias = jnp.pad(jnp.concatenate([b_igate[l], b_fgate[l]]), (0, LANES - 2 * N_HEADS)).reshape(1, LANES)
    w_r = jnp.concatenate([w_grp_router[l], w_exp_router[l]], axis=1)
    n_r = N_GROUPS + N_EXPERTS
    wr_hi, wr_lo = _split_weights(jnp.pad(w_r, ((0, 0), (0, LANES - n_r))))
    rbias = jnp.pad(jnp.concatenate([b_grp_router[l], b_exp_router[l]]), (0, LANES - n_r)).reshape(1, LANES)
    wts = dict(dm=dm, g1=norm1_g[l].reshape(1, d), g2=norm2_g[l].reshape(1, d), wmain=wmain,
               wg_hi=wg_hi, wg_lo=wg_lo, gbias=gbias, mhg=mh_norm_g[l],
               wdw=w_dw[l].reshape(CONV_W, dc), bdw=b_dw[l].reshape(1, dc),
               clg=conv_ln_g[l].reshape(1, dc), clb=conv_ln_b[l].reshape(1, dc),
               wout=wout_hi, wr_hi=wr_hi, wr_lo=wr_lo, rbias=rbias, wk_t=wkt_hi,
               wmain_lo=wmain_lo, wout_lo=wout_lo, wk_t_lo=wkt_lo)

    mod_p, mod_s = _ada(c_prompt, c_sample, w_ada[l], b_ada[l])

    n_p = bp * tp_
    n_all = n_p + bs
    x1_p, h2_p, route_p, c_p, n_pr, m_p, cv_p, base_p, tot_p = _mix_prompt(x_prompt, mod_p, wts)
    x1_s, h2_s, route_s, c_s, n_s, m_s, u_s = _mix_sample(
        x_sample.reshape(bs, d), mod_s, state_mlstm_C[l], state_mlstm_n[l], state_mlstm_m[l],
        cache_conv[l], wts)

    tm = MOE_TILE
    n_tiles = (2 * n_all) // tm + N_EXPERTS
    tp_p, tp_s = min(PERM_TILE, tp_), min(PERM_TILE, bs)
    assert tp_p == min(MIX_TILE, tp_), "the prompt mixer counts experts per PERM_TILE tokens"
    base_s, tot_s = _count(route_s, tp_s)
    counts = (tot_p[0, :N_EXPERTS] + tot_s[0, :N_EXPERTS]).astype(I32)
    seg_start, tile_expert, n_used, zoff = _plan(counts, tm, n_tiles)
    pos_p = _positions(route_p, base_p, seg_start, tp_p)
    pos_s = _positions(route_s, base_s + tot_p[None], seg_start, tp_s)

    xs = _scatter_rows(h2_p, pos_p, zoff, n_tiles * tm, tm)
    xs = _scatter_rows(h2_s, pos_s, zoff, xs, tm)
    ys = _moe(xs, tile_expert, n_used, w_gate[l], w_up[l], w_down[l], tm)

    y_p = _finish(x1_p, route_p, mod_p.reshape(bp, 1, -1), final_g, ys, pos_p)
    y_s = _finish(x1_s.reshape(1, bs, d), route_s, mod_s.reshape(1, bs, -1), final_g, ys,
                  pos_s).reshape(bs, 1, d)

    conv_s = jnp.concatenate([cache_conv[l][:, 1:, :], u_s[:, None, :]], axis=1)
    return (y_p, y_s,
            c_p, n_pr, m_p[:, 0, :N_HEADS].reshape(1, bp, N_HEADS), cv_p,
            c_s[None], n_s.reshape(1, bs, N_HEADS, dk), m_s[:, :N_HEADS].reshape(1, bs, N_HEADS),
            conv_s[None])
```

```python
import functools

import jax
import jax.numpy as jnp
from jax import lax
from jax.experimental import pallas as pl
from jax.experimental.pallas import tpu as pltpu

F32 = jnp.float32
BF16 = jnp.bfloat16
I32 = jnp.int32

EPS = 1e-6
LANES = 128
ROW_SUB = 8
CHUNK = 128
N_HEADS = 4
N_GROUPS = 4
EXP_PER_GROUP = 8
N_EXPERTS = N_GROUPS * EXP_PER_GROUP
CONV_W = 31
CONV_PAD = 32
CONV_OFF = CONV_PAD - (CONV_W - 1)
MIX_TILE = 512
MIX_SUB = 256
MOE_TILE = 512
PERM_TILE = 512
FIN_CHUNK = 64
VMEM_LIMIT = 56 * 1024 * 1024


def _sigmoid(x):
    return 1.0 / (1.0 + jnp.exp(-x))


def _silu(x):
    return x * _sigmoid(x)


def _log_sigmoid(x):
    return jnp.minimum(x, 0.0) - jnp.log(1.0 + jnp.exp(-jnp.abs(x)))


def _bdot_nt(a, b):
    return lax.dot_general(a.astype(BF16), b.astype(BF16), (((1,), (1,)), ((), ())),
                           preferred_element_type=F32)


def _split2(x):
    hi = x.astype(BF16)
    lo = (x - hi.astype(F32)).astype(BF16)
    return hi, lo


def _split_kernel(w_ref, hi_ref, lo_ref):
    hi, lo = _split2(w_ref[...])
    hi_ref[...] = hi
    lo_ref[...] = lo


def _split_weights(w):
    rows, cols = w.shape
    blk = min(cols, 512)
    assert cols % blk == 0
    spec = pl.BlockSpec((rows, blk), lambda j: (0, j))
    return pl.pallas_call(
        _split_kernel, grid=(cols // blk,), in_specs=[spec], out_specs=[spec, spec],
        out_shape=[jax.ShapeDtypeStruct(w.shape, BF16)] * 2,
        compiler_params=pltpu.CompilerParams(dimension_semantics=("arbitrary",),
                                             vmem_limit_bytes=VMEM_LIMIT),
        name="split",
    )(w)


def _dot3(a, w_hi, w_lo):
    a_hi, a_lo = _split2(a)
    return (jnp.dot(a_hi, w_hi, preferred_element_type=F32)
            + jnp.dot(a_lo, w_hi, preferred_element_type=F32)
            + jnp.dot(a_hi, w_lo, preferred_element_type=F32))


def _cumsum_lanes(triu_bf16, x):
    hi = x.astype(BF16)
    r1 = x - hi.astype(F32)
    mid = r1.astype(BF16)
    lo = (r1 - mid.astype(F32)).astype(BF16)
    return (jnp.dot(hi, triu_bf16, preferred_element_type=F32)
            + jnp.dot(mid, triu_bf16, preferred_element_type=F32)
            + jnp.dot(lo, triu_bf16, preferred_element_type=F32))


def _rms(x, g):
    return x * lax.rsqrt(jnp.mean(x * x, axis=-1, keepdims=True) + EPS) * g


def _layer_norm(x, g, b=None):
    mu = jnp.mean(x, axis=-1, keepdims=True)
    xc = x - mu
    var = jnp.mean(xc * xc, axis=-1, keepdims=True)
    y = xc * lax.rsqrt(var + EPS) * g
    return y if b is None else y + b


def _store_rows(ref, x, row0=0):
    r = x.shape[0]
    for k in range(ROW_SUB):
        ref[pl.ds(row0 * ROW_SUB + k, r, stride=ROW_SUB), :] = x[:, k * LANES:(k + 1) * LANES]


def _load_rows(ref, r, row0=0):
    return jnp.concatenate([ref[pl.ds(row0 * ROW_SUB + k, r, stride=ROW_SUB), :]
                            for k in range(ROW_SUB)], axis=1)


def _route(logits):
    lane = lax.broadcasted_iota(I32, logits.shape, 1).astype(F32)
    neg = jnp.float32(-jnp.inf)
    big = jnp.float32(1e9)
    is_g = lane < N_GROUPS
    gl = jnp.where(is_g, logits, neg)
    gmax = jnp.max(gl, axis=1, keepdims=True)
    gsel = jnp.min(jnp.where(gl == gmax, lane, big), axis=1, keepdims=True)
    pg = 1.0 / jnp.sum(jnp.where(is_g, jnp.exp(gl - gmax), 0.0), axis=1, keepdims=True)
    lo = N_GROUPS + EXP_PER_GROUP * gsel
    emask = (lane >= lo) & (lane < lo + EXP_PER_GROUP)
    el = jnp.where(emask, logits, neg)
    v1 = jnp.max(el, axis=1, keepdims=True)
    i1 = jnp.min(jnp.where(el == v1, lane, big), axis=1, keepdims=True)
    el2 = jnp.where(lane == i1, neg, el)
    v2 = jnp.max(el2, axis=1, keepdims=True)
    i2 = jnp.min(jnp.where(el2 == v2, lane, big), axis=1, keepdims=True)
    d = jnp.exp(v2 - v1)
    w1 = pg / (1.0 + d)
    w2 = pg * d / (1.0 + d)
    return jnp.where(lane == 0, i1 - N_GROUPS,
                     jnp.where(lane == 1, i2 - N_GROUPS,
                               jnp.where(lane == 2, w1, jnp.where(lane == 3, w2, 0.0))))


def _ada_kernel(cp_ref, cs_ref, w_ref, b_ref, op_ref, os_ref):
    w_hi, w_lo = _split2(w_ref[...])
    op_ref[...] = _dot3(_silu(cp_ref[...]), w_hi, w_lo) + b_ref[...]
    os_ref[...] = _dot3(_silu(cs_ref[...]), w_hi, w_lo) + b_ref[...]


def _ada(c_p, c_s, w_ada, b_ada):
    (rp, d), rs = c_p.shape, c_s.shape[0]
    n_out = w_ada.shape[1]
    blk = 1024
    return pl.pallas_call(
        _ada_kernel,
        grid=(n_out // blk,),
        in_specs=[pl.BlockSpec((rp, d), lambda j: (0, 0)),
                  pl.BlockSpec((rs, d), lambda j: (0, 0)),
                  pl.BlockSpec((d, blk), lambda j: (0, j)),
                  pl.BlockSpec((1, blk), lambda j: (0, j))],
        out_specs=[pl.BlockSpec((rp, blk), lambda j: (0, j)),
                   pl.BlockSpec((rs, blk), lambda j: (0, j))],
        out_shape=[jax.ShapeDtypeStruct((rp, n_out), F32), jax.ShapeDtypeStruct((rs, n_out), F32)],
        compiler_params=pltpu.CompilerParams(dimension_semantics=("arbitrary",),
                                             vmem_limit_bytes=VMEM_LIMIT),
        name="ada",
    )(c_p, c_s, w_ada, b_ada.reshape(1, n_out))


def _post(x, attn_cat, mod, g2, wout, wr_hi, wr_lo, rbias, d, wout_lo=None):
    gate1 = mod[:, 2 * d:3 * d]
    sh2 = mod[:, 3 * d:4 * d]
    sc2 = mod[:, 4 * d:5 * d]
    if wout_lo is None:
        proj = jnp.dot(attn_cat.astype(BF16), wout, preferred_element_type=F32)
    else:
        proj = _dot3(attn_cat, wout, wout_lo)
    x1 = x + gate1 * proj
    h2 = _rms(x1, g2) * (1.0 + sc2) + sh2
    logits = _dot3(h2, wr_hi, wr_lo) + rbias
    return x1, h2, _route(logits)


def _mix_prompt_kernel(x_ref, mod_ref, g1_ref, g2_ref, wmain_ref, wgh_ref, wgl_ref, gbias_ref,
                       mhg_ref, wdw_ref, bdw_ref, clg_ref, clb_ref, wout_ref, wrh_ref, wrl_ref,
                       rbias_ref,
                       x1_ref, h2_ref, route_ref, c_ref, n_ref, m_ref, cv_ref, base_ref, tot_ref,
                       ubuf, yc_s, q_s, k_s, v_s, so_s, hm_s, p_s, u_s, cm_s, nb_s, m_s, cnt_s,
                       *, tt, d, dm, dk):
    t = pl.program_id(1)
    dc = d - dm
    n_lt = dc // LANES
    sub = min(MIX_SUB, tt)

    @pl.when(t == 0)
    def _():
        c_ref[...] = jnp.zeros_like(c_ref)
        nb_s[...] = jnp.zeros_like(nb_s)
        m_s[...] = jnp.zeros_like(m_s)
        ubuf[0:CONV_PAD, :] = jnp.zeros((CONV_PAD, dc), F32)

    mod = mod_ref[0]
    row8 = lax.broadcasted_iota(I32, (8, LANES), 0)
    row = lax.broadcasted_iota(I32, (CHUNK, CHUNK), 0)
    col = lax.broadcasted_iota(I32, (CHUNK, CHUNK), 1)
    causal = col <= row
    triu = (row <= col).astype(BF16)
    neg = jnp.float32(-jnp.inf)
    ones_b = jnp.ones((CHUNK, dk), BF16)
    pad_rows = jnp.zeros((CHUNK - 8, LANES), F32)

    lanes_of = [slice(lt * LANES, (lt + 1) * LANES) for lt in range(n_lt)]
    wrows = [[jnp.broadcast_to(wdw_ref[j:j + 1, ls], (8, LANES)) for j in range(CONV_W)]
             for ls in lanes_of]
    bias = [jnp.broadcast_to(bdw_ref[:, ls], (8, LANES)) for ls in lanes_of]

    def partial_sums(r0, lt):
        blocks = [ubuf[r0 + 8 * a:r0 + 8 * a + 8, lanes_of[lt]] for a in range(CONV_PAD // 8)]
        sums = []
        for s in range(8):
            acc = None
            for a in range(CONV_PAD // 8):
                j = 8 * a + s - CONV_OFF
                if 0 <= j < CONV_W:
                    term = blocks[a] * wrows[lt][j]
                    acc = term if acc is None else acc + term
            sums.append(acc)
        return tuple(sums)

    q_prev = [None] * n_lt
    ca = [jnp.concatenate([c_ref[0, 0, hd], nb_s[hd]], axis=1) for hd in range(N_HEADS)]
    m_prev = [m_s[hd:hd + 1, :] for hd in range(N_HEADS)]

    n_sub = tt // sub
    gates_of = {}
    hc_of = {}
    tile_counts = []

    def rows_of(sb):
        return slice(sb * sub, (sb + 1) * sub)

    def proj_items(sb):
        rs = rows_of(sb)
        st = {}

        def head():
            x = x_ref[0, rs, :]
            h = _rms(x, g1_ref[...]) * (1.0 + mod[:, d:2 * d]) + mod[:, 0:d]
            st["hb"] = h.astype(BF16)
            gates_of[sb] = _dot3(h, wgh_ref[...], wgl_ref[...]) + gbias_ref[...]

        def proj(lo, hi):
            return jnp.dot(st["hb"], wmain_ref[:, lo:hi], preferred_element_type=F32)

        def glu():
            ubuf[CONV_PAD + sb * sub:CONV_PAD + (sb + 1) * sub, :] = (
                proj(4 * dm, 4 * dm + dc) * _sigmoid(proj(4 * dm + dc, 4 * dm + 2 * dc)))

        def q():
            q_s[rs, :] = proj(0, dm).astype(BF16)

        def k():
            k_s[rs, :] = proj(dm, 2 * dm) * (dk ** -0.5)

        def v():
            v_s[rs, :] = proj(2 * dm, 3 * dm).astype(BF16)

        def o():
            so_s[rs, :] = _sigmoid(proj(3 * dm, 4 * dm))

        return [head, glu, q, k, v, o]

    def conv_items(sb):
        items = []
        for lt in range(n_lt):
            for i in range(sb * sub // 8 + 1, (sb + 1) * sub // 8 + 1):
                def block(lt=lt, i=i):
                    if i == 1:
                        q_prev[lt] = partial_sums(0, lt)
                    q_cur = partial_sums(i * 8, lt)
                    cur = ubuf[(i - 1) * 8 + CONV_PAD:i * 8 + CONV_PAD, lanes_of[lt]]
                    y = bias[lt] + q_prev[lt][0] + cur * wrows[lt][CONV_W - 1]
                    for s in range(1, 8):
                        merged = jnp.where(row8 < s, q_cur[s], q_prev[lt][s])
                        y = y + pltpu.roll(merged, 8 - s, 0)
                    yc_s[(i - 1) * 8:i * 8, lanes_of[lt]] = y
                    q_prev[lt] = q_cur
                items.append(block)
        return items

    def post_items(sb):
        def post():
            rs = rows_of(sb)
            cat = jnp.concatenate([hm_s[rs, :], hc_of[sb]], axis=1)
            x1, h2, route = _post(x_ref[0, rs, :], cat, mod, g2_ref[...], wout_ref[...],
                                  wrh_ref[...], wrl_ref[...], rbias_ref[...], d)
            x1_ref[0, rs, :] = x1
            _store_rows(h2_ref, h2, sb * sub)
            route_ref[rs, :] = route
            oh0, oh1 = _expert_onehots(route)
            tile_counts.append(jnp.sum(oh0 + oh1, axis=0, keepdims=True))
        return [post]

    def interleave(main, side):
        gap = len(main) / (len(side) + 1)
        due, done = gap, 0
        for n, item in enumerate(main):
            item()
            while done < len(side) and n + 1 >= due:
                side[done]()
                done += 1
                due += gap
        for item in side[done:]:
            item()

    for item in proj_items(0):
        item()
    for sb in range(n_sub):
        r_lo = sb * sub
        rs = rows_of(sb)
        side = post_items(sb - 1) if sb > 0 else []
        if sb + 1 < n_sub:
            side = side + proj_items(sb + 1)
        interleave(conv_items(sb), side)
        hc_of[sb] = _silu(_layer_norm(yc_s[rs, :], clg_ref[...], clb_ref[...]))
        gates = gates_of[sb]

        chunks = range(r_lo // CHUNK, (r_lo + sub) // CHUNK)
        b_cols = {}
        for c in chunks:
            r0 = c * CHUNK
            g8 = gates[r0 - r_lo:r0 - r_lo + CHUNK, :].T[0:8, :]
            b8 = _cumsum_lanes(triu, _log_sigmoid(g8))
            pk8 = jnp.where(row8 < N_HEADS, g8 - pltpu.roll(b8, N_HEADS, 0), b8)
            pk = jnp.concatenate([pk8, pad_rows], axis=0).T
            for hd in range(N_HEADS):
                cs = slice(hd * dk, (hd + 1) * dk)
                idx = c * N_HEADS + hd
                kf = k_s[r0:r0 + CHUNK, cs]
                va = jnp.concatenate([v_s[r0:r0 + CHUNK, cs], ones_b], axis=1)
                gm = jnp.where(causal, pk8[hd:hd + 1, :], neg)
                cm = jnp.max(gm, axis=1, keepdims=True)
                s = _bdot_nt(q_s[r0:r0 + CHUNK, cs], kf) * jnp.exp(gm - cm)
                p_s[idx] = jnp.dot(s.astype(BF16), va, preferred_element_type=F32)
                cm_s[idx] = jnp.broadcast_to(cm, (CHUNK, LANES))
                kw = kf * jnp.exp(pk[:, hd:hd + 1] - cm[CHUNK - 1:CHUNK, :])
                u_s[idx] = jnp.dot(kw.T.astype(BF16), va, preferred_element_type=F32)
                b_cols[idx] = pk[:, N_HEADS + hd:N_HEADS + hd + 1]

        for hd in range(N_HEADS):
            cs = slice(hd * dk, (hd + 1) * dk)
            for c in chunks:
                r0 = c * CHUNK
                idx = c * N_HEADS + hd
                cm = cm_s[idx]
                b_col = b_cols[idx]
                mt = jnp.maximum(m_prev[hd], cm)
                f_loc = jnp.exp(cm - mt)
                a_int = jnp.exp(m_prev[hd] - mt)
                qc = jnp.dot(q_s[r0:r0 + CHUNK, cs], ca[hd].astype(BF16),
                             preferred_element_type=F32)
                p = p_s[idx]
                num = f_loc * p[:, :dk] + a_int * qc[:, :dk]
                den = f_loc * p[:, dk:] + a_int * qc[:, dk:]
                hh = num / jnp.maximum(jnp.abs(den), jnp.exp(-(b_col + mt)))
                hm_s[r0:r0 + CHUNK, cs] = (_layer_norm(hh, mhg_ref[hd:hd + 1, :])
                                           * so_s[r0:r0 + CHUNK, cs])
                mt_l = mt[CHUNK - 1:CHUNK, :]
                u = u_s[idx]
                f_l = f_loc[CHUNK - 1:CHUNK, :]
                a_l = a_int[CHUNK - 1:CHUNK, :]
                ca[hd] = jnp.concatenate([a_l * ca[hd][:, :dk] + f_l * u[:, :dk],
                                          a_l * ca[hd][:, dk:] + f_l * u[:, dk:]], axis=1)
                m_prev[hd] = b_col[CHUNK - 1:CHUNK, :] + mt_l

    for item in post_items(n_sub - 1):
        item()

    for hd in range(N_HEADS):
        c_ref[0, 0, hd] = ca[hd][:, :dk]
        nb_s[hd] = ca[hd][:, dk:]
        m_s[hd:hd + 1, :] = m_prev[hd]
    cv_ref[0, 0] = ubuf[tt + CONV_PAD - (CONV_W - 1):tt + CONV_PAD, :]
    ubuf[0:CONV_PAD, :] = ubuf[tt:tt + CONV_PAD, :]

    @pl.when(t == pl.num_programs(1) - 1)
    def _():
        for hd in range(N_HEADS):
            n_ref[0, 0, hd:hd + 1, :] = nb_s[hd].T[0:1, :]

    lane1 = lax.broadcasted_iota(I32, (1, LANES), 1)
    m_row = jnp.zeros((1, LANES), F32)
    for hd in range(N_HEADS):
        m_row = jnp.where(lane1 == hd, m_s[hd:hd + 1, :], m_row)
    m_ref[0] = m_row

    @pl.when((pl.program_id(0) == 0) & (t == 0))
    def _():
        cnt_s[...] = jnp.zeros_like(cnt_s)

    base_ref[0] = cnt_s[...]
    cnt_s[...] = cnt_s[...] + sum(tile_counts)
    tot_ref[...] = cnt_s[...]


def _const_spec(shape):
    nd = len(shape)
    return pl.BlockSpec(shape, lambda *_: (0,) * nd)


def _mix_prompt(x, mod, wts):
    b, t, d = x.shape
    dm = wts["dm"]
    dk = dm // N_HEADS
    dc = d - dm
    tt = min(MIX_TILE, t)
    assert t % tt == 0 and tt % CHUNK == 0 and tt >= CONV_PAD
    nt = t // tt
    kern = functools.partial(_mix_prompt_kernel, tt=tt, d=d, dm=dm, dk=dk)
    const_names = ["g1", "g2", "wmain", "wg_hi", "wg_lo", "gbias", "mhg", "wdw", "bdw", "clg",
                   "clb", "wout", "wr_hi", "wr_lo", "rbias"]
    consts = [wts[k] for k in const_names]
    in_specs = ([pl.BlockSpec((1, tt, d), lambda i, j: (i, j, 0)),
                 pl.BlockSpec((1, 1, mod.shape[-1]), lambda i, j: (i, 0, 0))]
                + [_const_spec(c.shape) for c in consts])
    out_shape = [
        jax.ShapeDtypeStruct((b, t, d), F32),
        jax.ShapeDtypeStruct((b * t * ROW_SUB, LANES), F32),
        jax.ShapeDtypeStruct((b * t, LANES), F32),
        jax.ShapeDtypeStruct((1, b, N_HEADS, dk, dk), F32),
        jax.ShapeDtypeStruct((1, b, N_HEADS, dk), F32),
        jax.ShapeDtypeStruct((b, 1, LANES), F32),
        jax.ShapeDtypeStruct((1, b, CONV_W - 1, dc), F32),
        jax.ShapeDtypeStruct((b * nt, 8, LANES), F32),
        jax.ShapeDtypeStruct((8, LANES), F32),
    ]
    out_specs = [
        pl.BlockSpec((1, tt, d), lambda i, j: (i, j, 0)),
        pl.BlockSpec((tt * ROW_SUB, LANES), lambda i, j: (i * nt + j, 0)),
        pl.BlockSpec((tt, LANES), lambda i, j: (i * nt + j, 0)),
        pl.BlockSpec((1, 1, N_HEADS, dk, dk), lambda i, j: (0, i, 0, 0, 0)),
        pl.BlockSpec((1, 1, N_HEADS, dk), lambda i, j: (0, i, 0, 0)),
        pl.BlockSpec((1, 1, LANES), lambda i, j: (i, 0, 0)),
        pl.BlockSpec((1, 1, CONV_W - 1, dc), lambda i, j: (0, i, 0, 0)),
        pl.BlockSpec((1, 8, LANES), lambda i, j: (i * nt + j, 0, 0)),
        pl.BlockSpec((8, LANES), lambda i, j: (0, 0)),
    ]
    n_hc = (tt // CHUNK) * N_HEADS
    scratch = [pltpu.VMEM((tt + CONV_PAD, dc), F32),
               pltpu.VMEM((tt, dc), F32),
               pltpu.VMEM((tt, dm), BF16),
               pltpu.VMEM((tt, dm), F32),
               pltpu.VMEM((tt, dm), BF16),
               pltpu.VMEM((tt, dm), F32),
               pltpu.VMEM((tt, dm), F32),
               pltpu.VMEM((n_hc, CHUNK, 2 * dk), F32),
               pltpu.VMEM((n_hc, CHUNK, 2 * dk), F32),
               pltpu.VMEM((n_hc, CHUNK, LANES), F32),
               pltpu.VMEM((N_HEADS, dk, LANES), F32),
               pltpu.VMEM((8, LANES), F32),
               pltpu.VMEM((8, LANES), F32)]
    return pl.pallas_call(
        kern, grid=(b, nt), in_specs=in_specs, out_specs=out_specs, out_shape=out_shape,
        scratch_shapes=scratch,
        compiler_params=pltpu.CompilerParams(dimension_semantics=("arbitrary", "arbitrary"),
                                             vmem_limit_bytes=VMEM_LIMIT),
        name="mix_p",
    )(x, mod.reshape(b, 1, -1), *consts)


def _s_pre_kernel(x_ref, mod_ref, g1_ref, wmain_ref, wmainlo_ref, wgh_ref, wgl_ref, gbias_ref,
                  wkt_ref, wktlo_ref, wdw_ref, bdw_ref, clg_ref, clb_ref, cache_ref, n0_ref, m0_ref,
                  q_ref, kt_ref, vs_ref, ab_ref, sv_ref, den_ref, eb_ref, o_ref, hc_ref, u_ref,
                  n_ref, m_ref, *, d, dm, dk):
    dc = d - dm
    x = x_ref[...]
    mod = mod_ref[...]
    sh1 = mod[:, 0:d]
    sc1 = mod[:, d:2 * d]
    h = _rms(x, g1_ref[...]) * (1.0 + sc1) + sh1
    z = _dot3(h, wmain_ref[...], wmainlo_ref[...])
    gates = _dot3(h, wgh_ref[...], wgl_ref[...]) + gbias_ref[...]
    scale = dk ** -0.5
    h_hi, h_lo = _split2(h)
    kt = _bdot_nt(wkt_ref[...], h_hi) + _bdot_nt(wktlo_ref[...], h_hi) + _bdot_nt(wkt_ref[...], h_lo)
    kt_ref[...] = (kt * scale).astype(BF16)
    k_all = z[:, dm:2 * dm] * scale
    ga = z[:, 4 * dm:4 * dm + dc]
    gb = z[:, 4 * dm + dc:4 * dm + 2 * dc]
    u = ga * _sigmoid(gb)
    u_ref[...] = u
    acc = jnp.broadcast_to(bdw_ref[...], u.shape) + u * wdw_ref[CONV_W - 1:CONV_W, :]
    for j in range(CONV_W - 1):
        acc = acc + cache_ref[j] * wdw_ref[j:j + 1, :]
    hc_ref[...] = _silu(_layer_norm(acc, clg_ref[...], clb_ref[...]))
    o_ref[...] = z[:, 3 * dm:4 * dm]
    q_ref[...] = z[:, 0:dm]
    m0 = m0_ref[...]
    n0 = n0_ref[...]
    lane1 = lax.broadcasted_iota(I32, (1, LANES), 1)
    m_new = jnp.zeros(m0.shape, F32)
    for hd in range(N_HEADS):
        cs = slice(hd * dk, (hd + 1) * dk)
        ig = gates[:, hd:hd + 1]
        lf = _log_sigmoid(gates[:, N_HEADS + hd:N_HEADS + hd + 1])
        mp = m0[:, hd:hd + 1]
        inter = lf + mp
        mt = jnp.maximum(inter, ig)
        w = jnp.exp(ig - mt)
        a_int = jnp.exp(inter - mt)
        qf = z[:, cs]
        kf = k_all[:, cs]
        vf = z[:, 2 * dm + hd * dk:2 * dm + (hd + 1) * dk]
        s = jnp.sum(qf * kf, axis=1, keepdims=True) * w
        sv_ref[:, cs] = s * vf
        den_ref[:, cs] = jnp.broadcast_to(
            s + a_int * jnp.sum(qf * n0[:, cs], axis=1, keepdims=True), (x.shape[0], dk))
        eb_ref[:, cs] = jnp.broadcast_to(jnp.exp(-mt), (x.shape[0], dk))
        ab_ref[:, cs] = jnp.broadcast_to(a_int, (x.shape[0], dk))
        vs_ref[:, cs] = (vf * w).astype(BF16)
        n_ref[:, cs] = a_int * n0[:, cs] + w * kf
        m_new = jnp.where(lane1 == hd, mt, m_new)
    m_ref[...] = m_new


def _s_state_kernel(q_ref, kt_ref, vs_ref, ab_ref, c0_ref, c_ref, r_ref, *, bb, dk):
    i = pl.program_id(0)
    nb = q_ref.shape[0]
    rows = lax.broadcasted_iota(I32, (nb, dk), 0)

    @pl.when(i == 0)
    def _():
        r_ref[...] = jnp.zeros_like(r_ref)

    a_blk = ab_ref[pl.ds(pl.multiple_of(i * bb, bb), bb), :]
    for j in range(bb):
        sel = rows == i * bb + j
        for hd in range(N_HEADS):
            cs = slice(hd * dk, (hd + 1) * dk)
            c0 = c0_ref[j, hd]
            vmask = jnp.where(sel, vs_ref[:, cs], jnp.zeros((), BF16))
            c_ref[j, hd] = (a_blk[j:j + 1, cs] * c0
                            + jnp.dot(kt_ref[cs, :], vmask, preferred_element_type=F32))
            c_hi, c_lo = _split2(c0)
            q_hi, q_lo = _split2(q_ref[:, cs])
            r = (jnp.dot(q_hi, c_hi, preferred_element_type=F32)
                 + jnp.dot(q_lo, c_hi, preferred_element_type=F32)
                 + jnp.dot(q_hi, c_lo, preferred_element_type=F32))
            r_ref[:, cs] = r_ref[:, cs] + jnp.where(sel, r, 0.0)


def _s_post_kernel(x_ref, mod_ref, g2_ref, mhg_ref, r_ref, ab_ref, sv_ref, den_ref, eb_ref, o_ref,
                   hc_ref, wout_ref, woutlo_ref, wrh_ref, wrl_ref, rbias_ref,
                   x1_ref, h2_ref, route_ref, *, d, dm, dk):
    hm = []
    for hd in range(N_HEADS):
        cs = slice(hd * dk, (hd + 1) * dk)
        num = sv_ref[:, cs] + ab_ref[:, cs] * r_ref[:, cs]
        hh = num / jnp.maximum(jnp.abs(den_ref[:, cs]), eb_ref[:, cs])
        hm.append(_layer_norm(hh, mhg_ref[hd:hd + 1, :]) * _sigmoid(o_ref[:, cs]))
    cat = jnp.concatenate(hm + [hc_ref[...]], axis=1)
    x1, h2, route = _post(x_ref[...], cat, mod_ref[...], g2_ref[...], wout_ref[...], wrh_ref[...],
                          wrl_ref[...], rbias_ref[...], d, wout_lo=woutlo_ref[...])
    x1_ref[...] = x1
    _store_rows(h2_ref, h2)
    route_ref[...] = route


def _mix_sample(x, mod, c0, n0, m0, cache, wts):
    nb, d = x.shape
    dm = wts["dm"]
    dk = dm // N_HEADS
    dc = d - dm
    cp = pltpu.CompilerParams(dimension_semantics=("arbitrary",), vmem_limit_bytes=VMEM_LIMIT)
    cache_t = jnp.transpose(cache, (1, 0, 2))
    m0p = jnp.pad(m0, ((0, 0), (0, LANES - N_HEADS)))
    pre_in = [x, mod, wts["g1"], wts["wmain"], wts["wmain_lo"], wts["wg_hi"], wts["wg_lo"],
              wts["gbias"], wts["wk_t"], wts["wk_t_lo"], wts["wdw"], wts["bdw"], wts["clg"],
              wts["clb"], cache_t, n0.reshape(nb, dm), m0p]
    pre_out = [jax.ShapeDtypeStruct((nb, dm), F32),
               jax.ShapeDtypeStruct((dm, nb), BF16),
               jax.ShapeDtypeStruct((nb, dm), BF16),
               jax.ShapeDtypeStruct((nb, dm), F32),
               jax.ShapeDtypeStruct((nb, dm), F32),
               jax.ShapeDtypeStruct((nb, dm), F32),
               jax.ShapeDtypeStruct((nb, dm), F32),
               jax.ShapeDtypeStruct((nb, dm), F32),
               jax.ShapeDtypeStruct((nb, dc), F32),
               jax.ShapeDtypeStruct((nb, dc), F32),
               jax.ShapeDtypeStruct((nb, dm), F32),
               jax.ShapeDtypeStruct((nb, LANES), F32)]
    (q, kt, vs, ab, sv, den, eb, o, hc, u, n1, m1) = pl.pallas_call(
        functools.partial(_s_pre_kernel, d=d, dm=dm, dk=dk),
        grid=(1,),
        in_specs=[_const_spec(a.shape) for a in pre_in],
        out_specs=[_const_spec(s.shape) for s in pre_out],
        out_shape=pre_out, compiler_params=cp, name="s_pre")(*pre_in)

    bb = 8
    assert nb % bb == 0
    c1, r = pl.pallas_call(
        functools.partial(_s_state_kernel, bb=bb, dk=dk),
        grid=(nb // bb,),
        in_specs=[_const_spec(q.shape), _const_spec(kt.shape), _const_spec(vs.shape),
                  _const_spec(ab.shape),
                  pl.BlockSpec((bb, N_HEADS, dk, dk), lambda i: (i, 0, 0, 0))],
        out_specs=[pl.BlockSpec((bb, N_HEADS, dk, dk), lambda i: (i, 0, 0, 0)),
                   _const_spec((nb, dm))],
        out_shape=[jax.ShapeDtypeStruct((nb, N_HEADS, dk, dk), F32),
                   jax.ShapeDtypeStruct((nb, dm), F32)],
        compiler_params=cp, name="s_state")(q, kt, vs, ab, c0)

    post_in = [x, mod, wts["g2"], wts["mhg"], r, ab, sv, den, eb, o, hc, wts["wout"],
               wts["wout_lo"], wts["wr_hi"], wts["wr_lo"], wts["rbias"]]
    post_out = [jax.ShapeDtypeStruct((nb, d), F32),
                jax.ShapeDtypeStruct((nb * ROW_SUB, LANES), F32),
                jax.ShapeDtypeStruct((nb, LANES), F32)]
    x1, h2, route = pl.pallas_call(
        functools.partial(_s_post_kernel, d=d, dm=dm, dk=dk),
        grid=(1,),
        in_specs=[_const_spec(a.shape) for a in post_in],
        out_specs=[_const_spec(s.shape) for s in post_out],
        out_shape=post_out, compiler_params=cp, name="s_post")(*post_in)
    return x1, h2, route, c1, n1, m1, u


def _scatter_kernel(pos_ref, zoff_ref, src_ref, *rest, tp, tm, create):
    xs_ref, zbuf, sem, zsem = rest[-4:]
    i = pl.program_id(0)

    if create:
        @pl.when(i == 0)
        def _():
            zbuf[...] = jnp.zeros_like(zbuf)

            def zero_copy(e):
                start = pl.multiple_of(zoff_ref[e] * ROW_SUB, tm * ROW_SUB)
                return pltpu.make_async_copy(zbuf, xs_ref.at[pl.ds(start, tm * ROW_SUB)], zsem)

            for e in range(zoff_ref.shape[0]):
                @pl.when(zoff_ref[e] >= 0)
                def _():
                    zero_copy(e).start()
            for e in range(zoff_ref.shape[0]):
                @pl.when(zoff_ref[e] >= 0)
                def _():
                    zero_copy(e).wait()

    for r in range(tp):
        src = src_ref.at[pl.ds(r * ROW_SUB, ROW_SUB)]
        for slot in range(2):
            dst = pl.multiple_of(pos_ref[0, slot, r], ROW_SUB)
            pltpu.make_async_copy(src, xs_ref.at[pl.ds(dst, ROW_SUB)], sem).start(priority=slot)
    for slot in range(2):
        pltpu.make_async_copy(src_ref, xs_ref.at[pl.ds(0, tp * ROW_SUB)], sem).wait()


def _scatter_rows(h2, pos3, zoff, xs_or_rows, tm):
    n, c = h2.shape[0] // ROW_SUB, LANES
    tp = pos3.shape[2]
    assert pos3.shape == (n // tp, 2, tp)
    create = isinstance(xs_or_rows, int)
    n_sorted = xs_or_rows * ROW_SUB if create else xs_or_rows.shape[0]
    in_specs = [pl.BlockSpec((1, 2, tp), lambda i: (i, 0, 0), memory_space=pltpu.SMEM),
                pl.BlockSpec(memory_space=pltpu.SMEM),
                pl.BlockSpec((tp * ROW_SUB, c), lambda i: (i, 0))]
    args = [pos3, zoff, h2]
    if not create:
        in_specs.append(pl.BlockSpec(memory_space=pl.ANY))
        args.append(xs_or_rows)
    return pl.pallas_call(
        functools.partial(_scatter_kernel, tp=tp, tm=tm, create=create),
        grid_spec=pltpu.PrefetchScalarGridSpec(
            num_scalar_prefetch=0,
            grid=(n // tp,),
            in_specs=in_specs,
            out_specs=pl.BlockSpec(memory_space=pl.ANY),
            scratch_shapes=[pltpu.VMEM((tm * ROW_SUB, c), F32), pltpu.SemaphoreType.DMA(()),
                            pltpu.SemaphoreType.DMA(())]),
        out_shape=jax.ShapeDtypeStruct((n_sorted, c), F32),
        input_output_aliases={} if create else {3: 0},
        compiler_params=pltpu.CompilerParams(dimension_semantics=("arbitrary",),
                                             vmem_limit_bytes=VMEM_LIMIT),
        name="scatter",
    )(*args)


def _moe_kernel(te_ref, nu_ref, sg_ref, nx_ref, xs_ref, wg_hbm, wu_hbm, wd_hbm, ys_ref,
                wg_f, wu_f, wd_f, wg_b, wu_b, wd_b, sem):
    i = pl.program_id(0)
    used = i < nu_ref[0]
    first = used & ((i == 0) | (te_ref[i] != te_ref[jnp.maximum(i - 1, 0)]))
    buf = sg_ref[i] % 2

    def weight_copies(expert, b):
        return [pltpu.make_async_copy(hbm.at[expert], vmem.at[b], sem.at[b])
                for hbm, vmem in ((wg_hbm, wg_f), (wu_hbm, wu_f), (wd_hbm, wd_f))]

    @pl.when(used & (i == 0))
    def _():
        for cp in weight_copies(te_ref[0], 0):
            cp.start()

    @pl.when(first)
    def _():
        for cp in weight_copies(te_ref[i], buf):
            cp.wait()
        wg_b[...] = wg_f[buf].astype(BF16)
        wu_b[...] = wu_f[buf].astype(BF16)
        wd_b[...] = wd_f[buf].astype(BF16)

        @pl.when(nx_ref[i] >= 0)
        def _():
            for cp in weight_copies(nx_ref[i], 1 - buf):
                cp.start()

    @pl.when(used)
    def _():
        xb = _load_rows(xs_ref, xs_ref.shape[0] // ROW_SUB).astype(BF16)
        g = jnp.dot(xb, wg_b[...], preferred_element_type=F32)
        u = jnp.dot(xb, wu_b[...], preferred_element_type=F32)
        hid = (_silu(g) * u).astype(BF16)
        _store_rows(ys_ref, jnp.dot(hid, wd_b[...], preferred_element_type=F32))

    @pl.when(jnp.logical_not(used))
    def _():
        ys_ref[...] = jnp.zeros_like(ys_ref)


def _moe(xs, tile_expert, n_used, seg_index, next_expert, w_gate, w_up, w_down, tm):
    p, c = xs.shape[0] // ROW_SUB, LANES
    ne, d, de = w_gate.shape
    n_tiles = p // tm

    def x_map(i, te, nu, sg, nx):
        return (jnp.minimum(i, jnp.maximum(nu[0] - 1, 0)), 0)

    return pl.pallas_call(
        _moe_kernel,
        grid_spec=pltpu.PrefetchScalarGridSpec(
            num_scalar_prefetch=4,
            grid=(n_tiles,),
            in_specs=[pl.BlockSpec((tm * ROW_SUB, c), x_map),
                      pl.BlockSpec(memory_space=pl.ANY),
                      pl.BlockSpec(memory_space=pl.ANY),
                      pl.BlockSpec(memory_space=pl.ANY)],
            out_specs=pl.BlockSpec((tm * ROW_SUB, c), lambda i, te, nu, sg, nx: (i, 0)),
            scratch_shapes=[pltpu.VMEM((2, d, de), F32), pltpu.VMEM((2, d, de), F32),
                            pltpu.VMEM((2, de, d), F32),
                            pltpu.VMEM((d, de), BF16), pltpu.VMEM((d, de), BF16),
                            pltpu.VMEM((de, d), BF16), pltpu.SemaphoreType.DMA((2,))]),
        out_shape=jax.ShapeDtypeStruct((p * ROW_SUB, c), F32),
        compiler_params=pltpu.CompilerParams(dimension_semantics=("arbitrary",),
                                             vmem_limit_bytes=VMEM_LIMIT),
        name="moe",
    )(tile_expert, n_used, seg_index, next_expert, xs, w_gate, w_up, w_down)


def _fin_kernel(pos_ref, posn_ref, x1_ref, route_ref, mod_ref, fg_ref, ys_ref, y_ref, ybuf, sem,
                *, tp, d, n_steps):
    i = pl.program_id(0)
    cur = i % 2

    def issue(p_ref, buf):
        def row_start(r, carry):
            dst = pl.ds(pl.multiple_of(r * ROW_SUB, ROW_SUB), ROW_SUB)
            for slot in range(2):
                src = pl.multiple_of(p_ref[0, slot, r], ROW_SUB)
                pltpu.make_async_copy(ys_ref.at[pl.ds(src, ROW_SUB)],
                                      ybuf.at[buf, slot, dst], sem.at[buf]).start(priority=slot)
            return carry
        lax.fori_loop(0, tp, row_start, 0, unroll=8)

    def wait_buf(buf):
        for slot in range(2):
            pltpu.make_async_copy(ys_ref.at[pl.ds(0, tp * ROW_SUB)], ybuf.at[buf, slot],
                                  sem.at[buf]).wait()

    @pl.when(i == 0)
    def _():
        issue(pos_ref, 0)

    wait_buf(cur)
    nxt = 1 - cur
    per_token_mod = mod_ref.shape[1] != 1
    for lo in range(0, tp, FIN_CHUNK):
        hi = min(lo + FIN_CHUNK, tp)
        for r in range(lo, hi):
            for slot in range(2):
                src = pl.multiple_of(posn_ref[0, slot, r], ROW_SUB)
                pltpu.make_async_copy(
                    ys_ref.at[pl.ds(src, ROW_SUB)],
                    ybuf.at[nxt, slot, pl.ds(r * ROW_SUB, ROW_SUB)], sem.at[nxt]).start(priority=slot)
        route = route_ref[lo:hi, :]
        moe = (route[:, 2:3] * _load_rows(ybuf.at[cur, 0], hi - lo, lo)
               + route[:, 3:4] * _load_rows(ybuf.at[cur, 1], hi - lo, lo))
        mod = mod_ref[0, lo:hi, :] if per_token_mod else mod_ref[0]
        y_ref[0, lo:hi, :] = _rms(x1_ref[0, lo:hi, :] + mod[:, 5 * d:6 * d] * moe, fg_ref[...])

    @pl.when(i == n_steps - 1)
    def _():
        wait_buf(nxt)


def _finish(x1, route, mod, final_g, ys, pos3):
    b, t, d = x1.shape
    tp = pos3.shape[2]
    assert t % tp == 0
    nt = t // tp
    n_steps = b * nt
    assert pos3.shape[0] == n_steps
    blk0 = 0
    if mod.shape[1] == 1:
        mod_spec = pl.BlockSpec((1, 1, 6 * d), lambda i: (i // nt, 0, 0))
    else:
        mod_spec = pl.BlockSpec((1, tp, 6 * d), lambda i: (i // nt, i % nt, 0))
    return pl.pallas_call(
        functools.partial(_fin_kernel, tp=tp, d=d, n_steps=n_steps),
        grid_spec=pltpu.PrefetchScalarGridSpec(
            num_scalar_prefetch=0,
            grid=(n_steps,),
            in_specs=[pl.BlockSpec((1, 2, tp), lambda i: (blk0 + i, 0, 0), memory_space=pltpu.SMEM),
                      pl.BlockSpec((1, 2, tp), lambda i: (blk0 + jnp.minimum(i + 1, n_steps - 1), 0, 0),
                                   memory_space=pltpu.SMEM),
                      pl.BlockSpec((1, tp, d), lambda i: (i // nt, i % nt, 0)),
                      pl.BlockSpec((tp, LANES), lambda i: (i, 0)),
                      mod_spec,
                      _const_spec((1, d)),
                      pl.BlockSpec(memory_space=pl.ANY)],
            out_specs=pl.BlockSpec((1, tp, d), lambda i: (i // nt, i % nt, 0)),
            scratch_shapes=[pltpu.VMEM((2, 2, tp * ROW_SUB, LANES), F32),
                            pltpu.SemaphoreType.DMA((2,))]),
        out_shape=jax.ShapeDtypeStruct((b, t, d), F32),
        compiler_params=pltpu.CompilerParams(dimension_semantics=("arbitrary",),
                                             vmem_limit_bytes=VMEM_LIMIT),
        name="fin",
    )(pos3, pos3, x1, route, mod, final_g.reshape(1, d), ys)


def _expert_onehots(route):
    lane = lax.broadcasted_iota(I32, route.shape, 1).astype(F32)
    return ((lane == route[:, 0:1]).astype(F32), (lane == route[:, 1:2]).astype(F32))


def _count_kernel(route_ref, base_ref, tot_ref, acc):
    @pl.when(pl.program_id(0) == 0)
    def _():
        acc[...] = jnp.zeros_like(acc)

    oh0, oh1 = _expert_onehots(route_ref[...])
    base_ref[0] = acc[...]
    acc[...] = acc[...] + jnp.sum(oh0 + oh1, axis=0, keepdims=True)
    tot_ref[...] = acc[...]


def _pos_kernel(route_ref, base_ref, seg_ref, pos_ref, tril_s, *, tp):
    @pl.when(pl.program_id(0) == 0)
    def _():
        r_i = lax.broadcasted_iota(I32, (tp, tp), 0)
        c_i = lax.broadcasted_iota(I32, (tp, tp), 1)
        tril_s[...] = (c_i < r_i).astype(BF16)

    oh0, oh1 = _expert_onehots(route_ref[...])
    before = jnp.dot(tril_s[...], (oh0 + oh1).astype(BF16), preferred_element_type=F32)
    offs = seg_ref[0:1, :] + base_ref[0][0:1, :] + before
    ones8 = jnp.ones((8, LANES), BF16)

    def lane_sum_as_row(v):
        hi = v.astype(BF16)
        r1 = v - hi.astype(F32)
        mid = r1.astype(BF16)
        lo = (r1 - mid.astype(F32)).astype(BF16)
        return _bdot_nt(ones8, hi) + _bdot_nt(ones8, mid) + _bdot_nt(ones8, lo)

    p0 = lane_sum_as_row(oh0 * offs)
    p1 = lane_sum_as_row(oh1 * offs)
    row = lax.broadcasted_iota(I32, (8, tp), 0)
    pos_ref[0] = (jnp.where(row == 0, p0, jnp.where(row == 1, p1, 0.0)) * ROW_SUB).astype(I32)


def _count(route, tp):
    n = route.shape[0]
    assert n % tp == 0
    return pl.pallas_call(
        _count_kernel, grid=(n // tp,),
        in_specs=[pl.BlockSpec((tp, LANES), lambda i: (i, 0))],
        out_specs=[pl.BlockSpec((1, 8, LANES), lambda i: (i, 0, 0)), _const_spec((8, LANES))],
        out_shape=[jax.ShapeDtypeStruct((n // tp, 8, LANES), F32),
                   jax.ShapeDtypeStruct((8, LANES), F32)],
        scratch_shapes=[pltpu.VMEM((8, LANES), F32)],
        compiler_params=pltpu.CompilerParams(dimension_semantics=("arbitrary",),
                                             vmem_limit_bytes=VMEM_LIMIT),
        name="count",
    )(route)


def _positions(route, base, seg_start, tp):
    n = route.shape[0]
    seg = jnp.broadcast_to(jnp.pad(seg_start.astype(F32), (0, LANES - N_EXPERTS))[None, :], (8, LANES))
    pos = pl.pallas_call(
        functools.partial(_pos_kernel, tp=tp), grid=(n // tp,),
        in_specs=[pl.BlockSpec((tp, LANES), lambda i: (i, 0)),
                  pl.BlockSpec((1, 8, LANES), lambda i: (i, 0, 0)),
                  _const_spec((8, LANES))],
        out_specs=pl.BlockSpec((1, 8, tp), lambda i: (i, 0, 0)),
        out_shape=jax.ShapeDtypeStruct((n // tp, 8, tp), I32),
        scratch_shapes=[pltpu.VMEM((tp, tp), BF16)],
        compiler_params=pltpu.CompilerParams(dimension_semantics=("arbitrary",),
                                             vmem_limit_bytes=VMEM_LIMIT),
        name="positions",
    )(route, base, seg)
    return pos[:, 0:2, :]


def _plan(counts, tm, n_tiles):
    tiles_per = (counts + tm - 1) // tm
    tile_end = jnp.cumsum(tiles_per)
    seg_start = (tile_end - tiles_per) * tm
    n_used = tile_end[-1]
    tile_ids = jnp.arange(n_tiles, dtype=I32)
    tile_expert = jnp.sum((tile_ids[:, None] >= tile_end[None, :]).astype(I32), axis=1)
    last_used = jnp.sum((n_used - 1 >= tile_end).astype(I32))
    tile_expert = jnp.where(tile_ids < n_used, tile_expert, last_used).astype(I32)
    z_expert = jnp.where((counts % tm) != 0, (tile_end - 1) * tm, -1)
    spare = n_used + jnp.arange(N_EXPERTS, dtype=I32)
    z_spare = jnp.where(spare < n_tiles, spare * tm, -1)
    zoff = jnp.concatenate([z_expert, z_spare]).astype(I32)
    experts = jnp.arange(N_EXPERTS, dtype=I32)
    nonempty = tiles_per > 0
    seg_of_expert = jnp.cumsum(nonempty.astype(I32)) - 1
    later = nonempty[None, :] & (experts[None, :] > experts[:, None])
    next_of_expert = jnp.min(jnp.where(later, experts[None, :], N_EXPERTS), axis=1)
    next_of_expert = jnp.where(next_of_expert < N_EXPERTS, next_of_expert, -1)
    seg_index = seg_of_expert[tile_expert].astype(I32)
    next_expert = next_of_expert[tile_expert].astype(I32)
    return seg_start, tile_expert, n_used.reshape(1).astype(I32), zoff, seg_index, next_expert


def kernel(x_prompt, x_sample, c_prompt, c_sample, state_mlstm_C, state_mlstm_n, state_mlstm_m,
           cache_conv, w_ada, b_ada, norm1_g, w_in, b_igate, b_fgate, mh_norm_g, w_dw, b_dw,
           conv_ln_g, conv_ln_b, w_out, norm2_g, w_grp_router, b_grp_router, w_exp_router,
           b_exp_router, w_gate, w_up, w_down, final_g):
    depth = w_ada.shape[0]
    assert depth == 1, "one layer per step"
    bp, tp_, d = x_prompt.shape
    bs = x_sample.shape[0]
    assert x_sample.shape[1] == 1 and d == ROW_SUB * LANES
    dk = state_mlstm_C.shape[-1]
    dm = N_HEADS * dk
    dc = d - dm
    l = 0

    win = w_in[l]
    w_qkvo = win[:, 0:4 * dm]
    w_gates = win[:, 4 * dm:4 * dm + 2 * N_HEADS]
    w_glu = win[:, 4 * dm + 2 * N_HEADS:]
    wmain, wmain_lo = _split_weights(jnp.concatenate([w_qkvo, w_glu], axis=1))
    wout_hi, wout_lo = _split_weights(w_out[l])
    wkt_hi, wkt_lo = _split_weights(win[:, dm:2 * dm].T)
    wg_pad = jnp.pad(w_gates, ((0, 0), (0, LANES - 2 * N_HEADS)))
    wg_hi, wg_lo = _split_weights(wg_pad)
    gbias = jnp.pad(jnp.concatenate([b_igate[l], b_fgate[l]]), (0, LANES - 2 * N_HEADS)).reshape(1, LANES)
    w_r = jnp.concatenate([w_grp_router[l], w_exp_router[l]], axis=1)
    n_r = N_GROUPS + N_EXPERTS
    wr_hi, wr_lo = _split_weights(jnp.pad(w_r, ((0, 0), (0, LANES - n_r))))
    rbias = jnp.pad(jnp.concatenate([b_grp_router[l], b_exp_router[l]]), (0, LANES - n_r)).reshape(1, LANES)
    wts = dict(dm=dm, g1=norm1_g[l].reshape(1, d), g2=norm2_g[l].reshape(1, d), wmain=wmain,
               wg_hi=wg_hi, wg_lo=wg_lo, gbias=gbias, mhg=mh_norm_g[l],
               wdw=w_dw[l].reshape(CONV_W, dc), bdw=b_dw[l].reshape(1, dc),
               clg=conv_ln_g[l].reshape(1, dc), clb=conv_ln_b[l].reshape(1, dc),
               wout=wout_hi, wr_hi=wr_hi, wr_lo=wr_lo, rbias=rbias, wk_t=wkt_hi,
               wmain_lo=wmain_lo, wout_lo=wout_lo, wk_t_lo=wkt_lo)

    mod_p, mod_s = _ada(c_prompt, c_sample, w_ada[l], b_ada[l])

    n_p = bp * tp_
    n_all = n_p + bs
    x1_p, h2_p, route_p, c_p, n_pr, m_p, cv_p, base_p, tot_p = _mix_prompt(x_prompt, mod_p, wts)
    x1_s, h2_s, route_s, c_s, n_s, m_s, u_s = _mix_sample(
        x_sample.reshape(bs, d), mod_s, state_mlstm_C[l], state_mlstm_n[l], state_mlstm_m[l],
        cache_conv[l], wts)

    tm = MOE_TILE
    n_tiles = (2 * n_all) // tm + N_EXPERTS
    tp_p, tp_s = min(PERM_TILE, tp_), min(PERM_TILE, bs)
    assert tp_p == min(MIX_TILE, tp_), "the prompt mixer counts experts per PERM_TILE tokens"
    base_s, tot_s = _count(route_s, tp_s)
    counts = (tot_p[0, :N_EXPERTS] + tot_s[0, :N_EXPERTS]).astype(I32)
    seg_start, tile_expert, n_used, zoff, seg_index, next_expert = _plan(counts, tm, n_tiles)
    pos_p = _positions(route_p, base_p, seg_start, tp_p)
    pos_s = _positions(route_s, base_s + tot_p[None], seg_start, tp_s)

    xs = _scatter_rows(h2_p, pos_p, zoff, n_tiles * tm, tm)
    xs = _scatter_rows(h2_s, pos_s, zoff, xs, tm)
    ys = _moe(xs, tile_expert, n_used, seg_index, next_expert, w_gate[l], w_up[l], w_down[l], tm)

    y_p = _finish(x1_p, route_p, mod_p.reshape(bp, 1, -1), final_g, ys, pos_p)
    y_s = _finish(x1_s.reshape(1, bs, d), route_s, mod_s.reshape(1, bs, -1), final_g, ys,
                  pos_s).reshape(bs, 1, d)

    conv_s = jnp.concatenate([cache_conv[l][:, 1:, :], u_s[:, None, :]], axis=1)
    return (y_p, y_s,
            c_p, n_pr, m_p[:, 0, :N_HEADS].reshape(1, bp, N_HEADS), cv_p,
            c_s[None], n_s.reshape(1, bs, N_HEADS, dk), m_s[:, :N_HEADS].reshape(1, bs, N_HEADS),
            conv_s[None])
```

```python
import functools

import jax
import jax.numpy as jnp
from jax import lax
from jax.experimental import pallas as pl
from jax.experimental.pallas import tpu as pltpu

F32 = jnp.float32
BF16 = jnp.bfloat16
I32 = jnp.int32

EPS = 1e-6
LANES = 128
ROW_SUB = 8
CHUNK = 128
N_HEADS = 4
N_GROUPS = 4
EXP_PER_GROUP = 8
N_EXPERTS = N_GROUPS * EXP_PER_GROUP
CONV_W = 31
CONV_PAD = 32
CONV_OFF = CONV_PAD - (CONV_W - 1)
MIX_TILE = 512
MIX_SUB = 256
MOE_TILE = 512
PERM_TILE = 512
FIN_CHUNK = 64
ZERO_CHUNK = 64
VMEM_LIMIT = 56 * 1024 * 1024


def _sigmoid(x):
    return 1.0 / (1.0 + jnp.exp(-x))


def _silu(x):
    return x * _sigmoid(x)


def _log_sigmoid(x):
    return jnp.minimum(x, 0.0) - jnp.log(1.0 + jnp.exp(-jnp.abs(x)))


def _bdot_nt(a, b):
    return lax.dot_general(a.astype(BF16), b.astype(BF16), (((1,), (1,)), ((), ())),
                           preferred_element_type=F32)


def _split2(x):
    hi = x.astype(BF16)
    lo = (x - hi.astype(F32)).astype(BF16)
    return hi, lo


def _split_kernel(w_ref, hi_ref, lo_ref):
    hi, lo = _split2(w_ref[...])
    hi_ref[...] = hi
    lo_ref[...] = lo


def _split_weights(w):
    rows, cols = w.shape
    blk = min(cols, 512)
    assert cols % blk == 0
    spec = pl.BlockSpec((rows, blk), lambda j: (0, j))
    return pl.pallas_call(
        _split_kernel, grid=(cols // blk,), in_specs=[spec], out_specs=[spec, spec],
        out_shape=[jax.ShapeDtypeStruct(w.shape, BF16)] * 2,
        compiler_params=pltpu.CompilerParams(dimension_semantics=("arbitrary",),
                                             vmem_limit_bytes=VMEM_LIMIT),
        name="split",
    )(w)


def _dot3(a, w_hi, w_lo):
    a_hi, a_lo = _split2(a)
    return (jnp.dot(a_hi, w_hi, preferred_element_type=F32)
            + jnp.dot(a_lo, w_hi, preferred_element_type=F32)
            + jnp.dot(a_hi, w_lo, preferred_element_type=F32))


def _cumsum_lanes(triu_bf16, x):
    hi = x.astype(BF16)
    r1 = x - hi.astype(F32)
    mid = r1.astype(BF16)
    lo = (r1 - mid.astype(F32)).astype(BF16)
    return (jnp.dot(hi, triu_bf16, preferred_element_type=F32)
            + jnp.dot(mid, triu_bf16, preferred_element_type=F32)
            + jnp.dot(lo, triu_bf16, preferred_element_type=F32))


def _rms(x, g):
    return x * lax.rsqrt(jnp.mean(x * x, axis=-1, keepdims=True) + EPS) * g


def _layer_norm(x, g, b=None):
    mu = jnp.mean(x, axis=-1, keepdims=True)
    xc = x - mu
    var = jnp.mean(xc * xc, axis=-1, keepdims=True)
    y = xc * lax.rsqrt(var + EPS) * g
    return y if b is None else y + b


def _store_rows(ref, x, row0=0):
    r = x.shape[0]
    for k in range(ROW_SUB):
        ref[pl.ds(row0 * ROW_SUB + k, r, stride=ROW_SUB), :] = x[:, k * LANES:(k + 1) * LANES]


def _load_rows(ref, r, row0=0):
    return jnp.concatenate([ref[pl.ds(row0 * ROW_SUB + k, r, stride=ROW_SUB), :]
                            for k in range(ROW_SUB)], axis=1)


def _route(logits):
    lane = lax.broadcasted_iota(I32, logits.shape, 1).astype(F32)
    neg = jnp.float32(-jnp.inf)
    big = jnp.float32(1e9)
    is_g = lane < N_GROUPS
    gl = jnp.where(is_g, logits, neg)
    gmax = jnp.max(gl, axis=1, keepdims=True)
    gsel = jnp.min(jnp.where(gl == gmax, lane, big), axis=1, keepdims=True)
    pg = 1.0 / jnp.sum(jnp.where(is_g, jnp.exp(gl - gmax), 0.0), axis=1, keepdims=True)
    lo = N_GROUPS + EXP_PER_GROUP * gsel
    emask = (lane >= lo) & (lane < lo + EXP_PER_GROUP)
    el = jnp.where(emask, logits, neg)
    v1 = jnp.max(el, axis=1, keepdims=True)
    i1 = jnp.min(jnp.where(el == v1, lane, big), axis=1, keepdims=True)
    el2 = jnp.where(lane == i1, neg, el)
    v2 = jnp.max(el2, axis=1, keepdims=True)
    i2 = jnp.min(jnp.where(el2 == v2, lane, big), axis=1, keepdims=True)
    d = jnp.exp(v2 - v1)
    w1 = pg / (1.0 + d)
    w2 = pg * d / (1.0 + d)
    return jnp.where(lane == 0, i1 - N_GROUPS,
                     jnp.where(lane == 1, i2 - N_GROUPS,
                               jnp.where(lane == 2, w1, jnp.where(lane == 3, w2, 0.0))))


def _ada_kernel(cp_ref, cs_ref, w_ref, b_ref, op_ref, os_ref):
    w_hi, w_lo = _split2(w_ref[...])
    op_ref[...] = _dot3(_silu(cp_ref[...]), w_hi, w_lo) + b_ref[...]
    os_ref[...] = _dot3(_silu(cs_ref[...]), w_hi, w_lo) + b_ref[...]


def _ada(c_p, c_s, w_ada, b_ada):
    (rp, d), rs = c_p.shape, c_s.shape[0]
    n_out = w_ada.shape[1]
    blk = 1024
    return pl.pallas_call(
        _ada_kernel,
        grid=(n_out // blk,),
        in_specs=[pl.BlockSpec((rp, d), lambda j: (0, 0)),
                  pl.BlockSpec((rs, d), lambda j: (0, 0)),
                  pl.BlockSpec((d, blk), lambda j: (0, j)),
                  pl.BlockSpec((1, blk), lambda j: (0, j))],
        out_specs=[pl.BlockSpec((rp, blk), lambda j: (0, j)),
                   pl.BlockSpec((rs, blk), lambda j: (0, j))],
        out_shape=[jax.ShapeDtypeStruct((rp, n_out), F32), jax.ShapeDtypeStruct((rs, n_out), F32)],
        compiler_params=pltpu.CompilerParams(dimension_semantics=("arbitrary",),
                                             vmem_limit_bytes=VMEM_LIMIT),
        name="ada",
    )(c_p, c_s, w_ada, b_ada.reshape(1, n_out))


def _post(x, attn_cat, mod, g2, wout, wr_hi, wr_lo, rbias, d, wout_lo=None):
    gate1 = mod[:, 2 * d:3 * d]
    sh2 = mod[:, 3 * d:4 * d]
    sc2 = mod[:, 4 * d:5 * d]
    if wout_lo is None:
        proj = jnp.dot(attn_cat.astype(BF16), wout, preferred_element_type=F32)
    else:
        proj = _dot3(attn_cat, wout, wout_lo)
    x1 = x + gate1 * proj
    h2 = _rms(x1, g2) * (1.0 + sc2) + sh2
    logits = _dot3(h2, wr_hi, wr_lo) + rbias
    return x1, h2, _route(logits)


def _mix_prompt_kernel(x_ref, mod_ref, g1_ref, g2_ref, wmain_ref, wgh_ref, wgl_ref, gbias_ref,
                       mhg_ref, wdw_ref, bdw_ref, clg_ref, clb_ref, wout_ref, wrh_ref, wrl_ref,
                       rbias_ref,
                       x1_ref, h2_ref, route_ref, c_ref, n_ref, m_ref, cv_ref, base_ref, tot_ref,
                       ubuf, yc_s, q_s, k_s, v_s, so_s, hm_s, p_s, u_s, cm_s, nb_s, m_s, cnt_s,
                       *, tt, d, dm, dk):
    t = pl.program_id(1)
    dc = d - dm
    n_lt = dc // LANES
    sub = min(MIX_SUB, tt)

    @pl.when(t == 0)
    def _():
        c_ref[...] = jnp.zeros_like(c_ref)
        nb_s[...] = jnp.zeros_like(nb_s)
        m_s[...] = jnp.zeros_like(m_s)
        ubuf[0:CONV_PAD, :] = jnp.zeros((CONV_PAD, dc), F32)

    mod = mod_ref[0]
    row8 = lax.broadcasted_iota(I32, (8, LANES), 0)
    row = lax.broadcasted_iota(I32, (CHUNK, CHUNK), 0)
    col = lax.broadcasted_iota(I32, (CHUNK, CHUNK), 1)
    causal = col <= row
    triu = (row <= col).astype(BF16)
    neg = jnp.float32(-jnp.inf)
    ones_b = jnp.ones((CHUNK, dk), BF16)
    pad_rows = jnp.zeros((CHUNK - 8, LANES), F32)

    lanes_of = [slice(lt * LANES, (lt + 1) * LANES) for lt in range(n_lt)]
    wrows = [[jnp.broadcast_to(wdw_ref[j:j + 1, ls], (8, LANES)) for j in range(CONV_W)]
             for ls in lanes_of]
    bias = [jnp.broadcast_to(bdw_ref[:, ls], (8, LANES)) for ls in lanes_of]

    def partial_sums(r0, lt):
        blocks = [ubuf[r0 + 8 * a:r0 + 8 * a + 8, lanes_of[lt]] for a in range(CONV_PAD // 8)]
        sums = []
        for s in range(8):
            acc = None
            for a in range(CONV_PAD // 8):
                j = 8 * a + s - CONV_OFF
                if 0 <= j < CONV_W:
                    term = blocks[a] * wrows[lt][j]
                    acc = term if acc is None else acc + term
            sums.append(acc)
        return tuple(sums)

    q_prev = [None] * n_lt
    ca = [jnp.concatenate([c_ref[0, 0, hd], nb_s[hd]], axis=1) for hd in range(N_HEADS)]
    m_prev = [m_s[hd:hd + 1, :] for hd in range(N_HEADS)]

    n_sub = tt // sub
    gates_of = {}
    hc_of = {}
    tile_counts = []

    def rows_of(sb):
        return slice(sb * sub, (sb + 1) * sub)

    def proj_items(sb):
        rs = rows_of(sb)
        st = {}

        def head():
            x = x_ref[0, rs, :]
            h = _rms(x, g1_ref[...]) * (1.0 + mod[:, d:2 * d]) + mod[:, 0:d]
            st["hb"] = h.astype(BF16)
            gates_of[sb] = _dot3(h, wgh_ref[...], wgl_ref[...]) + gbias_ref[...]

        def proj(lo, hi):
            return jnp.dot(st["hb"], wmain_ref[:, lo:hi], preferred_element_type=F32)

        def glu():
            ubuf[CONV_PAD + sb * sub:CONV_PAD + (sb + 1) * sub, :] = (
                proj(4 * dm, 4 * dm + dc) * _sigmoid(proj(4 * dm + dc, 4 * dm + 2 * dc)))

        def q():
            q_s[rs, :] = proj(0, dm).astype(BF16)

        def k():
            k_s[rs, :] = proj(dm, 2 * dm) * (dk ** -0.5)

        def v():
            v_s[rs, :] = proj(2 * dm, 3 * dm).astype(BF16)

        def o():
            so_s[rs, :] = _sigmoid(proj(3 * dm, 4 * dm))

        return [head, glu, q, k, v, o]

    def conv_items(sb):
        items = []
        for lt in range(n_lt):
            for i in range(sb * sub // 8 + 1, (sb + 1) * sub // 8 + 1):
                def block(lt=lt, i=i):
                    if i == 1:
                        q_prev[lt] = partial_sums(0, lt)
                    q_cur = partial_sums(i * 8, lt)
                    cur = ubuf[(i - 1) * 8 + CONV_PAD:i * 8 + CONV_PAD, lanes_of[lt]]
                    y = bias[lt] + q_prev[lt][0] + cur * wrows[lt][CONV_W - 1]
                    for s in range(1, 8):
                        merged = jnp.where(row8 < s, q_cur[s], q_prev[lt][s])
                        y = y + pltpu.roll(merged, 8 - s, 0)
                    yc_s[(i - 1) * 8:i * 8, lanes_of[lt]] = y
                    q_prev[lt] = q_cur
                items.append(block)
        return items

    def post_items(sb):
        def post():
            rs = rows_of(sb)
            cat = jnp.concatenate([hm_s[rs, :], hc_of[sb]], axis=1)
            x1, h2, route = _post(x_ref[0, rs, :], cat, mod, g2_ref[...], wout_ref[...],
                                  wrh_ref[...], wrl_ref[...], rbias_ref[...], d)
            x1_ref[0, rs, :] = x1
            _store_rows(h2_ref, h2, sb * sub)
            route_ref[rs, :] = route
            oh0, oh1 = _expert_onehots(route)
            tile_counts.append(jnp.sum(oh0 + oh1, axis=0, keepdims=True))
        return [post]

    def interleave(main, side):
        gap = len(main) / (len(side) + 1)
        due, done = gap, 0
        for n, item in enumerate(main):
            item()
            while done < len(side) and n + 1 >= due:
                side[done]()
                done += 1
                due += gap
        for item in side[done:]:
            item()

    for item in proj_items(0):
        item()
    for sb in range(n_sub):
        r_lo = sb * sub
        rs = rows_of(sb)
        side = post_items(sb - 1) if sb > 0 else []
        if sb + 1 < n_sub:
            side = side + proj_items(sb + 1)
        interleave(conv_items(sb), side)
        hc_of[sb] = _silu(_layer_norm(yc_s[rs, :], clg_ref[...], clb_ref[...]))
        gates = gates_of[sb]

        chunks = range(r_lo // CHUNK, (r_lo + sub) // CHUNK)
        b_cols = {}
        for c in chunks:
            r0 = c * CHUNK
            g8 = gates[r0 - r_lo:r0 - r_lo + CHUNK, :].T[0:8, :]
            b8 = _cumsum_lanes(triu, _log_sigmoid(g8))
            pk8 = jnp.where(row8 < N_HEADS, g8 - pltpu.roll(b8, N_HEADS, 0), b8)
            pk = jnp.concatenate([pk8, pad_rows], axis=0).T
            for hd in range(N_HEADS):
                cs = slice(hd * dk, (hd + 1) * dk)
                idx = c * N_HEADS + hd
                kf = k_s[r0:r0 + CHUNK, cs]
                va = jnp.concatenate([v_s[r0:r0 + CHUNK, cs], ones_b], axis=1)
                gm = jnp.where(causal, pk8[hd:hd + 1, :], neg)
                cm = jnp.max(gm, axis=1, keepdims=True)
                s = _bdot_nt(q_s[r0:r0 + CHUNK, cs], kf) * jnp.exp(gm - cm)
                p_s[idx] = jnp.dot(s.astype(BF16), va, preferred_element_type=F32)
                cm_s[idx] = jnp.broadcast_to(cm, (CHUNK, LANES))
                kw = kf * jnp.exp(pk[:, hd:hd + 1] - cm[CHUNK - 1:CHUNK, :])
                u_s[idx] = jnp.dot(kw.T.astype(BF16), va, preferred_element_type=F32)
                b_cols[idx] = pk[:, N_HEADS + hd:N_HEADS + hd + 1]

        for hd in range(N_HEADS):
            cs = slice(hd * dk, (hd + 1) * dk)
            for c in chunks:
                r0 = c * CHUNK
                idx = c * N_HEADS + hd
                cm = cm_s[idx]
                b_col = b_cols[idx]
                mt = jnp.maximum(m_prev[hd], cm)
                f_loc = jnp.exp(cm - mt)
                a_int = jnp.exp(m_prev[hd] - mt)
                qc = jnp.dot(q_s[r0:r0 + CHUNK, cs], ca[hd].astype(BF16),
                             preferred_element_type=F32)
                p = p_s[idx]
                num = f_loc * p[:, :dk] + a_int * qc[:, :dk]
                den = f_loc * p[:, dk:] + a_int * qc[:, dk:]
                hh = num / jnp.maximum(jnp.abs(den), jnp.exp(-(b_col + mt)))
                hm_s[r0:r0 + CHUNK, cs] = (_layer_norm(hh, mhg_ref[hd:hd + 1, :])
                                           * so_s[r0:r0 + CHUNK, cs])
                mt_l = mt[CHUNK - 1:CHUNK, :]
                u = u_s[idx]
                f_l = f_loc[CHUNK - 1:CHUNK, :]
                a_l = a_int[CHUNK - 1:CHUNK, :]
                ca[hd] = jnp.concatenate([a_l * ca[hd][:, :dk] + f_l * u[:, :dk],
                                          a_l * ca[hd][:, dk:] + f_l * u[:, dk:]], axis=1)
                m_prev[hd] = b_col[CHUNK - 1:CHUNK, :] + mt_l

    for item in post_items(n_sub - 1):
        item()

    for hd in range(N_HEADS):
        c_ref[0, 0, hd] = ca[hd][:, :dk]
        nb_s[hd] = ca[hd][:, dk:]
        m_s[hd:hd + 1, :] = m_prev[hd]
    cv_ref[0, 0] = ubuf[tt + CONV_PAD - (CONV_W - 1):tt + CONV_PAD, :]
    ubuf[0:CONV_PAD, :] = ubuf[tt:tt + CONV_PAD, :]

    @pl.when(t == pl.num_programs(1) - 1)
    def _():
        for hd in range(N_HEADS):
            n_ref[0, 0, hd:hd + 1, :] = nb_s[hd].T[0:1, :]

    lane1 = lax.broadcasted_iota(I32, (1, LANES), 1)
    m_row = jnp.zeros((1, LANES), F32)
    for hd in range(N_HEADS):
        m_row = jnp.where(lane1 == hd, m_s[hd:hd + 1, :], m_row)
    m_ref[0] = m_row

    @pl.when((pl.program_id(0) == 0) & (t == 0))
    def _():
        cnt_s[...] = jnp.zeros_like(cnt_s)

    base_ref[0] = cnt_s[...]
    cnt_s[...] = cnt_s[...] + sum(tile_counts)
    tot_ref[...] = cnt_s[...]


def _const_spec(shape):
    nd = len(shape)
    return pl.BlockSpec(shape, lambda *_: (0,) * nd)


def _mix_prompt(x, mod, wts):
    b, t, d = x.shape
    dm = wts["dm"]
    dk = dm // N_HEADS
    dc = d - dm
    tt = min(MIX_TILE, t)
    assert t % tt == 0 and tt % CHUNK == 0 and tt >= CONV_PAD
    nt = t // tt
    kern = functools.partial(_mix_prompt_kernel, tt=tt, d=d, dm=dm, dk=dk)
    const_names = ["g1", "g2", "wmain", "wg_hi", "wg_lo", "gbias", "mhg", "wdw", "bdw", "clg",
                   "clb", "wout", "wr_hi", "wr_lo", "rbias"]
    consts = [wts[k] for k in const_names]
    in_specs = ([pl.BlockSpec((1, tt, d), lambda i, j: (i, j, 0)),
                 pl.BlockSpec((1, 1, mod.shape[-1]), lambda i, j: (i, 0, 0))]
                + [_const_spec(c.shape) for c in consts])
    out_shape = [
        jax.ShapeDtypeStruct((b, t, d), F32),
        jax.ShapeDtypeStruct((b * t * ROW_SUB, LANES), F32),
        jax.ShapeDtypeStruct((b * t, LANES), F32),
        jax.ShapeDtypeStruct((1, b, N_HEADS, dk, dk), F32),
        jax.ShapeDtypeStruct((1, b, N_HEADS, dk), F32),
        jax.ShapeDtypeStruct((b, 1, LANES), F32),
        jax.ShapeDtypeStruct((1, b, CONV_W - 1, dc), F32),
        jax.ShapeDtypeStruct((b * nt, 8, LANES), F32),
        jax.ShapeDtypeStruct((8, LANES), F32),
    ]
    out_specs = [
        pl.BlockSpec((1, tt, d), lambda i, j: (i, j, 0)),
        pl.BlockSpec((tt * ROW_SUB, LANES), lambda i, j: (i * nt + j, 0)),
        pl.BlockSpec((tt, LANES), lambda i, j: (i * nt + j, 0)),
        pl.BlockSpec((1, 1, N_HEADS, dk, dk), lambda i, j: (0, i, 0, 0, 0)),
        pl.BlockSpec((1, 1, N_HEADS, dk), lambda i, j: (0, i, 0, 0)),
        pl.BlockSpec((1, 1, LANES), lambda i, j: (i, 0, 0)),
        pl.BlockSpec((1, 1, CONV_W - 1, dc), lambda i, j: (0, i, 0, 0)),
        pl.BlockSpec((1, 8, LANES), lambda i, j: (i * nt + j, 0, 0)),
        pl.BlockSpec((8, LANES), lambda i, j: (0, 0)),
    ]
    n_hc = (tt // CHUNK) * N_HEADS
    scratch = [pltpu.VMEM((tt + CONV_PAD, dc), F32),
               pltpu.VMEM((tt, dc), F32),
               pltpu.VMEM((tt, dm), BF16),
               pltpu.VMEM((tt, dm), F32),
               pltpu.VMEM((tt, dm), BF16),
               pltpu.VMEM((tt, dm), F32),
               pltpu.VMEM((tt, dm), F32),
               pltpu.VMEM((n_hc, CHUNK, 2 * dk), F32),
               pltpu.VMEM((n_hc, CHUNK, 2 * dk), F32),
               pltpu.VMEM((n_hc, CHUNK, LANES), F32),
               pltpu.VMEM((N_HEADS, dk, LANES), F32),
               pltpu.VMEM((8, LANES), F32),
               pltpu.VMEM((8, LANES), F32)]
    return pl.pallas_call(
        kern, grid=(b, nt), in_specs=in_specs, out_specs=out_specs, out_shape=out_shape,
        scratch_shapes=scratch,
        compiler_params=pltpu.CompilerParams(dimension_semantics=("arbitrary", "arbitrary"),
                                             vmem_limit_bytes=VMEM_LIMIT),
        name="mix_p",
    )(x, mod.reshape(b, 1, -1), *consts)


def _s_pre_kernel(x_ref, mod_ref, g1_ref, wmain_ref, wmainlo_ref, wgh_ref, wgl_ref, gbias_ref,
                  wkt_ref, wktlo_ref, wdw_ref, bdw_ref, clg_ref, clb_ref, cache_ref, n0_ref, m0_ref,
                  q_ref, kt_ref, vs_ref, ab_ref, sv_ref, den_ref, eb_ref, o_ref, hc_ref, u_ref,
                  n_ref, m_ref, *, d, dm, dk):
    dc = d - dm
    x = x_ref[...]
    mod = mod_ref[...]
    sh1 = mod[:, 0:d]
    sc1 = mod[:, d:2 * d]
    h = _rms(x, g1_ref[...]) * (1.0 + sc1) + sh1
    z = _dot3(h, wmain_ref[...], wmainlo_ref[...])
    gates = _dot3(h, wgh_ref[...], wgl_ref[...]) + gbias_ref[...]
    scale = dk ** -0.5
    h_hi, h_lo = _split2(h)
    kt = _bdot_nt(wkt_ref[...], h_hi) + _bdot_nt(wktlo_ref[...], h_hi) + _bdot_nt(wkt_ref[...], h_lo)
    kt_ref[...] = (kt * scale).astype(BF16)
    k_all = z[:, dm:2 * dm] * scale
    ga = z[:, 4 * dm:4 * dm + dc]
    gb = z[:, 4 * dm + dc:4 * dm + 2 * dc]
    u = ga * _sigmoid(gb)
    u_ref[...] = u
    acc = jnp.broadcast_to(bdw_ref[...], u.shape) + u * wdw_ref[CONV_W - 1:CONV_W, :]
    for j in range(CONV_W - 1):
        acc = acc + cache_ref[j] * wdw_ref[j:j + 1, :]
    hc_ref[...] = _silu(_layer_norm(acc, clg_ref[...], clb_ref[...]))
    o_ref[...] = z[:, 3 * dm:4 * dm]
    q_ref[...] = z[:, 0:dm]
    m0 = m0_ref[...]
    n0 = n0_ref[...]
    lane1 = lax.broadcasted_iota(I32, (1, LANES), 1)
    m_new = jnp.zeros(m0.shape, F32)
    for hd in range(N_HEADS):
        cs = slice(hd * dk, (hd + 1) * dk)
        ig = gates[:, hd:hd + 1]
        lf = _log_sigmoid(gates[:, N_HEADS + hd:N_HEADS + hd + 1])
        mp = m0[:, hd:hd + 1]
        inter = lf + mp
        mt = jnp.maximum(inter, ig)
        w = jnp.exp(ig - mt)
        a_int = jnp.exp(inter - mt)
        qf = z[:, cs]
        kf = k_all[:, cs]
        vf = z[:, 2 * dm + hd * dk:2 * dm + (hd + 1) * dk]
        s = jnp.sum(qf * kf, axis=1, keepdims=True) * w
        sv_ref[:, cs] = s * vf
        den_ref[:, cs] = jnp.broadcast_to(
            s + a_int * jnp.sum(qf * n0[:, cs], axis=1, keepdims=True), (x.shape[0], dk))
        eb_ref[:, cs] = jnp.broadcast_to(jnp.exp(-mt), (x.shape[0], dk))
        ab_ref[:, cs] = jnp.broadcast_to(a_int, (x.shape[0], dk))
        vs_ref[:, cs] = (vf * w).astype(BF16)
        n_ref[:, cs] = a_int * n0[:, cs] + w * kf
        m_new = jnp.where(lane1 == hd, mt, m_new)
    m_ref[...] = m_new


def _s_state_kernel(q_ref, kt_ref, vs_ref, ab_ref, c0_ref, c_ref, r_ref, *, bb, dk):
    i = pl.program_id(0)
    nb = q_ref.shape[0]
    rows = lax.broadcasted_iota(I32, (nb, dk), 0)

    @pl.when(i == 0)
    def _():
        r_ref[...] = jnp.zeros_like(r_ref)

    a_blk = ab_ref[pl.ds(pl.multiple_of(i * bb, bb), bb), :]
    for j in range(bb):
        sel = rows == i * bb + j
        for hd in range(N_HEADS):
            cs = slice(hd * dk, (hd + 1) * dk)
            c0 = c0_ref[j, hd]
            vmask = jnp.where(sel, vs_ref[:, cs], jnp.zeros((), BF16))
            c_ref[j, hd] = (a_blk[j:j + 1, cs] * c0
                            + jnp.dot(kt_ref[cs, :], vmask, preferred_element_type=F32))
            c_hi, c_lo = _split2(c0)
            q_hi, q_lo = _split2(q_ref[:, cs])
            r = (jnp.dot(q_hi, c_hi, preferred_element_type=F32)
                 + jnp.dot(q_lo, c_hi, preferred_element_type=F32)
                 + jnp.dot(q_hi, c_lo, preferred_element_type=F32))
            r_ref[:, cs] = r_ref[:, cs] + jnp.where(sel, r, 0.0)


def _s_post_kernel(x_ref, mod_ref, g2_ref, mhg_ref, r_ref, ab_ref, sv_ref, den_ref, eb_ref, o_ref,
                   hc_ref, wout_ref, woutlo_ref, wrh_ref, wrl_ref, rbias_ref,
                   x1_ref, h2_ref, route_ref, *, d, dm, dk):
    hm = []
    for hd in range(N_HEADS):
        cs = slice(hd * dk, (hd + 1) * dk)
        num = sv_ref[:, cs] + ab_ref[:, cs] * r_ref[:, cs]
        hh = num / jnp.maximum(jnp.abs(den_ref[:, cs]), eb_ref[:, cs])
        hm.append(_layer_norm(hh, mhg_ref[hd:hd + 1, :]) * _sigmoid(o_ref[:, cs]))
    cat = jnp.concatenate(hm + [hc_ref[...]], axis=1)
    x1, h2, route = _post(x_ref[...], cat, mod_ref[...], g2_ref[...], wout_ref[...], wrh_ref[...],
                          wrl_ref[...], rbias_ref[...], d, wout_lo=woutlo_ref[...])
    x1_ref[...] = x1
    _store_rows(h2_ref, h2)
    route_ref[...] = route


def _mix_sample(x, mod, c0, n0, m0, cache, wts):
    nb, d = x.shape
    dm = wts["dm"]
    dk = dm // N_HEADS
    dc = d - dm
    cp = pltpu.CompilerParams(dimension_semantics=("arbitrary",), vmem_limit_bytes=VMEM_LIMIT)
    cache_t = jnp.transpose(cache, (1, 0, 2))
    m0p = jnp.pad(m0, ((0, 0), (0, LANES - N_HEADS)))
    pre_in = [x, mod, wts["g1"], wts["wmain"], wts["wmain_lo"], wts["wg_hi"], wts["wg_lo"],
              wts["gbias"], wts["wk_t"], wts["wk_t_lo"], wts["wdw"], wts["bdw"], wts["clg"],
              wts["clb"], cache_t, n0.reshape(nb, dm), m0p]
    pre_out = [jax.ShapeDtypeStruct((nb, dm), F32),
               jax.ShapeDtypeStruct((dm, nb), BF16),
               jax.ShapeDtypeStruct((nb, dm), BF16),
               jax.ShapeDtypeStruct((nb, dm), F32),
               jax.ShapeDtypeStruct((nb, dm), F32),
               jax.ShapeDtypeStruct((nb, dm), F32),
               jax.ShapeDtypeStruct((nb, dm), F32),
               jax.ShapeDtypeStruct((nb, dm), F32),
               jax.ShapeDtypeStruct((nb, dc), F32),
               jax.ShapeDtypeStruct((nb, dc), F32),
               jax.ShapeDtypeStruct((nb, dm), F32),
               jax.ShapeDtypeStruct((nb, LANES), F32)]
    (q, kt, vs, ab, sv, den, eb, o, hc, u, n1, m1) = pl.pallas_call(
        functools.partial(_s_pre_kernel, d=d, dm=dm, dk=dk),
        grid=(1,),
        in_specs=[_const_spec(a.shape) for a in pre_in],
        out_specs=[_const_spec(s.shape) for s in pre_out],
        out_shape=pre_out, compiler_params=cp, name="s_pre")(*pre_in)

    bb = 8
    assert nb % bb == 0
    c1, r = pl.pallas_call(
        functools.partial(_s_state_kernel, bb=bb, dk=dk),
        grid=(nb // bb,),
        in_specs=[_const_spec(q.shape), _const_spec(kt.shape), _const_spec(vs.shape),
                  _const_spec(ab.shape),
                  pl.BlockSpec((bb, N_HEADS, dk, dk), lambda i: (i, 0, 0, 0))],
        out_specs=[pl.BlockSpec((bb, N_HEADS, dk, dk), lambda i: (i, 0, 0, 0)),
                   _const_spec((nb, dm))],
        out_shape=[jax.ShapeDtypeStruct((nb, N_HEADS, dk, dk), F32),
                   jax.ShapeDtypeStruct((nb, dm), F32)],
        compiler_params=cp, name="s_state")(q, kt, vs, ab, c0)

    post_in = [x, mod, wts["g2"], wts["mhg"], r, ab, sv, den, eb, o, hc, wts["wout"],
               wts["wout_lo"], wts["wr_hi"], wts["wr_lo"], wts["rbias"]]
    post_out = [jax.ShapeDtypeStruct((nb, d), F32),
                jax.ShapeDtypeStruct((nb * ROW_SUB, LANES), F32),
                jax.ShapeDtypeStruct((nb, LANES), F32)]
    x1, h2, route = pl.pallas_call(
        functools.partial(_s_post_kernel, d=d, dm=dm, dk=dk),
        grid=(1,),
        in_specs=[_const_spec(a.shape) for a in post_in],
        out_specs=[_const_spec(s.shape) for s in post_out],
        out_shape=post_out, compiler_params=cp, name="s_post")(*post_in)
    return x1, h2, route, c1, n1, m1, u


def _scatter_kernel(pos_ref, zoff_ref, src_ref, *rest, tp, tm, create):
    xs_ref, zbuf, sem, zsem = rest[-4:]
    i = pl.program_id(0)

    if create:
        @pl.when(i == 0)
        def _():
            zbuf[...] = jnp.zeros_like(zbuf)

            def zero_copy(k):
                start = pl.multiple_of(zoff_ref[k] * ROW_SUB, ZERO_CHUNK * ROW_SUB)
                return pltpu.make_async_copy(zbuf, xs_ref.at[pl.ds(start, ZERO_CHUNK * ROW_SUB)], zsem)

            def start(k, carry):
                @pl.when(zoff_ref[k] >= 0)
                def _():
                    zero_copy(k).start()
                return carry

            def wait(k, carry):
                @pl.when(zoff_ref[k] >= 0)
                def _():
                    zero_copy(k).wait()
                return carry

            lax.fori_loop(0, zoff_ref.shape[0], start, 0)
            lax.fori_loop(0, zoff_ref.shape[0], wait, 0)

    for r in range(tp):
        src = src_ref.at[pl.ds(r * ROW_SUB, ROW_SUB)]
        for slot in range(2):
            dst = pl.multiple_of(pos_ref[0, slot, r], ROW_SUB)
            pltpu.make_async_copy(src, xs_ref.at[pl.ds(dst, ROW_SUB)], sem).start(priority=slot)
    for slot in range(2):
        pltpu.make_async_copy(src_ref, xs_ref.at[pl.ds(0, tp * ROW_SUB)], sem).wait()


def _scatter_rows(h2, pos3, zoff, xs_or_rows, tm):
    n, c = h2.shape[0] // ROW_SUB, LANES
    tp = pos3.shape[2]
    assert pos3.shape == (n // tp, 2, tp)
    create = isinstance(xs_or_rows, int)
    n_sorted = xs_or_rows * ROW_SUB if create else xs_or_rows.shape[0]
    in_specs = [pl.BlockSpec((1, 2, tp), lambda i: (i, 0, 0), memory_space=pltpu.SMEM),
                pl.BlockSpec(memory_space=pltpu.SMEM),
                pl.BlockSpec((tp * ROW_SUB, c), lambda i: (i, 0))]
    args = [pos3, zoff, h2]
    if not create:
        in_specs.append(pl.BlockSpec(memory_space=pl.ANY))
        args.append(xs_or_rows)
    return pl.pallas_call(
        functools.partial(_scatter_kernel, tp=tp, tm=tm, create=create),
        grid_spec=pltpu.PrefetchScalarGridSpec(
            num_scalar_prefetch=0,
            grid=(n // tp,),
            in_specs=in_specs,
            out_specs=pl.BlockSpec(memory_space=pl.ANY),
            scratch_shapes=[pltpu.VMEM((ZERO_CHUNK * ROW_SUB, c), F32), pltpu.SemaphoreType.DMA(()),
                            pltpu.SemaphoreType.DMA(())]),
        out_shape=jax.ShapeDtypeStruct((n_sorted, c), F32),
        input_output_aliases={} if create else {3: 0},
        compiler_params=pltpu.CompilerParams(dimension_semantics=("arbitrary",),
                                             vmem_limit_bytes=VMEM_LIMIT),
        name="scatter",
    )(*args)


def _moe_kernel(te_ref, nu_ref, sg_ref, nx_ref, xs_ref, wg_hbm, wu_hbm, wd_hbm, ys_ref,
                wg_f, wu_f, wd_f, wg_b, wu_b, wd_b, sem):
    i = pl.program_id(0)
    used = i < nu_ref[0]
    first = used & ((i == 0) | (te_ref[i] != te_ref[jnp.maximum(i - 1, 0)]))
    buf = sg_ref[i] % 2

    def weight_copies(expert, b):
        return [pltpu.make_async_copy(hbm.at[expert], vmem.at[b], sem.at[b])
                for hbm, vmem in ((wg_hbm, wg_f), (wu_hbm, wu_f), (wd_hbm, wd_f))]

    @pl.when(used & (i == 0))
    def _():
        for cp in weight_copies(te_ref[0], 0):
            cp.start()

    @pl.when(first)
    def _():
        for cp in weight_copies(te_ref[i], buf):
            cp.wait()
        wg_b[...] = wg_f[buf].astype(BF16)
        wu_b[...] = wu_f[buf].astype(BF16)
        wd_b[...] = wd_f[buf].astype(BF16)

        @pl.when(nx_ref[i] >= 0)
        def _():
            for cp in weight_copies(nx_ref[i], 1 - buf):
                cp.start()

    @pl.when(used)
    def _():
        xb = _load_rows(xs_ref, xs_ref.shape[0] // ROW_SUB).astype(BF16)
        g = jnp.dot(xb, wg_b[...], preferred_element_type=F32)
        u = jnp.dot(xb, wu_b[...], preferred_element_type=F32)
        hid = (_silu(g) * u).astype(BF16)
        _store_rows(ys_ref, jnp.dot(hid, wd_b[...], preferred_element_type=F32))

    @pl.when(jnp.logical_not(used))
    def _():
        ys_ref[...] = jnp.zeros_like(ys_ref)


def _moe(xs, tile_expert, n_used, seg_index, next_expert, w_gate, w_up, w_down, tm):
    p, c = xs.shape[0] // ROW_SUB, LANES
    ne, d, de = w_gate.shape
    n_tiles = p // tm

    def x_map(i, te, nu, sg, nx):
        return (jnp.minimum(i, jnp.maximum(nu[0] - 1, 0)), 0)

    return pl.pallas_call(
        _moe_kernel,
        grid_spec=pltpu.PrefetchScalarGridSpec(
            num_scalar_prefetch=4,
            grid=(n_tiles,),
            in_specs=[pl.BlockSpec((tm * ROW_SUB, c), x_map),
                      pl.BlockSpec(memory_space=pl.ANY),
                      pl.BlockSpec(memory_space=pl.ANY),
                      pl.BlockSpec(memory_space=pl.ANY)],
            out_specs=pl.BlockSpec((tm * ROW_SUB, c), lambda i, te, nu, sg, nx: (i, 0)),
            scratch_shapes=[pltpu.VMEM((2, d, de), F32), pltpu.VMEM((2, d, de), F32),
                            pltpu.VMEM((2, de, d), F32),
                            pltpu.VMEM((d, de), BF16), pltpu.VMEM((d, de), BF16),
                            pltpu.VMEM((de, d), BF16), pltpu.SemaphoreType.DMA((2,))]),
        out_shape=jax.ShapeDtypeStruct((p * ROW_SUB, c), F32),
        compiler_params=pltpu.CompilerParams(dimension_semantics=("arbitrary",),
                                             vmem_limit_bytes=VMEM_LIMIT),
        name="moe",
    )(tile_expert, n_used, seg_index, next_expert, xs, w_gate, w_up, w_down)


def _fin_kernel(pos_ref, posn_ref, x1_ref, route_ref, mod_ref, fg_ref, ys_ref, y_ref, ybuf, sem,
                *, tp, d, n_steps):
    i = pl.program_id(0)
    cur = i % 2

    def issue(p_ref, buf):
        def row_start(r, carry):
            dst = pl.ds(pl.multiple_of(r * ROW_SUB, ROW_SUB), ROW_SUB)
            for slot in range(2):
                src = pl.multiple_of(p_ref[0, slot, r], ROW_SUB)
                pltpu.make_async_copy(ys_ref.at[pl.ds(src, ROW_SUB)],
                                      ybuf.at[buf, slot, dst], sem.at[buf]).start(priority=slot)
            return carry
        lax.fori_loop(0, tp, row_start, 0, unroll=8)

    def wait_buf(buf):
        for slot in range(2):
            pltpu.make_async_copy(ys_ref.at[pl.ds(0, tp * ROW_SUB)], ybuf.at[buf, slot],
                                  sem.at[buf]).wait()

    @pl.when(i == 0)
    def _():
        issue(pos_ref, 0)

    wait_buf(cur)
    nxt = 1 - cur
    per_token_mod = mod_ref.shape[1] != 1
    for lo in range(0, tp, FIN_CHUNK):
        hi = min(lo + FIN_CHUNK, tp)
        for r in range(lo, hi):
            for slot in range(2):
                src = pl.multiple_of(posn_ref[0, slot, r], ROW_SUB)
                pltpu.make_async_copy(
                    ys_ref.at[pl.ds(src, ROW_SUB)],
                    ybuf.at[nxt, slot, pl.ds(r * ROW_SUB, ROW_SUB)], sem.at[nxt]).start(priority=slot)
        route = route_ref[lo:hi, :]
        moe = (route[:, 2:3] * _load_rows(ybuf.at[cur, 0], hi - lo, lo)
               + route[:, 3:4] * _load_rows(ybuf.at[cur, 1], hi - lo, lo))
        mod = mod_ref[0, lo:hi, :] if per_token_mod else mod_ref[0]
        y_ref[0, lo:hi, :] = _rms(x1_ref[0, lo:hi, :] + mod[:, 5 * d:6 * d] * moe, fg_ref[...])

    @pl.when(i == n_steps - 1)
    def _():
        wait_buf(nxt)


def _finish(x1, route, mod, final_g, ys, pos3):
    b, t, d = x1.shape
    tp = pos3.shape[2]
    assert t % tp == 0
    nt = t // tp
    n_steps = b * nt
    assert pos3.shape[0] == n_steps
    blk0 = 0
    if mod.shape[1] == 1:
        mod_spec = pl.BlockSpec((1, 1, 6 * d), lambda i: (i // nt, 0, 0))
    else:
        mod_spec = pl.BlockSpec((1, tp, 6 * d), lambda i: (i // nt, i % nt, 0))
    return pl.pallas_call(
        functools.partial(_fin_kernel, tp=tp, d=d, n_steps=n_steps),
        grid_spec=pltpu.PrefetchScalarGridSpec(
            num_scalar_prefetch=0,
            grid=(n_steps,),
            in_specs=[pl.BlockSpec((1, 2, tp), lambda i: (blk0 + i, 0, 0), memory_space=pltpu.SMEM),
                      pl.BlockSpec((1, 2, tp), lambda i: (blk0 + jnp.minimum(i + 1, n_steps - 1), 0, 0),
                                   memory_space=pltpu.SMEM),
                      pl.BlockSpec((1, tp, d), lambda i: (i // nt, i % nt, 0)),
                      pl.BlockSpec((tp, LANES), lambda i: (i, 0)),
                      mod_spec,
                      _const_spec((1, d)),
                      pl.BlockSpec(memory_space=pl.ANY)],
            out_specs=pl.BlockSpec((1, tp, d), lambda i: (i // nt, i % nt, 0)),
            scratch_shapes=[pltpu.VMEM((2, 2, tp * ROW_SUB, LANES), F32),
                            pltpu.SemaphoreType.DMA((2,))]),
        out_shape=jax.ShapeDtypeStruct((b, t, d), F32),
        compiler_params=pltpu.CompilerParams(dimension_semantics=("arbitrary",),
                                             vmem_limit_bytes=VMEM_LIMIT),
        name="fin",
    )(pos3, pos3, x1, route, mod, final_g.reshape(1, d), ys)


def _expert_onehots(route):
    lane = lax.broadcasted_iota(I32, route.shape, 1).astype(F32)
    return ((lane == route[:, 0:1]).astype(F32), (lane == route[:, 1:2]).astype(F32))


def _count_kernel(route_ref, base_ref, tot_ref, acc):
    @pl.when(pl.program_id(0) == 0)
    def _():
        acc[...] = jnp.zeros_like(acc)

    oh0, oh1 = _expert_onehots(route_ref[...])
    base_ref[0] = acc[...]
    acc[...] = acc[...] + jnp.sum(oh0 + oh1, axis=0, keepdims=True)
    tot_ref[...] = acc[...]


def _pos_kernel(route_ref, base_ref, seg_ref, pos_ref, tril_s, *, tp):
    @pl.when(pl.program_id(0) == 0)
    def _():
        r_i = lax.broadcasted_iota(I32, (tp, tp), 0)
        c_i = lax.broadcasted_iota(I32, (tp, tp), 1)
        tril_s[...] = (c_i < r_i).astype(BF16)

    oh0, oh1 = _expert_onehots(route_ref[...])
    before = jnp.dot(tril_s[...], (oh0 + oh1).astype(BF16), preferred_element_type=F32)
    offs = seg_ref[0:1, :] + base_ref[0][0:1, :] + before
    ones8 = jnp.ones((8, LANES), BF16)

    def lane_sum_as_row(v):
        hi = v.astype(BF16)
        r1 = v - hi.astype(F32)
        mid = r1.astype(BF16)
        lo = (r1 - mid.astype(F32)).astype(BF16)
        return _bdot_nt(ones8, hi) + _bdot_nt(ones8, mid) + _bdot_nt(ones8, lo)

    p0 = lane_sum_as_row(oh0 * offs)
    p1 = lane_sum_as_row(oh1 * offs)
    row = lax.broadcasted_iota(I32, (8, tp), 0)
    pos_ref[0] = (jnp.where(row == 0, p0, jnp.where(row == 1, p1, 0.0)) * ROW_SUB).astype(I32)


def _count(route, tp):
    n = route.shape[0]
    assert n % tp == 0
    return pl.pallas_call(
        _count_kernel, grid=(n // tp,),
        in_specs=[pl.BlockSpec((tp, LANES), lambda i: (i, 0))],
        out_specs=[pl.BlockSpec((1, 8, LANES), lambda i: (i, 0, 0)), _const_spec((8, LANES))],
        out_shape=[jax.ShapeDtypeStruct((n // tp, 8, LANES), F32),
                   jax.ShapeDtypeStruct((8, LANES), F32)],
        scratch_shapes=[pltpu.VMEM((8, LANES), F32)],
        compiler_params=pltpu.CompilerParams(dimension_semantics=("arbitrary",),
                                             vmem_limit_bytes=VMEM_LIMIT),
        name="count",
    )(route)


def _positions(route, base, seg_start, tp):
    n = route.shape[0]
    seg = jnp.broadcast_to(jnp.pad(seg_start.astype(F32), (0, LANES - N_EXPERTS))[None, :], (8, LANES))
    pos = pl.pallas_call(
        functools.partial(_pos_kernel, tp=tp), grid=(n // tp,),
        in_specs=[pl.BlockSpec((tp, LANES), lambda i: (i, 0)),
                  pl.BlockSpec((1, 8, LANES), lambda i: (i, 0, 0)),
                  _const_spec((8, LANES))],
        out_specs=pl.BlockSpec((1, 8, tp), lambda i: (i, 0, 0)),
        out_shape=jax.ShapeDtypeStruct((n // tp, 8, tp), I32),
        scratch_shapes=[pltpu.VMEM((tp, tp), BF16)],
        compiler_params=pltpu.CompilerParams(dimension_semantics=("arbitrary",),
                                             vmem_limit_bytes=VMEM_LIMIT),
        name="positions",
    )(route, base, seg)
    return pos[:, 0:2, :]


def _plan(counts, counts_first, n_second, tm, n_tiles):
    tiles_per = (counts + tm - 1) // tm
    tile_end = jnp.cumsum(tiles_per)
    seg_start = (tile_end - tiles_per) * tm
    n_used = tile_end[-1]
    tile_ids = jnp.arange(n_tiles, dtype=I32)
    tile_expert = jnp.sum((tile_ids[:, None] >= tile_end[None, :]).astype(I32), axis=1)
    last_used = jnp.sum((n_used - 1 >= tile_end).astype(I32))
    tile_expert = jnp.where(tile_ids < n_used, tile_expert, last_used).astype(I32)
    assert tm % ZERO_CHUNK == 0
    per_tile = tm // ZERO_CHUNK
    piece = jnp.arange(per_tile, dtype=I32)[None, :] * ZERO_CHUNK
    n_piece = per_tile + 1 + -(-n_second // ZERO_CHUNK)
    first = seg_start + (counts_first // ZERO_CHUNK) * ZERO_CHUNK
    z_expert = first[:, None] + jnp.arange(n_piece, dtype=I32)[None, :] * ZERO_CHUNK
    z_expert = jnp.where(z_expert < (tile_end * tm)[:, None], z_expert, -1)
    spare = n_used + jnp.arange(N_EXPERTS, dtype=I32)
    z_spare = jnp.where((spare < n_tiles)[:, None], spare[:, None] * tm + piece, -1)
    zoff = jnp.concatenate([z_expert.reshape(-1), z_spare.reshape(-1)]).astype(I32)
    experts = jnp.arange(N_EXPERTS, dtype=I32)
    nonempty = tiles_per > 0
    seg_of_expert = jnp.cumsum(nonempty.astype(I32)) - 1
    later = nonempty[None, :] & (experts[None, :] > experts[:, None])
    next_of_expert = jnp.min(jnp.where(later, experts[None, :], N_EXPERTS), axis=1)
    next_of_expert = jnp.where(next_of_expert < N_EXPERTS, next_of_expert, -1)
    seg_index = seg_of_expert[tile_expert].astype(I32)
    next_expert = next_of_expert[tile_expert].astype(I32)
    return seg_start, tile_expert, n_used.reshape(1).astype(I32), zoff, seg_index, next_expert


def kernel(x_prompt, x_sample, c_prompt, c_sample, state_mlstm_C, state_mlstm_n, state_mlstm_m,
           cache_conv, w_ada, b_ada, norm1_g, w_in, b_igate, b_fgate, mh_norm_g, w_dw, b_dw,
           conv_ln_g, conv_ln_b, w_out, norm2_g, w_grp_router, b_grp_router, w_exp_router,
           b_exp_router, w_gate, w_up, w_down, final_g):
    depth = w_ada.shape[0]
    assert depth == 1, "one layer per step"
    bp, tp_, d = x_prompt.shape
    bs = x_sample.shape[0]
    assert x_sample.shape[1] == 1 and d == ROW_SUB * LANES
    dk = state_mlstm_C.shape[-1]
    dm = N_HEADS * dk
    dc = d - dm
    l = 0

    win = w_in[l]
    w_qkvo = win[:, 0:4 * dm]
    w_gates = win[:, 4 * dm:4 * dm + 2 * N_HEADS]
    w_glu = win[:, 4 * dm + 2 * N_HEADS:]
    wmain, wmain_lo = _split_weights(jnp.concatenate([w_qkvo, w_glu], axis=1))
    wout_hi, wout_lo = _split_weights(w_out[l])
    wkt_hi, wkt_lo = _split_weights(win[:, dm:2 * dm].T)
    wg_pad = jnp.pad(w_gates, ((0, 0), (0, LANES - 2 * N_HEADS)))
    wg_hi, wg_lo = _split_weights(wg_pad)
    gbias = jnp.pad(jnp.concatenate([b_igate[l], b_fgate[l]]), (0, LANES - 2 * N_HEADS)).reshape(1, LANES)
    w_r = jnp.concatenate([w_grp_router[l], w_exp_router[l]], axis=1)
    n_r = N_GROUPS + N_EXPERTS
    wr_hi, wr_lo = _split_weights(jnp.pad(w_r, ((0, 0), (0, LANES - n_r))))
    rbias = jnp.pad(jnp.concatenate([b_grp_router[l], b_exp_router[l]]), (0, LANES - n_r)).reshape(1, LANES)
    wts = dict(dm=dm, g1=norm1_g[l].reshape(1, d), g2=norm2_g[l].reshape(1, d), wmain=wmain,
               wg_hi=wg_hi, wg_lo=wg_lo, gbias=gbias, mhg=mh_norm_g[l],
               wdw=w_dw[l].reshape(CONV_W, dc), bdw=b_dw[l].reshape(1, dc),
               clg=conv_ln_g[l].reshape(1, dc), clb=conv_ln_b[l].reshape(1, dc),
               wout=wout_hi, wr_hi=wr_hi, wr_lo=wr_lo, rbias=rbias, wk_t=wkt_hi,
               wmain_lo=wmain_lo, wout_lo=wout_lo, wk_t_lo=wkt_lo)

    mod_p, mod_s = _ada(c_prompt, c_sample, w_ada[l], b_ada[l])

    n_p = bp * tp_
    n_all = n_p + bs
    x1_p, h2_p, route_p, c_p, n_pr, m_p, cv_p, base_p, tot_p = _mix_prompt(x_prompt, mod_p, wts)
    x1_s, h2_s, route_s, c_s, n_s, m_s, u_s = _mix_sample(
        x_sample.reshape(bs, d), mod_s, state_mlstm_C[l], state_mlstm_n[l], state_mlstm_m[l],
        cache_conv[l], wts)

    tm = MOE_TILE
    n_tiles = (2 * n_all) // tm + N_EXPERTS
    tp_p, tp_s = min(PERM_TILE, tp_), min(PERM_TILE, bs)
    assert tp_p == min(MIX_TILE, tp_), "the prompt mixer counts experts per PERM_TILE tokens"
    base_s, tot_s = _count(route_s, tp_s)
    counts = (tot_p[0, :N_EXPERTS] + tot_s[0, :N_EXPERTS]).astype(I32)
    seg_start, tile_expert, n_used, zoff, seg_index, next_expert = _plan(
        counts, tot_p[0, :N_EXPERTS].astype(I32), 2 * bs, tm, n_tiles)
    pos_p = _positions(route_p, base_p, seg_start, tp_p)
    pos_s = _positions(route_s, base_s + tot_p[None], seg_start, tp_s)

    xs = _scatter_rows(h2_p, pos_p, zoff, n_tiles * tm, tm)
    xs = _scatter_rows(h2_s, pos_s, zoff, xs, tm)
    ys = _moe(xs, tile_expert, n_used, seg_index, next_expert, w_gate[l], w_up[l], w_down[l], tm)

    y_p = _finish(x1_p, route_p, mod_p.reshape(bp, 1, -1), final_g, ys, pos_p)
    y_s = _finish(x1_s.reshape(1, bs, d), route_s, mod_s.reshape(1, bs, -1), final_g, ys,
                  pos_s).reshape(bs, 1, d)

    conv_s = jnp.concatenate([cache_conv[l][:, 1:, :], u_s[:, None, :]], axis=1)
    return (y_p, y_s,
            c_p, n_pr, m_p[:, 0, :N_HEADS].reshape(1, bp, N_HEADS), cv_p,
            c_s[None], n_s.reshape(1, bs, N_HEADS, dk), m_s[:, :N_HEADS].reshape(1, bs, N_HEADS),
            conv_s[None])
```

```python
import functools

import jax
import jax.numpy as jnp
from jax import lax
from jax.experimental import pallas as pl
from jax.experimental.pallas import tpu as pltpu

F32 = jnp.float32
BF16 = jnp.bfloat16
I32 = jnp.int32

EPS = 1e-6
LANES = 128
ROW_SUB = 8
CHUNK = 128
N_HEADS = 4
N_GROUPS = 4
EXP_PER_GROUP = 8
N_EXPERTS = N_GROUPS * EXP_PER_GROUP
CONV_W = 31
CONV_PAD = 32
CONV_OFF = CONV_PAD - (CONV_W - 1)
MIX_TILE = 512
MIX_SUB = 256
MOE_TILE = 512
PERM_TILE = 512
FIN_CHUNK = 64
ZERO_CHUNK = 64
VMEM_LIMIT = 56 * 1024 * 1024


def _sigmoid(x):
    return 1.0 / (1.0 + jnp.exp(-x))


def _silu(x):
    return x * _sigmoid(x)


def _log_sigmoid(x):
    return jnp.minimum(x, 0.0) - jnp.log(1.0 + jnp.exp(-jnp.abs(x)))


def _bdot_nt(a, b):
    return lax.dot_general(a.astype(BF16), b.astype(BF16), (((1,), (1,)), ((), ())),
                           preferred_element_type=F32)


def _split2(x):
    hi = x.astype(BF16)
    lo = (x - hi.astype(F32)).astype(BF16)
    return hi, lo


def _split_kernel(w_ref, hi_ref, lo_ref):
    hi, lo = _split2(w_ref[...])
    hi_ref[...] = hi
    lo_ref[...] = lo


def _split_weights(w):
    rows, cols = w.shape
    blk = min(cols, 512)
    assert cols % blk == 0
    spec = pl.BlockSpec((rows, blk), lambda j: (0, j))
    return pl.pallas_call(
        _split_kernel, grid=(cols // blk,), in_specs=[spec], out_specs=[spec, spec],
        out_shape=[jax.ShapeDtypeStruct(w.shape, BF16)] * 2,
        compiler_params=pltpu.CompilerParams(dimension_semantics=("arbitrary",),
                                             vmem_limit_bytes=VMEM_LIMIT),
        name="split",
    )(w)


def _dot3(a, w_hi, w_lo):
    a_hi, a_lo = _split2(a)
    return (jnp.dot(a_hi, w_hi, preferred_element_type=F32)
            + jnp.dot(a_lo, w_hi, preferred_element_type=F32)
            + jnp.dot(a_hi, w_lo, preferred_element_type=F32))


def _cumsum_lanes(triu_bf16, x):
    hi = x.astype(BF16)
    r1 = x - hi.astype(F32)
    mid = r1.astype(BF16)
    lo = (r1 - mid.astype(F32)).astype(BF16)
    return (jnp.dot(hi, triu_bf16, preferred_element_type=F32)
            + jnp.dot(mid, triu_bf16, preferred_element_type=F32)
            + jnp.dot(lo, triu_bf16, preferred_element_type=F32))


def _rms(x, g):
    return x * lax.rsqrt(jnp.mean(x * x, axis=-1, keepdims=True) + EPS) * g


def _layer_norm(x, g, b=None):
    mu = jnp.mean(x, axis=-1, keepdims=True)
    xc = x - mu
    var = jnp.mean(xc * xc, axis=-1, keepdims=True)
    y = xc * lax.rsqrt(var + EPS) * g
    return y if b is None else y + b


def _store_rows(ref, x, row0=0):
    r = x.shape[0]
    for k in range(ROW_SUB):
        ref[pl.ds(row0 * ROW_SUB + k, r, stride=ROW_SUB), :] = x[:, k * LANES:(k + 1) * LANES]


def _load_rows(ref, r, row0=0):
    return jnp.concatenate([ref[pl.ds(row0 * ROW_SUB + k, r, stride=ROW_SUB), :]
                            for k in range(ROW_SUB)], axis=1)


def _route(logits):
    lane = lax.broadcasted_iota(I32, logits.shape, 1).astype(F32)
    neg = jnp.float32(-jnp.inf)
    big = jnp.float32(1e9)
    is_g = lane < N_GROUPS
    gl = jnp.where(is_g, logits, neg)
    gmax = jnp.max(gl, axis=1, keepdims=True)
    gsel = jnp.min(jnp.where(gl == gmax, lane, big), axis=1, keepdims=True)
    pg = 1.0 / jnp.sum(jnp.where(is_g, jnp.exp(gl - gmax), 0.0), axis=1, keepdims=True)
    lo = N_GROUPS + EXP_PER_GROUP * gsel
    emask = (lane >= lo) & (lane < lo + EXP_PER_GROUP)
    el = jnp.where(emask, logits, neg)
    v1 = jnp.max(el, axis=1, keepdims=True)
    i1 = jnp.min(jnp.where(el == v1, lane, big), axis=1, keepdims=True)
    el2 = jnp.where(lane == i1, neg, el)
    v2 = jnp.max(el2, axis=1, keepdims=True)
    i2 = jnp.min(jnp.where(el2 == v2, lane, big), axis=1, keepdims=True)
    d = jnp.exp(v2 - v1)
    w1 = pg / (1.0 + d)
    w2 = pg * d / (1.0 + d)
    return jnp.where(lane == 0, i1 - N_GROUPS,
                     jnp.where(lane == 1, i2 - N_GROUPS,
                               jnp.where(lane == 2, w1, jnp.where(lane == 3, w2, 0.0))))


def _ada_kernel(cp_ref, cs_ref, w_ref, b_ref, op_ref, os_ref):
    w_hi, w_lo = _split2(w_ref[...])
    op_ref[...] = _dot3(_silu(cp_ref[...]), w_hi, w_lo) + b_ref[...]
    os_ref[...] = _dot3(_silu(cs_ref[...]), w_hi, w_lo) + b_ref[...]


def _ada(c_p, c_s, w_ada, b_ada):
    (rp, d), rs = c_p.shape, c_s.shape[0]
    n_out = w_ada.shape[1]
    blk = 1024
    return pl.pallas_call(
        _ada_kernel,
        grid=(n_out // blk,),
        in_specs=[pl.BlockSpec((rp, d), lambda j: (0, 0)),
                  pl.BlockSpec((rs, d), lambda j: (0, 0)),
                  pl.BlockSpec((d, blk), lambda j: (0, j)),
                  pl.BlockSpec((1, blk), lambda j: (0, j))],
        out_specs=[pl.BlockSpec((rp, blk), lambda j: (0, j)),
                   pl.BlockSpec((rs, blk), lambda j: (0, j))],
        out_shape=[jax.ShapeDtypeStruct((rp, n_out), F32), jax.ShapeDtypeStruct((rs, n_out), F32)],
        compiler_params=pltpu.CompilerParams(dimension_semantics=("arbitrary",),
                                             vmem_limit_bytes=VMEM_LIMIT),
        name="ada",
    )(c_p, c_s, w_ada, b_ada.reshape(1, n_out))


def _post(x, attn_cat, mod, g2, wout, wr_hi, wr_lo, rbias, d, wout_lo=None):
    gate1 = mod[:, 2 * d:3 * d]
    sh2 = mod[:, 3 * d:4 * d]
    sc2 = mod[:, 4 * d:5 * d]
    if wout_lo is None:
        proj = jnp.dot(attn_cat.astype(BF16), wout, preferred_element_type=F32)
    else:
        proj = _dot3(attn_cat, wout, wout_lo)
    x1 = x + gate1 * proj
    h2 = _rms(x1, g2) * (1.0 + sc2) + sh2
    logits = _dot3(h2, wr_hi, wr_lo) + rbias
    return x1, h2, _route(logits)


def _mix_prompt_kernel(x_ref, mod_ref, g1_ref, g2_ref, wmain_ref, wgh_ref, wgl_ref, gbias_ref,
                       mhg_ref, wdw_ref, bdw_ref, clg_ref, clb_ref, wout_ref, wrh_ref, wrl_ref,
                       rbias_ref,
                       x1_ref, h2_ref, route_ref, c_ref, n_ref, m_ref, cv_ref, base_ref, tot_ref,
                       ubuf, yc_s, q_s, k_s, v_s, so_s, hm_s, p_s, u_s, cm_s, nb_s, m_s, cnt_s,
                       *, tt, d, dm, dk):
    t = pl.program_id(1)
    dc = d - dm
    n_lt = dc // LANES
    sub = min(MIX_SUB, tt)

    @pl.when(t == 0)
    def _():
        c_ref[...] = jnp.zeros_like(c_ref)
        nb_s[...] = jnp.zeros_like(nb_s)
        m_s[...] = jnp.zeros_like(m_s)
        ubuf[0:CONV_PAD, :] = jnp.zeros((CONV_PAD, dc), F32)

    mod = mod_ref[0]
    row8 = lax.broadcasted_iota(I32, (8, LANES), 0)
    row = lax.broadcasted_iota(I32, (CHUNK, CHUNK), 0)
    col = lax.broadcasted_iota(I32, (CHUNK, CHUNK), 1)
    causal = col <= row
    triu = (row <= col).astype(BF16)
    neg = jnp.float32(-jnp.inf)
    ones_b = jnp.ones((CHUNK, dk), BF16)
    pad_rows = jnp.zeros((CHUNK - 8, LANES), F32)

    lanes_of = [slice(lt * LANES, (lt + 1) * LANES) for lt in range(n_lt)]
    wrows = [[jnp.broadcast_to(wdw_ref[j:j + 1, ls], (8, LANES)) for j in range(CONV_W)]
             for ls in lanes_of]
    bias = [jnp.broadcast_to(bdw_ref[:, ls], (8, LANES)) for ls in lanes_of]

    def partial_sums(r0, lt):
        blocks = [ubuf[r0 + 8 * a:r0 + 8 * a + 8, lanes_of[lt]] for a in range(CONV_PAD // 8)]
        sums = []
        for s in range(8):
            acc = None
            for a in range(CONV_PAD // 8):
                j = 8 * a + s - CONV_OFF
                if 0 <= j < CONV_W:
                    term = blocks[a] * wrows[lt][j]
                    acc = term if acc is None else acc + term
            sums.append(acc)
        return tuple(sums)

    q_prev = [None] * n_lt
    ca = [jnp.concatenate([c_ref[0, 0, hd], nb_s[hd]], axis=1) for hd in range(N_HEADS)]
    m_prev = [m_s[hd:hd + 1, :] for hd in range(N_HEADS)]

    n_sub = tt // sub
    gates_of = {}
    hc_of = {}
    tile_counts = []

    def rows_of(sb):
        return slice(sb * sub, (sb + 1) * sub)

    def proj_items(sb):
        rs = rows_of(sb)
        st = {}

        def head():
            x = x_ref[0, rs, :]
            h = _rms(x, g1_ref[...]) * (1.0 + mod[:, d:2 * d]) + mod[:, 0:d]
            st["hb"] = h.astype(BF16)
            gates_of[sb] = _dot3(h, wgh_ref[...], wgl_ref[...]) + gbias_ref[...]

        def proj(lo, hi):
            return jnp.dot(st["hb"], wmain_ref[:, lo:hi], preferred_element_type=F32)

        def glu():
            ubuf[CONV_PAD + sb * sub:CONV_PAD + (sb + 1) * sub, :] = (
                proj(4 * dm, 4 * dm + dc) * _sigmoid(proj(4 * dm + dc, 4 * dm + 2 * dc)))

        def q():
            q_s[rs, :] = proj(0, dm).astype(BF16)

        def k():
            k_s[rs, :] = proj(dm, 2 * dm) * (dk ** -0.5)

        def v():
            v_s[rs, :] = proj(2 * dm, 3 * dm).astype(BF16)

        def o():
            so_s[rs, :] = _sigmoid(proj(3 * dm, 4 * dm))

        return [head, glu, q, k, v, o]

    def conv_items(sb):
        items = []
        for lt in range(n_lt):
            for i in range(sb * sub // 8 + 1, (sb + 1) * sub // 8 + 1):
                def block(lt=lt, i=i):
                    if i == 1:
                        q_prev[lt] = partial_sums(0, lt)
                    q_cur = partial_sums(i * 8, lt)
                    cur = ubuf[(i - 1) * 8 + CONV_PAD:i * 8 + CONV_PAD, lanes_of[lt]]
                    y = bias[lt] + q_prev[lt][0] + cur * wrows[lt][CONV_W - 1]
                    for s in range(1, 8):
                        merged = jnp.where(row8 < s, q_cur[s], q_prev[lt][s])
                        y = y + pltpu.roll(merged, 8 - s, 0)
                    yc_s[(i - 1) * 8:i * 8, lanes_of[lt]] = y
                    q_prev[lt] = q_cur
                items.append(block)
        return items

    def post_items(sb):
        def post():
            rs = rows_of(sb)
            cat = jnp.concatenate([hm_s[rs, :], hc_of[sb]], axis=1)
            x1, h2, route = _post(x_ref[0, rs, :], cat, mod, g2_ref[...], wout_ref[...],
                                  wrh_ref[...], wrl_ref[...], rbias_ref[...], d)
            x1_ref[0, rs, :] = x1
            _store_rows(h2_ref, h2, sb * sub)
            route_ref[rs, :] = route
            oh0, oh1 = _expert_onehots(route)
            tile_counts.append(jnp.sum(oh0 + oh1, axis=0, keepdims=True))
        return [post]

    def interleave(main, side):
        gap = len(main) / (len(side) + 1)
        due, done = gap, 0
        for n, item in enumerate(main):
            item()
            while done < len(side) and n + 1 >= due:
                side[done]()
                done += 1
                due += gap
        for item in side[done:]:
            item()

    for item in proj_items(0):
        item()
    for sb in range(n_sub):
        r_lo = sb * sub
        rs = rows_of(sb)
        side = post_items(sb - 1) if sb > 0 else []
        if sb + 1 < n_sub:
            side = side + proj_items(sb + 1)
        interleave(conv_items(sb), side)
        hc_of[sb] = _silu(_layer_norm(yc_s[rs, :], clg_ref[...], clb_ref[...]))
        gates = gates_of[sb]

        chunks = range(r_lo // CHUNK, (r_lo + sub) // CHUNK)
        b_cols = {}
        for c in chunks:
            r0 = c * CHUNK
            g8 = gates[r0 - r_lo:r0 - r_lo + CHUNK, :].T[0:8, :]
            b8 = _cumsum_lanes(triu, _log_sigmoid(g8))
            pk8 = jnp.where(row8 < N_HEADS, g8 - pltpu.roll(b8, N_HEADS, 0), b8)
            pk = jnp.concatenate([pk8, pad_rows], axis=0).T
            for hd in range(N_HEADS):
                cs = slice(hd * dk, (hd + 1) * dk)
                idx = c * N_HEADS + hd
                kf = k_s[r0:r0 + CHUNK, cs]
                va = jnp.concatenate([v_s[r0:r0 + CHUNK, cs], ones_b], axis=1)
                gm = jnp.where(causal, pk8[hd:hd + 1, :], neg)
                cm = jnp.max(gm, axis=1, keepdims=True)
                s = _bdot_nt(q_s[r0:r0 + CHUNK, cs], kf) * jnp.exp(gm - cm)
                p_s[idx] = jnp.dot(s.astype(BF16), va, preferred_element_type=F32)
                cm_s[idx] = jnp.broadcast_to(cm, (CHUNK, LANES))
                kw = kf * jnp.exp(pk[:, hd:hd + 1] - cm[CHUNK - 1:CHUNK, :])
                u_s[idx] = jnp.dot(kw.T.astype(BF16), va, preferred_element_type=F32)
                b_cols[idx] = pk[:, N_HEADS + hd:N_HEADS + hd + 1]

        for hd in range(N_HEADS):
            cs = slice(hd * dk, (hd + 1) * dk)
            for c in chunks:
                r0 = c * CHUNK
                idx = c * N_HEADS + hd
                cm = cm_s[idx]
                b_col = b_cols[idx]
                mt = jnp.maximum(m_prev[hd], cm)
                f_loc = jnp.exp(cm - mt)
                a_int = jnp.exp(m_prev[hd] - mt)
                qc = jnp.dot(q_s[r0:r0 + CHUNK, cs], ca[hd].astype(BF16),
                             preferred_element_type=F32)
                p = p_s[idx]
                num = f_loc * p[:, :dk] + a_int * qc[:, :dk]
                den = f_loc * p[:, dk:] + a_int * qc[:, dk:]
                hh = num / jnp.maximum(jnp.abs(den), jnp.exp(-(b_col + mt)))
                hm_s[r0:r0 + CHUNK, cs] = (_layer_norm(hh, mhg_ref[hd:hd + 1, :])
                                           * so_s[r0:r0 + CHUNK, cs])
                mt_l = mt[CHUNK - 1:CHUNK, :]
                u = u_s[idx]
                f_l = f_loc[CHUNK - 1:CHUNK, :]
                a_l = a_int[CHUNK - 1:CHUNK, :]
                ca[hd] = jnp.concatenate([a_l * ca[hd][:, :dk] + f_l * u[:, :dk],
                                          a_l * ca[hd][:, dk:] + f_l * u[:, dk:]], axis=1)
                m_prev[hd] = b_col[CHUNK - 1:CHUNK, :] + mt_l

    for item in post_items(n_sub - 1):
        item()

    for hd in range(N_HEADS):
        c_ref[0, 0, hd] = ca[hd][:, :dk]
        nb_s[hd] = ca[hd][:, dk:]
        m_s[hd:hd + 1, :] = m_prev[hd]
    cv_ref[0, 0] = ubuf[tt + CONV_PAD - (CONV_W - 1):tt + CONV_PAD, :]
    ubuf[0:CONV_PAD, :] = ubuf[tt:tt + CONV_PAD, :]

    @pl.when(t == pl.num_programs(1) - 1)
    def _():
        for hd in range(N_HEADS):
            n_ref[0, 0, hd:hd + 1, :] = nb_s[hd].T[0:1, :]

    lane1 = lax.broadcasted_iota(I32, (1, LANES), 1)
    m_row = jnp.zeros((1, LANES), F32)
    for hd in range(N_HEADS):
        m_row = jnp.where(lane1 == hd, m_s[hd:hd + 1, :], m_row)
    m_ref[0] = m_row

    @pl.when((pl.program_id(0) == 0) & (t == 0))
    def _():
        cnt_s[...] = jnp.zeros_like(cnt_s)

    base_ref[0] = cnt_s[...]
    cnt_s[...] = cnt_s[...] + sum(tile_counts)
    tot_ref[...] = cnt_s[...]


def _const_spec(shape):
    nd = len(shape)
    return pl.BlockSpec(shape, lambda *_: (0,) * nd)


def _mix_prompt(x, mod, wts):
    b, t, d = x.shape
    dm = wts["dm"]
    dk = dm // N_HEADS
    dc = d - dm
    tt = min(MIX_TILE, t)
    assert t % tt == 0 and tt % CHUNK == 0 and tt >= CONV_PAD
    nt = t // tt
    kern = functools.partial(_mix_prompt_kernel, tt=tt, d=d, dm=dm, dk=dk)
    const_names = ["g1", "g2", "wmain", "wg_hi", "wg_lo", "gbias", "mhg", "wdw", "bdw", "clg",
                   "clb", "wout", "wr_hi", "wr_lo", "rbias"]
    consts = [wts[k] for k in const_names]
    in_specs = ([pl.BlockSpec((1, tt, d), lambda i, j: (i, j, 0)),
                 pl.BlockSpec((1, 1, mod.shape[-1]), lambda i, j: (i, 0, 0))]
                + [_const_spec(c.shape) for c in consts])
    out_shape = [
        jax.ShapeDtypeStruct((b, t, d), F32),
        jax.ShapeDtypeStruct((b * t * ROW_SUB, LANES), F32),
        jax.ShapeDtypeStruct((b * t, LANES), F32),
        jax.ShapeDtypeStruct((1, b, N_HEADS, dk, dk), F32),
        jax.ShapeDtypeStruct((1, b, N_HEADS, dk), F32),
        jax.ShapeDtypeStruct((b, 1, LANES), F32),
        jax.ShapeDtypeStruct((1, b, CONV_W - 1, dc), F32),
        jax.ShapeDtypeStruct((b * nt, 8, LANES), F32),
        jax.ShapeDtypeStruct((8, LANES), F32),
    ]
    out_specs = [
        pl.BlockSpec((1, tt, d), lambda i, j: (i, j, 0)),
        pl.BlockSpec((tt * ROW_SUB, LANES), lambda i, j: (i * nt + j, 0)),
        pl.BlockSpec((tt, LANES), lambda i, j: (i * nt + j, 0)),
        pl.BlockSpec((1, 1, N_HEADS, dk, dk), lambda i, j: (0, i, 0, 0, 0)),
        pl.BlockSpec((1, 1, N_HEADS, dk), lambda i, j: (0, i, 0, 0)),
        pl.BlockSpec((1, 1, LANES), lambda i, j: (i, 0, 0)),
        pl.BlockSpec((1, 1, CONV_W - 1, dc), lambda i, j: (0, i, 0, 0)),
        pl.BlockSpec((1, 8, LANES), lambda i, j: (i * nt + j, 0, 0)),
        pl.BlockSpec((8, LANES), lambda i, j: (0, 0)),
    ]
    n_hc = (tt // CHUNK) * N_HEADS
    scratch = [pltpu.VMEM((tt + CONV_PAD, dc), F32),
               pltpu.VMEM((tt, dc), F32),
               pltpu.VMEM((tt, dm), BF16),
               pltpu.VMEM((tt, dm), F32),
               pltpu.VMEM((tt, dm), BF16),
               pltpu.VMEM((tt, dm), F32),
               pltpu.VMEM((tt, dm), F32),
               pltpu.VMEM((n_hc, CHUNK, 2 * dk), F32),
               pltpu.VMEM((n_hc, CHUNK, 2 * dk), F32),
               pltpu.VMEM((n_hc, CHUNK, LANES), F32),
               pltpu.VMEM((N_HEADS, dk, LANES), F32),
               pltpu.VMEM((8, LANES), F32),
               pltpu.VMEM((8, LANES), F32)]
    return pl.pallas_call(
        kern, grid=(b, nt), in_specs=in_specs, out_specs=out_specs, out_shape=out_shape,
        scratch_shapes=scratch,
        compiler_params=pltpu.CompilerParams(dimension_semantics=("arbitrary", "arbitrary"),
                                             vmem_limit_bytes=VMEM_LIMIT),
        name="mix_p",
    )(x, mod.reshape(b, 1, -1), *consts)


def _s_pre_kernel(x_ref, mod_ref, g1_ref, wmain_ref, wmainlo_ref, wgh_ref, wgl_ref, gbias_ref,
                  wkt_ref, wktlo_ref, wdw_ref, bdw_ref, clg_ref, clb_ref, cache_ref, n0_ref, m0_ref,
                  q_ref, kt_ref, vs_ref, ab_ref, sv_ref, den_ref, eb_ref, o_ref, hc_ref, u_ref,
                  n_ref, m_ref, *, d, dm, dk):
    dc = d - dm
    x = x_ref[...]
    mod = mod_ref[...]
    sh1 = mod[:, 0:d]
    sc1 = mod[:, d:2 * d]
    h = _rms(x, g1_ref[...]) * (1.0 + sc1) + sh1
    z = _dot3(h, wmain_ref[...], wmainlo_ref[...])
    gates = _dot3(h, wgh_ref[...], wgl_ref[...]) + gbias_ref[...]
    scale = dk ** -0.5
    h_hi, h_lo = _split2(h)
    kt = _bdot_nt(wkt_ref[...], h_hi) + _bdot_nt(wktlo_ref[...], h_hi) + _bdot_nt(wkt_ref[...], h_lo)
    kt_ref[...] = (kt * scale).astype(BF16)
    k_all = z[:, dm:2 * dm] * scale
    ga = z[:, 4 * dm:4 * dm + dc]
    gb = z[:, 4 * dm + dc:4 * dm + 2 * dc]
    u = ga * _sigmoid(gb)
    u_ref[...] = u
    acc = jnp.broadcast_to(bdw_ref[...], u.shape) + u * wdw_ref[CONV_W - 1:CONV_W, :]
    for j in range(CONV_W - 1):
        acc = acc + cache_ref[j] * wdw_ref[j:j + 1, :]
    hc_ref[...] = _silu(_layer_norm(acc, clg_ref[...], clb_ref[...]))
    o_ref[...] = z[:, 3 * dm:4 * dm]
    q_ref[...] = z[:, 0:dm]
    m0 = m0_ref[...]
    n0 = n0_ref[...]
    lane1 = lax.broadcasted_iota(I32, (1, LANES), 1)
    m_new = jnp.zeros(m0.shape, F32)
    for hd in range(N_HEADS):
        cs = slice(hd * dk, (hd + 1) * dk)
        ig = gates[:, hd:hd + 1]
        lf = _log_sigmoid(gates[:, N_HEADS + hd:N_HEADS + hd + 1])
        mp = m0[:, hd:hd + 1]
        inter = lf + mp
        mt = jnp.maximum(inter, ig)
        w = jnp.exp(ig - mt)
        a_int = jnp.exp(inter - mt)
        qf = z[:, cs]
        kf = k_all[:, cs]
        vf = z[:, 2 * dm + hd * dk:2 * dm + (hd + 1) * dk]
        s = jnp.sum(qf * kf, axis=1, keepdims=True) * w
        sv_ref[:, cs] = s * vf
        den_ref[:, cs] = jnp.broadcast_to(
            s + a_int * jnp.sum(qf * n0[:, cs], axis=1, keepdims=True), (x.shape[0], dk))
        eb_ref[:, cs] = jnp.broadcast_to(jnp.exp(-mt), (x.shape[0], dk))
        ab_ref[:, cs] = jnp.broadcast_to(a_int, (x.shape[0], dk))
        vs_ref[:, cs] = (vf * w).astype(BF16)
        n_ref[:, cs] = a_int * n0[:, cs] + w * kf
        m_new = jnp.where(lane1 == hd, mt, m_new)
    m_ref[...] = m_new


def _s_state_kernel(q_ref, kt_ref, vs_ref, ab_ref, c0_ref, c_ref, r_ref, *, bb, dk):
    i = pl.program_id(0)
    nb = q_ref.shape[0]
    rows = lax.broadcasted_iota(I32, (nb, dk), 0)

    @pl.when(i == 0)
    def _():
        r_ref[...] = jnp.zeros_like(r_ref)

    a_blk = ab_ref[pl.ds(pl.multiple_of(i * bb, bb), bb), :]
    for j in range(bb):
        sel = rows == i * bb + j
        for hd in range(N_HEADS):
            cs = slice(hd * dk, (hd + 1) * dk)
            c0 = c0_ref[j, hd]
            vmask = jnp.where(sel, vs_ref[:, cs], jnp.zeros((), BF16))
            c_ref[j, hd] = (a_blk[j:j + 1, cs] * c0
                            + jnp.dot(kt_ref[cs, :], vmask, preferred_element_type=F32))
            c_hi, c_lo = _split2(c0)
            q_hi, q_lo = _split2(q_ref[:, cs])
            r = (jnp.dot(q_hi, c_hi, preferred_element_type=F32)
                 + jnp.dot(q_lo, c_hi, preferred_element_type=F32)
                 + jnp.dot(q_hi, c_lo, preferred_element_type=F32))
            r_ref[:, cs] = r_ref[:, cs] + jnp.where(sel, r, 0.0)


def _s_post_kernel(x_ref, mod_ref, g2_ref, mhg_ref, r_ref, ab_ref, sv_ref, den_ref, eb_ref, o_ref,
                   hc_ref, wout_ref, woutlo_ref, wrh_ref, wrl_ref, rbias_ref,
                   x1_ref, h2_ref, route_ref, *, d, dm, dk):
    hm = []
    for hd in range(N_HEADS):
        cs = slice(hd * dk, (hd + 1) * dk)
        num = sv_ref[:, cs] + ab_ref[:, cs] * r_ref[:, cs]
        hh = num / jnp.maximum(jnp.abs(den_ref[:, cs]), eb_ref[:, cs])
        hm.append(_layer_norm(hh, mhg_ref[hd:hd + 1, :]) * _sigmoid(o_ref[:, cs]))
    cat = jnp.concatenate(hm + [hc_ref[...]], axis=1)
    x1, h2, route = _post(x_ref[...], cat, mod_ref[...], g2_ref[...], wout_ref[...], wrh_ref[...],
                          wrl_ref[...], rbias_ref[...], d, wout_lo=woutlo_ref[...])
    x1_ref[...] = x1
    _store_rows(h2_ref, h2)
    route_ref[...] = route


def _mix_sample(x, mod, c0, n0, m0, cache, wts):
    nb, d = x.shape
    dm = wts["dm"]
    dk = dm // N_HEADS
    dc = d - dm
    cp = pltpu.CompilerParams(dimension_semantics=("arbitrary",), vmem_limit_bytes=VMEM_LIMIT)
    cache_t = jnp.transpose(cache, (1, 0, 2))
    m0p = jnp.pad(m0, ((0, 0), (0, LANES - N_HEADS)))
    pre_in = [x, mod, wts["g1"], wts["wmain"], wts["wmain_lo"], wts["wg_hi"], wts["wg_lo"],
              wts["gbias"], wts["wk_t"], wts["wk_t_lo"], wts["wdw"], wts["bdw"], wts["clg"],
              wts["clb"], cache_t, n0.reshape(nb, dm), m0p]
    pre_out = [jax.ShapeDtypeStruct((nb, dm), F32),
               jax.ShapeDtypeStruct((dm, nb), BF16),
               jax.ShapeDtypeStruct((nb, dm), BF16),
               jax.ShapeDtypeStruct((nb, dm), F32),
               jax.ShapeDtypeStruct((nb, dm), F32),
               jax.ShapeDtypeStruct((nb, dm), F32),
               jax.ShapeDtypeStruct((nb, dm), F32),
               jax.ShapeDtypeStruct((nb, dm), F32),
               jax.ShapeDtypeStruct((nb, dc), F32),
               jax.ShapeDtypeStruct((nb, dc), F32),
               jax.ShapeDtypeStruct((nb, dm), F32),
               jax.ShapeDtypeStruct((nb, LANES), F32)]
    (q, kt, vs, ab, sv, den, eb, o, hc, u, n1, m1) = pl.pallas_call(
        functools.partial(_s_pre_kernel, d=d, dm=dm, dk=dk),
        grid=(1,),
        in_specs=[_const_spec(a.shape) for a in pre_in],
        out_specs=[_const_spec(s.shape) for s in pre_out],
        out_shape=pre_out, compiler_params=cp, name="s_pre")(*pre_in)

    bb = 8
    assert nb % bb == 0
    c1, r = pl.pallas_call(
        functools.partial(_s_state_kernel, bb=bb, dk=dk),
        grid=(nb // bb,),
        in_specs=[_const_spec(q.shape), _const_spec(kt.shape), _const_spec(vs.shape),
                  _const_spec(ab.shape),
                  pl.BlockSpec((bb, N_HEADS, dk, dk), lambda i: (i, 0, 0, 0))],
        out_specs=[pl.BlockSpec((bb, N_HEADS, dk, dk), lambda i: (i, 0, 0, 0)),
                   _const_spec((nb, dm))],
        out_shape=[jax.ShapeDtypeStruct((nb, N_HEADS, dk, dk), F32),
                   jax.ShapeDtypeStruct((nb, dm), F32)],
        compiler_params=cp, name="s_state")(q, kt, vs, ab, c0)

    post_in = [x, mod, wts["g2"], wts["mhg"], r, ab, sv, den, eb, o, hc, wts["wout"],
               wts["wout_lo"], wts["wr_hi"], wts["wr_lo"], wts["rbias"]]
    post_out = [jax.ShapeDtypeStruct((nb, d), F32),
                jax.ShapeDtypeStruct((nb * ROW_SUB, LANES), F32),
                jax.ShapeDtypeStruct((nb, LANES), F32)]
    x1, h2, route = pl.pallas_call(
        functools.partial(_s_post_kernel, d=d, dm=dm, dk=dk),
        grid=(1,),
        in_specs=[_const_spec(a.shape) for a in post_in],
        out_specs=[_const_spec(s.shape) for s in post_out],
        out_shape=post_out, compiler_params=cp, name="s_post")(*post_in)
    return x1, h2, route, c1, n1, m1, u


def _scatter_kernel(pos_ref, zoff_ref, src_ref, *rest, tp, tm, create):
    xs_ref, zbuf, sem, zsem = rest[-4:]
    i = pl.program_id(0)

    if create:
        @pl.when(i == 0)
        def _():
            zbuf[...] = jnp.zeros_like(zbuf)

            def zero_copy(k):
                start = pl.multiple_of(zoff_ref[k] * ROW_SUB, ZERO_CHUNK * ROW_SUB)
                return pltpu.make_async_copy(zbuf, xs_ref.at[pl.ds(start, ZERO_CHUNK * ROW_SUB)], zsem)

            def start(k, carry):
                @pl.when(zoff_ref[k] >= 0)
                def _():
                    zero_copy(k).start()
                return carry

            def wait(k, carry):
                @pl.when(zoff_ref[k] >= 0)
                def _():
                    zero_copy(k).wait()
                return carry

            lax.fori_loop(0, zoff_ref.shape[0], start, 0)
            lax.fori_loop(0, zoff_ref.shape[0], wait, 0)

    for r in range(tp):
        src = src_ref.at[pl.ds(r * ROW_SUB, ROW_SUB)]
        for slot in range(2):
            dst = pl.multiple_of(pos_ref[0, slot, r], ROW_SUB)
            pltpu.make_async_copy(src, xs_ref.at[pl.ds(dst, ROW_SUB)], sem).start(priority=slot)
    for slot in range(2):
        pltpu.make_async_copy(src_ref, xs_ref.at[pl.ds(0, tp * ROW_SUB)], sem).wait()


def _scatter_rows(h2, pos3, zoff, xs_or_rows, tm):
    n, c = h2.shape[0] // ROW_SUB, LANES
    tp = pos3.shape[2]
    assert pos3.shape == (n // tp, 2, tp)
    create = isinstance(xs_or_rows, int)
    n_sorted = xs_or_rows * ROW_SUB if create else xs_or_rows.shape[0]
    in_specs = [pl.BlockSpec((1, 2, tp), lambda i: (i, 0, 0), memory_space=pltpu.SMEM),
                pl.BlockSpec(memory_space=pltpu.SMEM),
                pl.BlockSpec((tp * ROW_SUB, c), lambda i: (i, 0))]
    args = [pos3, zoff, h2]
    if not create:
        in_specs.append(pl.BlockSpec(memory_space=pl.ANY))
        args.append(xs_or_rows)
    return pl.pallas_call(
        functools.partial(_scatter_kernel, tp=tp, tm=tm, create=create),
        grid_spec=pltpu.PrefetchScalarGridSpec(
            num_scalar_prefetch=0,
            grid=(n // tp,),
            in_specs=in_specs,
            out_specs=pl.BlockSpec(memory_space=pl.ANY),
            scratch_shapes=[pltpu.VMEM((ZERO_CHUNK * ROW_SUB, c), F32), pltpu.SemaphoreType.DMA(()),
                            pltpu.SemaphoreType.DMA(())]),
        out_shape=jax.ShapeDtypeStruct((n_sorted, c), F32),
        input_output_aliases={} if create else {3: 0},
        compiler_params=pltpu.CompilerParams(dimension_semantics=("arbitrary",),
                                             vmem_limit_bytes=VMEM_LIMIT),
        name="scatter",
    )(*args)


def _moe_kernel(te_ref, nu_ref, sg_ref, nx_ref, xs_ref, wg_hbm, wu_hbm, wd_hbm, ys_ref,
                wg_f, wu_f, wd_f, wg_b, wu_b, wd_b, sem):
    i = pl.program_id(0)
    used = i < nu_ref[0]
    first = used & ((i == 0) | (te_ref[i] != te_ref[jnp.maximum(i - 1, 0)]))
    buf = sg_ref[i] % 2

    def weight_copies(expert, b):
        return [pltpu.make_async_copy(hbm.at[expert], vmem.at[b], sem.at[b])
                for hbm, vmem in ((wg_hbm, wg_f), (wu_hbm, wu_f), (wd_hbm, wd_f))]

    @pl.when(used & (i == 0))
    def _():
        for cp in weight_copies(te_ref[0], 0):
            cp.start()

    @pl.when(first)
    def _():
        for cp in weight_copies(te_ref[i], buf):
            cp.wait()
        wg_b[...] = wg_f[buf].astype(BF16)
        wu_b[...] = wu_f[buf].astype(BF16)
        wd_b[...] = wd_f[buf].astype(BF16)

        @pl.when(nx_ref[i] >= 0)
        def _():
            for cp in weight_copies(nx_ref[i], 1 - buf):
                cp.start()

    @pl.when(used)
    def _():
        xb = _load_rows(xs_ref, xs_ref.shape[0] // ROW_SUB).astype(BF16)
        g = jnp.dot(xb, wg_b[...], preferred_element_type=F32)
        u = jnp.dot(xb, wu_b[...], preferred_element_type=F32)
        hid = (_silu(g) * u).astype(BF16)
        _store_rows(ys_ref, jnp.dot(hid, wd_b[...], preferred_element_type=F32))


def _moe(xs, tile_expert, n_used, seg_index, next_expert, w_gate, w_up, w_down, tm):
    p, c = xs.shape[0] // ROW_SUB, LANES
    ne, d, de = w_gate.shape
    n_tiles = p // tm

    def x_map(i, te, nu, sg, nx):
        return (jnp.minimum(i, jnp.maximum(nu[0] - 1, 0)), 0)

    return pl.pallas_call(
        _moe_kernel,
        grid_spec=pltpu.PrefetchScalarGridSpec(
            num_scalar_prefetch=4,
            grid=(n_tiles,),
            in_specs=[pl.BlockSpec((tm * ROW_SUB, c), x_map),
                      pl.BlockSpec(memory_space=pl.ANY),
                      pl.BlockSpec(memory_space=pl.ANY),
                      pl.BlockSpec(memory_space=pl.ANY)],
            out_specs=pl.BlockSpec((tm * ROW_SUB, c), x_map),
            scratch_shapes=[pltpu.VMEM((2, d, de), F32), pltpu.VMEM((2, d, de), F32),
                            pltpu.VMEM((2, de, d), F32),
                            pltpu.VMEM((d, de), BF16), pltpu.VMEM((d, de), BF16),
                            pltpu.VMEM((de, d), BF16), pltpu.SemaphoreType.DMA((2,))]),
        out_shape=jax.ShapeDtypeStruct((p * ROW_SUB, c), F32),
        input_output_aliases={4: 0},
        compiler_params=pltpu.CompilerParams(dimension_semantics=("arbitrary",),
                                             vmem_limit_bytes=VMEM_LIMIT),
        name="moe",
    )(tile_expert, n_used, seg_index, next_expert, xs, w_gate, w_up, w_down)


def _fin_kernel(pos_ref, posn_ref, x1_ref, route_ref, mod_ref, fg_ref, ys_ref, y_ref, ybuf, sem,
                *, tp, d, n_steps):
    i = pl.program_id(0)
    cur = i % 2

    def issue(p_ref, buf):
        def row_start(r, carry):
            dst = pl.ds(pl.multiple_of(r * ROW_SUB, ROW_SUB), ROW_SUB)
            for slot in range(2):
                src = pl.multiple_of(p_ref[0, slot, r], ROW_SUB)
                pltpu.make_async_copy(ys_ref.at[pl.ds(src, ROW_SUB)],
                                      ybuf.at[buf, slot, dst], sem.at[buf]).start(priority=slot)
            return carry
        lax.fori_loop(0, tp, row_start, 0, unroll=8)

    def wait_buf(buf):
        for slot in range(2):
            pltpu.make_async_copy(ys_ref.at[pl.ds(0, tp * ROW_SUB)], ybuf.at[buf, slot],
                                  sem.at[buf]).wait()

    @pl.when(i == 0)
    def _():
        issue(pos_ref, 0)

    wait_buf(cur)
    nxt = 1 - cur
    per_token_mod = mod_ref.shape[1] != 1
    for lo in range(0, tp, FIN_CHUNK):
        hi = min(lo + FIN_CHUNK, tp)
        for r in range(lo, hi):
            for slot in range(2):
                src = pl.multiple_of(posn_ref[0, slot, r], ROW_SUB)
                pltpu.make_async_copy(
                    ys_ref.at[pl.ds(src, ROW_SUB)],
                    ybuf.at[nxt, slot, pl.ds(r * ROW_SUB, ROW_SUB)], sem.at[nxt]).start(priority=slot)
        route = route_ref[lo:hi, :]
        moe = (route[:, 2:3] * _load_rows(ybuf.at[cur, 0], hi - lo, lo)
               + route[:, 3:4] * _load_rows(ybuf.at[cur, 1], hi - lo, lo))
        mod = mod_ref[0, lo:hi, :] if per_token_mod else mod_ref[0]
        y_ref[0, lo:hi, :] = _rms(x1_ref[0, lo:hi, :] + mod[:, 5 * d:6 * d] * moe, fg_ref[...])

    @pl.when(i == n_steps - 1)
    def _():
        wait_buf(nxt)


def _finish(x1, route, mod, final_g, ys, pos3):
    b, t, d = x1.shape
    tp = pos3.shape[2]
    assert t % tp == 0
    nt = t // tp
    n_steps = b * nt
    assert pos3.shape[0] == n_steps
    blk0 = 0
    if mod.shape[1] == 1:
        mod_spec = pl.BlockSpec((1, 1, 6 * d), lambda i: (i // nt, 0, 0))
    else:
        mod_spec = pl.BlockSpec((1, tp, 6 * d), lambda i: (i // nt, i % nt, 0))
    return pl.pallas_call(
        functools.partial(_fin_kernel, tp=tp, d=d, n_steps=n_steps),
        grid_spec=pltpu.PrefetchScalarGridSpec(
            num_scalar_prefetch=0,
            grid=(n_steps,),
            in_specs=[pl.BlockSpec((1, 2, tp), lambda i: (blk0 + i, 0, 0), memory_space=pltpu.SMEM),
                      pl.BlockSpec((1, 2, tp), lambda i: (blk0 + jnp.minimum(i + 1, n_steps - 1), 0, 0),
                                   memory_space=pltpu.SMEM),
                      pl.BlockSpec((1, tp, d), lambda i: (i // nt, i % nt, 0)),
                      pl.BlockSpec((tp, LANES), lambda i: (i, 0)),
                      mod_spec,
                      _const_spec((1, d)),
                      pl.BlockSpec(memory_space=pl.ANY)],
            out_specs=pl.BlockSpec((1, tp, d), lambda i: (i // nt, i % nt, 0)),
            scratch_shapes=[pltpu.VMEM((2, 2, tp * ROW_SUB, LANES), F32),
                            pltpu.SemaphoreType.DMA((2,))]),
        out_shape=jax.ShapeDtypeStruct((b, t, d), F32),
        compiler_params=pltpu.CompilerParams(dimension_semantics=("arbitrary",),
                                             vmem_limit_bytes=VMEM_LIMIT),
        name="fin",
    )(pos3, pos3, x1, route, mod, final_g.reshape(1, d), ys)


def _expert_onehots(route):
    lane = lax.broadcasted_iota(I32, route.shape, 1).astype(F32)
    return ((lane == route[:, 0:1]).astype(F32), (lane == route[:, 1:2]).astype(F32))


def _count_kernel(route_ref, base_ref, tot_ref, acc):
    @pl.when(pl.program_id(0) == 0)
    def _():
        acc[...] = jnp.zeros_like(acc)

    oh0, oh1 = _expert_onehots(route_ref[...])
    base_ref[0] = acc[...]
    acc[...] = acc[...] + jnp.sum(oh0 + oh1, axis=0, keepdims=True)
    tot_ref[...] = acc[...]


def _pos_kernel(route_ref, base_ref, seg_ref, pos_ref, tril_s, *, tp):
    @pl.when(pl.program_id(0) == 0)
    def _():
        r_i = lax.broadcasted_iota(I32, (tp, tp), 0)
        c_i = lax.broadcasted_iota(I32, (tp, tp), 1)
        tril_s[...] = (c_i < r_i).astype(BF16)

    oh0, oh1 = _expert_onehots(route_ref[...])
    before = jnp.dot(tril_s[...], (oh0 + oh1).astype(BF16), preferred_element_type=F32)
    offs = seg_ref[0:1, :] + base_ref[0][0:1, :] + before
    ones8 = jnp.ones((8, LANES), BF16)

    def lane_sum_as_row(v):
        hi = v.astype(BF16)
        r1 = v - hi.astype(F32)
        mid = r1.astype(BF16)
        lo = (r1 - mid.astype(F32)).astype(BF16)
        return _bdot_nt(ones8, hi) + _bdot_nt(ones8, mid) + _bdot_nt(ones8, lo)

    p0 = lane_sum_as_row(oh0 * offs)
    p1 = lane_sum_as_row(oh1 * offs)
    row = lax.broadcasted_iota(I32, (8, tp), 0)
    pos_ref[0] = (jnp.where(row == 0, p0, jnp.where(row == 1, p1, 0.0)) * ROW_SUB).astype(I32)


def _count(route, tp):
    n = route.shape[0]
    assert n % tp == 0
    return pl.pallas_call(
        _count_kernel, grid=(n // tp,),
        in_specs=[pl.BlockSpec((tp, LANES), lambda i: (i, 0))],
        out_specs=[pl.BlockSpec((1, 8, LANES), lambda i: (i, 0, 0)), _const_spec((8, LANES))],
        out_shape=[jax.ShapeDtypeStruct((n // tp, 8, LANES), F32),
                   jax.ShapeDtypeStruct((8, LANES), F32)],
        scratch_shapes=[pltpu.VMEM((8, LANES), F32)],
        compiler_params=pltpu.CompilerParams(dimension_semantics=("arbitrary",),
                                             vmem_limit_bytes=VMEM_LIMIT),
        name="count",
    )(route)


def _positions(route, base, seg_start, tp):
    n = route.shape[0]
    seg = jnp.broadcast_to(jnp.pad(seg_start.astype(F32), (0, LANES - N_EXPERTS))[None, :], (8, LANES))
    pos = pl.pallas_call(
        functools.partial(_pos_kernel, tp=tp), grid=(n // tp,),
        in_specs=[pl.BlockSpec((tp, LANES), lambda i: (i, 0)),
                  pl.BlockSpec((1, 8, LANES), lambda i: (i, 0, 0)),
                  _const_spec((8, LANES))],
        out_specs=pl.BlockSpec((1, 8, tp), lambda i: (i, 0, 0)),
        out_shape=jax.ShapeDtypeStruct((n // tp, 8, tp), I32),
        scratch_shapes=[pltpu.VMEM((tp, tp), BF16)],
        compiler_params=pltpu.CompilerParams(dimension_semantics=("arbitrary",),
                                             vmem_limit_bytes=VMEM_LIMIT),
        name="positions",
    )(route, base, seg)
    return pos[:, 0:2, :]


def _plan(counts, counts_first, n_second, tm, n_tiles):
    tiles_per = (counts + tm - 1) // tm
    tile_end = jnp.cumsum(tiles_per)
    seg_start = (tile_end - tiles_per) * tm
    n_used = tile_end[-1]
    tile_ids = jnp.arange(n_tiles, dtype=I32)
    tile_expert = jnp.sum((tile_ids[:, None] >= tile_end[None, :]).astype(I32), axis=1)
    last_used = jnp.sum((n_used - 1 >= tile_end).astype(I32))
    tile_expert = jnp.where(tile_ids < n_used, tile_expert, last_used).astype(I32)
    assert tm % ZERO_CHUNK == 0
    per_tile = tm // ZERO_CHUNK
    piece = jnp.arange(per_tile, dtype=I32)[None, :] * ZERO_CHUNK
    n_piece = per_tile + 1 + -(-n_second // ZERO_CHUNK)
    first = seg_start + (counts_first // ZERO_CHUNK) * ZERO_CHUNK
    z_expert = first[:, None] + jnp.arange(n_piece, dtype=I32)[None, :] * ZERO_CHUNK
    z_expert = jnp.where(z_expert < (tile_end * tm)[:, None], z_expert, -1)
    spare = n_used + jnp.arange(N_EXPERTS, dtype=I32)
    z_spare = jnp.where((spare < n_tiles)[:, None], spare[:, None] * tm + piece, -1)
    zoff = jnp.concatenate([z_expert.reshape(-1), z_spare.reshape(-1)]).astype(I32)
    experts = jnp.arange(N_EXPERTS, dtype=I32)
    nonempty = tiles_per > 0
    seg_of_expert = jnp.cumsum(nonempty.astype(I32)) - 1
    later = nonempty[None, :] & (experts[None, :] > experts[:, None])
    next_of_expert = jnp.min(jnp.where(later, experts[None, :], N_EXPERTS), axis=1)
    next_of_expert = jnp.where(next_of_expert < N_EXPERTS, next_of_expert, -1)
    seg_index = seg_of_expert[tile_expert].astype(I32)
    next_expert = next_of_expert[tile_expert].astype(I32)
    return seg_start, tile_expert, n_used.reshape(1).astype(I32), zoff, seg_index, next_expert


def kernel(x_prompt, x_sample, c_prompt, c_sample, state_mlstm_C, state_mlstm_n, state_mlstm_m,
           cache_conv, w_ada, b_ada, norm1_g, w_in, b_igate, b_fgate, mh_norm_g, w_dw, b_dw,
           conv_ln_g, conv_ln_b, w_out, norm2_g, w_grp_router, b_grp_router, w_exp_router,
           b_exp_router, w_gate, w_up, w_down, final_g):
    depth = w_ada.shape[0]
    assert depth == 1, "one layer per step"
    bp, tp_, d = x_prompt.shape
    bs = x_sample.shape[0]
    assert x_sample.shape[1] == 1 and d == ROW_SUB * LANES
    dk = state_mlstm_C.shape[-1]
    dm = N_HEADS * dk
    dc = d - dm
    l = 0

    win = w_in[l]
    w_qkvo = win[:, 0:4 * dm]
    w_gates = win[:, 4 * dm:4 * dm + 2 * N_HEADS]
    w_glu = win[:, 4 * dm + 2 * N_HEADS:]
    wmain, wmain_lo = _split_weights(jnp.concatenate([w_qkvo, w_glu], axis=1))
    wout_hi, wout_lo = _split_weights(w_out[l])
    wkt_hi, wkt_lo = _split_weights(win[:, dm:2 * dm].T)
    wg_pad = jnp.pad(w_gates, ((0, 0), (0, LANES - 2 * N_HEADS)))
    wg_hi, wg_lo = _split_weights(wg_pad)
    gbias = jnp.pad(jnp.concatenate([b_igate[l], b_fgate[l]]), (0, LANES - 2 * N_HEADS)).reshape(1, LANES)
    w_r = jnp.concatenate([w_grp_router[l], w_exp_router[l]], axis=1)
    n_r = N_GROUPS + N_EXPERTS
    wr_hi, wr_lo = _split_weights(jnp.pad(w_r, ((0, 0), (0, LANES - n_r))))
    rbias = jnp.pad(jnp.concatenate([b_grp_router[l], b_exp_router[l]]), (0, LANES - n_r)).reshape(1, LANES)
    wts = dict(dm=dm, g1=norm1_g[l].reshape(1, d), g2=norm2_g[l].reshape(1, d), wmain=wmain,
               wg_hi=wg_hi, wg_lo=wg_lo, gbias=gbias, mhg=mh_norm_g[l],
               wdw=w_dw[l].reshape(CONV_W, dc), bdw=b_dw[l].reshape(1, dc),
               clg=conv_ln_g[l].reshape(1, dc), clb=conv_ln_b[l].reshape(1, dc),
               wout=wout_hi, wr_hi=wr_hi, wr_lo=wr_lo, rbias=rbias, wk_t=wkt_hi,
               wmain_lo=wmain_lo, wout_lo=wout_lo, wk_t_lo=wkt_lo)

    mod_p, mod_s = _ada(c_prompt, c_sample, w_ada[l], b_ada[l])

    n_p = bp * tp_
    n_all = n_p + bs
    x1_p, h2_p, route_p, c_p, n_pr, m_p, cv_p, base_p, tot_p = _mix_prompt(x_prompt, mod_p, wts)
    x1_s, h2_s, route_s, c_s, n_s, m_s, u_s = _mix_sample(
        x_sample.reshape(bs, d), mod_s, state_mlstm_C[l], state_mlstm_n[l], state_mlstm_m[l],
        cache_conv[l], wts)

    tm = MOE_TILE
    n_tiles = (2 * n_all) // tm + N_EXPERTS
    tp_p, tp_s = min(PERM_TILE, tp_), min(PERM_TILE, bs)
    assert tp_p == min(MIX_TILE, tp_), "the prompt mixer counts experts per PERM_TILE tokens"
    base_s, tot_s = _count(route_s, tp_s)
    counts = (tot_p[0, :N_EXPERTS] + tot_s[0, :N_EXPERTS]).astype(I32)
    seg_start, tile_expert, n_used, zoff, seg_index, next_expert = _plan(
        counts, tot_p[0, :N_EXPERTS].astype(I32), 2 * bs, tm, n_tiles)
    pos_p = _positions(route_p, base_p, seg_start, tp_p)
    pos_s = _positions(route_s, base_s + tot_p[None], seg_start, tp_s)

    xs = _scatter_rows(h2_p, pos_p, zoff, n_tiles * tm, tm)
    xs = _scatter_rows(h2_s, pos_s, zoff, xs, tm)
    ys = _moe(xs, tile_expert, n_used, seg_index, next_expert, w_gate[l], w_up[l], w_down[l], tm)

    y_p = _finish(x1_p, route_p, mod_p.reshape(bp, 1, -1), final_g, ys, pos_p)
    y_s = _finish(x1_s.reshape(1, bs, d), route_s, mod_s.reshape(1, bs, -1), final_g, ys,
                  pos_s).reshape(bs, 1, d)

    conv_s = jnp.concatenate([cache_conv[l][:, 1:, :], u_s[:, None, :]], axis=1)
    return (y_p, y_s,
            c_p, n_pr, m_p[:, 0, :N_HEADS].reshape(1, bp, N_HEADS), cv_p,
            c_s[None], n_s.reshape(1, bs, N_HEADS, dk), m_s[:, :N_HEADS].reshape(1, bs, N_HEADS),
            conv_s[None])
```

```python
import functools

import jax
import jax.numpy as jnp
from jax import lax
from jax.experimental import pallas as pl
from jax.experimental.pallas import tpu as pltpu

F32 = jnp.float32
BF16 = jnp.bfloat16
I32 = jnp.int32

EPS = 1e-6
LANES = 128
ROW_SUB = 8
CHUNK = 128
N_HEADS = 4
N_GROUPS = 4
EXP_PER_GROUP = 8
N_EXPERTS = N_GROUPS * EXP_PER_GROUP
CONV_W = 31
CONV_PAD = 32
CONV_OFF = CONV_PAD - (CONV_W - 1)
MIX_TILE = 512
MIX_SUB = 256
MOE_TILE = 512
PERM_TILE = 512
FIN_CHUNK = 64
ZERO_CHUNK = 64
VMEM_LIMIT = 56 * 1024 * 1024


def _sigmoid(x):
    return 1.0 / (1.0 + jnp.exp(-x))


def _silu(x):
    return x * _sigmoid(x)


def _log_sigmoid(x):
    return jnp.minimum(x, 0.0) - jnp.log(1.0 + jnp.exp(-jnp.abs(x)))


def _bdot_nt(a, b):
    return lax.dot_general(a.astype(BF16), b.astype(BF16), (((1,), (1,)), ((), ())),
                           preferred_element_type=F32)


def _split2(x):
    hi = x.astype(BF16)
    lo = (x - hi.astype(F32)).astype(BF16)
    return hi, lo


def _split_kernel(w_ref, hi_ref, lo_ref):
    hi, lo = _split2(w_ref[...])
    hi_ref[...] = hi
    lo_ref[...] = lo


def _split_weights(w):
    rows, cols = w.shape
    blk = min(cols, 512)
    assert cols % blk == 0
    spec = pl.BlockSpec((rows, blk), lambda j: (0, j))
    return pl.pallas_call(
        _split_kernel, grid=(cols // blk,), in_specs=[spec], out_specs=[spec, spec],
        out_shape=[jax.ShapeDtypeStruct(w.shape, BF16)] * 2,
        compiler_params=pltpu.CompilerParams(dimension_semantics=("arbitrary",),
                                             vmem_limit_bytes=VMEM_LIMIT),
        name="split",
    )(w)


def _dot3(a, w_hi, w_lo):
    a_hi, a_lo = _split2(a)
    return (jnp.dot(a_hi, w_hi, preferred_element_type=F32)
            + jnp.dot(a_lo, w_hi, preferred_element_type=F32)
            + jnp.dot(a_hi, w_lo, preferred_element_type=F32))


def _cumsum_lanes(triu_bf16, x):
    hi = x.astype(BF16)
    r1 = x - hi.astype(F32)
    mid = r1.astype(BF16)
    lo = (r1 - mid.astype(F32)).astype(BF16)
    return (jnp.dot(hi, triu_bf16, preferred_element_type=F32)
            + jnp.dot(mid, triu_bf16, preferred_element_type=F32)
            + jnp.dot(lo, triu_bf16, preferred_element_type=F32))


def _rms(x, g):
    return x * lax.rsqrt(jnp.mean(x * x, axis=-1, keepdims=True) + EPS) * g


def _layer_norm(x, g, b=None):
    mu = jnp.mean(x, axis=-1, keepdims=True)
    xc = x - mu
    var = jnp.mean(xc * xc, axis=-1, keepdims=True)
    y = xc * lax.rsqrt(var + EPS) * g
    return y if b is None else y + b


def _store_rows(ref, x, row0=0):
    r = x.shape[0]
    for k in range(ROW_SUB):
        ref[pl.ds(row0 * ROW_SUB + k, r, stride=ROW_SUB), :] = x[:, k * LANES:(k + 1) * LANES]


def _load_rows(ref, r, row0=0):
    return jnp.concatenate([ref[pl.ds(row0 * ROW_SUB + k, r, stride=ROW_SUB), :]
                            for k in range(ROW_SUB)], axis=1)


def _route(logits):
    lane = lax.broadcasted_iota(I32, logits.shape, 1).astype(F32)
    neg = jnp.float32(-jnp.inf)
    big = jnp.float32(1e9)
    is_g = lane < N_GROUPS
    gl = jnp.where(is_g, logits, neg)
    gmax = jnp.max(gl, axis=1, keepdims=True)
    gsel = jnp.min(jnp.where(gl == gmax, lane, big), axis=1, keepdims=True)
    pg = 1.0 / jnp.sum(jnp.where(is_g, jnp.exp(gl - gmax), 0.0), axis=1, keepdims=True)
    lo = N_GROUPS + EXP_PER_GROUP * gsel
    emask = (lane >= lo) & (lane < lo + EXP_PER_GROUP)
    el = jnp.where(emask, logits, neg)
    v1 = jnp.max(el, axis=1, keepdims=True)
    i1 = jnp.min(jnp.where(el == v1, lane, big), axis=1, keepdims=True)
    el2 = jnp.where(lane == i1, neg, el)
    v2 = jnp.max(el2, axis=1, keepdims=True)
    i2 = jnp.min(jnp.where(el2 == v2, lane, big), axis=1, keepdims=True)
    d = jnp.exp(v2 - v1)
    w1 = pg / (1.0 + d)
    w2 = pg * d / (1.0 + d)
    return jnp.where(lane == 0, i1 - N_GROUPS,
                     jnp.where(lane == 1, i2 - N_GROUPS,
                               jnp.where(lane == 2, w1, jnp.where(lane == 3, w2, 0.0))))


def _ada_kernel(cp_ref, cs_ref, w_ref, b_ref, op_ref, os_ref):
    w_hi, w_lo = _split2(w_ref[...])
    op_ref[...] = _dot3(_silu(cp_ref[...]), w_hi, w_lo) + b_ref[...]
    os_ref[...] = _dot3(_silu(cs_ref[...]), w_hi, w_lo) + b_ref[...]


def _ada(c_p, c_s, w_ada, b_ada):
    (rp, d), rs = c_p.shape, c_s.shape[0]
    n_out = w_ada.shape[1]
    blk = 1024
    return pl.pallas_call(
        _ada_kernel,
        grid=(n_out // blk,),
        in_specs=[pl.BlockSpec((rp, d), lambda j: (0, 0)),
                  pl.BlockSpec((rs, d), lambda j: (0, 0)),
                  pl.BlockSpec((d, blk), lambda j: (0, j)),
                  pl.BlockSpec((1, blk), lambda j: (0, j))],
        out_specs=[pl.BlockSpec((rp, blk), lambda j: (0, j)),
                   pl.BlockSpec((rs, blk), lambda j: (0, j))],
        out_shape=[jax.ShapeDtypeStruct((rp, n_out), F32), jax.ShapeDtypeStruct((rs, n_out), F32)],
        compiler_params=pltpu.CompilerParams(dimension_semantics=("arbitrary",),
                                             vmem_limit_bytes=VMEM_LIMIT),
        name="ada",
    )(c_p, c_s, w_ada, b_ada.reshape(1, n_out))


def _post(x, attn_cat, mod, g2, wout, wr_hi, wr_lo, rbias, d, wout_lo=None):
    gate1 = mod[:, 2 * d:3 * d]
    sh2 = mod[:, 3 * d:4 * d]
    sc2 = mod[:, 4 * d:5 * d]
    if wout_lo is None:
        proj = jnp.dot(attn_cat.astype(BF16), wout, preferred_element_type=F32)
    else:
        proj = _dot3(attn_cat, wout, wout_lo)
    x1 = x + gate1 * proj
    h2 = _rms(x1, g2) * (1.0 + sc2) + sh2
    logits = _dot3(h2, wr_hi, wr_lo) + rbias
    return x1, h2, _route(logits)


def _mix_prompt_kernel(x_ref, mod_ref, g1_ref, g2_ref, wmain_ref, wgh_ref, wgl_ref, gbias_ref,
                       mhg_ref, wdw_ref, bdw_ref, clg_ref, clb_ref, wout_ref, wrh_ref, wrl_ref,
                       rbias_ref,
                       x1_ref, h2_ref, route_ref, c_ref, n_ref, m_ref, cv_ref, base_ref, tot_ref,
                       ubuf, yc_s, q_s, k_s, v_s, so_s, hm_s, p_s, u_s, cm_s, nb_s, m_s, cnt_s,
                       *, tt, d, dm, dk):
    t = pl.program_id(1)
    dc = d - dm
    n_lt = dc // LANES
    sub = min(MIX_SUB, tt)

    @pl.when(t == 0)
    def _():
        c_ref[...] = jnp.zeros_like(c_ref)
        nb_s[...] = jnp.zeros_like(nb_s)
        m_s[...] = jnp.zeros_like(m_s)
        ubuf[0:CONV_PAD, :] = jnp.zeros((CONV_PAD, dc), F32)

    mod = mod_ref[0]
    row8 = lax.broadcasted_iota(I32, (8, LANES), 0)
    row = lax.broadcasted_iota(I32, (CHUNK, CHUNK), 0)
    col = lax.broadcasted_iota(I32, (CHUNK, CHUNK), 1)
    causal = col <= row
    triu = (row <= col).astype(BF16)
    neg = jnp.float32(-jnp.inf)
    ones_b = jnp.ones((CHUNK, dk), BF16)
    pad_rows = jnp.zeros((CHUNK - 8, LANES), F32)

    lanes_of = [slice(lt * LANES, (lt + 1) * LANES) for lt in range(n_lt)]
    wrows = [[jnp.broadcast_to(wdw_ref[j:j + 1, ls], (8, LANES)) for j in range(CONV_W)]
             for ls in lanes_of]
    bias = [jnp.broadcast_to(bdw_ref[:, ls], (8, LANES)) for ls in lanes_of]

    def partial_sums(r0, lt):
        blocks = [ubuf[r0 + 8 * a:r0 + 8 * a + 8, lanes_of[lt]] for a in range(CONV_PAD // 8)]
        sums = []
        for s in range(8):
            acc = None
            for a in range(CONV_PAD // 8):
                j = 8 * a + s - CONV_OFF
                if 0 <= j < CONV_W:
                    term = blocks[a] * wrows[lt][j]
                    acc = term if acc is None else acc + term
            sums.append(acc)
        return tuple(sums)

    q_prev = [None] * n_lt
    ca = [jnp.concatenate([c_ref[0, 0, hd], nb_s[hd]], axis=1) for hd in range(N_HEADS)]
    m_prev = [m_s[hd:hd + 1, :] for hd in range(N_HEADS)]

    n_sub = tt // sub
    gates_of = {}
    hc_of = {}
    tile_counts = []

    def rows_of(sb):
        return slice(sb * sub, (sb + 1) * sub)

    def proj_items(sb):
        rs = rows_of(sb)
        st = {}

        def head():
            x = x_ref[0, rs, :]
            h = _rms(x, g1_ref[...]) * (1.0 + mod[:, d:2 * d]) + mod[:, 0:d]
            st["hb"] = h.astype(BF16)
            gates_of[sb] = _dot3(h, wgh_ref[...], wgl_ref[...]) + gbias_ref[...]

        def proj(lo, hi):
            return jnp.dot(st["hb"], wmain_ref[:, lo:hi], preferred_element_type=F32)

        def glu():
            ubuf[CONV_PAD + sb * sub:CONV_PAD + (sb + 1) * sub, :] = (
                proj(4 * dm, 4 * dm + dc) * _sigmoid(proj(4 * dm + dc, 4 * dm + 2 * dc)))

        def q():
            q_s[rs, :] = proj(0, dm).astype(BF16)

        def k():
            k_s[rs, :] = proj(dm, 2 * dm) * (dk ** -0.5)

        def v():
            v_s[rs, :] = proj(2 * dm, 3 * dm).astype(BF16)

        def o():
            so_s[rs, :] = _sigmoid(proj(3 * dm, 4 * dm))

        return [head, glu, q, k, v, o]

    def conv_items(sb):
        items = []
        for lt in range(n_lt):
            for i in range(sb * sub // 8 + 1, (sb + 1) * sub // 8 + 1):
                def block(lt=lt, i=i):
                    if i == 1:
                        q_prev[lt] = partial_sums(0, lt)
                    q_cur = partial_sums(i * 8, lt)
                    cur = ubuf[(i - 1) * 8 + CONV_PAD:i * 8 + CONV_PAD, lanes_of[lt]]
                    y = bias[lt] + q_prev[lt][0] + cur * wrows[lt][CONV_W - 1]
                    for s in range(1, 8):
                        merged = jnp.where(row8 < s, q_cur[s], q_prev[lt][s])
                        y = y + pltpu.roll(merged, 8 - s, 0)
                    yc_s[(i - 1) * 8:i * 8, lanes_of[lt]] = y
                    q_prev[lt] = q_cur
                items.append(block)
        return items

    def post_items(sb):
        def post():
            rs = rows_of(sb)
            cat = jnp.concatenate([hm_s[rs, :], hc_of[sb]], axis=1)
            x1, h2, route = _post(x_ref[0, rs, :], cat, mod, g2_ref[...], wout_ref[...],
                                  wrh_ref[...], wrl_ref[...], rbias_ref[...], d)
            x1_ref[0, rs, :] = x1
            _store_rows(h2_ref, h2, sb * sub)
            route_ref[rs, :] = route
            oh0, oh1 = _expert_onehots(route)
            tile_counts.append(jnp.sum(oh0 + oh1, axis=0, keepdims=True))
        return [post]

    def interleave(main, side):
        gap = len(main) / (len(side) + 1)
        due, done = gap, 0
        for n, item in enumerate(main):
            item()
            while done < len(side) and n + 1 >= due:
                side[done]()
                done += 1
                due += gap
        for item in side[done:]:
            item()

    for item in proj_items(0):
        item()
    for sb in range(n_sub):
        r_lo = sb * sub
        rs = rows_of(sb)
        side = post_items(sb - 1) if sb > 0 else []
        if sb + 1 < n_sub:
            side = side + proj_items(sb + 1)
        interleave(conv_items(sb), side)
        hc_of[sb] = _silu(_layer_norm(yc_s[rs, :], clg_ref[...], clb_ref[...]))
        gates = gates_of[sb]

        chunks = range(r_lo // CHUNK, (r_lo + sub) // CHUNK)
        b_cols = {}
        for c in chunks:
            r0 = c * CHUNK
            g8 = gates[r0 - r_lo:r0 - r_lo + CHUNK, :].T[0:8, :]
            b8 = _cumsum_lanes(triu, _log_sigmoid(g8))
            pk8 = jnp.where(row8 < N_HEADS, g8 - pltpu.roll(b8, N_HEADS, 0), b8)
            pk = jnp.concatenate([pk8, pad_rows], axis=0).T
            for hd in range(N_HEADS):
                cs = slice(hd * dk, (hd + 1) * dk)
                idx = c * N_HEADS + hd
                kf = k_s[r0:r0 + CHUNK, cs]
                va = jnp.concatenate([v_s[r0:r0 + CHUNK, cs], ones_b], axis=1)
                gm = jnp.where(causal, pk8[hd:hd + 1, :], neg)
                cm = jnp.max(gm, axis=1, keepdims=True)
                s = _bdot_nt(q_s[r0:r0 + CHUNK, cs], kf) * jnp.exp(gm - cm)
                p_s[idx] = jnp.dot(s.astype(BF16), va, preferred_element_type=F32)
                cm_s[idx] = jnp.broadcast_to(cm, (CHUNK, LANES))
                kw = kf * jnp.exp(pk[:, hd:hd + 1] - cm[CHUNK - 1:CHUNK, :])
                u_s[idx] = jnp.dot(kw.T.astype(BF16), va, preferred_element_type=F32)
                b_cols[idx] = pk[:, N_HEADS + hd:N_HEADS + hd + 1]

        for hd in range(N_HEADS):
            cs = slice(hd * dk, (hd + 1) * dk)
            for c in chunks:
                r0 = c * CHUNK
                idx = c * N_HEADS + hd
                cm = cm_s[idx]
                b_col = b_cols[idx]
                mt = jnp.maximum(m_prev[hd], cm)
                f_loc = jnp.exp(cm - mt)
                a_int = jnp.exp(m_prev[hd] - mt)
                qc = jnp.dot(q_s[r0:r0 + CHUNK, cs], ca[hd].astype(BF16),
                             preferred_element_type=F32)
                p = p_s[idx]
                num = f_loc * p[:, :dk] + a_int * qc[:, :dk]
                den = f_loc * p[:, dk:] + a_int * qc[:, dk:]
                hh = num / jnp.maximum(jnp.abs(den), jnp.exp(-(b_col + mt)))
                hm_s[r0:r0 + CHUNK, cs] = (_layer_norm(hh, mhg_ref[hd:hd + 1, :])
                                           * so_s[r0:r0 + CHUNK, cs])
                mt_l = mt[CHUNK - 1:CHUNK, :]
                u = u_s[idx]
                f_l = f_loc[CHUNK - 1:CHUNK, :]
                a_l = a_int[CHUNK - 1:CHUNK, :]
                ca[hd] = jnp.concatenate([a_l * ca[hd][:, :dk] + f_l * u[:, :dk],
                                          a_l * ca[hd][:, dk:] + f_l * u[:, dk:]], axis=1)
                m_prev[hd] = b_col[CHUNK - 1:CHUNK, :] + mt_l

    for item in post_items(n_sub - 1):
        item()

    for hd in range(N_HEADS):
        c_ref[0, 0, hd] = ca[hd][:, :dk]
        nb_s[hd] = ca[hd][:, dk:]
        m_s[hd:hd + 1, :] = m_prev[hd]
    cv_ref[0, 0] = ubuf[tt + CONV_PAD - (CONV_W - 1):tt + CONV_PAD, :]
    ubuf[0:CONV_PAD, :] = ubuf[tt:tt + CONV_PAD, :]

    @pl.when(t == pl.num_programs(1) - 1)
    def _():
        for hd in range(N_HEADS):
            n_ref[0, 0, hd:hd + 1, :] = nb_s[hd].T[0:1, :]

    lane1 = lax.broadcasted_iota(I32, (1, LANES), 1)
    m_row = jnp.zeros((1, LANES), F32)
    for hd in range(N_HEADS):
        m_row = jnp.where(lane1 == hd, m_s[hd:hd + 1, :], m_row)
    m_ref[0] = m_row

    @pl.when((pl.program_id(0) == 0) & (t == 0))
    def _():
        cnt_s[...] = jnp.zeros_like(cnt_s)

    base_ref[0] = cnt_s[...]
    cnt_s[...] = cnt_s[...] + sum(tile_counts)
    tot_ref[...] = cnt_s[...]


def _const_spec(shape):
    nd = len(shape)
    return pl.BlockSpec(shape, lambda *_: (0,) * nd)


def _mix_prompt(x, mod, wts):
    b, t, d = x.shape
    dm = wts["dm"]
    dk = dm // N_HEADS
    dc = d - dm
    tt = min(MIX_TILE, t)
    assert t % tt == 0 and tt % CHUNK == 0 and tt >= CONV_PAD
    nt = t // tt
    kern = functools.partial(_mix_prompt_kernel, tt=tt, d=d, dm=dm, dk=dk)
    const_names = ["g1", "g2", "wmain", "wg_hi", "wg_lo", "gbias", "mhg", "wdw", "bdw", "clg",
                   "clb", "wout", "wr_hi", "wr_lo", "rbias"]
    consts = [wts[k] for k in const_names]
    in_specs = ([pl.BlockSpec((1, tt, d), lambda i, j: (i, j, 0)),
                 pl.BlockSpec((1, 1, mod.shape[-1]), lambda i, j: (i, 0, 0))]
                + [_const_spec(c.shape) for c in consts])
    out_shape = [
        jax.ShapeDtypeStruct((b, t, d), F32),
        jax.ShapeDtypeStruct((b * t * ROW_SUB, LANES), F32),
        jax.ShapeDtypeStruct((b * t, LANES), F32),
        jax.ShapeDtypeStruct((1, b, N_HEADS, dk, dk), F32),
        jax.ShapeDtypeStruct((1, b, N_HEADS, dk), F32),
        jax.ShapeDtypeStruct((b, 1, LANES), F32),
        jax.ShapeDtypeStruct((1, b, CONV_W - 1, dc), F32),
        jax.ShapeDtypeStruct((b * nt, 8, LANES), F32),
        jax.ShapeDtypeStruct((8, LANES), F32),
    ]
    out_specs = [
        pl.BlockSpec((1, tt, d), lambda i, j: (i, j, 0)),
        pl.BlockSpec((tt * ROW_SUB, LANES), lambda i, j: (i * nt + j, 0)),
        pl.BlockSpec((tt, LANES), lambda i, j: (i * nt + j, 0)),
        pl.BlockSpec((1, 1, N_HEADS, dk, dk), lambda i, j: (0, i, 0, 0, 0)),
        pl.BlockSpec((1, 1, N_HEADS, dk), lambda i, j: (0, i, 0, 0)),
        pl.BlockSpec((1, 1, LANES), lambda i, j: (i, 0, 0)),
        pl.BlockSpec((1, 1, CONV_W - 1, dc), lambda i, j: (0, i, 0, 0)),
        pl.BlockSpec((1, 8, LANES), lambda i, j: (i * nt + j, 0, 0)),
        pl.BlockSpec((8, LANES), lambda i, j: (0, 0)),
    ]
    n_hc = (tt // CHUNK) * N_HEADS
    scratch = [pltpu.VMEM((tt + CONV_PAD, dc), F32),
               pltpu.VMEM((tt, dc), F32),
               pltpu.VMEM((tt, dm), BF16),
               pltpu.VMEM((tt, dm), F32),
               pltpu.VMEM((tt, dm), BF16),
               pltpu.VMEM((tt, dm), F32),
               pltpu.VMEM((tt, dm), F32),
               pltpu.VMEM((n_hc, CHUNK, 2 * dk), F32),
               pltpu.VMEM((n_hc, CHUNK, 2 * dk), F32),
               pltpu.VMEM((n_hc, CHUNK, LANES), F32),
               pltpu.VMEM((N_HEADS, dk, LANES), F32),
               pltpu.VMEM((8, LANES), F32),
               pltpu.VMEM((8, LANES), F32)]
    return pl.pallas_call(
        kern, grid=(b, nt), in_specs=in_specs, out_specs=out_specs, out_shape=out_shape,
        scratch_shapes=scratch,
        compiler_params=pltpu.CompilerParams(dimension_semantics=("arbitrary", "arbitrary"),
                                             vmem_limit_bytes=VMEM_LIMIT),
        name="mix_p",
    )(x, mod.reshape(b, 1, -1), *consts)


def _s_pre_kernel(x_ref, mod_ref, g1_ref, wmain_ref, wmainlo_ref, wgh_ref, wgl_ref, gbias_ref,
                  wkt_ref, wktlo_ref, wdw_ref, bdw_ref, clg_ref, clb_ref, cache_ref, n0_ref, m0_ref,
                  q_ref, kt_ref, vs_ref, ab_ref, sv_ref, den_ref, eb_ref, o_ref, hc_ref, u_ref,
                  n_ref, m_ref, *, d, dm, dk):
    dc = d - dm
    x = x_ref[...]
    mod = mod_ref[...]
    sh1 = mod[:, 0:d]
    sc1 = mod[:, d:2 * d]
    h = _rms(x, g1_ref[...]) * (1.0 + sc1) + sh1
    z = _dot3(h, wmain_ref[...], wmainlo_ref[...])
    gates = _dot3(h, wgh_ref[...], wgl_ref[...]) + gbias_ref[...]
    scale = dk ** -0.5
    h_hi, h_lo = _split2(h)
    kt = _bdot_nt(wkt_ref[...], h_hi) + _bdot_nt(wktlo_ref[...], h_hi) + _bdot_nt(wkt_ref[...], h_lo)
    kt_ref[...] = (kt * scale).astype(BF16)
    k_all = z[:, dm:2 * dm] * scale
    ga = z[:, 4 * dm:4 * dm + dc]
    gb = z[:, 4 * dm + dc:4 * dm + 2 * dc]
    u = ga * _sigmoid(gb)
    u_ref[...] = u
    acc = jnp.broadcast_to(bdw_ref[...], u.shape) + u * wdw_ref[CONV_W - 1:CONV_W, :]
    for j in range(CONV_W - 1):
        acc = acc + cache_ref[j] * wdw_ref[j:j + 1, :]
    hc_ref[...] = _silu(_layer_norm(acc, clg_ref[...], clb_ref[...]))
    o_ref[...] = z[:, 3 * dm:4 * dm]
    q_ref[...] = z[:, 0:dm]
    m0 = m0_ref[...]
    n0 = n0_ref[...]
    lane1 = lax.broadcasted_iota(I32, (1, LANES), 1)
    m_new = jnp.zeros(m0.shape, F32)
    for hd in range(N_HEADS):
        cs = slice(hd * dk, (hd + 1) * dk)
        ig = gates[:, hd:hd + 1]
        lf = _log_sigmoid(gates[:, N_HEADS + hd:N_HEADS + hd + 1])
        mp = m0[:, hd:hd + 1]
        inter = lf + mp
        mt = jnp.maximum(inter, ig)
        w = jnp.exp(ig - mt)
        a_int = jnp.exp(inter - mt)
        qf = z[:, cs]
        kf = k_all[:, cs]
        vf = z[:, 2 * dm + hd * dk:2 * dm + (hd + 1) * dk]
        s = jnp.sum(qf * kf, axis=1, keepdims=True) * w
        sv_ref[:, cs] = s * vf
        den_ref[:, cs] = jnp.broadcast_to(
            s + a_int * jnp.sum(qf * n0[:, cs], axis=1, keepdims=True), (x.shape[0], dk))
        eb_ref[:, cs] = jnp.broadcast_to(jnp.exp(-mt), (x.shape[0], dk))
        ab_ref[:, cs] = jnp.broadcast_to(a_int, (x.shape[0], dk))
        vs_ref[:, cs] = (vf * w).astype(BF16)
        n_ref[:, cs] = a_int * n0[:, cs] + w * kf
        m_new = jnp.where(lane1 == hd, mt, m_new)
    m_ref[...] = m_new


def _s_state_kernel(q_ref, kt_ref, vs_ref, ab_ref, c0_ref, c_ref, r_ref, *, bb, dk):
    i = pl.program_id(0)
    nb = q_ref.shape[0]
    rows = lax.broadcasted_iota(I32, (nb, dk), 0)

    step_rows = pl.ds(pl.multiple_of(i * bb, bb), bb)
    a_blk = ab_ref[step_rows, :]
    q_hi, q_lo = _split2(q_ref[step_rows, :])
    row_b = lax.broadcasted_iota(I32, (bb, dk), 0)
    r_acc = [jnp.zeros((bb, dk), F32) for _ in range(N_HEADS)]
    for j in range(bb):
        sel = rows == i * bb + j
        for hd in range(N_HEADS):
            cs = slice(hd * dk, (hd + 1) * dk)
            c0 = c0_ref[j, hd]
            vmask = jnp.where(sel, vs_ref[:, cs], jnp.zeros((), BF16))
            c_ref[j, hd] = (a_blk[j:j + 1, cs] * c0
                            + jnp.dot(kt_ref[cs, :], vmask, preferred_element_type=F32))
            c_hi, c_lo = _split2(c0)
            r = (jnp.dot(q_hi[:, cs], c_hi, preferred_element_type=F32)
                 + jnp.dot(q_lo[:, cs], c_hi, preferred_element_type=F32)
                 + jnp.dot(q_hi[:, cs], c_lo, preferred_element_type=F32))
            r_acc[hd] = r_acc[hd] + jnp.where(row_b == j, r, 0.0)
    for hd in range(N_HEADS):
        r_ref[step_rows, hd * dk:(hd + 1) * dk] = r_acc[hd]


def _s_post_kernel(x_ref, mod_ref, g2_ref, mhg_ref, r_ref, ab_ref, sv_ref, den_ref, eb_ref, o_ref,
                   hc_ref, wout_ref, woutlo_ref, wrh_ref, wrl_ref, rbias_ref,
                   x1_ref, h2_ref, route_ref, *, d, dm, dk):
    hm = []
    for hd in range(N_HEADS):
        cs = slice(hd * dk, (hd + 1) * dk)
        num = sv_ref[:, cs] + ab_ref[:, cs] * r_ref[:, cs]
        hh = num / jnp.maximum(jnp.abs(den_ref[:, cs]), eb_ref[:, cs])
        hm.append(_layer_norm(hh, mhg_ref[hd:hd + 1, :]) * _sigmoid(o_ref[:, cs]))
    cat = jnp.concatenate(hm + [hc_ref[...]], axis=1)
    x1, h2, route = _post(x_ref[...], cat, mod_ref[...], g2_ref[...], wout_ref[...], wrh_ref[...],
                          wrl_ref[...], rbias_ref[...], d, wout_lo=woutlo_ref[...])
    x1_ref[...] = x1
    _store_rows(h2_ref, h2)
    route_ref[...] = route


def _mix_sample(x, mod, c0, n0, m0, cache, wts):
    nb, d = x.shape
    dm = wts["dm"]
    dk = dm // N_HEADS
    dc = d - dm
    cp = pltpu.CompilerParams(dimension_semantics=("arbitrary",), vmem_limit_bytes=VMEM_LIMIT)
    cache_t = jnp.transpose(cache, (1, 0, 2))
    m0p = jnp.pad(m0, ((0, 0), (0, LANES - N_HEADS)))
    pre_in = [x, mod, wts["g1"], wts["wmain"], wts["wmain_lo"], wts["wg_hi"], wts["wg_lo"],
              wts["gbias"], wts["wk_t"], wts["wk_t_lo"], wts["wdw"], wts["bdw"], wts["clg"],
              wts["clb"], cache_t, n0.reshape(nb, dm), m0p]
    pre_out = [jax.ShapeDtypeStruct((nb, dm), F32),
               jax.ShapeDtypeStruct((dm, nb), BF16),
               jax.ShapeDtypeStruct((nb, dm), BF16),
               jax.ShapeDtypeStruct((nb, dm), F32),
               jax.ShapeDtypeStruct((nb, dm), F32),
               jax.ShapeDtypeStruct((nb, dm), F32),
               jax.ShapeDtypeStruct((nb, dm), F32),
               jax.ShapeDtypeStruct((nb, dm), F32),
               jax.ShapeDtypeStruct((nb, dc), F32),
               jax.ShapeDtypeStruct((nb, dc), F32),
               jax.ShapeDtypeStruct((nb, dm), F32),
               jax.ShapeDtypeStruct((nb, LANES), F32)]
    (q, kt, vs, ab, sv, den, eb, o, hc, u, n1, m1) = pl.pallas_call(
        functools.partial(_s_pre_kernel, d=d, dm=dm, dk=dk),
        grid=(1,),
        in_specs=[_const_spec(a.shape) for a in pre_in],
        out_specs=[_const_spec(s.shape) for s in pre_out],
        out_shape=pre_out, compiler_params=cp, name="s_pre")(*pre_in)

    bb = 8
    assert nb % bb == 0
    c1, r = pl.pallas_call(
        functools.partial(_s_state_kernel, bb=bb, dk=dk),
        grid=(nb // bb,),
        in_specs=[_const_spec(q.shape), _const_spec(kt.shape), _const_spec(vs.shape),
                  _const_spec(ab.shape),
                  pl.BlockSpec((bb, N_HEADS, dk, dk), lambda i: (i, 0, 0, 0))],
        out_specs=[pl.BlockSpec((bb, N_HEADS, dk, dk), lambda i: (i, 0, 0, 0)),
                   _const_spec((nb, dm))],
        out_shape=[jax.ShapeDtypeStruct((nb, N_HEADS, dk, dk), F32),
                   jax.ShapeDtypeStruct((nb, dm), F32)],
        compiler_params=cp, name="s_state")(q, kt, vs, ab, c0)

    post_in = [x, mod, wts["g2"], wts["mhg"], r, ab, sv, den, eb, o, hc, wts["wout"],
               wts["wout_lo"], wts["wr_hi"], wts["wr_lo"], wts["rbias"]]
    post_out = [jax.ShapeDtypeStruct((nb, d), F32),
                jax.ShapeDtypeStruct((nb * ROW_SUB, LANES), F32),
                jax.ShapeDtypeStruct((nb, LANES), F32)]
    x1, h2, route = pl.pallas_call(
        functools.partial(_s_post_kernel, d=d, dm=dm, dk=dk),
        grid=(1,),
        in_specs=[_const_spec(a.shape) for a in post_in],
        out_specs=[_const_spec(s.shape) for s in post_out],
        out_shape=post_out, compiler_params=cp, name="s_post")(*post_in)
    return x1, h2, route, c1, n1, m1, u


def _scatter_kernel(pos_ref, zoff_ref, src_ref, *rest, tp, tm, create):
    xs_ref, zbuf, sem, zsem = rest[-4:]
    i = pl.program_id(0)

    if create:
        @pl.when(i == 0)
        def _():
            zbuf[...] = jnp.zeros_like(zbuf)

            def zero_copy(k):
                start = pl.multiple_of(zoff_ref[k] * ROW_SUB, ZERO_CHUNK * ROW_SUB)
                return pltpu.make_async_copy(zbuf, xs_ref.at[pl.ds(start, ZERO_CHUNK * ROW_SUB)], zsem)

            def start(k, carry):
                @pl.when(zoff_ref[k] >= 0)
                def _():
                    zero_copy(k).start()
                return carry

            def wait(k, carry):
                @pl.when(zoff_ref[k] >= 0)
                def _():
                    zero_copy(k).wait()
                return carry

            lax.fori_loop(0, zoff_ref.shape[0], start, 0)
            lax.fori_loop(0, zoff_ref.shape[0], wait, 0)

    for r in range(tp):
        src = src_ref.at[pl.ds(r * ROW_SUB, ROW_SUB)]
        for slot in range(2):
            dst = pl.multiple_of(pos_ref[0, slot, r], ROW_SUB)
            pltpu.make_async_copy(src, xs_ref.at[pl.ds(dst, ROW_SUB)], sem).start(priority=slot)
    for slot in range(2):
        pltpu.make_async_copy(src_ref, xs_ref.at[pl.ds(0, tp * ROW_SUB)], sem).wait()


def _scatter_rows(h2, pos3, zoff, xs_or_rows, tm):
    n, c = h2.shape[0] // ROW_SUB, LANES
    tp = pos3.shape[2]
    assert pos3.shape == (n // tp, 2, tp)
    create = isinstance(xs_or_rows, int)
    n_sorted = xs_or_rows * ROW_SUB if create else xs_or_rows.shape[0]
    in_specs = [pl.BlockSpec((1, 2, tp), lambda i: (i, 0, 0), memory_space=pltpu.SMEM),
                pl.BlockSpec(memory_space=pltpu.SMEM),
                pl.BlockSpec((tp * ROW_SUB, c), lambda i: (i, 0))]
    args = [pos3, zoff, h2]
    if not create:
        in_specs.append(pl.BlockSpec(memory_space=pl.ANY))
        args.append(xs_or_rows)
    return pl.pallas_call(
        functools.partial(_scatter_kernel, tp=tp, tm=tm, create=create),
        grid_spec=pltpu.PrefetchScalarGridSpec(
            num_scalar_prefetch=0,
            grid=(n // tp,),
            in_specs=in_specs,
            out_specs=pl.BlockSpec(memory_space=pl.ANY),
            scratch_shapes=[pltpu.VMEM((ZERO_CHUNK * ROW_SUB, c), F32), pltpu.SemaphoreType.DMA(()),
                            pltpu.SemaphoreType.DMA(())]),
        out_shape=jax.ShapeDtypeStruct((n_sorted, c), F32),
        input_output_aliases={} if create else {3: 0},
        compiler_params=pltpu.CompilerParams(dimension_semantics=("arbitrary",),
                                             vmem_limit_bytes=VMEM_LIMIT),
        name="scatter",
    )(*args)


def _moe_kernel(te_ref, nu_ref, sg_ref, nx_ref, xs_ref, wg_hbm, wu_hbm, wd_hbm, ys_ref,
                wg_f, wu_f, wd_f, wg_b, wu_b, wd_b, sem):
    i = pl.program_id(0)
    used = i < nu_ref[0]
    first = used & ((i == 0) | (te_ref[i] != te_ref[jnp.maximum(i - 1, 0)]))
    buf = sg_ref[i] % 2

    def weight_copies(expert, b):
        return [pltpu.make_async_copy(hbm.at[expert], vmem.at[b], sem.at[b])
                for hbm, vmem in ((wg_hbm, wg_f), (wu_hbm, wu_f), (wd_hbm, wd_f))]

    @pl.when(used & (i == 0))
    def _():
        for cp in weight_copies(te_ref[0], 0):
            cp.start()

    @pl.when(first)
    def _():
        for cp in weight_copies(te_ref[i], buf):
            cp.wait()
        wg_b[...] = wg_f[buf].astype(BF16)
        wu_b[...] = wu_f[buf].astype(BF16)
        wd_b[...] = wd_f[buf].astype(BF16)

        @pl.when(nx_ref[i] >= 0)
        def _():
            for cp in weight_copies(nx_ref[i], 1 - buf):
                cp.start()

    @pl.when(used)
    def _():
        xb = _load_rows(xs_ref, xs_ref.shape[0] // ROW_SUB).astype(BF16)
        g = jnp.dot(xb, wg_b[...], preferred_element_type=F32)
        u = jnp.dot(xb, wu_b[...], preferred_element_type=F32)
        hid = (_silu(g) * u).astype(BF16)
        _store_rows(ys_ref, jnp.dot(hid, wd_b[...], preferred_element_type=F32))


def _moe(xs, tile_expert, n_used, seg_index, next_expert, w_gate, w_up, w_down, tm):
    p, c = xs.shape[0] // ROW_SUB, LANES
    ne, d, de = w_gate.shape
    n_tiles = p // tm

    def x_map(i, te, nu, sg, nx):
        return (jnp.minimum(i, jnp.maximum(nu[0] - 1, 0)), 0)

    return pl.pallas_call(
        _moe_kernel,
        grid_spec=pltpu.PrefetchScalarGridSpec(
            num_scalar_prefetch=4,
            grid=(n_tiles,),
            in_specs=[pl.BlockSpec((tm * ROW_SUB, c), x_map),
                      pl.BlockSpec(memory_space=pl.ANY),
                      pl.BlockSpec(memory_space=pl.ANY),
                      pl.BlockSpec(memory_space=pl.ANY)],
            out_specs=pl.BlockSpec((tm * ROW_SUB, c), x_map),
            scratch_shapes=[pltpu.VMEM((2, d, de), F32), pltpu.VMEM((2, d, de), F32),
                            pltpu.VMEM((2, de, d), F32),
                            pltpu.VMEM((d, de), BF16), pltpu.VMEM((d, de), BF16),
                            pltpu.VMEM((de, d), BF16), pltpu.SemaphoreType.DMA((2,))]),
        out_shape=jax.ShapeDtypeStruct((p * ROW_SUB, c), F32),
        input_output_aliases={4: 0},
        compiler_params=pltpu.CompilerParams(dimension_semantics=("arbitrary",),
                                             vmem_limit_bytes=VMEM_LIMIT),
        name="moe",
    )(tile_expert, n_used, seg_index, next_expert, xs, w_gate, w_up, w_down)


def _fin_kernel(pos_ref, posn_ref, x1_ref, route_ref, mod_ref, fg_ref, ys_ref, y_ref, ybuf, sem,
                *, tp, d, n_steps):
    i = pl.program_id(0)
    cur = i % 2

    def issue(p_ref, buf):
        def row_start(r, carry):
            dst = pl.ds(pl.multiple_of(r * ROW_SUB, ROW_SUB), ROW_SUB)
            for slot in range(2):
                src = pl.multiple_of(p_ref[0, slot, r], ROW_SUB)
                pltpu.make_async_copy(ys_ref.at[pl.ds(src, ROW_SUB)],
                                      ybuf.at[buf, slot, dst], sem.at[buf]).start(priority=slot)
            return carry
        lax.fori_loop(0, tp, row_start, 0, unroll=8)

    def wait_buf(buf):
        for slot in range(2):
            pltpu.make_async_copy(ys_ref.at[pl.ds(0, tp * ROW_SUB)], ybuf.at[buf, slot],
                                  sem.at[buf]).wait()

    @pl.when(i == 0)
    def _():
        issue(pos_ref, 0)

    wait_buf(cur)
    nxt = 1 - cur
    per_token_mod = mod_ref.shape[1] != 1
    for lo in range(0, tp, FIN_CHUNK):
        hi = min(lo + FIN_CHUNK, tp)
        for r in range(lo, hi):
            for slot in range(2):
                src = pl.multiple_of(posn_ref[0, slot, r], ROW_SUB)
                pltpu.make_async_copy(
                    ys_ref.at[pl.ds(src, ROW_SUB)],
                    ybuf.at[nxt, slot, pl.ds(r * ROW_SUB, ROW_SUB)], sem.at[nxt]).start(priority=slot)
        route = route_ref[lo:hi, :]
        moe = (route[:, 2:3] * _load_rows(ybuf.at[cur, 0], hi - lo, lo)
               + route[:, 3:4] * _load_rows(ybuf.at[cur, 1], hi - lo, lo))
        mod = mod_ref[0, lo:hi, :] if per_token_mod else mod_ref[0]
        y_ref[0, lo:hi, :] = _rms(x1_ref[0, lo:hi, :] + mod[:, 5 * d:6 * d] * moe, fg_ref[...])

    @pl.when(i == n_steps - 1)
    def _():
        wait_buf(nxt)


def _finish(x1, route, mod, final_g, ys, pos3):
    b, t, d = x1.shape
    tp = pos3.shape[2]
    assert t % tp == 0
    nt = t // tp
    n_steps = b * nt
    assert pos3.shape[0] == n_steps
    blk0 = 0
    if mod.shape[1] == 1:
        mod_spec = pl.BlockSpec((1, 1, 6 * d), lambda i: (i // nt, 0, 0))
    else:
        mod_spec = pl.BlockSpec((1, tp, 6 * d), lambda i: (i // nt, i % nt, 0))
    return pl.pallas_call(
        functools.partial(_fin_kernel, tp=tp, d=d, n_steps=n_steps),
        grid_spec=pltpu.PrefetchScalarGridSpec(
            num_scalar_prefetch=0,
            grid=(n_steps,),
            in_specs=[pl.BlockSpec((1, 2, tp), lambda i: (blk0 + i, 0, 0), memory_space=pltpu.SMEM),
                      pl.BlockSpec((1, 2, tp), lambda i: (blk0 + jnp.minimum(i + 1, n_steps - 1), 0, 0),
                                   memory_space=pltpu.SMEM),
                      pl.BlockSpec((1, tp, d), lambda i: (i // nt, i % nt, 0)),
                      pl.BlockSpec((tp, LANES), lambda i: (i, 0)),
                      mod_spec,
                      _const_spec((1, d)),
                      pl.BlockSpec(memory_space=pl.ANY)],
            out_specs=pl.BlockSpec((1, tp, d), lambda i: (i // nt, i % nt, 0)),
            scratch_shapes=[pltpu.VMEM((2, 2, tp * ROW_SUB, LANES), F32),
                            pltpu.SemaphoreType.DMA((2,))]),
        out_shape=jax.ShapeDtypeStruct((b, t, d), F32),
        compiler_params=pltpu.CompilerParams(dimension_semantics=("arbitrary",),
                                             vmem_limit_bytes=VMEM_LIMIT),
        name="fin",
    )(pos3, pos3, x1, route, mod, final_g.reshape(1, d), ys)


def _expert_onehots(route):
    lane = lax.broadcasted_iota(I32, route.shape, 1).astype(F32)
    return ((lane == route[:, 0:1]).astype(F32), (lane == route[:, 1:2]).astype(F32))


def _count_kernel(route_ref, base_ref, tot_ref, acc):
    @pl.when(pl.program_id(0) == 0)
    def _():
        acc[...] = jnp.zeros_like(acc)

    oh0, oh1 = _expert_onehots(route_ref[...])
    base_ref[0] = acc[...]
    acc[...] = acc[...] + jnp.sum(oh0 + oh1, axis=0, keepdims=True)
    tot_ref[...] = acc[...]


def _pos_kernel(route_ref, base_ref, seg_ref, pos_ref, tril_s, *, tp):
    @pl.when(pl.program_id(0) == 0)
    def _():
        r_i = lax.broadcasted_iota(I32, (tp, tp), 0)
        c_i = lax.broadcasted_iota(I32, (tp, tp), 1)
        tril_s[...] = (c_i < r_i).astype(BF16)

    oh0, oh1 = _expert_onehots(route_ref[...])
    before = jnp.dot(tril_s[...], (oh0 + oh1).astype(BF16), preferred_element_type=F32)
    offs = seg_ref[0:1, :] + base_ref[0][0:1, :] + before
    ones8 = jnp.ones((8, LANES), BF16)

    def lane_sum_as_row(v):
        hi = v.astype(BF16)
        r1 = v - hi.astype(F32)
        mid = r1.astype(BF16)
        lo = (r1 - mid.astype(F32)).astype(BF16)
        return _bdot_nt(ones8, hi) + _bdot_nt(ones8, mid) + _bdot_nt(ones8, lo)

    p0 = lane_sum_as_row(oh0 * offs)
    p1 = lane_sum_as_row(oh1 * offs)
    row = lax.broadcasted_iota(I32, (8, tp), 0)
    pos_ref[0] = (jnp.where(row == 0, p0, jnp.where(row == 1, p1, 0.0)) * ROW_SUB).astype(I32)


def _count(route, tp):
    n = route.shape[0]
    assert n % tp == 0
    return pl.pallas_call(
        _count_kernel, grid=(n // tp,),
        in_specs=[pl.BlockSpec((tp, LANES), lambda i: (i, 0))],
        out_specs=[pl.BlockSpec((1, 8, LANES), lambda i: (i, 0, 0)), _const_spec((8, LANES))],
        out_shape=[jax.ShapeDtypeStruct((n // tp, 8, LANES), F32),
                   jax.ShapeDtypeStruct((8, LANES), F32)],
        scratch_shapes=[pltpu.VMEM((8, LANES), F32)],
        compiler_params=pltpu.CompilerParams(dimension_semantics=("arbitrary",),
                                             vmem_limit_bytes=VMEM_LIMIT),
        name="count",
    )(route)


def _positions(route, base, seg_start, tp):
    n = route.shape[0]
    seg = jnp.broadcast_to(jnp.pad(seg_start.astype(F32), (0, LANES - N_EXPERTS))[None, :], (8, LANES))
    pos = pl.pallas_call(
        functools.partial(_pos_kernel, tp=tp), grid=(n // tp,),
        in_specs=[pl.BlockSpec((tp, LANES), lambda i: (i, 0)),
                  pl.BlockSpec((1, 8, LANES), lambda i: (i, 0, 0)),
                  _const_spec((8, LANES))],
        out_specs=pl.BlockSpec((1, 8, tp), lambda i: (i, 0, 0)),
        out_shape=jax.ShapeDtypeStruct((n // tp, 8, tp), I32),
        scratch_shapes=[pltpu.VMEM((tp, tp), BF16)],
        compiler_params=pltpu.CompilerParams(dimension_semantics=("arbitrary",),
                                             vmem_limit_bytes=VMEM_LIMIT),
        name="positions",
    )(route, base, seg)
    return pos[:, 0:2, :]


def _plan(counts, counts_first, n_second, tm, n_tiles):
    tiles_per = (counts + tm - 1) // tm
    tile_end = jnp.cumsum(tiles_per)
    seg_start = (tile_end - tiles_per) * tm
    n_used = tile_end[-1]
    tile_ids = jnp.arange(n_tiles, dtype=I32)
    tile_expert = jnp.sum((tile_ids[:, None] >= tile_end[None, :]).astype(I32), axis=1)
    last_used = jnp.sum((n_used - 1 >= tile_end).astype(I32))
    tile_expert = jnp.where(tile_ids < n_used, tile_expert, last_used).astype(I32)
    assert tm % ZERO_CHUNK == 0
    per_tile = tm // ZERO_CHUNK
    piece = jnp.arange(per_tile, dtype=I32)[None, :] * ZERO_CHUNK
    n_piece = per_tile + 1 + -(-n_second // ZERO_CHUNK)
    first = seg_start + (counts_first // ZERO_CHUNK) * ZERO_CHUNK
    z_expert = first[:, None] + jnp.arange(n_piece, dtype=I32)[None, :] * ZERO_CHUNK
    z_expert = jnp.where(z_expert < (tile_end * tm)[:, None], z_expert, -1)
    spare = n_used + jnp.arange(N_EXPERTS, dtype=I32)
    z_spare = jnp.where((spare < n_tiles)[:, None], spare[:, None] * tm + piece, -1)
    zoff = jnp.concatenate([z_expert.reshape(-1), z_spare.reshape(-1)]).astype(I32)
    experts = jnp.arange(N_EXPERTS, dtype=I32)
    nonempty = tiles_per > 0
    seg_of_expert = jnp.cumsum(nonempty.astype(I32)) - 1
    later = nonempty[None, :] & (experts[None, :] > experts[:, None])
    next_of_expert = jnp.min(jnp.where(later, experts[None, :], N_EXPERTS), axis=1)
    next_of_expert = jnp.where(next_of_expert < N_EXPERTS, next_of_expert, -1)
    seg_index = seg_of_expert[tile_expert].astype(I32)
    next_expert = next_of_expert[tile_expert].astype(I32)
    return seg_start, tile_expert, n_used.reshape(1).astype(I32), zoff, seg_index, next_expert


def kernel(x_prompt, x_sample, c_prompt, c_sample, state_mlstm_C, state_mlstm_n, state_mlstm_m,
           cache_conv, w_ada, b_ada, norm1_g, w_in, b_igate, b_fgate, mh_norm_g, w_dw, b_dw,
           conv_ln_g, conv_ln_b, w_out, norm2_g, w_grp_router, b_grp_router, w_exp_router,
           b_exp_router, w_gate, w_up, w_down, final_g):
    depth = w_ada.shape[0]
    assert depth == 1, "one layer per step"
    bp, tp_, d = x_prompt.shape
    bs = x_sample.shape[0]
    assert x_sample.shape[1] == 1 and d == ROW_SUB * LANES
    dk = state_mlstm_C.shape[-1]
    dm = N_HEADS * dk
    dc = d - dm
    l = 0

    win = w_in[l]
    w_qkvo = win[:, 0:4 * dm]
    w_gates = win[:, 4 * dm:4 * dm + 2 * N_HEADS]
    w_glu = win[:, 4 * dm + 2 * N_HEADS:]
    wmain, wmain_lo = _split_weights(jnp.concatenate([w_qkvo, w_glu], axis=1))
    wout_hi, wout_lo = _split_weights(w_out[l])
    wkt_hi, wkt_lo = _split_weights(win[:, dm:2 * dm].T)
    wg_pad = jnp.pad(w_gates, ((0, 0), (0, LANES - 2 * N_HEADS)))
    wg_hi, wg_lo = _split_weights(wg_pad)
    gbias = jnp.pad(jnp.concatenate([b_igate[l], b_fgate[l]]), (0, LANES - 2 * N_HEADS)).reshape(1, LANES)
    w_r = jnp.concatenate([w_grp_router[l], w_exp_router[l]], axis=1)
    n_r = N_GROUPS + N_EXPERTS
    wr_hi, wr_lo = _split_weights(jnp.pad(w_r, ((0, 0), (0, LANES - n_r))))
    rbias = jnp.pad(jnp.concatenate([b_grp_router[l], b_exp_router[l]]), (0, LANES - n_r)).reshape(1, LANES)
    wts = dict(dm=dm, g1=norm1_g[l].reshape(1, d), g2=norm2_g[l].reshape(1, d), wmain=wmain,
               wg_hi=wg_hi, wg_lo=wg_lo, gbias=gbias, mhg=mh_norm_g[l],
               wdw=w_dw[l].reshape(CONV_W, dc), bdw=b_dw[l].reshape(1, dc),
               clg=conv_ln_g[l].reshape(1, dc), clb=conv_ln_b[l].reshape(1, dc),
               wout=wout_hi, wr_hi=wr_hi, wr_lo=wr_lo, rbias=rbias, wk_t=wkt_hi,
               wmain_lo=wmain_lo, wout_lo=wout_lo, wk_t_lo=wkt_lo)

    mod_p, mod_s = _ada(c_prompt, c_sample, w_ada[l], b_ada[l])

    n_p = bp * tp_
    n_all = n_p + bs
    x1_p, h2_p, route_p, c_p, n_pr, m_p, cv_p, base_p, tot_p = _mix_prompt(x_prompt, mod_p, wts)
    x1_s, h2_s, route_s, c_s, n_s, m_s, u_s = _mix_sample(
        x_sample.reshape(bs, d), mod_s, state_mlstm_C[l], state_mlstm_n[l], state_mlstm_m[l],
        cache_conv[l], wts)

    tm = MOE_TILE
    n_tiles = (2 * n_all) // tm + N_EXPERTS
    tp_p, tp_s = min(PERM_TILE, tp_), min(PERM_TILE, bs)
    assert tp_p == min(MIX_TILE, tp_), "the prompt mixer counts experts per PERM_TILE tokens"
    base_s, tot_s = _count(route_s, tp_s)
    counts = (tot_p[0, :N_EXPERTS] + tot_s[0, :N_EXPERTS]).astype(I32)
    seg_start, tile_expert, n_used, zoff, seg_index, next_expert = _plan(
        counts, tot_p[0, :N_EXPERTS].astype(I32), 2 * bs, tm, n_tiles)
    pos_p = _positions(route_p, base_p, seg_start, tp_p)
    pos_s = _positions(route_s, base_s + tot_p[None], seg_start, tp_s)

    xs = _scatter_rows(h2_p, pos_p, zoff, n_tiles * tm, tm)
    xs = _scatter_rows(h2_s, pos_s, zoff, xs, tm)
    ys = _moe(xs, tile_expert, n_used, seg_index, next_expert, w_gate[l], w_up[l], w_down[l], tm)

    y_p = _finish(x1_p, route_p, mod_p.reshape(bp, 1, -1), final_g, ys, pos_p)
    y_s = _finish(x1_s.reshape(1, bs, d), route_s, mod_s.reshape(1, bs, -1), final_g, ys,
                  pos_s).reshape(bs, 1, d)

    conv_s = jnp.concatenate([cache_conv[l][:, 1:, :], u_s[:, None, :]], axis=1)
    return (y_p, y_s,
            c_p, n_pr, m_p[:, 0, :N_HEADS].reshape(1, bp, N_HEADS), cv_p,
            c_s[None], n_s.reshape(1, bs, N_HEADS, dk), m_s[:, :N_HEADS].reshape(1, bs, N_HEADS),
            conv_s[None])
```
